```python
import jax, jax.numpy as jnp
from jax import lax
import numpy as np

D_MODEL = 1024
BATCH = 8
SEQ = 8192
DEPTH = 4

N_MIXERS = 4
RMS_EPS = 1e-6
N_MOD = 6

SB_HEADS = 16
SB_HEAD_DIM = D_MODEL // SB_HEADS
SB_QBLOCK = 128

GLA_HEADS = 4
GLA_DK = D_MODEL // (2 * GLA_HEADS)
GLA_DV = D_MODEL // GLA_HEADS
GLA_GATE_RANK = 16
GLA_GATE_TAU = 16.0
GLA_CHUNK = 64
GLA_IN = 2 * GLA_HEADS * GLA_DK + 2 * GLA_HEADS * GLA_DV + GLA_GATE_RANK

CONV_WIDTH = 3

DIL_PATTERNS = ((128, 1), (512, 4), (2048, 16))
DIL_GROUPS = len(DIL_PATTERNS)
DIL_HEADS = 8
DIL_HEAD_DIM = 64
DIL_QBLOCK = 128

MOE_GROUPS = 4
MOE_PER_GROUP = 8
MOE_EXPERTS = MOE_GROUPS * MOE_PER_GROUP
MOE_TOPK = 2
MOE_D_FF = D_MODEL // 2
MOE_BLOCK = 128

N_SB = (DEPTH + 3) // N_MIXERS
N_GLA = (DEPTH + 2) // N_MIXERS
N_CONV = (DEPTH + 1) // N_MIXERS
N_DIL = DEPTH // N_MIXERS

kernel_name = "hybrid_sb_gla_conv_dilated_hmoe"


def rms_norm(x, g):
    x32 = x.astype(jnp.float32)
    y = x32 * lax.rsqrt(jnp.mean(x32 * x32, axis=-1, keepdims=True) + RMS_EPS)
    return (y * g.astype(jnp.float32)).astype(x.dtype)


def stick_breaking_attention(q, k, v):
    B, H, S, d = q.shape
    nb = S // SB_QBLOCK
    scale = d ** -0.5
    kpos = jnp.arange(S)
    qb = jnp.moveaxis(q.reshape(B, H, nb, SB_QBLOCK, d), 2, 0)

    def one_block(args):
        q_blk, blk = args
        qpos = blk * SB_QBLOCK + jnp.arange(SB_QBLOCK)
        z = jnp.einsum('bhqd,bhkd->bhqk', q_blk, k, preferred_element_type=jnp.float32) * scale
        causal = kpos[None, :] < qpos[:, None]
        log_keep = jnp.where(causal, jax.nn.log_sigmoid(-z), 0.0)
        suffix = lax.cumsum(log_keep, axis=3, reverse=True) - log_keep
        a = jnp.where(causal, jnp.exp(jax.nn.log_sigmoid(z) + suffix), 0.0)
        return jnp.einsum('bhqk,bhkd->bhqd', a.astype(v.dtype), v)

    out = lax.map(one_block, (qb, jnp.arange(nb)))
    return jnp.moveaxis(out, 0, 2).reshape(B, H, S, d)


def sb_mixer(h, w_in, w_out):
    B, S, _ = h.shape
    qkv = (h @ w_in).reshape(B, S, 3, SB_HEADS, SB_HEAD_DIM)
    qkv = jnp.transpose(qkv, (2, 0, 3, 1, 4))
    o = stick_breaking_attention(qkv[0], qkv[1], qkv[2])
    return jnp.transpose(o, (0, 2, 1, 3)).reshape(B, S, SB_HEADS * SB_HEAD_DIM) @ w_out


def gla_chunked(q, k, v, log_a):
    B, H, S, dk = q.shape
    dv = v.shape[-1]
    C = GLA_CHUNK
    nc = S // C
    f32 = jnp.float32

    def chunks(a):
        return a.astype(f32).reshape(B, H, nc, C, a.shape[-1])

    q, k, v, g = chunks(q), chunks(k), chunks(v), chunks(log_a)
    b = jnp.cumsum(g, axis=3)
    b_last = b[:, :, :, -1:, :]
    q_in = q * jnp.exp(b)
    k_in = k * jnp.exp(-b)
    k_state = k * jnp.exp(b_last - b)
    causal = jnp.tril(jnp.ones((C, C), dtype=bool))
    scores = jnp.where(causal, jnp.einsum('bhnid,bhnjd->bhnij', q_in, k_in), 0.0)
    o_intra = jnp.einsum('bhnij,bhnjv->bhniv', scores, v)

    def step(state, inp):
        q_n, k_n, v_n, decay_n = inp
        o_n = jnp.einsum('bhid,bhdv->bhiv', q_n, state)
        state = decay_n[..., None] * state + jnp.einsum('bhjd,bhjv->bhdv', k_n, v_n)
        return state, o_n

    xs = (jnp.moveaxis(q_in, 2, 0), jnp.moveaxis(k_state, 2, 0), jnp.moveaxis(v, 2, 0),
          jnp.moveaxis(jnp.exp(b_last[:, :, :, 0, :]), 2, 0))
    _, o_inter = lax.scan(step, jnp.zeros((B, H, dk, dv), f32), xs)
    o = o_intra + jnp.moveaxis(o_inter, 0, 2)
    return o.reshape(B, H, S, dv)


def gla_mixer(h, w_in, w_gate_up, b_gate, norm_g, w_out):
    B, S, _ = h.shape
    dqk = GLA_HEADS * GLA_DK
    dv = GLA_HEADS * GLA_DV
    proj = h @ w_in
    q, k, v, r, g_down = jnp.split(proj, [dqk, 2 * dqk, 2 * dqk + dv, 2 * dqk + 2 * dv], axis=-1)
    log_a = jax.nn.log_sigmoid((g_down @ w_gate_up + b_gate).astype(jnp.float32)) / GLA_GATE_TAU

    def heads(a, d):
        return jnp.transpose(a.reshape(B, S, GLA_HEADS, d), (0, 2, 1, 3))

    o = gla_chunked(heads(q, GLA_DK) * GLA_DK ** -0.5, heads(k, GLA_DK),
                    heads(v, GLA_DV), heads(log_a, GLA_DK))
    o = rms_norm(o, norm_g)
    o = jnp.transpose(o, (0, 2, 1, 3)).reshape(B, S, dv).astype(h.dtype) * jax.nn.silu(r)
    return o @ w_out


def conv_mixer(h, w_in, conv_w, conv_b, w_out):
    S = h.shape[1]
    gate_b, gate_c, u = jnp.split(h @ w_in, 3, axis=-1)
    u = gate_c * u
    u_pad = jnp.pad(u, ((0, 0), (CONV_WIDTH - 1, 0), (0, 0)))
    y = sum(conv_w[j] * u_pad[:, j:j + S] for j in range(CONV_WIDTH)) + conv_b
    return (gate_b * y) @ w_out


def dilated_window_attention(q, k, v, window, dilation):
    B, H, S, d = q.shape
    Q = DIL_QBLOCK
    steps = window // dilation
    span = dilation * Q
    Sp = -(-S // span) * span
    L = Sp // dilation
    nb = L // Q

    def strided(a):
        a = jnp.pad(a, ((0, 0), (0, 0), (0, Sp - S), (0, 0)))
        a = jnp.transpose(a.reshape(B, H, L, dilation, d), (0, 1, 3, 2, 4))
        return a.reshape(B, H, dilation, nb, Q, d)

    def with_prev(a):
        prev = jnp.concatenate([jnp.zeros_like(a[:, :, :, :1]), a[:, :, :, :-1]], axis=3)
        return jnp.concatenate([prev, a], axis=4)

    qb = strided(q)
    kc, vc = with_prev(strided(k)), with_prev(strided(v))
    z = jnp.einsum('bhrnqd,bhrnkd->bhrnqk', qb, kc, preferred_element_type=jnp.float32) * d ** -0.5
    qi = jnp.arange(Q)[:, None]
    kj = jnp.arange(2 * Q)[None, :]
    dist = qi + Q - kj
    in_window = (dist >= 0) & (dist <= steps)
    before_start = (jnp.arange(nb)[:, None, None] == 0) & (kj < Q)[None]
    valid = in_window[None] & ~before_start
    z = jnp.where(valid, z, -jnp.inf)
    m = jnp.max(z, axis=-1, keepdims=True)
    lse = m + jnp.log(jnp.sum(jnp.exp(z - m), axis=-1, keepdims=True))
    p = jnp.exp(z - lse)
    o = jnp.einsum('bhrnqk,bhrnkd->bhrnqd', p.astype(vc.dtype), vc)

    def unstride(a):
        e = a.shape[-1]
        a = jnp.moveaxis(a.reshape(B, H, dilation, L, e), 2, 3).reshape(B, H, Sp, e)
        return a[:, :, :S]

    return unstride(o), unstride(lse)[..., 0]


def dil_mixer(h, w_in, w_out):
    B, S, _ = h.shape
    proj = (h @ w_in).reshape(B, S, DIL_GROUPS, 3, DIL_HEADS, DIL_HEAD_DIM)
    proj = jnp.transpose(proj, (2, 3, 0, 4, 1, 5))
    outs, lses = [], []
    for g, (window, dilation) in enumerate(DIL_PATTERNS):
        o_g, lse_g = dilated_window_attention(proj[g, 0], proj[g, 1], proj[g, 2], window, dilation)
        outs.append(o_g)
        lses.append(lse_g)
    o = jnp.stack(outs)
    w = jax.nn.softmax(jnp.stack(lses), axis=0)
    o = jnp.einsum('gbhs,gbhsd->bshd', w.astype(o.dtype), o).reshape(B, S, DIL_HEADS * DIL_HEAD_DIM)
    return o @ w_out


def hier_moe(h, w_grp, b_grp, w_exp, b_exp, w_gate, w_up, w_down):
    T, D = h.shape
    f32 = jnp.float32
    grp_logits = (h @ w_grp + b_grp).astype(f32)
    grp = jnp.argmax(grp_logits, axis=-1)
    p_grp = jnp.take_along_axis(jax.nn.softmax(grp_logits, axis=-1), grp[:, None], axis=-1)
    exp_logits = (h @ w_exp + b_exp).astype(f32).reshape(T, MOE_GROUPS, MOE_PER_GROUP)
    in_grp = jnp.take_along_axis(exp_logits, grp[:, None, None], axis=1)[:, 0]
    p_in, top_i = lax.top_k(jax.nn.softmax(in_grp, axis=-1), MOE_TOPK)
    gate = p_grp * p_in / jnp.sum(p_in, axis=-1, keepdims=True)
    expert = grp[:, None] * MOE_PER_GROUP + top_i

    A = T * MOE_TOPK
    flat_e = expert.reshape(A)
    order = jnp.argsort(flat_e)
    sorted_e = flat_e[order]
    sorted_tok = order // MOE_TOPK
    sorted_w = gate.reshape(A)[order]
    counts = jnp.bincount(flat_e, length=MOE_EXPERTS)
    padded = (counts + MOE_BLOCK - 1) // MOE_BLOCK * MOE_BLOCK
    pad_end = jnp.cumsum(padded)
    pad_start = pad_end - padded
    start = jnp.cumsum(counts) - counts
    dest = pad_start[sorted_e] + jnp.arange(A) - start[sorted_e]
    P = (-(-A // MOE_BLOCK) + MOE_EXPERTS) * MOE_BLOCK
    NB = P // MOE_BLOCK
    slot_tok = jnp.full((P,), T, dtype=jnp.int32).at[dest].set(sorted_tok.astype(jnp.int32))
    slot_w = jnp.zeros((P,), f32).at[dest].set(sorted_w)
    blk_e = jnp.minimum(jnp.searchsorted(pad_end, jnp.arange(NB) * MOE_BLOCK, side='right'),
                        MOE_EXPERTS - 1)
    h_pad = jnp.concatenate([h, jnp.zeros((1, D), h.dtype)], axis=0)
    xb = h_pad[slot_tok].reshape(NB, MOE_BLOCK, D)

    def expert_block(args):
        x_blk, e = args
        return (jax.nn.silu(x_blk @ w_gate[e]) * (x_blk @ w_up[e])) @ w_down[e]

    yb = lax.map(expert_block, (xb, blk_e)).reshape(P, D)
    y = jax.ops.segment_sum(yb.astype(f32) * slot_w[:, None], slot_tok, num_segments=T + 1)[:T]
    return y.astype(h.dtype)


def setup_inputs(seed: int = 0) -> dict:
    key = jax.random.key(seed)
    keys = jax.random.split(key, 26)
    D = D_MODEL

    def nrm(i, shape, scale):
        return jax.random.normal(keys[i], shape, jnp.float32) * scale

    sb_w = SB_HEADS * SB_HEAD_DIM
    gla_dqk = GLA_HEADS * GLA_DK
    gla_dv = GLA_HEADS * GLA_DV
    dil_w = DIL_HEADS * DIL_HEAD_DIM
    return {
        "x": nrm(0, (BATCH, SEQ, D), 1.0),
        "c": nrm(1, (BATCH, D), 1.0),
        "ada_w": nrm(2, (DEPTH, D, N_MOD * D), 0.5 * D ** -0.5),
        "ada_b": nrm(3, (DEPTH, N_MOD * D), 0.01),
        "norm_g": 1.0 + nrm(4, (DEPTH, 2, D), 0.02),
        "final_g": 1.0 + nrm(5, (D,), 0.02),
        "sb_w_in": nrm(6, (N_SB, D, 3 * sb_w), D ** -0.5),
        "sb_w_out": nrm(7, (N_SB, sb_w, D), sb_w ** -0.5),
        "gla_w_in": nrm(8, (N_GLA, D, GLA_IN), D ** -0.5),
        "gla_w_gate_up": nrm(9, (N_GLA, GLA_GATE_RANK, gla_dqk), GLA_GATE_RANK ** -0.5),
        "gla_b_gate": nrm(10, (N_GLA, gla_dqk), 0.01),
        "gla_norm_g": 1.0 + nrm(11, (N_GLA, GLA_DV), 0.02),
        "gla_w_out": nrm(12, (N_GLA, gla_dv, D), gla_dv ** -0.5),
        "conv_w_in": nrm(13, (N_CONV, D, 3 * D), D ** -0.5),
        "conv_w": nrm(14, (N_CONV, CONV_WIDTH, D), CONV_WIDTH ** -0.5),
        "conv_b": nrm(15, (N_CONV, D), 0.01),
        "conv_w_out": nrm(16, (N_CONV, D, D), D ** -0.5),
        "dil_w_in": nrm(17, (N_DIL, D, DIL_GROUPS * 3 * dil_w), D ** -0.5),
        "dil_w_out": nrm(18, (N_DIL, dil_w, D), dil_w ** -0.5),
        "moe_w_grp": nrm(19, (DEPTH, D, MOE_GROUPS), D ** -0.5),
        "moe_b_grp": nrm(20, (DEPTH, MOE_GROUPS), 0.01),
        "moe_w_exp": nrm(21, (DEPTH, D, MOE_EXPERTS), D ** -0.5),
        "moe_b_exp": nrm(22, (DEPTH, MOE_EXPERTS), 0.01),
        "moe_w_gate": nrm(23, (DEPTH, MOE_EXPERTS, D, MOE_D_FF), D ** -0.5),
        "moe_w_up": nrm(24, (DEPTH, MOE_EXPERTS, D, MOE_D_FF), D ** -0.5),
        "moe_w_down": nrm(25, (DEPTH, MOE_EXPERTS, MOE_D_FF, D), MOE_D_FF ** -0.5),
    }


def reference(x, c, ada_w, ada_b, norm_g, final_g, sb_w_in, sb_w_out, gla_w_in, gla_w_gate_up,
              gla_b_gate, gla_norm_g, gla_w_out, conv_w_in, conv_w, conv_b, conv_w_out,
              dil_w_in, dil_w_out, moe_w_grp, moe_b_grp, moe_w_exp, moe_b_exp, moe_w_gate,
              moe_w_up, moe_w_down):
    B, S, D = x.shape
    cond = jax.nn.silu(c)
    for i in range(DEPTH):
        mod = cond @ ada_w[i] + ada_b[i]
        sh1, sc1, g1, sh2, sc2, g2 = jnp.split(mod[:, None, :], N_MOD, axis=-1)

        h = rms_norm(x, norm_g[i, 0]) * (1.0 + sc1) + sh1
        kind, j = i % N_MIXERS, i // N_MIXERS
        if kind == 0:
            y = sb_mixer(h, sb_w_in[j], sb_w_out[j])
        elif kind == 1:
            y = gla_mixer(h, gla_w_in[j], gla_w_gate_up[j], gla_b_gate[j], gla_norm_g[j], gla_w_out[j])
        elif kind == 2:
            y = conv_mixer(h, conv_w_in[j], conv_w[j], conv_b[j], conv_w_out[j])
        else:
            y = dil_mixer(h, dil_w_in[j], dil_w_out[j])
        x = (x + g1 * y).astype(x.dtype)

        h = rms_norm(x, norm_g[i, 1]) * (1.0 + sc2) + sh2
        y = hier_moe(h.reshape(B * S, D), moe_w_grp[i], moe_b_grp[i], moe_w_exp[i], moe_b_exp[i],
                     moe_w_gate[i], moe_w_up[i], moe_w_down[i]).reshape(B, S, D)
        x = (x + g2 * y).astype(x.dtype)
    return rms_norm(x, final_g)
```

```python
import functools

import jax
import jax.numpy as jnp
from jax import lax
from jax.experimental import pallas as pl
from jax.experimental.pallas import tpu as pltpu

F32 = jnp.float32
BF16 = jnp.bfloat16

LANES = 128
RMS_EPS = 1e-6
N_MOD = 6

SB_HEADS = 16
GLA_HEADS = 4
GLA_GATE_RANK = 16
GLA_GATE_TAU = 16.0
GLA_CHUNK = 64
CONV_WIDTH = 3
DIL_PATTERNS = ((128, 1), (512, 4), (2048, 16))
DIL_HEADS = 8
DIL_HEAD_DIM = 64
DIL_QBLOCK = 128
MOE_GROUPS = 4
MOE_PER_GROUP = 8
MOE_EXPERTS = MOE_GROUPS * MOE_PER_GROUP

SB_LOG_ZERO = -104.0

VMEM_LIMIT = 48 * 1024 * 1024


def _params(*sem):
    return pltpu.CompilerParams(dimension_semantics=sem, vmem_limit_bytes=VMEM_LIMIT)


def _sigmoid(x):
    return 1.0 / (1.0 + jnp.exp(-x))


def _softplus(x):
    return jnp.maximum(x, 0.0) + jnp.log(1.0 + jnp.exp(-jnp.abs(x)))


def _split_bf16(x):
    hi = x.astype(BF16)
    lo = (x - hi.astype(F32)).astype(BF16)
    return hi, lo


def _dot(a, b):
    return jnp.dot(a, b, preferred_element_type=F32)


def _dot_nt(a, b):
    return lax.dot_general(a, b, (((1,), (1,)), ((), ())), preferred_element_type=F32)


def _dot_tn(a, b):
    return lax.dot_general(a, b, (((0,), (0,)), ((), ())), preferred_element_type=F32)


def _mod_norm(x, g, sc, sh):
    r = lax.rsqrt(jnp.mean(x * x, axis=-1, keepdims=True) + RMS_EPS)
    return (x * r) * g * (1.0 + sc) + sh


def _ada_kernel(c_ref, w_ref, b_ref, o_ref):
    c = c_ref[...]
    cond = c * _sigmoid(c)
    o_ref[0] = _dot(cond.astype(BF16), w_ref[0].astype(BF16)) + b_ref[0]


def _ada_mod(c, ada_w, ada_b):
    depth, d, n = ada_w.shape
    b = c.shape[0]
    tn = 1536
    return pl.pallas_call(
        _ada_kernel,
        grid=(depth, n // tn),
        in_specs=[
            pl.BlockSpec((b, d), lambda i, j: (0, 0)),
            pl.BlockSpec((1, d, tn), lambda i, j: (i, 0, j)),
            pl.BlockSpec((1, 1, tn), lambda i, j: (i, 0, j)),
        ],
        out_specs=pl.BlockSpec((1, b, tn), lambda i, j: (i, 0, j)),
        out_shape=jax.ShapeDtypeStruct((depth, b, n), F32),
        compiler_params=_params("parallel", "parallel"),
        name="ada_mod",
    )(c, ada_w, ada_b.reshape(depth, 1, n))


def _norm_proj_kernel(x_ref, g_ref, sc_ref, sh_ref, w_ref, o_ref, h_ref):
    @pl.when(pl.program_id(2) == 0)
    def _():
        h_ref[...] = _mod_norm(x_ref[0], g_ref[...], sc_ref[0], sh_ref[0]).astype(BF16)

    o_ref[0] = _dot(h_ref[...], w_ref[...]).astype(o_ref.dtype)


def _norm_proj(x, g, sc, sh, w, tm=1024, tn=512):
    b, s, d = x.shape
    n = w.shape[1]
    tm = min(tm, s)
    assert s % tm == 0 and n % tn == 0
    return pl.pallas_call(
        _norm_proj_kernel,
        grid=(b, s // tm, n // tn),
        in_specs=[
            pl.BlockSpec((1, tm, d), lambda bi, i, j: (bi, i, 0)),
            pl.BlockSpec((1, d), lambda bi, i, j: (0, 0)),
            pl.BlockSpec((1, 1, d), lambda bi, i, j: (bi, 0, 0)),
            pl.BlockSpec((1, 1, d), lambda bi, i, j: (bi, 0, 0)),
            pl.BlockSpec((d, tn), lambda bi, i, j: (0, j)),
        ],
        out_specs=pl.BlockSpec((1, tm, tn), lambda bi, i, j: (bi, i, j)),
        out_shape=jax.ShapeDtypeStruct((b, s, n), BF16),
        scratch_shapes=[pltpu.VMEM((tm, d), BF16)],
        compiler_params=_params("parallel", "parallel", "arbitrary"),
        name="norm_proj",
    )(x, g.reshape(1, d), sc, sh, w)


def _proj_res_kernel(o_ref, w_ref, x_ref, g_ref, y_ref):
    y_ref[0] = x_ref[0] + g_ref[0] * _dot(o_ref[0], w_ref[...])


def _proj_res(o, w, x, gate, tm=512):
    b, s, d = x.shape
    k = o.shape[-1]
    tm = min(tm, s)
    return pl.pallas_call(
        _proj_res_kernel,
        grid=(b, s // tm),
        in_specs=[
            pl.BlockSpec((1, tm, k), lambda bi, i: (bi, i, 0)),
            pl.BlockSpec((k, d), lambda bi, i: (0, 0)),
            pl.BlockSpec((1, tm, d), lambda bi, i: (bi, i, 0)),
            pl.BlockSpec((1, 1, d), lambda bi, i: (bi, 0, 0)),
        ],
        out_specs=pl.BlockSpec((1, tm, d), lambda bi, i: (bi, i, 0)),
        out_shape=jax.ShapeDtypeStruct((b, s, d), F32),
        compiler_params=_params("parallel", "parallel"),
        name="proj_res",
    )(o, w, x, gate)


def _sb_kernel(q_ref, k_ref, v_ref, o_ref, acc_ref, c_ref, *, blk, hd):
    i = pl.program_id(2)
    scale = hd ** -0.5
    first = lax.broadcasted_iota(jnp.int32, (blk, LANES), 1) < hd
    q = q_ref[0]
    zero = jnp.zeros_like(q)
    q_heads = (jnp.where(first, q, zero), jnp.where(first, zero, q))
    row = lax.broadcasted_iota(jnp.int32, (blk, blk), 0)
    col = lax.broadcasted_iota(jnp.int32, (blk, blk), 1)
    later = jnp.where(row > col, 1.0, 0.0).astype(BF16)
    causal = col < row

    def visit(start, mask):
        kb = k_ref[0, pl.ds(start, blk), :]
        vb = v_ref[0, pl.ds(start, blk), :]
        outs = []
        for h in range(2):
            z = _dot_nt(q_heads[h], kb) * scale
            sp = _softplus(z)
            log_keep = -sp
            if mask is not None:
                log_keep = jnp.where(mask, log_keep, 0.0)
            suffix = _dot(log_keep.astype(BF16), later)
            c = c_ref[h]
            a = jnp.exp((z - sp) + suffix + c)
            if mask is not None:
                a = jnp.where(mask, a, 0.0)
            outs.append(_dot(a.astype(BF16), vb))
            c_ref[h] = c + jnp.sum(log_keep, axis=1, keepdims=True)
        acc_ref[...] += jnp.where(first, outs[0], outs[1])

    acc_ref[...] = jnp.zeros_like(acc_ref)
    c_ref[...] = jnp.zeros_like(c_ref)
    visit(pl.multiple_of(i * blk, blk), causal)

    def cond(carry):
        j, cmax = carry
        return jnp.logical_and(j >= 0, cmax > SB_LOG_ZERO)

    def body(carry):
        j, _ = carry
        visit(pl.multiple_of(j * blk, blk), None)
        return j - 1, jnp.max(c_ref[...])

    lax.while_loop(cond, body, (i - 1, jnp.max(c_ref[...])))
    o_ref[0] = acc_ref[...].astype(o_ref.dtype)


def _sb_attention(qkv, blk=256):
    b, s, n = qkv.shape
    width = n // 3
    hd = width // SB_HEADS
    assert 2 * hd == LANES
    pairs = width // LANES
    blk = min(blk, s)
    return pl.pallas_call(
        functools.partial(_sb_kernel, blk=blk, hd=hd),
        grid=(b, pairs, s // blk),
        in_specs=[
            pl.BlockSpec((1, blk, LANES), lambda bi, p, i: (bi, i, p)),
            pl.BlockSpec((1, s, LANES), lambda bi, p, i: (bi, 0, pairs + p)),
            pl.BlockSpec((1, s, LANES), lambda bi, p, i: (bi, 0, 2 * pairs + p)),
        ],
        out_specs=pl.BlockSpec((1, blk, LANES), lambda bi, p, i: (bi, i, p)),
        out_shape=jax.ShapeDtypeStruct((b, s, width), BF16),
        scratch_shapes=[pltpu.VMEM((blk, LANES), F32), pltpu.VMEM((2, blk, 1), F32)],
        compiler_params=_params("parallel", "parallel", "arbitrary"),
        name="sb_attention",
    )(qkv, qkv, qkv)


def _gla_kernel(p_ref, wg_ref, bg_ref, ng_ref, o_ref, st_ref, *, rows, dk, dv):
    @pl.when(pl.program_id(1) == 0)
    def _():
        st_ref[...] = jnp.zeros_like(st_ref)

    heads = GLA_HEADS
    ch = GLA_CHUNK
    k0 = heads * dk
    v0 = 2 * heads * dk
    r0 = v0 + heads * dv
    g0 = r0 + heads * dv
    row = lax.broadcasted_iota(jnp.int32, (ch, ch), 0)
    col = lax.broadcasted_iota(jnp.int32, (ch, ch), 1)
    causal = col <= row
    lower = jnp.where(causal, 1.0, 0.0).astype(BF16)
    for s in range(rows // ch):
        t = slice(s * ch, (s + 1) * ch)
        g_pre = _dot(p_ref[0, t, g0:g0 + LANES], wg_ref[...]) + bg_ref[...]
        log_a = (jnp.minimum(g_pre, 0.0) - jnp.log(1.0 + jnp.exp(-jnp.abs(g_pre)))) * (1.0 / GLA_GATE_TAU)
        hi, lo = _split_bf16(log_a)
        cum = _dot(lower, hi) + _dot(lower, lo)
        for h in range(heads):
            bh = cum[:, h * dk:(h + 1) * dk]
            b_last = bh[ch - 1:ch, :]
            q = p_ref[0, t, h * dk:(h + 1) * dk].astype(F32) * dk ** -0.5
            k = p_ref[0, t, k0 + h * dk:k0 + (h + 1) * dk].astype(F32)
            v = p_ref[0, t, v0 + h * dv:v0 + (h + 1) * dv]
            r = p_ref[0, t, r0 + h * dv:r0 + (h + 1) * dv].astype(F32)
            q_in = (q * jnp.exp(bh)).astype(BF16)
            k_in = (k * jnp.exp(-bh)).astype(BF16)
            k_state = (k * jnp.exp(b_last - bh)).astype(BF16)
            scores = jnp.where(causal, _dot_nt(q_in, k_in), 0.0)
            state = st_ref[h]
            o = _dot(scores.astype(BF16), v) + _dot_nt(q_in, state.astype(BF16))
            st_ref[h] = jnp.exp(b_last) * state + _dot_tn(v, k_state)
            o = o * lax.rsqrt(jnp.mean(o * o, axis=-1, keepdims=True) + RMS_EPS) * ng_ref[...]
            o_ref[0, t, h * dv:(h + 1) * dv] = (o * (r * _sigmoid(r))).astype(o_ref.dtype)


def _gla_core(proj, w_gate_up, b_gate, norm_g, dk, dv, rows=256):
    b, s, n = proj.shape
    rows = min(rows, s)
    hk = GLA_HEADS * dk
    return pl.pallas_call(
        functools.partial(_gla_kernel, rows=rows, dk=dk, dv=dv),
        grid=(b, s // rows),
        in_specs=[
            pl.BlockSpec((1, rows, n), lambda bi, i: (bi, i, 0)),
            pl.BlockSpec((LANES, hk), lambda bi, i: (0, 0)),
            pl.BlockSpec((1, hk), lambda bi, i: (0, 0)),
            pl.BlockSpec((1, dv), lambda bi, i: (0, 0)),
        ],
        out_specs=pl.BlockSpec((1, rows, GLA_HEADS * dv), lambda bi, i: (bi, i, 0)),
        out_shape=jax.ShapeDtypeStruct((b, s, GLA_HEADS * dv), BF16),
        scratch_shapes=[pltpu.VMEM((GLA_HEADS, dv, dk), F32)],
        compiler_params=_params("parallel", "arbitrary"),
        name="gla_core",
    )(proj, w_gate_up, b_gate, norm_g)


def _conv_res_kernel(gb_ref, gc_ref, u_ref, hgc_ref, hu_ref, cw_ref, cb_ref, w_ref, x_ref, g_ref, y_ref):
    u2 = gc_ref[0].astype(F32) * u_ref[0].astype(F32)
    halo = hgc_ref[0].astype(F32) * hu_ref[0].astype(F32)
    halo = jnp.where(pl.program_id(1) > 0, halo, 0.0)
    rows = lax.broadcasted_iota(jnp.int32, u2.shape, 0)
    prev1 = jnp.where(rows == 0, halo[7:8], pltpu.roll(u2, 1, 0))
    prev2 = jnp.where(rows == 0, halo[6:7], jnp.where(rows == 1, halo[7:8], pltpu.roll(u2, 2, 0)))
    cw = cw_ref[...]
    y = cw[0:1] * prev2 + cw[1:2] * prev1 + cw[2:3] * u2 + cb_ref[...]
    o = (gb_ref[0].astype(F32) * y).astype(BF16)
    y_ref[0] = x_ref[0] + g_ref[0] * _dot(o, w_ref[...])


def _conv_res(proj, conv_w, conv_b, w_out, x, gate, tm=512):
    b, s, d = x.shape
    tm = min(tm, s)
    halo_blocks = tm // 8

    def halo_map(col):
        return lambda bi, i: (bi, jnp.maximum(i * halo_blocks - 1, 0), col)

    return pl.pallas_call(
        _conv_res_kernel,
        grid=(b, s // tm),
        in_specs=[
            pl.BlockSpec((1, tm, d), lambda bi, i: (bi, i, 0)),
            pl.BlockSpec((1, tm, d), lambda bi, i: (bi, i, 1)),
            pl.BlockSpec((1, tm, d), lambda bi, i: (bi, i, 2)),
            pl.BlockSpec((1, 8, d), halo_map(1)),
            pl.BlockSpec((1, 8, d), halo_map(2)),
            pl.BlockSpec((CONV_WIDTH, d), lambda bi, i: (0, 0)),
            pl.BlockSpec((1, d), lambda bi, i: (0, 0)),
            pl.BlockSpec((d, d), lambda bi, i: (0, 0)),
            pl.BlockSpec((1, tm, d), lambda bi, i: (bi, i, 0)),
            pl.BlockSpec((1, 1, d), lambda bi, i: (bi, 0, 0)),
        ],
        out_specs=pl.BlockSpec((1, tm, d), lambda bi, i: (bi, i, 0)),
        out_shape=jax.ShapeDtypeStruct((b, s, d), F32),
        compiler_params=_params("parallel", "parallel"),
        name="conv_res",
    )(proj, proj, proj, proj, proj, conv_w, conv_b.reshape(1, d), w_out, x, gate)


def _dil_kernel(q_ref, kc_ref, vc_ref, kp_ref, vp_ref, o_ref, l_ref, *, rows, hd):
    qb = DIL_QBLOCK
    n = pl.program_id(2)
    scale = hd ** -0.5
    first = lax.broadcasted_iota(jnp.int32, (qb, LANES), 1) < hd
    row = lax.broadcasted_iota(jnp.int32, (qb, qb), 0)
    col = lax.broadcasted_iota(jnp.int32, (qb, qb), 1)
    in_prev = col >= row
    in_cur = col <= row
    neg = -jnp.inf
    for s in range(rows // qb):
        t = slice(s * qb, (s + 1) * qb)
        if s == 0:
            prev_mask = jnp.logical_and(in_prev, n > 0)
        else:
            prev_mask = in_prev
            tp = slice((s - 1) * qb, s * qb)
        for p in range(q_ref.shape[-1] // LANES):
            c = slice(p * LANES, (p + 1) * LANES)
            q = q_ref[0, t, c]
            zero = jnp.zeros_like(q)
            k_cur, v_cur = kc_ref[0, t, c], vc_ref[0, t, c]
            if s == 0:
                k_prev, v_prev = kp_ref[0, :, c], vp_ref[0, :, c]
            else:
                k_prev, v_prev = kc_ref[0, tp, c], vc_ref[0, tp, c]
            outs, lses = [], []
            for h in range(2):
                qh = jnp.where(first, q, zero) if h == 0 else jnp.where(first, zero, q)
                z_prev = jnp.where(prev_mask, _dot_nt(qh, k_prev) * scale, neg)
                z_cur = jnp.where(in_cur, _dot_nt(qh, k_cur) * scale, neg)
                m = jnp.maximum(jnp.max(z_prev, axis=1, keepdims=True), jnp.max(z_cur, axis=1, keepdims=True))
                p_prev = jnp.exp(z_prev - m)
                p_cur = jnp.exp(z_cur - m)
                den = jnp.sum(p_prev, axis=1, keepdims=True) + jnp.sum(p_cur, axis=1, keepdims=True)
                num = _dot(p_prev.astype(BF16), v_prev) + _dot(p_cur.astype(BF16), v_cur)
                outs.append(num / den)
                lses.append(jnp.broadcast_to(m + jnp.log(den), (qb, LANES)))
            o_ref[0, t, c] = jnp.where(first, outs[0], outs[1])
            l_ref[0, t, c] = jnp.where(first, lses[0], lses[1])


def _dil_group(proj, g, dilation, rows=512):
    b, s, n = proj.shape
    width = DIL_HEADS * DIL_HEAD_DIM
    groups = n // (3 * width)
    length = s // dilation
    assert s % (dilation * DIL_QBLOCK) == 0 and DIL_PATTERNS[g][0] == dilation * DIL_QBLOCK
    rows = min(rows, length)
    view = proj.reshape(b, length, dilation * n)
    per_row = n // width
    prev_blocks = rows // DIL_QBLOCK

    def cur(part):
        return lambda bi, rho, i: (bi, i, rho * per_row + g * 3 + part)

    def prev(part):
        return lambda bi, rho, i: (bi, jnp.maximum(i * prev_blocks - 1, 0), rho * per_row + g * 3 + part)

    out_shape = jax.ShapeDtypeStruct((b, length, dilation * width), F32)
    out_spec = pl.BlockSpec((1, rows, width), lambda bi, rho, i: (bi, i, rho))
    o, lse = pl.pallas_call(
        functools.partial(_dil_kernel, rows=rows, hd=DIL_HEAD_DIM),
        grid=(b, dilation, length // rows),
        in_specs=[
            pl.BlockSpec((1, rows, width), cur(0)),
            pl.BlockSpec((1, rows, width), cur(1)),
            pl.BlockSpec((1, rows, width), cur(2)),
            pl.BlockSpec((1, DIL_QBLOCK, width), prev(1)),
            pl.BlockSpec((1, DIL_QBLOCK, width), prev(2)),
        ],
        out_specs=[out_spec, out_spec],
        out_shape=[out_shape, out_shape],
        compiler_params=_params("parallel", "parallel", "arbitrary"),
        name=f"dil_group{g}",
    )(view, view, view, view, view)
    return o.reshape(b, s, width), lse.reshape(b, s, width)


def _dil_res_kernel(o0, o1, o2, l0, l1, l2, w_ref, x_ref, g_ref, y_ref):
    ls = (l0[0], l1[0], l2[0])
    m = jnp.maximum(jnp.maximum(ls[0], ls[1]), ls[2])
    es = [jnp.exp(l - m) for l in ls]
    o = (es[0] * o0[0] + es[1] * o1[0] + es[2] * o2[0]) / (es[0] + es[1] + es[2])
    y_ref[0] = x_ref[0] + g_ref[0] * _dot(o.astype(BF16), w_ref[...])


def _dil_res(outs, lses, w_out, x, gate, tm=512):
    b, s, d = x.shape
    k = w_out.shape[0]
    tm = min(tm, s)
    part = pl.BlockSpec((1, tm, k), lambda bi, i: (bi, i, 0))
    return pl.pallas_call(
        _dil_res_kernel,
        grid=(b, s // tm),
        in_specs=[part] * 6 + [
            pl.BlockSpec((k, d), lambda bi, i: (0, 0)),
            pl.BlockSpec((1, tm, d), lambda bi, i: (bi, i, 0)),
            pl.BlockSpec((1, 1, d), lambda bi, i: (bi, 0, 0)),
        ],
        out_specs=pl.BlockSpec((1, tm, d), lambda bi, i: (bi, i, 0)),
        out_shape=jax.ShapeDtypeStruct((b, s, d), F32),
        compiler_params=_params("parallel", "parallel"),
        name="dil_res",
    )(*outs, *lses, w_out, x, gate)


def _router_kernel(x_ref, g_ref, sc_ref, sh_ref, wr_ref, br_ref, h_ref, route_ref):
    h = _mod_norm(x_ref[0], g_ref[...], sc_ref[0], sh_ref[0])
    for s in range(h.shape[1] // LANES):
        h_ref[:, s, :] = h[:, s * LANES:(s + 1) * LANES]
    h_hi, h_lo = _split_bf16(h)
    w_hi, w_lo = _split_bf16(wr_ref[...])
    logits = _dot(h_hi, w_hi) + (_dot(h_hi, w_lo) + _dot(h_lo, w_hi)) + br_ref[...]
    lane = lax.broadcasted_iota(jnp.int32, logits.shape, 1).astype(F32)
    neg = -jnp.inf
    far = float(LANES)

    def first_argmax(vals):
        top = jnp.max(vals, axis=1, keepdims=True)
        return top, jnp.min(jnp.where(vals == top, lane, far), axis=1, keepdims=True)

    grp_logits = jnp.where(lane < MOE_GROUPS, logits, neg)
    grp_max, grp = first_argmax(grp_logits)
    p_grp = 1.0 / jnp.sum(jnp.exp(grp_logits - grp_max), axis=1, keepdims=True)
    base = MOE_GROUPS + grp * MOE_PER_GROUP
    in_grp = jnp.logical_and(lane >= base, lane < base + MOE_PER_GROUP)
    exp_logits = jnp.where(in_grp, logits, neg)
    m1, i1 = first_argmax(exp_logits)
    m2, i2 = first_argmax(jnp.where(lane == i1, neg, exp_logits))
    e2 = jnp.exp(m2 - m1)
    g1 = p_grp / (1.0 + e2)
    g2 = g1 * e2
    route = jnp.where(lane == 0, i1 - MOE_GROUPS, jnp.where(lane == 1, i2 - MOE_GROUPS, 0.0))
    route_ref[...] = jnp.where(lane == 2, g1, jnp.where(lane == 3, g2, route))


def _router(x, g, sc, sh, w_route, b_route, tm=512):
    b, s, d = x.shape
    tm = min(tm, s)
    steps = s // tm
    chunks = d // LANES
    return pl.pallas_call(
        _router_kernel,
        grid=(b, steps),
        in_specs=[
            pl.BlockSpec((1, tm, d), lambda bi, i: (bi, i, 0)),
            pl.BlockSpec((1, d), lambda bi, i: (0, 0)),
            pl.BlockSpec((1, 1, d), lambda bi, i: (bi, 0, 0)),
            pl.BlockSpec((1, 1, d), lambda bi, i: (bi, 0, 0)),
            pl.BlockSpec((d, LANES), lambda bi, i: (0, 0)),
            pl.BlockSpec((1, LANES), lambda bi, i: (0, 0)),
        ],
        out_specs=[
            pl.BlockSpec((tm, chunks, LANES), lambda bi, i: (bi * steps + i, 0, 0)),
            pl.BlockSpec((tm, LANES), lambda bi, i: (bi * steps + i, 0)),
        ],
        out_shape=[
            jax.ShapeDtypeStruct((b * s, chunks, LANES), F32),
            jax.ShapeDtypeStruct((b * s, LANES), F32),
        ],
        compiler_params=_params("parallel", "parallel"),
        name="moe_router",
    )(x, g.reshape(1, d), sc, sh, w_route, b_route)


def _gather_rows(idx_smem, offset, count, src_hbm, dst, sem):
    def issue(r, carry):
        pltpu.make_async_copy(src_hbm.at[idx_smem[offset + r]], dst.at[r], sem).start()
        return carry

    lax.fori_loop(0, count, issue, 0, unroll=8)


def _wait_rows(count, src_hbm, dst, sem):
    pltpu.make_async_copy(src_hbm.at[pl.ds(0, count)], dst, sem).wait()


def _expert_kernel(blk_e_ref, tok_ref, h_hbm, wg_ref, wu_ref, wd_ref, y_ref, idx_smem, buf, sem_idx, sem_rows):
    del blk_e_ref
    bm = buf.shape[0]
    idx_copy = pltpu.make_async_copy(tok_ref.at[0, 0], idx_smem, sem_idx)
    idx_copy.start()
    idx_copy.wait()
    _gather_rows(idx_smem, 0, bm, h_hbm, buf, sem_rows)
    _wait_rows(bm, h_hbm, buf, sem_rows)
    x = jnp.concatenate([buf[:, s, :] for s in range(buf.shape[1])], axis=1).astype(BF16)
    gate = _dot(x, wg_ref[0])
    up = _dot(x, wu_ref[0])
    act = (gate * _sigmoid(gate) * up).astype(BF16)
    y = _dot(act, wd_ref[0])
    for s in range(y_ref.shape[1]):
        y_ref[:, s, :] = y[:, s * LANES:(s + 1) * LANES]


def _expert_blocks(blk_e, slot_tok, h_rows, w_gate, w_up, w_down, bm):
    nb = blk_e.shape[0]
    _, chunks, _ = h_rows.shape
    _, d, f = w_gate.shape
    grid_spec = pltpu.PrefetchScalarGridSpec(
        num_scalar_prefetch=1,
        grid=(nb,),
        in_specs=[
            pl.BlockSpec((1, 1, bm), lambda i, e: (i, 0, 0)),
            pl.BlockSpec(memory_space=pl.ANY),
            pl.BlockSpec((1, d, f), lambda i, e: (e[i], 0, 0)),
            pl.BlockSpec((1, d, f), lambda i, e: (e[i], 0, 0)),
            pl.BlockSpec((1, f, d), lambda i, e: (e[i], 0, 0)),
        ],
        out_specs=pl.BlockSpec((bm, chunks, LANES), lambda i, e: (i, 0, 0)),
        scratch_shapes=[
            pltpu.SMEM((bm,), jnp.int32),
            pltpu.VMEM((bm, chunks, LANES), F32),
            pltpu.SemaphoreType.DMA,
            pltpu.SemaphoreType.DMA,
        ],
    )
    return pl.pallas_call(
        _expert_kernel,
        grid_spec=grid_spec,
        out_shape=jax.ShapeDtypeStruct((nb * bm, chunks, LANES), F32),
        compiler_params=_params("arbitrary"),
        name="moe_experts",
    )(blk_e, slot_tok, h_rows, w_gate, w_up, w_down)


def _combine_kernel(dest_ref, route_ref, y_hbm, x_ref, g_ref, o_ref, idx_smem, buf, sem_idx, sem_rows):
    tm = buf.shape[1]
    idx_copy = pltpu.make_async_copy(dest_ref.at[0, 0], idx_smem, sem_idx)
    idx_copy.start()
    idx_copy.wait()
    for k in range(2):
        _gather_rows(idx_smem, k * tm, tm, y_hbm, buf.at[k], sem_rows)
    for k in range(2):
        _wait_rows(tm, y_hbm, buf.at[k], sem_rows)
    route = route_ref[...]
    w0 = route[:, 2:3]
    w1 = route[:, 3:4]
    for s in range(buf.shape[2]):
        c = slice(s * LANES, (s + 1) * LANES)
        y = w0 * buf[0, :, s, :] + w1 * buf[1, :, s, :]
        o_ref[:, c] = x_ref[:, c] + g_ref[0, :, c] * y


def _combine(dest, route, y_rows, x, gate, tm=256):
    b, s, d = x.shape
    t = b * s
    tm = min(tm, s)
    chunks = d // LANES
    per_seq = s // tm
    out = pl.pallas_call(
        _combine_kernel,
        grid=(t // tm,),
        in_specs=[
            pl.BlockSpec((1, 1, 2 * tm), lambda i: (i, 0, 0)),
            pl.BlockSpec((tm, LANES), lambda i: (i, 0)),
            pl.BlockSpec(memory_space=pl.ANY),
            pl.BlockSpec((tm, d), lambda i: (i, 0)),
            pl.BlockSpec((1, 1, d), lambda i: (i // per_seq, 0, 0)),
        ],
        out_specs=pl.BlockSpec((tm, d), lambda i: (i, 0)),
        out_shape=jax.ShapeDtypeStruct((t, d), F32),
        scratch_shapes=[
            pltpu.SMEM((2 * tm,), jnp.int32),
            pltpu.VMEM((2, tm, chunks, LANES), F32),
            pltpu.SemaphoreType.DMA,
            pltpu.SemaphoreType.DMA,
        ],
        compiler_params=_params("arbitrary"),
        name="moe_combine",
    )(dest, route, y_rows, x.reshape(t, d), gate)
    return out.reshape(b, s, d)


def _slot_tables(expert, bm, tm):
    t = expert.shape[0]
    a = 2 * t
    flat = expert.reshape(a)
    onehot = (flat[:, None] == jnp.arange(MOE_EXPERTS, dtype=jnp.int32)[None, :]).astype(jnp.int32)
    running = jnp.cumsum(onehot, axis=0)
    rank = jnp.sum(running * onehot, axis=1) - 1
    counts = running[-1]
    padded = (counts + bm - 1) // bm * bm
    pad_end = jnp.cumsum(padded)
    pad_start = pad_end - padded
    dest = pad_start[flat] + rank
    nb = a // bm + MOE_EXPERTS
    slot_tok = jnp.zeros((nb * bm,), jnp.int32).at[dest].set(jnp.arange(a, dtype=jnp.int32) // 2)
    blk_e = jnp.minimum(jnp.searchsorted(pad_end, jnp.arange(nb, dtype=jnp.int32) * bm, side="right"),
                        MOE_EXPERTS - 1).astype(jnp.int32)
    dest_tiles = dest.reshape(t // tm, tm, 2).transpose(0, 2, 1).reshape(t // tm, 1, 2 * tm)
    return slot_tok.reshape(nb, 1, bm), blk_e, dest_tiles


def _moe_layer(x, g, sc, sh, gate, w_route, b_route, w_gate, w_up, w_down, bm=256, tm=256):
    b, s, d = x.shape
    tm = min(tm, s)
    h_rows, route = _router(x, g, sc, sh, w_route, b_route)
    expert = route[:, :2].astype(jnp.int32)
    slot_tok, blk_e, dest = _slot_tables(expert, bm, tm)
    y_rows = _expert_blocks(blk_e, slot_tok, h_rows, w_gate, w_up, w_down, bm)
    return _combine(dest, route, y_rows, x, gate, tm)


def _final_norm_kernel(x_ref, g_ref, o_ref):
    x = x_ref[0]
    o_ref[0] = x * lax.rsqrt(jnp.mean(x * x, axis=-1, keepdims=True) + RMS_EPS) * g_ref[...]


def _final_norm(x, g, tm=1024):
    b, s, d = x.shape
    tm = min(tm, s)
    return pl.pallas_call(
        _final_norm_kernel,
        grid=(b, s // tm),
        in_specs=[pl.BlockSpec((1, tm, d), lambda bi, i: (bi, i, 0)), pl.BlockSpec((1, d), lambda bi, i: (0, 0))],
        out_specs=pl.BlockSpec((1, tm, d), lambda bi, i: (bi, i, 0)),
        out_shape=jax.ShapeDtypeStruct((b, s, d), F32),
        compiler_params=_params("parallel", "parallel"),
        name="final_norm",
    )(x, g.reshape(1, d))


def kernel(x, c, ada_w, ada_b, norm_g, final_g, sb_w_in, sb_w_out, gla_w_in, gla_w_gate_up, gla_b_gate, gla_norm_g, gla_w_out, conv_w_in, conv_w, conv_b, conv_w_out, dil_w_in, dil_w_out, moe_w_grp, moe_b_grp, moe_w_exp, moe_b_exp, moe_w_gate, moe_w_up, moe_w_down):
    b, s, d = x.shape
    depth = ada_w.shape[0]
    mod = _ada_mod(c, ada_w, ada_b).reshape(depth, b, N_MOD, 1, d)
    for i in range(depth):
        sh1, sc1, g1, sh2, sc2, g2 = (mod[i, :, m] for m in range(N_MOD))
        kind, j = i % 4, i // 4
        if kind == 0:
            qkv = _norm_proj(x, norm_g[i, 0], sc1, sh1, sb_w_in[j].astype(BF16))
            x = _proj_res(_sb_attention(qkv), sb_w_out[j].astype(BF16), x, g1)
        elif kind == 1:
            dk = gla_w_gate_up.shape[-1] // GLA_HEADS
            dv = gla_norm_g.shape[-1]
            w_in = jnp.pad(gla_w_in[j], ((0, 0), (0, LANES - GLA_GATE_RANK))).astype(BF16)
            w_up = jnp.pad(gla_w_gate_up[j], ((0, LANES - GLA_GATE_RANK), (0, 0))).astype(BF16)
            proj = _norm_proj(x, norm_g[i, 0], sc1, sh1, w_in, tn=640)
            o = _gla_core(proj, w_up, gla_b_gate[j].reshape(1, -1), gla_norm_g[j].reshape(1, -1), dk, dv)
            x = _proj_res(o, gla_w_out[j].astype(BF16), x, g1)
        elif kind == 2:
            proj = _norm_proj(x, norm_g[i, 0], sc1, sh1, conv_w_in[j].astype(BF16))
            x = _conv_res(proj, conv_w[j], conv_b[j], conv_w_out[j].astype(BF16), x, g1)
        else:
            proj = _norm_proj(x, norm_g[i, 0], sc1, sh1, dil_w_in[j].astype(BF16))
            parts = [_dil_group(proj, g, dilation) for g, (_, dilation) in enumerate(DIL_PATTERNS)]
            x = _dil_res([p[0] for p in parts], [p[1] for p in parts], dil_w_out[j].astype(BF16), x, g1)
        w_route = jnp.pad(jnp.concatenate([moe_w_grp[i], moe_w_exp[i]], axis=1),
                          ((0, 0), (0, LANES - MOE_GROUPS - MOE_EXPERTS)))
        b_route = jnp.pad(jnp.concatenate([moe_b_grp[i], moe_b_exp[i]]), (0, LANES - MOE_GROUPS - MOE_EXPERTS))
        x = _moe_layer(x, norm_g[i, 1], sc2, sh2, g2, w_route, b_route.reshape(1, LANES),
                       moe_w_gate[i].astype(BF16), moe_w_up[i].astype(BF16), moe_w_down[i].astype(BF16))
    return _final_norm(x, final_g)
```

```python
import functools

import jax
import jax.numpy as jnp
from jax import lax
from jax.experimental import pallas as pl
from jax.experimental.pallas import tpu as pltpu

F32 = jnp.float32
BF16 = jnp.bfloat16

LANES = 128
RMS_EPS = 1e-6
N_MOD = 6

SB_HEADS = 16
GLA_HEADS = 4
GLA_GATE_RANK = 16
GLA_GATE_TAU = 16.0
GLA_CHUNK = 64
CONV_WIDTH = 3
DIL_PATTERNS = ((128, 1), (512, 4), (2048, 16))
DIL_HEADS = 8
DIL_HEAD_DIM = 64
DIL_QBLOCK = 128
DIL_TILE_TOKENS = 2048
MOE_GROUPS = 4
MOE_PER_GROUP = 8
MOE_EXPERTS = MOE_GROUPS * MOE_PER_GROUP
MOE_TOPK = 2

SB_LOG_ZERO = -104.0

VMEM_LIMIT = 48 * 1024 * 1024


def _params(*sem):
    return pltpu.CompilerParams(dimension_semantics=sem, vmem_limit_bytes=VMEM_LIMIT)


def _sigmoid(x):
    return 1.0 / (1.0 + jnp.exp(-x))


def _softplus(x):
    return jnp.maximum(x, 0.0) + jnp.log(1.0 + jnp.exp(-jnp.abs(x)))


def _split_bf16(x):
    hi = x.astype(BF16)
    lo = (x - hi.astype(F32)).astype(BF16)
    return hi, lo


def _dot(a, b):
    return jnp.dot(a, b, preferred_element_type=F32)


def _dot_nt(a, b):
    return lax.dot_general(a, b, (((1,), (1,)), ((), ())), preferred_element_type=F32)


def _dot_tn(a, b):
    return lax.dot_general(a, b, (((0,), (0,)), ((), ())), preferred_element_type=F32)


def _mod_norm(x, g, sc, sh):
    r = lax.rsqrt(jnp.mean(x * x, axis=-1, keepdims=True) + RMS_EPS)
    return (x * r) * g * (1.0 + sc) + sh


def _ada_kernel(c_ref, w_ref, b_ref, o_ref):
    c = c_ref[...]
    cond = c * _sigmoid(c)
    o_ref[0] = _dot(cond.astype(BF16), w_ref[0].astype(BF16)) + b_ref[0]


def _ada_mod(c, ada_w, ada_b):
    depth, d, n = ada_w.shape
    b = c.shape[0]
    tn = 1536
    return pl.pallas_call(
        _ada_kernel,
        grid=(depth, n // tn),
        in_specs=[
            pl.BlockSpec((b, d), lambda i, j: (0, 0)),
            pl.BlockSpec((1, d, tn), lambda i, j: (i, 0, j)),
            pl.BlockSpec((1, 1, tn), lambda i, j: (i, 0, j)),
        ],
        out_specs=pl.BlockSpec((1, b, tn), lambda i, j: (i, 0, j)),
        out_shape=jax.ShapeDtypeStruct((depth, b, n), F32),
        compiler_params=_params("parallel", "parallel"),
        name="ada_mod",
    )(c, ada_w, ada_b.reshape(depth, 1, n))


def _norm_proj_kernel(x_ref, g_ref, sc_ref, sh_ref, w_ref, o_ref, h_ref, *res_ref, dilation):
    @pl.when(pl.program_id(2) == 0)
    def _():
        h_ref[...] = _mod_norm(x_ref[0], g_ref[...], sc_ref[0], sh_ref[0]).astype(BF16)

    res = _dot(h_ref[...], w_ref[...])
    if dilation == 1:
        o_ref[0] = res.astype(o_ref.dtype)
    else:
        (res_ref,) = res_ref
        tm, tn = res.shape
        for c in range(tn // LANES):
            res_ref[c] = res[:, c * LANES:(c + 1) * LANES]
        for rho in range(dilation):
            for c in range(tn // LANES):
                rows = res_ref[c, pl.ds(rho, tm // dilation, stride=dilation), :]
                o_ref[0, :, rho * tn + c * LANES:rho * tn + (c + 1) * LANES] = rows.astype(o_ref.dtype)


def _norm_proj(x, g, sc, sh, w, tm=1024, tn=512, dilation=1):
    b, s, d = x.shape
    n = w.shape[1]
    tm = min(tm, s)
    assert s % tm == 0 and n % tn == 0 and tm % (8 * dilation) == 0
    scratch = [pltpu.VMEM((tm, d), BF16)]
    if dilation > 1:
        scratch.append(pltpu.VMEM((tn // LANES, tm, LANES), F32))
    return pl.pallas_call(
        functools.partial(_norm_proj_kernel, dilation=dilation),
        grid=(b, s // tm, n // tn),
        in_specs=[
            pl.BlockSpec((1, tm, d), lambda bi, i, j: (bi, i, 0)),
            pl.BlockSpec((1, d), lambda bi, i, j: (0, 0)),
            pl.BlockSpec((1, 1, d), lambda bi, i, j: (bi, 0, 0)),
            pl.BlockSpec((1, 1, d), lambda bi, i, j: (bi, 0, 0)),
            pl.BlockSpec((d, tn), lambda bi, i, j: (0, j)),
        ],
        out_specs=pl.BlockSpec((1, tm // dilation, dilation * tn), lambda bi, i, j: (bi, i, j)),
        out_shape=jax.ShapeDtypeStruct((b, s // dilation, dilation * n), BF16),
        scratch_shapes=scratch,
        compiler_params=_params("parallel", "parallel", "arbitrary"),
        name="norm_proj",
    )(x, g.reshape(1, d), sc, sh, w)


def _proj_res_kernel(o_ref, w_ref, x_ref, g_ref, y_ref):
    y_ref[0] = x_ref[0] + g_ref[0] * _dot(o_ref[0], w_ref[...])


def _proj_res(o, w, x, gate, tm=512):
    b, s, d = x.shape
    k = o.shape[-1]
    tm = min(tm, s)
    return pl.pallas_call(
        _proj_res_kernel,
        grid=(b, s // tm),
        in_specs=[
            pl.BlockSpec((1, tm, k), lambda bi, i: (bi, i, 0)),
            pl.BlockSpec((k, d), lambda bi, i: (0, 0)),
            pl.BlockSpec((1, tm, d), lambda bi, i: (bi, i, 0)),
            pl.BlockSpec((1, 1, d), lambda bi, i: (bi, 0, 0)),
        ],
        out_specs=pl.BlockSpec((1, tm, d), lambda bi, i: (bi, i, 0)),
        out_shape=jax.ShapeDtypeStruct((b, s, d), F32),
        compiler_params=_params("parallel", "parallel"),
        name="proj_res",
    )(o, w, x, gate)


def _sb_kernel(q_ref, k_ref, v_ref, o_ref, acc_ref, c_ref, *, blk, hd):
    i = pl.program_id(2)
    scale = hd ** -0.5
    first = lax.broadcasted_iota(jnp.int32, (blk, LANES), 1) < hd
    q = q_ref[0]
    zero = jnp.zeros_like(q)
    q_heads = (jnp.where(first, q, zero), jnp.where(first, zero, q))
    row = lax.broadcasted_iota(jnp.int32, (blk, blk), 0)
    col = lax.broadcasted_iota(jnp.int32, (blk, blk), 1)
    later = jnp.where(row > col, 1.0, 0.0).astype(BF16)
    causal = col < row

    def visit(start, mask):
        kb = k_ref[0, pl.ds(start, blk), :]
        vb = v_ref[0, pl.ds(start, blk), :]
        outs = []
        for h in range(2):
            z = _dot_nt(q_heads[h], kb) * scale
            sp = _softplus(z)
            log_keep = -sp
            if mask is not None:
                log_keep = jnp.where(mask, log_keep, 0.0)
            suffix = _dot(log_keep.astype(BF16), later)
            c = c_ref[h]
            a = jnp.exp((z - sp) + suffix + c)
            if mask is not None:
                a = jnp.where(mask, a, 0.0)
            outs.append(_dot(a.astype(BF16), vb))
            c_ref[h] = c + jnp.sum(log_keep, axis=1, keepdims=True)
        acc_ref[...] += jnp.where(first, outs[0], outs[1])

    acc_ref[...] = jnp.zeros_like(acc_ref)
    c_ref[...] = jnp.zeros_like(c_ref)
    visit(pl.multiple_of(i * blk, blk), causal)

    def cond(carry):
        j, cmax = carry
        return jnp.logical_and(j >= 0, cmax > SB_LOG_ZERO)

    def body(carry):
        j, _ = carry
        visit(pl.multiple_of(j * blk, blk), None)
        return j - 1, jnp.max(c_ref[...])

    lax.while_loop(cond, body, (i - 1, jnp.max(c_ref[...])))
    o_ref[0] = acc_ref[...].astype(o_ref.dtype)


def _sb_attention(qkv, blk=256):
    b, s, n = qkv.shape
    width = n // 3
    hd = width // SB_HEADS
    assert 2 * hd == LANES
    pairs = width // LANES
    blk = min(blk, s)
    return pl.pallas_call(
        functools.partial(_sb_kernel, blk=blk, hd=hd),
        grid=(b, pairs, s // blk),
        in_specs=[
            pl.BlockSpec((1, blk, LANES), lambda bi, p, i: (bi, i, p)),
            pl.BlockSpec((1, s, LANES), lambda bi, p, i: (bi, 0, pairs + p)),
            pl.BlockSpec((1, s, LANES), lambda bi, p, i: (bi, 0, 2 * pairs + p)),
        ],
        out_specs=pl.BlockSpec((1, blk, LANES), lambda bi, p, i: (bi, i, p)),
        out_shape=jax.ShapeDtypeStruct((b, s, width), BF16),
        scratch_shapes=[pltpu.VMEM((blk, LANES), F32), pltpu.VMEM((2, blk, 1), F32)],
        compiler_params=_params("parallel", "parallel", "arbitrary"),
        name="sb_attention",
    )(qkv, qkv, qkv)


def _gla_kernel(p_ref, wg_ref, bg_ref, ng_ref, o_ref, st_ref, *, rows, dk, dv):
    @pl.when(pl.program_id(1) == 0)
    def _():
        st_ref[...] = jnp.zeros_like(st_ref)

    heads = GLA_HEADS
    ch = GLA_CHUNK
    k0 = heads * dk
    v0 = 2 * heads * dk
    r0 = v0 + heads * dv
    g0 = r0 + heads * dv
    row = lax.broadcasted_iota(jnp.int32, (ch, ch), 0)
    col = lax.broadcasted_iota(jnp.int32, (ch, ch), 1)
    causal = col <= row
    lower = jnp.where(causal, 1.0, 0.0).astype(BF16)
    for s in range(rows // ch):
        t = slice(s * ch, (s + 1) * ch)
        g_pre = _dot(p_ref[0, t, g0:g0 + LANES], wg_ref[...]) + bg_ref[...]
        log_a = (jnp.minimum(g_pre, 0.0) - jnp.log(1.0 + jnp.exp(-jnp.abs(g_pre)))) * (1.0 / GLA_GATE_TAU)
        hi, lo = _split_bf16(log_a)
        cum = _dot(lower, hi) + _dot(lower, lo)
        for h in range(heads):
            bh = cum[:, h * dk:(h + 1) * dk]
            b_last = bh[ch - 1:ch, :]
            q = p_ref[0, t, h * dk:(h + 1) * dk].astype(F32) * dk ** -0.5
            k = p_ref[0, t, k0 + h * dk:k0 + (h + 1) * dk].astype(F32)
            v = p_ref[0, t, v0 + h * dv:v0 + (h + 1) * dv]
            r = p_ref[0, t, r0 + h * dv:r0 + (h + 1) * dv].astype(F32)
            q_in = (q * jnp.exp(bh)).astype(BF16)
            k_in = (k * jnp.exp(-bh)).astype(BF16)
            k_state = (k * jnp.exp(b_last - bh)).astype(BF16)
            scores = jnp.where(causal, _dot_nt(q_in, k_in), 0.0)
            state = st_ref[h]
            o = _dot(scores.astype(BF16), v) + _dot_nt(q_in, state.astype(BF16))
            st_ref[h] = jnp.exp(b_last) * state + _dot_tn(v, k_state)
            o = o * lax.rsqrt(jnp.mean(o * o, axis=-1, keepdims=True) + RMS_EPS) * ng_ref[...]
            o_ref[0, t, h * dv:(h + 1) * dv] = (o * (r * _sigmoid(r))).astype(o_ref.dtype)


def _gla_core(proj, w_gate_up, b_gate, norm_g, dk, dv, rows=256):
    b, s, n = proj.shape
    rows = min(rows, s)
    hk = GLA_HEADS * dk
    return pl.pallas_call(
        functools.partial(_gla_kernel, rows=rows, dk=dk, dv=dv),
        grid=(b, s // rows),
        in_specs=[
            pl.BlockSpec((1, rows, n), lambda bi, i: (bi, i, 0)),
            pl.BlockSpec((LANES, hk), lambda bi, i: (0, 0)),
            pl.BlockSpec((1, hk), lambda bi, i: (0, 0)),
            pl.BlockSpec((1, dv), lambda bi, i: (0, 0)),
        ],
        out_specs=pl.BlockSpec((1, rows, GLA_HEADS * dv), lambda bi, i: (bi, i, 0)),
        out_shape=jax.ShapeDtypeStruct((b, s, GLA_HEADS * dv), BF16),
        scratch_shapes=[pltpu.VMEM((GLA_HEADS, dv, dk), F32)],
        compiler_params=_params("parallel", "arbitrary"),
        name="gla_core",
    )(proj, w_gate_up, b_gate, norm_g)


def _conv_res_kernel(gb_ref, gc_ref, u_ref, hgc_ref, hu_ref, cw_ref, cb_ref, w_ref, x_ref, g_ref, y_ref):
    u2 = gc_ref[0].astype(F32) * u_ref[0].astype(F32)
    halo = hgc_ref[0].astype(F32) * hu_ref[0].astype(F32)
    halo = jnp.where(pl.program_id(1) > 0, halo, 0.0)
    rows = lax.broadcasted_iota(jnp.int32, u2.shape, 0)
    prev1 = jnp.where(rows == 0, halo[7:8], pltpu.roll(u2, 1, 0))
    prev2 = jnp.where(rows == 0, halo[6:7], jnp.where(rows == 1, halo[7:8], pltpu.roll(u2, 2, 0)))
    cw = cw_ref[...]
    y = cw[0:1] * prev2 + cw[1:2] * prev1 + cw[2:3] * u2 + cb_ref[...]
    o = (gb_ref[0].astype(F32) * y).astype(BF16)
    y_ref[0] = x_ref[0] + g_ref[0] * _dot(o, w_ref[...])


def _conv_res(proj, conv_w, conv_b, w_out, x, gate, tm=512):
    b, s, d = x.shape
    tm = min(tm, s)
    halo_blocks = tm // 8

    def halo_map(col):
        return lambda bi, i: (bi, jnp.maximum(i * halo_blocks - 1, 0), col)

    return pl.pallas_call(
        _conv_res_kernel,
        grid=(b, s // tm),
        in_specs=[
            pl.BlockSpec((1, tm, d), lambda bi, i: (bi, i, 0)),
            pl.BlockSpec((1, tm, d), lambda bi, i: (bi, i, 1)),
            pl.BlockSpec((1, tm, d), lambda bi, i: (bi, i, 2)),
            pl.BlockSpec((1, 8, d), halo_map(1)),
            pl.BlockSpec((1, 8, d), halo_map(2)),
            pl.BlockSpec((CONV_WIDTH, d), lambda bi, i: (0, 0)),
            pl.BlockSpec((1, d), lambda bi, i: (0, 0)),
            pl.BlockSpec((d, d), lambda bi, i: (0, 0)),
            pl.BlockSpec((1, tm, d), lambda bi, i: (bi, i, 0)),
            pl.BlockSpec((1, 1, d), lambda bi, i: (bi, 0, 0)),
        ],
        out_specs=pl.BlockSpec((1, tm, d), lambda bi, i: (bi, i, 0)),
        out_shape=jax.ShapeDtypeStruct((b, s, d), F32),
        compiler_params=_params("parallel", "parallel"),
        name="conv_res",
    )(proj, proj, proj, proj, proj, conv_w, conv_b.reshape(1, d), w_out, x, gate)


def _dil_kernel(q_ref, kc_ref, vc_ref, kp_ref, vp_ref, o_ref, l_ref, *, rows, hd, dilation):
    qb = DIL_QBLOCK
    n = pl.program_id(1)
    rho = pl.program_id(2)
    scale = hd ** -0.5
    first = lax.broadcasted_iota(jnp.int32, (qb, LANES), 1) < hd
    row = lax.broadcasted_iota(jnp.int32, (qb, qb), 0)
    col = lax.broadcasted_iota(jnp.int32, (qb, qb), 1)
    in_prev = col >= row
    in_cur = col <= row
    neg = -jnp.inf
    for s in range(rows // qb):
        t = slice(s * qb, (s + 1) * qb)
        if dilation == 1:
            t_out = t
        else:
            t_out = pl.ds(s * qb * dilation + rho, qb, stride=dilation)
        if s == 0:
            prev_mask = jnp.logical_and(in_prev, n > 0)
        else:
            prev_mask = in_prev
            tp = slice((s - 1) * qb, s * qb)
        for p in range(q_ref.shape[-1] // LANES):
            c = slice(p * LANES, (p + 1) * LANES)
            q = q_ref[0, t, c]
            zero = jnp.zeros_like(q)
            k_cur, v_cur = kc_ref[0, t, c], vc_ref[0, t, c]
            if s == 0:
                k_prev, v_prev = kp_ref[0, :, c], vp_ref[0, :, c]
            else:
                k_prev, v_prev = kc_ref[0, tp, c], vc_ref[0, tp, c]
            outs, lses = [], []
            for h in range(2):
                qh = jnp.where(first, q, zero) if h == 0 else jnp.where(first, zero, q)
                z_prev = jnp.where(prev_mask, _dot_nt(qh, k_prev) * scale, neg)
                z_cur = jnp.where(in_cur, _dot_nt(qh, k_cur) * scale, neg)
                m = jnp.maximum(jnp.max(z_prev, axis=1, keepdims=True), jnp.max(z_cur, axis=1, keepdims=True))
                p_prev = jnp.exp(z_prev - m)
                p_cur = jnp.exp(z_cur - m)
                den = jnp.sum(p_prev, axis=1, keepdims=True) + jnp.sum(p_cur, axis=1, keepdims=True)
                num = _dot(p_prev.astype(BF16), v_prev) + _dot(p_cur.astype(BF16), v_cur)
                outs.append(num / den)
                lses.append(jnp.broadcast_to(m + jnp.log(den), (qb, LANES)))
            o_ref[0, p, t_out, :] = jnp.where(first, outs[0], outs[1])
            l_ref[0, p, t_out, :] = jnp.where(first, lses[0], lses[1])


def _dil_group(view, dilation, window, seq):
    b, length, _ = view.shape
    width = DIL_HEADS * DIL_HEAD_DIM
    assert window == dilation * DIL_QBLOCK and seq % (dilation * DIL_QBLOCK) == 0
    rows = min(max(DIL_TILE_TOKENS // dilation, DIL_QBLOCK), 512, length)
    prev_blocks = rows // DIL_QBLOCK

    def cur(part):
        return lambda bi, i, rho: (bi, i, part * dilation + rho)

    def prev(part):
        return lambda bi, i, rho: (bi, jnp.maximum(i * prev_blocks - 1, 0), part * dilation + rho)

    pairs = width // LANES
    out_shape = jax.ShapeDtypeStruct((b, pairs, seq, LANES), F32)
    out_spec = pl.BlockSpec((1, pairs, rows * dilation, LANES), lambda bi, i, rho: (bi, 0, i, 0))
    return pl.pallas_call(
        functools.partial(_dil_kernel, rows=rows, hd=DIL_HEAD_DIM, dilation=dilation),
        grid=(b, length // rows, dilation),
        in_specs=[
            pl.BlockSpec((1, rows, width), cur(0)),
            pl.BlockSpec((1, rows, width), cur(1)),
            pl.BlockSpec((1, rows, width), cur(2)),
            pl.BlockSpec((1, DIL_QBLOCK, width), prev(1)),
            pl.BlockSpec((1, DIL_QBLOCK, width), prev(2)),
        ],
        out_specs=[out_spec, out_spec],
        out_shape=[out_shape, out_shape],
        compiler_params=_params("parallel", "parallel", "arbitrary"),
        name=f"dil_attn_r{dilation}",
    )(view, view, view, view, view)


def _dil_res_kernel(o0, o1, o2, l0, l1, l2, w_ref, x_ref, g_ref, y_ref):
    parts = []
    for p in range(o0.shape[1]):
        ls = (l0[0, p], l1[0, p], l2[0, p])
        m = jnp.maximum(jnp.maximum(ls[0], ls[1]), ls[2])
        es = [jnp.exp(l - m) for l in ls]
        parts.append((es[0] * o0[0, p] + es[1] * o1[0, p] + es[2] * o2[0, p]) / (es[0] + es[1] + es[2]))
    o = jnp.concatenate(parts, axis=1)
    y_ref[0] = x_ref[0] + g_ref[0] * _dot(o.astype(BF16), w_ref[...])


def _dil_res(outs, lses, w_out, x, gate, tm=512):
    b, s, d = x.shape
    k = w_out.shape[0]
    tm = min(tm, s)
    part = pl.BlockSpec((1, k // LANES, tm, LANES), lambda bi, i: (bi, 0, i, 0))
    return pl.pallas_call(
        _dil_res_kernel,
        grid=(b, s // tm),
        in_specs=[part] * 6 + [
            pl.BlockSpec((k, d), lambda bi, i: (0, 0)),
            pl.BlockSpec((1, tm, d), lambda bi, i: (bi, i, 0)),
            pl.BlockSpec((1, 1, d), lambda bi, i: (bi, 0, 0)),
        ],
        out_specs=pl.BlockSpec((1, tm, d), lambda bi, i: (bi, i, 0)),
        out_shape=jax.ShapeDtypeStruct((b, s, d), F32),
        compiler_params=_params("parallel", "parallel"),
        name="dil_res",
    )(*outs, *lses, w_out, x, gate)


def _router_kernel(x_ref, g_ref, sc_ref, sh_ref, wr_ref, br_ref, h_ref, route_ref, cnt_ref, run_ref):
    @pl.when(jnp.logical_and(pl.program_id(0) == 0, pl.program_id(1) == 0))
    def _():
        run_ref[...] = jnp.zeros_like(run_ref)

    h = _mod_norm(x_ref[0], g_ref[...], sc_ref[0], sh_ref[0])
    for s in range(h.shape[1] // LANES):
        h_ref[:, s, :] = h[:, s * LANES:(s + 1) * LANES]
    h_hi, h_lo = _split_bf16(h)
    w_hi, w_lo = _split_bf16(wr_ref[...])
    logits = _dot(h_hi, w_hi) + (_dot(h_hi, w_lo) + _dot(h_lo, w_hi)) + br_ref[...]
    tm = logits.shape[0]
    lane = lax.broadcasted_iota(jnp.int32, logits.shape, 1).astype(F32)
    neg = -jnp.inf
    far = float(LANES)

    def first_argmax(vals):
        top = jnp.max(vals, axis=1, keepdims=True)
        return top, jnp.min(jnp.where(vals == top, lane, far), axis=1, keepdims=True)

    grp_logits = jnp.where(lane < MOE_GROUPS, logits, neg)
    grp_max, grp = first_argmax(grp_logits)
    p_grp = 1.0 / jnp.sum(jnp.exp(grp_logits - grp_max), axis=1, keepdims=True)
    base = MOE_GROUPS + grp * MOE_PER_GROUP
    in_grp = jnp.logical_and(lane >= base, lane < base + MOE_PER_GROUP)
    exp_logits = jnp.where(in_grp, logits, neg)
    m1, i1 = first_argmax(exp_logits)
    m2, i2 = first_argmax(jnp.where(lane == i1, neg, exp_logits))
    e2 = jnp.exp(m2 - m1)
    g1 = p_grp / (1.0 + e2)
    g2 = g1 * e2
    e_a = i1 - MOE_GROUPS
    e_b = i2 - MOE_GROUPS

    pick_a = lane == e_a
    pick_b = lane == e_b
    both = jnp.where(jnp.logical_or(pick_a, pick_b), 1.0, 0.0)
    row = lax.broadcasted_iota(jnp.int32, (tm, tm), 0)
    col = lax.broadcasted_iota(jnp.int32, (tm, tm), 1)
    earlier = jnp.where(col < row, 1.0, 0.0).astype(BF16)
    before = _dot(earlier, both.astype(BF16)) + run_ref[...]
    rank_a = jnp.sum(jnp.where(pick_a, before, 0.0), axis=1, keepdims=True)
    rank_b = jnp.sum(jnp.where(pick_b, before, 0.0), axis=1, keepdims=True)
    run_ref[...] += jnp.sum(both, axis=0, keepdims=True)
    cnt_ref[...] = run_ref[...]

    route = jnp.where(lane == 0, e_a, jnp.where(lane == 1, e_b, 0.0))
    route = jnp.where(lane == 2, g1, jnp.where(lane == 3, g2, route))
    route_ref[...] = jnp.where(lane == 4, rank_a, jnp.where(lane == 5, rank_b, route))


def _router(x, g, sc, sh, w_route, b_route, tm=512):
    b, s, d = x.shape
    tm = min(tm, s)
    steps = s // tm
    chunks = d // LANES
    return pl.pallas_call(
        _router_kernel,
        grid=(b, steps),
        in_specs=[
            pl.BlockSpec((1, tm, d), lambda bi, i: (bi, i, 0)),
            pl.BlockSpec((1, d), lambda bi, i: (0, 0)),
            pl.BlockSpec((1, 1, d), lambda bi, i: (bi, 0, 0)),
            pl.BlockSpec((1, 1, d), lambda bi, i: (bi, 0, 0)),
            pl.BlockSpec((d, LANES), lambda bi, i: (0, 0)),
            pl.BlockSpec((1, LANES), lambda bi, i: (0, 0)),
        ],
        out_specs=[
            pl.BlockSpec((tm, chunks, LANES), lambda bi, i: (bi * steps + i, 0, 0)),
            pl.BlockSpec((tm, LANES), lambda bi, i: (bi * steps + i, 0)),
            pl.BlockSpec((1, LANES), lambda bi, i: (0, 0)),
        ],
        out_shape=[
            jax.ShapeDtypeStruct((b * s, chunks, LANES), F32),
            jax.ShapeDtypeStruct((b * s, LANES), F32),
            jax.ShapeDtypeStruct((1, LANES), F32),
        ],
        scratch_shapes=[pltpu.VMEM((1, LANES), F32)],
        compiler_params=_params("arbitrary", "arbitrary"),
        name="moe_router",
    )(x, g.reshape(1, d), sc, sh, w_route, b_route)


def _copy_idx(src_ref, idx_smem, sem):
    cp = pltpu.make_async_copy(src_ref.at[0, 0], idx_smem, sem)
    cp.start()
    cp.wait()


def _dispatch_kernel(tail_ref, used_ref, dest_ref, h_ref, slots, idx_smem, zeros, sem_idx, sem_rows, sem_zero, *, nb):
    tm = h_ref.shape[0]
    bm = zeros.shape[0]

    @pl.when(pl.program_id(0) == 0)
    def _():
        zeros[...] = jnp.zeros_like(zeros)

        def clear(blk):
            return pltpu.make_async_copy(zeros, slots.at[pl.ds(blk * bm, bm)], sem_zero)

        def each_tail(fn):
            def body(e, carry):
                @pl.when(tail_ref[e] >= 0)
                def _():
                    fn(clear(tail_ref[e]))
                return carry
            lax.fori_loop(0, tail_ref.shape[0], body, 0)

        def each_unused(fn):
            def body(blk, carry):
                fn(clear(blk))
                return carry
            lax.fori_loop(used_ref[0], nb, body, 0)

        each_tail(lambda cp: cp.start())
        each_unused(lambda cp: cp.start())
        each_tail(lambda cp: cp.wait())
        each_unused(lambda cp: cp.wait())

    _copy_idx(dest_ref, idx_smem, sem_idx)

    def issue(r, carry):
        for k in range(MOE_TOPK):
            pltpu.make_async_copy(h_ref.at[r], slots.at[idx_smem[k * tm + r]], sem_rows).start()
        return carry

    lax.fori_loop(0, tm, issue, 0, unroll=8)
    for _ in range(MOE_TOPK):
        pltpu.make_async_copy(h_ref, slots.at[pl.ds(0, tm)], sem_rows).wait()


def _dispatch(tail_blk, used, dest, h_rows, nb, bm, tm):
    t, chunks, _ = h_rows.shape
    grid_spec = pltpu.PrefetchScalarGridSpec(
        num_scalar_prefetch=2,
        grid=(t // tm,),
        in_specs=[
            pl.BlockSpec((1, 1, MOE_TOPK * tm), lambda i, tail, n: (i, 0, 0)),
            pl.BlockSpec((tm, chunks, LANES), lambda i, tail, n: (i, 0, 0)),
        ],
        out_specs=pl.BlockSpec(memory_space=pl.ANY),
        scratch_shapes=[
            pltpu.SMEM((MOE_TOPK * tm,), jnp.int32),
            pltpu.VMEM((bm, chunks, LANES), F32),
            pltpu.SemaphoreType.DMA,
            pltpu.SemaphoreType.DMA,
            pltpu.SemaphoreType.DMA,
        ],
    )
    return pl.pallas_call(
        functools.partial(_dispatch_kernel, nb=nb),
        grid_spec=grid_spec,
        out_shape=jax.ShapeDtypeStruct((nb * bm, chunks, LANES), F32),
        compiler_params=_params("arbitrary"),
        name="moe_dispatch",
    )(tail_blk, used, dest, h_rows)


def _expert_kernel(blk_e_ref, used_ref, x_ref, wg_ref, wu_ref, wd_ref, y_ref):
    del blk_e_ref
    live = pl.program_id(0) < used_ref[0]

    @pl.when(live)
    def _():
        x = jnp.concatenate([x_ref[:, s, :] for s in range(x_ref.shape[1])], axis=1).astype(BF16)
        gate = _dot(x, wg_ref[0])
        up = _dot(x, wu_ref[0])
        act = (gate * _sigmoid(gate) * up).astype(BF16)
        y = _dot(act, wd_ref[0])
        for s in range(y_ref.shape[1]):
            y_ref[:, s, :] = y[:, s * LANES:(s + 1) * LANES]

    @pl.when(jnp.logical_not(live))
    def _():
        y_ref[...] = jnp.zeros_like(y_ref)


def _expert_blocks(blk_e, used, slots, w_gate, w_up, w_down, bm):
    nb = blk_e.shape[0]
    _, chunks, _ = slots.shape
    _, d, f = w_gate.shape

    def block(i, e, n):
        return (jnp.minimum(i, n[0] - 1), 0, 0)

    def weight(i, e, n):
        return (e[jnp.minimum(i, n[0] - 1)], 0, 0)

    grid_spec = pltpu.PrefetchScalarGridSpec(
        num_scalar_prefetch=2,
        grid=(nb,),
        in_specs=[
            pl.BlockSpec((bm, chunks, LANES), block),
            pl.BlockSpec((1, d, f), weight),
            pl.BlockSpec((1, d, f), weight),
            pl.BlockSpec((1, f, d), weight),
        ],
        out_specs=pl.BlockSpec((bm, chunks, LANES), lambda i, e, n: (i, 0, 0)),
    )
    return pl.pallas_call(
        _expert_kernel,
        grid_spec=grid_spec,
        out_shape=jax.ShapeDtypeStruct((nb * bm, chunks, LANES), F32),
        compiler_params=_params("arbitrary"),
        name="moe_experts",
    )(blk_e, used, slots, w_gate, w_up, w_down)


def _gather_rows(idx_smem, offset, count, src_hbm, dst, sem):
    def issue(r, carry):
        pltpu.make_async_copy(src_hbm.at[idx_smem[offset + r]], dst.at[r], sem).start()
        return carry

    lax.fori_loop(0, count, issue, 0, unroll=8)


def _wait_rows(count, src_hbm, dst, sem):
    pltpu.make_async_copy(src_hbm.at[pl.ds(0, count)], dst, sem).wait()


def _combine_kernel(dest_cur, dest_nxt, route_ref, y_hbm, x_ref, g_ref, fg_ref, o_ref,
                    idx_even, idx_odd, buf, sem_idx, sem_rows, *, final):
    tm = buf.shape[2]
    i = pl.program_id(0)
    idx_smem = (idx_even, idx_odd)

    def fetch(dest_ref, sl):
        _copy_idx(dest_ref, idx_smem[sl], sem_idx)
        for k in range(MOE_TOPK):
            _gather_rows(idx_smem[sl], k * tm, tm, y_hbm, buf.at[sl, k], sem_rows.at[sl])

    def run(slot):
        if slot == 0:
            @pl.when(i == 0)
            def _():
                fetch(dest_cur, 0)

        @pl.when(i + 1 < pl.num_programs(0))
        def _():
            fetch(dest_nxt, 1 - slot)

        for k in range(MOE_TOPK):
            _wait_rows(tm, y_hbm, buf.at[slot, k], sem_rows.at[slot])
        route = route_ref[...]
        w0 = route[:, 2:3]
        w1 = route[:, 3:4]
        y = jnp.concatenate(
            [w0 * buf[slot, 0, :, s, :] + w1 * buf[slot, 1, :, s, :] for s in range(buf.shape[3])], axis=1)
        out = x_ref[...] + g_ref[0] * y
        if final:
            out = out * lax.rsqrt(jnp.mean(out * out, axis=-1, keepdims=True) + RMS_EPS) * fg_ref[...]
        o_ref[...] = out

    for slot in range(2):
        pl.when(lax.rem(i, 2) == slot)(functools.partial(run, slot))


def _combine(dest, route, y_rows, x, gate, final_g, final, tm):
    b, s, d = x.shape
    t = b * s
    chunks = d // LANES
    per_seq = s // tm
    steps = t // tm
    out = pl.pallas_call(
        functools.partial(_combine_kernel, final=final),
        grid=(steps,),
        in_specs=[
            pl.BlockSpec((1, 1, MOE_TOPK * tm), lambda i: (i, 0, 0)),
            pl.BlockSpec((1, 1, MOE_TOPK * tm), lambda i: (jnp.minimum(i + 1, steps - 1), 0, 0)),
            pl.BlockSpec((tm, LANES), lambda i: (i, 0)),
            pl.BlockSpec(memory_space=pl.ANY),
            pl.BlockSpec((tm, d), lambda i: (i, 0)),
            pl.BlockSpec((1, 1, d), lambda i: (i // per_seq, 0, 0)),
            pl.BlockSpec((1, d), lambda i: (0, 0)),
        ],
        out_specs=pl.BlockSpec((tm, d), lambda i: (i, 0)),
        out_shape=jax.ShapeDtypeStruct((t, d), F32),
        scratch_shapes=[
            pltpu.SMEM((MOE_TOPK * tm,), jnp.int32),
            pltpu.SMEM((MOE_TOPK * tm,), jnp.int32),
            pltpu.VMEM((2, MOE_TOPK, tm, chunks, LANES), F32),
            pltpu.SemaphoreType.DMA,
            pltpu.SemaphoreType.DMA((2,)),
        ],
        compiler_params=_params("arbitrary"),
        name="moe_combine",
    )(dest, dest, route, y_rows, x.reshape(t, d), gate, final_g.reshape(1, d))
    return out.reshape(b, s, d)


def _slot_tables(route, counts, bm, tm):
    t = route.shape[0]
    nb = MOE_TOPK * t // bm + MOE_EXPERTS
    counts = counts[0, :MOE_EXPERTS].astype(jnp.int32)
    padded = (counts + bm - 1) // bm * bm
    pad_end = jnp.cumsum(padded)
    pad_start = pad_end - padded
    expert = route[:, 0:MOE_TOPK].astype(jnp.int32)
    rank = route[:, 4:4 + MOE_TOPK].astype(jnp.int32)
    dest = pad_start[expert] + rank
    dest_tiles = dest.reshape(t // tm, tm, MOE_TOPK).transpose(0, 2, 1).reshape(t // tm, 1, MOE_TOPK * tm)
    blk_e = jnp.minimum(jnp.searchsorted(pad_end, jnp.arange(nb, dtype=jnp.int32) * bm, side="right"),
                        MOE_EXPERTS - 1).astype(jnp.int32)
    used = (pad_end[-1:] // bm).astype(jnp.int32)
    tail_blk = jnp.where(counts % bm != 0, pad_end // bm - 1, -1).astype(jnp.int32)
    return dest_tiles, blk_e, used, tail_blk, nb


def _moe_layer(x, g, sc, sh, gate, w_route, b_route, w_gate, w_up, w_down, final_g, final, bm=256, tm=256):
    b, s, d = x.shape
    tm = min(tm, s)
    h_rows, route, counts = _router(x, g, sc, sh, w_route, b_route)
    dest, blk_e, used, tail_blk, nb = _slot_tables(route, counts, bm, tm)
    slots = _dispatch(tail_blk, used, dest, h_rows, nb, bm, tm)
    y_rows = _expert_blocks(blk_e, used, slots, w_gate, w_up, w_down, bm)
    return _combine(dest, route, y_rows, x, gate, final_g, final, tm)


def kernel(x, c, ada_w, ada_b, norm_g, final_g, sb_w_in, sb_w_out, gla_w_in, gla_w_gate_up, gla_b_gate, gla_norm_g, gla_w_out, conv_w_in, conv_w, conv_b, conv_w_out, dil_w_in, dil_w_out, moe_w_grp, moe_b_grp, moe_w_exp, moe_b_exp, moe_w_gate, moe_w_up, moe_w_down):
    b, s, d = x.shape
    depth = ada_w.shape[0]
    mod = _ada_mod(c, ada_w, ada_b).reshape(depth, b, N_MOD, 1, d)
    for i in range(depth):
        sh1, sc1, g1, sh2, sc2, g2 = (mod[i, :, m] for m in range(N_MOD))
        kind, j = i % 4, i // 4
        if kind == 0:
            qkv = _norm_proj(x, norm_g[i, 0], sc1, sh1, sb_w_in[j].astype(BF16))
            x = _proj_res(_sb_attention(qkv), sb_w_out[j].astype(BF16), x, g1)
        elif kind == 1:
            dk = gla_w_gate_up.shape[-1] // GLA_HEADS
            dv = gla_norm_g.shape[-1]
            w_in = jnp.pad(gla_w_in[j], ((0, 0), (0, LANES - GLA_GATE_RANK))).astype(BF16)
            w_up = jnp.pad(gla_w_gate_up[j], ((0, LANES - GLA_GATE_RANK), (0, 0))).astype(BF16)
            proj = _norm_proj(x, norm_g[i, 0], sc1, sh1, w_in, tn=640)
            o = _gla_core(proj, w_up, gla_b_gate[j].reshape(1, -1), gla_norm_g[j].reshape(1, -1), dk, dv)
            x = _proj_res(o, gla_w_out[j].astype(BF16), x, g1)
        elif kind == 2:
            proj = _norm_proj(x, norm_g[i, 0], sc1, sh1, conv_w_in[j].astype(BF16))
            x = _conv_res(proj, conv_w[j], conv_b[j], conv_w_out[j].astype(BF16), x, g1)
        else:
            group_w = 3 * DIL_HEADS * DIL_HEAD_DIM
            outs, lses = [], []
            for g, (window, dilation) in enumerate(DIL_PATTERNS):
                w_g = dil_w_in[j][:, g * group_w:(g + 1) * group_w].astype(BF16)
                view = _norm_proj(x, norm_g[i, 0], sc1, sh1, w_g, dilation=dilation)
                o_g, l_g = _dil_group(view, dilation, window, s)
                outs.append(o_g)
                lses.append(l_g)
            x = _dil_res(outs, lses, dil_w_out[j].astype(BF16), x, g1)
        w_route = jnp.pad(jnp.concatenate([moe_w_grp[i], moe_w_exp[i]], axis=1),
                          ((0, 0), (0, LANES - MOE_GROUPS - MOE_EXPERTS)))
        b_route = jnp.pad(jnp.concatenate([moe_b_grp[i], moe_b_exp[i]]), (0, LANES - MOE_GROUPS - MOE_EXPERTS))
        x = _moe_layer(x, norm_g[i, 1], sc2, sh2, g2, w_route, b_route.reshape(1, LANES),
                       moe_w_gate[i].astype(BF16), moe_w_up[i].astype(BF16), moe_w_down[i].astype(BF16),
                       final_g, final=(i == depth - 1))
    return x
```

```python
import functools

import jax
import jax.numpy as jnp
from jax import lax
from jax.experimental import pallas as pl
from jax.experimental.pallas import tpu as pltpu

F32 = jnp.float32
BF16 = jnp.bfloat16

LANES = 128
RMS_EPS = 1e-6
N_MOD = 6

SB_HEADS = 16
GLA_HEADS = 4
GLA_GATE_RANK = 16
GLA_GATE_TAU = 16.0
GLA_CHUNK = 64
CONV_WIDTH = 3
DIL_PATTERNS = ((128, 1), (512, 4), (2048, 16))
DIL_HEADS = 8
DIL_HEAD_DIM = 64
DIL_QBLOCK = 128
DIL_TILE_TOKENS = 2048
MOE_GROUPS = 4
MOE_PER_GROUP = 8
MOE_EXPERTS = MOE_GROUPS * MOE_PER_GROUP
MOE_TOPK = 2

SB_LOG_ZERO = -104.0

VMEM_LIMIT = 48 * 1024 * 1024


def _params(*sem):
    return pltpu.CompilerParams(dimension_semantics=sem, vmem_limit_bytes=VMEM_LIMIT)


def _sigmoid(x):
    return 1.0 / (1.0 + jnp.exp(-x))


def _softplus(x):
    return jnp.maximum(x, 0.0) + jnp.log(1.0 + jnp.exp(-jnp.abs(x)))


def _split_bf16(x):
    hi = x.astype(BF16)
    lo = (x - hi.astype(F32)).astype(BF16)
    return hi, lo


def _dot(a, b):
    return jnp.dot(a, b, preferred_element_type=F32)


def _dot_nt(a, b):
    return lax.dot_general(a, b, (((1,), (1,)), ((), ())), preferred_element_type=F32)


def _dot_tn(a, b):
    return lax.dot_general(a, b, (((0,), (0,)), ((), ())), preferred_element_type=F32)


def _mod_norm(x, g, sc, sh):
    r = lax.rsqrt(jnp.mean(x * x, axis=-1, keepdims=True) + RMS_EPS)
    return (x * r) * g * (1.0 + sc) + sh


def _ada_kernel(c_ref, w_ref, b_ref, o_ref):
    c = c_ref[...]
    cond = c * _sigmoid(c)
    o_ref[0] = _dot(cond.astype(BF16), w_ref[0].astype(BF16)) + b_ref[0]


def _ada_mod(c, ada_w, ada_b):
    depth, d, n = ada_w.shape
    b = c.shape[0]
    tn = 1536
    return pl.pallas_call(
        _ada_kernel,
        grid=(depth, n // tn),
        in_specs=[
            pl.BlockSpec((b, d), lambda i, j: (0, 0)),
            pl.BlockSpec((1, d, tn), lambda i, j: (i, 0, j)),
            pl.BlockSpec((1, 1, tn), lambda i, j: (i, 0, j)),
        ],
        out_specs=pl.BlockSpec((1, b, tn), lambda i, j: (i, 0, j)),
        out_shape=jax.ShapeDtypeStruct((depth, b, n), F32),
        compiler_params=_params("parallel", "parallel"),
        name="ada_mod",
    )(c, ada_w, ada_b.reshape(depth, 1, n))


def _norm_proj_kernel(x_ref, g_ref, sc_ref, sh_ref, w_ref, o_ref, h_ref, *res_ref, dilation, tn):
    h_ref[...] = _mod_norm(x_ref[0], g_ref[...], sc_ref[0], sh_ref[0]).astype(BF16)
    tm = h_ref.shape[0]
    for j in range(w_ref.shape[1] // tn):
        res = _dot(h_ref[...], w_ref[:, j * tn:(j + 1) * tn])
        if dilation == 1:
            o_ref[0, :, j * tn:(j + 1) * tn] = res.astype(o_ref.dtype)
            continue
        for c in range(tn // LANES):
            res_ref[0][j * (tn // LANES) + c] = res[:, c * LANES:(c + 1) * LANES]
        for rho in range(dilation):
            for c in range(tn // LANES):
                rows = res_ref[0][j * (tn // LANES) + c, pl.ds(rho, tm // dilation, stride=dilation), :]
                col = (j * dilation + rho) * tn + c * LANES
                o_ref[0, :, col:col + LANES] = rows.astype(o_ref.dtype)


def _norm_proj(x, g, sc, sh, w, tm=512, tn=512, dilation=1):
    b, s, d = x.shape
    n = w.shape[1]
    tm = min(tm, s)
    assert s % tm == 0 and n % tn == 0 and tm % (16 * dilation) == 0
    scratch = [pltpu.VMEM((tm, d), BF16)]
    if dilation > 1:
        scratch.append(pltpu.VMEM((n // LANES, tm, LANES), F32))
    return pl.pallas_call(
        functools.partial(_norm_proj_kernel, dilation=dilation, tn=tn),
        grid=(b, s // tm),
        in_specs=[
            pl.BlockSpec((1, tm, d), lambda bi, i: (bi, i, 0)),
            pl.BlockSpec((1, d), lambda bi, i: (0, 0)),
            pl.BlockSpec((1, 1, d), lambda bi, i: (bi, 0, 0)),
            pl.BlockSpec((1, 1, d), lambda bi, i: (bi, 0, 0)),
            pl.BlockSpec((d, n), lambda bi, i: (0, 0)),
        ],
        out_specs=pl.BlockSpec((1, tm // dilation, dilation * n), lambda bi, i: (bi, i, 0)),
        out_shape=jax.ShapeDtypeStruct((b, s // dilation, dilation * n), BF16),
        scratch_shapes=scratch,
        compiler_params=_params("parallel", "parallel"),
        name="norm_proj",
    )(x, g.reshape(1, d), sc, sh, w)


def _proj_res_kernel(o_ref, w_ref, x_ref, g_ref, y_ref):
    y_ref[0] = x_ref[0] + g_ref[0] * _dot(o_ref[0], w_ref[...])


def _proj_res(o, w, x, gate, tm=512):
    b, s, d = x.shape
    k = o.shape[-1]
    tm = min(tm, s)
    return pl.pallas_call(
        _proj_res_kernel,
        grid=(b, s // tm),
        in_specs=[
            pl.BlockSpec((1, tm, k), lambda bi, i: (bi, i, 0)),
            pl.BlockSpec((k, d), lambda bi, i: (0, 0)),
            pl.BlockSpec((1, tm, d), lambda bi, i: (bi, i, 0)),
            pl.BlockSpec((1, 1, d), lambda bi, i: (bi, 0, 0)),
        ],
        out_specs=pl.BlockSpec((1, tm, d), lambda bi, i: (bi, i, 0)),
        out_shape=jax.ShapeDtypeStruct((b, s, d), F32),
        compiler_params=_params("parallel", "parallel"),
        name="proj_res",
    )(o, w, x, gate)


def _sb_kernel(q_ref, k_ref, v_ref, o_ref, acc_ref, c_ref, *, blk, hd):
    scale = hd ** -0.5
    fold_scale = (hd & (hd - 1)) == 0 and (hd.bit_length() - 1) % 2 == 0
    first = lax.broadcasted_iota(jnp.int32, (blk, LANES), 1) < hd
    row = lax.broadcasted_iota(jnp.int32, (blk, blk), 0)
    col = lax.broadcasted_iota(jnp.int32, (blk, blk), 1)
    minus_later = jnp.where(row > col, -1.0, 0.0).astype(BF16)
    causal = col < row
    subs = q_ref.shape[1] // blk
    for sub in range(subs):
        i = pl.program_id(2) * subs + sub
        t = slice(sub * blk, (sub + 1) * blk)
        q = q_ref[0, t, :]
        if fold_scale:
            q = (q.astype(F32) * scale).astype(BF16)
        zero = jnp.zeros_like(q)
        q_heads = (jnp.where(first, q, zero), jnp.where(first, zero, q))

        def visit(start, mask, q_heads=q_heads):
            kb = k_ref[0, pl.ds(start, blk), :]
            vb = v_ref[0, pl.ds(start, blk), :]
            outs = []
            for h in range(2):
                z = _dot_nt(q_heads[h], kb)
                if not fold_scale:
                    z = z * scale
                sp = _softplus(z)
                log_beta = z - sp
                if mask is not None:
                    sp = jnp.where(mask, sp, 0.0)
                suffix = _dot(sp.astype(BF16), minus_later)
                c = c_ref[h]
                a = jnp.exp(log_beta + suffix + c)
                if mask is not None:
                    a = jnp.where(mask, a, 0.0)
                outs.append(_dot(a.astype(BF16), vb))
                c_ref[h] = c - jnp.sum(sp, axis=1, keepdims=True)
            acc_ref[...] += jnp.where(first, outs[0], outs[1])

        acc_ref[...] = jnp.zeros_like(acc_ref)
        c_ref[...] = jnp.zeros_like(c_ref)
        visit(pl.multiple_of(i * blk, blk), causal)

        def cond(carry):
            j, cmax = carry
            return jnp.logical_and(j >= 0, cmax > SB_LOG_ZERO)

        def body(carry, visit=visit):
            j, _ = carry
            visit(pl.multiple_of(j * blk, blk), None)
            return j - 1, jnp.max(c_ref[...])

        lax.while_loop(cond, body, (i - 1, jnp.max(c_ref[...])))
        o_ref[0, t, :] = acc_ref[...].astype(o_ref.dtype)


def _sb_attention(qkv, blk=128, rows=512):
    b, s, n = qkv.shape
    width = n // 3
    hd = width // SB_HEADS
    assert 2 * hd == LANES
    pairs = width // LANES
    rows = min(rows, s)
    return pl.pallas_call(
        functools.partial(_sb_kernel, blk=blk, hd=hd),
        grid=(b, pairs, s // rows),
        in_specs=[
            pl.BlockSpec((1, rows, LANES), lambda bi, p, i: (bi, i, p)),
            pl.BlockSpec((1, s, LANES), lambda bi, p, i: (bi, 0, pairs + p)),
            pl.BlockSpec((1, s, LANES), lambda bi, p, i: (bi, 0, 2 * pairs + p)),
        ],
        out_specs=pl.BlockSpec((1, rows, LANES), lambda bi, p, i: (bi, i, p)),
        out_shape=jax.ShapeDtypeStruct((b, s, width), BF16),
        scratch_shapes=[pltpu.VMEM((blk, LANES), F32), pltpu.VMEM((2, blk, 1), F32)],
        compiler_params=_params("parallel", "parallel", "arbitrary"),
        name="sb_attention",
    )(qkv, qkv, qkv)


def _gla_kernel(p_ref, wg_ref, bg_ref, ng_ref, o_ref, st_ref, *, rows, dk, dv):
    @pl.when(pl.program_id(1) == 0)
    def _():
        st_ref[...] = jnp.zeros_like(st_ref)

    heads = GLA_HEADS
    ch = GLA_CHUNK
    k0 = heads * dk
    v0 = 2 * heads * dk
    r0 = v0 + heads * dv
    g0 = r0 + heads * dv
    row = lax.broadcasted_iota(jnp.int32, (ch, ch), 0)
    col = lax.broadcasted_iota(jnp.int32, (ch, ch), 1)
    causal = col <= row
    lower = jnp.where(causal, 1.0, 0.0).astype(BF16)
    for s in range(rows // ch):
        t = slice(s * ch, (s + 1) * ch)
        g_pre = _dot(p_ref[0, t, g0:g0 + LANES], wg_ref[...]) + bg_ref[...]
        log_a = (jnp.minimum(g_pre, 0.0) - jnp.log(1.0 + jnp.exp(-jnp.abs(g_pre)))) * (1.0 / GLA_GATE_TAU)
        hi, lo = _split_bf16(log_a)
        cum = _dot(lower, hi) + _dot(lower, lo)
        for h in range(heads):
            bh = cum[:, h * dk:(h + 1) * dk]
            b_last = bh[ch - 1:ch, :]
            q = p_ref[0, t, h * dk:(h + 1) * dk].astype(F32) * dk ** -0.5
            k = p_ref[0, t, k0 + h * dk:k0 + (h + 1) * dk].astype(F32)
            v = p_ref[0, t, v0 + h * dv:v0 + (h + 1) * dv]
            r = p_ref[0, t, r0 + h * dv:r0 + (h + 1) * dv].astype(F32)
            q_in = (q * jnp.exp(bh)).astype(BF16)
            k_in = (k * jnp.exp(-bh)).astype(BF16)
            k_state = (k * jnp.exp(b_last - bh)).astype(BF16)
            scores = jnp.where(causal, _dot_nt(q_in, k_in), 0.0)
            state = st_ref[h]
            o = _dot(scores.astype(BF16), v) + _dot_nt(q_in, state.astype(BF16))
            st_ref[h] = jnp.exp(b_last) * state + _dot_tn(v, k_state)
            o = o * lax.rsqrt(jnp.mean(o * o, axis=-1, keepdims=True) + RMS_EPS) * ng_ref[...]
            o_ref[0, t, h * dv:(h + 1) * dv] = (o * (r * _sigmoid(r))).astype(o_ref.dtype)


def _gla_core(proj, w_gate_up, b_gate, norm_g, dk, dv, rows=256):
    b, s, n = proj.shape
    rows = min(rows, s)
    hk = GLA_HEADS * dk
    return pl.pallas_call(
        functools.partial(_gla_kernel, rows=rows, dk=dk, dv=dv),
        grid=(b, s // rows),
        in_specs=[
            pl.BlockSpec((1, rows, n), lambda bi, i: (bi, i, 0)),
            pl.BlockSpec((LANES, hk), lambda bi, i: (0, 0)),
            pl.BlockSpec((1, hk), lambda bi, i: (0, 0)),
            pl.BlockSpec((1, dv), lambda bi, i: (0, 0)),
        ],
        out_specs=pl.BlockSpec((1, rows, GLA_HEADS * dv), lambda bi, i: (bi, i, 0)),
        out_shape=jax.ShapeDtypeStruct((b, s, GLA_HEADS * dv), BF16),
        scratch_shapes=[pltpu.VMEM((GLA_HEADS, dv, dk), F32)],
        compiler_params=_params("parallel", "arbitrary"),
        name="gla_core",
    )(proj, w_gate_up, b_gate, norm_g)


def _conv_res_kernel(gb_ref, gc_ref, u_ref, hgc_ref, hu_ref, cw_ref, cb_ref, w_ref, x_ref, g_ref, y_ref):
    u2 = gc_ref[0].astype(F32) * u_ref[0].astype(F32)
    halo = hgc_ref[0].astype(F32) * hu_ref[0].astype(F32)
    halo = jnp.where(pl.program_id(1) > 0, halo, 0.0)
    rows = lax.broadcasted_iota(jnp.int32, u2.shape, 0)
    prev1 = jnp.where(rows == 0, halo[7:8], pltpu.roll(u2, 1, 0))
    prev2 = jnp.where(rows == 0, halo[6:7], jnp.where(rows == 1, halo[7:8], pltpu.roll(u2, 2, 0)))
    cw = cw_ref[...]
    y = cw[0:1] * prev2 + cw[1:2] * prev1 + cw[2:3] * u2 + cb_ref[...]
    o = (gb_ref[0].astype(F32) * y).astype(BF16)
    y_ref[0] = x_ref[0] + g_ref[0] * _dot(o, w_ref[...])


def _conv_res(proj, conv_w, conv_b, w_out, x, gate, tm=512):
    b, s, d = x.shape
    tm = min(tm, s)
    halo_blocks = tm // 8

    def halo_map(col):
        return lambda bi, i: (bi, jnp.maximum(i * halo_blocks - 1, 0), col)

    return pl.pallas_call(
        _conv_res_kernel,
        grid=(b, s // tm),
        in_specs=[
            pl.BlockSpec((1, tm, d), lambda bi, i: (bi, i, 0)),
            pl.BlockSpec((1, tm, d), lambda bi, i: (bi, i, 1)),
            pl.BlockSpec((1, tm, d), lambda bi, i: (bi, i, 2)),
            pl.BlockSpec((1, 8, d), halo_map(1)),
            pl.BlockSpec((1, 8, d), halo_map(2)),
            pl.BlockSpec((CONV_WIDTH, d), lambda bi, i: (0, 0)),
            pl.BlockSpec((1, d), lambda bi, i: (0, 0)),
            pl.BlockSpec((d, d), lambda bi, i: (0, 0)),
            pl.BlockSpec((1, tm, d), lambda bi, i: (bi, i, 0)),
            pl.BlockSpec((1, 1, d), lambda bi, i: (bi, 0, 0)),
        ],
        out_specs=pl.BlockSpec((1, tm, d), lambda bi, i: (bi, i, 0)),
        out_shape=jax.ShapeDtypeStruct((b, s, d), F32),
        compiler_params=_params("parallel", "parallel"),
        name="conv_res",
    )(proj, proj, proj, proj, proj, conv_w, conv_b.reshape(1, d), w_out, x, gate)


def _dil_kernel(q_ref, kc_ref, vc_ref, kp_ref, vp_ref, o_ref, l_ref, *, rows, hd, dilation):
    qb = DIL_QBLOCK
    n = pl.program_id(1)
    rho = pl.program_id(2)
    scale = hd ** -0.5
    first = lax.broadcasted_iota(jnp.int32, (qb, LANES), 1) < hd
    row = lax.broadcasted_iota(jnp.int32, (qb, qb), 0)
    col = lax.broadcasted_iota(jnp.int32, (qb, qb), 1)
    in_prev = col >= row
    in_cur = col <= row
    neg = -jnp.inf
    for s in range(rows // qb):
        t = slice(s * qb, (s + 1) * qb)
        if dilation == 1:
            t_out = t
        else:
            t_out = pl.ds(s * qb * dilation + rho, qb, stride=dilation)
        if s == 0:
            prev_mask = jnp.logical_and(in_prev, n > 0)
        else:
            prev_mask = in_prev
            tp = slice((s - 1) * qb, s * qb)
        for p in range(q_ref.shape[-1] // LANES):
            c = slice(p * LANES, (p + 1) * LANES)
            q = q_ref[0, t, c]
            zero = jnp.zeros_like(q)
            k_cur, v_cur = kc_ref[0, t, c], vc_ref[0, t, c]
            if s == 0:
                k_prev, v_prev = kp_ref[0, :, c], vp_ref[0, :, c]
            else:
                k_prev, v_prev = kc_ref[0, tp, c], vc_ref[0, tp, c]
            outs, lses = [], []
            for h in range(2):
                qh = jnp.where(first, q, zero) if h == 0 else jnp.where(first, zero, q)
                z_prev = jnp.where(prev_mask, _dot_nt(qh, k_prev) * scale, neg)
                z_cur = jnp.where(in_cur, _dot_nt(qh, k_cur) * scale, neg)
                m = jnp.maximum(jnp.max(z_prev, axis=1, keepdims=True), jnp.max(z_cur, axis=1, keepdims=True))
                p_prev = jnp.exp(z_prev - m)
                p_cur = jnp.exp(z_cur - m)
                den = jnp.sum(p_prev, axis=1, keepdims=True) + jnp.sum(p_cur, axis=1, keepdims=True)
                num = _dot(p_prev.astype(BF16), v_prev) + _dot(p_cur.astype(BF16), v_cur)
                outs.append(num / den)
                lses.append(jnp.broadcast_to(m + jnp.log(den), (qb, LANES)))
            o_ref[0, p, t_out, :] = jnp.where(first, outs[0], outs[1])
            l_ref[0, p, t_out, :] = jnp.where(first, lses[0], lses[1])


def _dil_group(view, dilation, window, seq):
    b, length, _ = view.shape
    width = DIL_HEADS * DIL_HEAD_DIM
    assert window == dilation * DIL_QBLOCK and seq % (dilation * DIL_QBLOCK) == 0
    rows = min(max(DIL_TILE_TOKENS // dilation, DIL_QBLOCK), 512, length)
    prev_blocks = rows // DIL_QBLOCK

    def cur(part):
        return lambda bi, i, rho: (bi, i, part * dilation + rho)

    def prev(part):
        return lambda bi, i, rho: (bi, jnp.maximum(i * prev_blocks - 1, 0), part * dilation + rho)

    pairs = width // LANES
    out_shape = jax.ShapeDtypeStruct((b, pairs, seq, LANES), F32)
    out_spec = pl.BlockSpec((1, pairs, rows * dilation, LANES), lambda bi, i, rho: (bi, 0, i, 0))
    return pl.pallas_call(
        functools.partial(_dil_kernel, rows=rows, hd=DIL_HEAD_DIM, dilation=dilation),
        grid=(b, length // rows, dilation),
        in_specs=[
            pl.BlockSpec((1, rows, width), cur(0)),
            pl.BlockSpec((1, rows, width), cur(1)),
            pl.BlockSpec((1, rows, width), cur(2)),
            pl.BlockSpec((1, DIL_QBLOCK, width), prev(1)),
            pl.BlockSpec((1, DIL_QBLOCK, width), prev(2)),
        ],
        out_specs=[out_spec, out_spec],
        out_shape=[out_shape, out_shape],
        compiler_params=_params("parallel", "parallel", "arbitrary"),
        name=f"dil_attn_r{dilation}",
    )(view, view, view, view, view)


def _dil_res_kernel(o0, o1, o2, l0, l1, l2, w_ref, x_ref, g_ref, y_ref):
    parts = []
    for p in range(o0.shape[1]):
        ls = (l0[0, p], l1[0, p], l2[0, p])
        m = jnp.maximum(jnp.maximum(ls[0], ls[1]), ls[2])
        es = [jnp.exp(l - m) for l in ls]
        parts.append((es[0] * o0[0, p] + es[1] * o1[0, p] + es[2] * o2[0, p]) / (es[0] + es[1] + es[2]))
    o = jnp.concatenate(parts, axis=1)
    y_ref[0] = x_ref[0] + g_ref[0] * _dot(o.astype(BF16), w_ref[...])


def _dil_res(outs, lses, w_out, x, gate, tm=512):
    b, s, d = x.shape
    k = w_out.shape[0]
    tm = min(tm, s)
    part = pl.BlockSpec((1, k // LANES, tm, LANES), lambda bi, i: (bi, 0, i, 0))
    return pl.pallas_call(
        _dil_res_kernel,
        grid=(b, s // tm),
        in_specs=[part] * 6 + [
            pl.BlockSpec((k, d), lambda bi, i: (0, 0)),
            pl.BlockSpec((1, tm, d), lambda bi, i: (bi, i, 0)),
            pl.BlockSpec((1, 1, d), lambda bi, i: (bi, 0, 0)),
        ],
        out_specs=pl.BlockSpec((1, tm, d), lambda bi, i: (bi, i, 0)),
        out_shape=jax.ShapeDtypeStruct((b, s, d), F32),
        compiler_params=_params("parallel", "parallel"),
        name="dil_res",
    )(*outs, *lses, w_out, x, gate)


def _router_kernel(x_ref, g_ref, sc_ref, sh_ref, wr_ref, br_ref, h_ref, route_ref, cnt_ref, run_ref):
    @pl.when(jnp.logical_and(pl.program_id(0) == 0, pl.program_id(1) == 0))
    def _():
        run_ref[...] = jnp.zeros_like(run_ref)

    h = _mod_norm(x_ref[0], g_ref[...], sc_ref[0], sh_ref[0])
    h_ref[...] = h
    h_hi, h_lo = _split_bf16(h)
    w_hi, w_lo = _split_bf16(wr_ref[...])
    logits = _dot(h_hi, w_hi) + (_dot(h_hi, w_lo) + _dot(h_lo, w_hi)) + br_ref[...]
    tm = logits.shape[0]
    lane = lax.broadcasted_iota(jnp.int32, logits.shape, 1).astype(F32)
    neg = -jnp.inf
    far = float(LANES)

    def first_argmax(vals):
        top = jnp.max(vals, axis=1, keepdims=True)
        return top, jnp.min(jnp.where(vals == top, lane, far), axis=1, keepdims=True)

    grp_logits = jnp.where(lane < MOE_GROUPS, logits, neg)
    grp_max, grp = first_argmax(grp_logits)
    p_grp = 1.0 / jnp.sum(jnp.exp(grp_logits - grp_max), axis=1, keepdims=True)
    base = MOE_GROUPS + grp * MOE_PER_GROUP
    in_grp = jnp.logical_and(lane >= base, lane < base + MOE_PER_GROUP)
    exp_logits = jnp.where(in_grp, logits, neg)
    m1, i1 = first_argmax(exp_logits)
    m2, i2 = first_argmax(jnp.where(lane == i1, neg, exp_logits))
    e2 = jnp.exp(m2 - m1)
    g1 = p_grp / (1.0 + e2)
    g2 = g1 * e2
    e_a = i1 - MOE_GROUPS
    e_b = i2 - MOE_GROUPS

    pick_a = lane == e_a
    pick_b = lane == e_b
    both = jnp.where(jnp.logical_or(pick_a, pick_b), 1.0, 0.0)
    row = lax.broadcasted_iota(jnp.int32, (tm, tm), 0)
    col = lax.broadcasted_iota(jnp.int32, (tm, tm), 1)
    earlier = jnp.where(col < row, 1.0, 0.0).astype(BF16)
    before = _dot(earlier, both.astype(BF16)) + run_ref[...]
    rank_a = jnp.sum(jnp.where(pick_a, before, 0.0), axis=1, keepdims=True)
    rank_b = jnp.sum(jnp.where(pick_b, before, 0.0), axis=1, keepdims=True)
    run_ref[...] += jnp.sum(both, axis=0, keepdims=True)
    cnt_ref[...] = run_ref[...]

    route = jnp.where(lane == 0, e_a, jnp.where(lane == 1, e_b, 0.0))
    route = jnp.where(lane == 2, g1, jnp.where(lane == 3, g2, route))
    route_ref[...] = jnp.where(lane == 4, rank_a, jnp.where(lane == 5, rank_b, route))


def _router(x, g, sc, sh, w_route, b_route, tm=512):
    b, s, d = x.shape
    tm = min(tm, s)
    steps = s // tm
    return pl.pallas_call(
        _router_kernel,
        grid=(b, steps),
        in_specs=[
            pl.BlockSpec((1, tm, d), lambda bi, i: (bi, i, 0)),
            pl.BlockSpec((1, d), lambda bi, i: (0, 0)),
            pl.BlockSpec((1, 1, d), lambda bi, i: (bi, 0, 0)),
            pl.BlockSpec((1, 1, d), lambda bi, i: (bi, 0, 0)),
            pl.BlockSpec((d, LANES), lambda bi, i: (0, 0)),
            pl.BlockSpec((1, LANES), lambda bi, i: (0, 0)),
        ],
        out_specs=[
            pl.BlockSpec((tm, d), lambda bi, i: (bi * steps + i, 0)),
            pl.BlockSpec((tm, LANES), lambda bi, i: (bi * steps + i, 0)),
            pl.BlockSpec((1, LANES), lambda bi, i: (0, 0)),
        ],
        out_shape=[
            jax.ShapeDtypeStruct((b * s, d), F32),
            jax.ShapeDtypeStruct((b * s, LANES), F32),
            jax.ShapeDtypeStruct((1, LANES), F32),
        ],
        scratch_shapes=[pltpu.VMEM((1, LANES), F32)],
        compiler_params=_params("arbitrary", "arbitrary"),
        name="moe_router",
    )(x, g.reshape(1, d), sc, sh, w_route, b_route)


def _copy_idx(src_ref, idx_smem, sem):
    cp = pltpu.make_async_copy(src_ref.at[0, 0], idx_smem, sem)
    cp.start()
    cp.wait()


def _dispatch_kernel(tail_ref, used_ref, dest_ref, h_ref, slots, idx_smem, zeros, sem_idx, sem_rows, sem_zero, *, nb):
    tm = h_ref.shape[0]
    bm = zeros.shape[0]

    @pl.when(pl.program_id(0) == 0)
    def _():
        zeros[...] = jnp.zeros_like(zeros)

        def clear(blk):
            return pltpu.make_async_copy(zeros, slots.at[pl.ds(blk * bm, bm)], sem_zero)

        def each_tail(fn):
            def body(e, carry):
                @pl.when(tail_ref[e] >= 0)
                def _():
                    fn(clear(tail_ref[e]))
                return carry
            lax.fori_loop(0, tail_ref.shape[0], body, 0)

        def each_unused(fn):
            def body(blk, carry):
                fn(clear(blk))
                return carry
            lax.fori_loop(used_ref[0], nb, body, 0)

        each_tail(lambda cp: cp.start())
        each_unused(lambda cp: cp.start())
        each_tail(lambda cp: cp.wait())
        each_unused(lambda cp: cp.wait())

    _copy_idx(dest_ref, idx_smem, sem_idx)

    def issue(r, carry):
        for k in range(MOE_TOPK):
            dst = slots.at[pl.ds(idx_smem[k * tm + r], 1)]
            pltpu.make_async_copy(h_ref.at[pl.ds(r, 1)], dst, sem_rows).start(priority=k)
        return carry

    lax.fori_loop(0, tm, issue, 0, unroll=8)
    for _ in range(MOE_TOPK):
        pltpu.make_async_copy(h_ref, slots.at[pl.ds(0, tm)], sem_rows).wait()


def _dispatch(tail_blk, used, dest, h_rows, nb, bm, tm):
    t, d = h_rows.shape
    grid_spec = pltpu.PrefetchScalarGridSpec(
        num_scalar_prefetch=2,
        grid=(t // tm,),
        in_specs=[
            pl.BlockSpec((1, 1, MOE_TOPK * tm), lambda i, tail, n: (i, 0, 0)),
            pl.BlockSpec((tm, d), lambda i, tail, n: (i, 0)),
        ],
        out_specs=pl.BlockSpec(memory_space=pl.ANY),
        scratch_shapes=[
            pltpu.SMEM((MOE_TOPK * tm,), jnp.int32),
            pltpu.VMEM((bm, d), F32),
            pltpu.SemaphoreType.DMA,
            pltpu.SemaphoreType.DMA,
            pltpu.SemaphoreType.DMA,
        ],
    )
    return pl.pallas_call(
        functools.partial(_dispatch_kernel, nb=nb),
        grid_spec=grid_spec,
        out_shape=jax.ShapeDtypeStruct((nb * bm, d), F32),
        compiler_params=_params("arbitrary"),
        name="moe_dispatch",
    )(tail_blk, used, dest, h_rows)


def _expert_kernel(blk_e_ref, used_ref, x_ref, wg_ref, wu_ref, wd_ref, y_ref):
    del blk_e_ref
    live = pl.program_id(0) < used_ref[0]

    @pl.when(live)
    def _():
        x = x_ref[...].astype(BF16)
        gate = _dot(x, wg_ref[0])
        up = _dot(x, wu_ref[0])
        act = (gate * _sigmoid(gate) * up).astype(BF16)
        y_ref[...] = _dot(act, wd_ref[0])

    @pl.when(jnp.logical_not(live))
    def _():
        y_ref[...] = jnp.zeros_like(y_ref)


def _expert_blocks(blk_e, used, slots, w_gate, w_up, w_down, bm):
    nb = blk_e.shape[0]
    _, d, f = w_gate.shape

    def block(i, e, n):
        return (jnp.minimum(i, n[0] - 1), 0)

    def weight(i, e, n):
        return (e[jnp.minimum(i, n[0] - 1)], 0, 0)

    grid_spec = pltpu.PrefetchScalarGridSpec(
        num_scalar_prefetch=2,
        grid=(nb,),
        in_specs=[
            pl.BlockSpec((bm, d), block),
            pl.BlockSpec((1, d, f), weight),
            pl.BlockSpec((1, d, f), weight),
            pl.BlockSpec((1, f, d), weight),
        ],
        out_specs=pl.BlockSpec((bm, d), lambda i, e, n: (i, 0)),
    )
    return pl.pallas_call(
        _expert_kernel,
        grid_spec=grid_spec,
        out_shape=jax.ShapeDtypeStruct((nb * bm, d), F32),
        compiler_params=_params("arbitrary"),
        name="moe_experts",
    )(blk_e, used, slots, w_gate, w_up, w_down)


def _gather_rows(idx_smem, offset, count, src_hbm, dst, sem):
    def issue(pair, carry):
        for lane in range(2):
            r = 2 * pair + lane
            src = src_hbm.at[pl.ds(idx_smem[offset + r], 1)]
            pltpu.make_async_copy(src, dst.at[pl.ds(r, 1)], sem).start(priority=lane)
        return carry

    lax.fori_loop(0, count // 2, issue, 0, unroll=4)


def _wait_rows(count, src_hbm, dst, sem):
    pltpu.make_async_copy(src_hbm.at[pl.ds(0, count)], dst, sem).wait()


def _combine_kernel(dest_cur, dest_nxt, route_ref, y_hbm, x_ref, g_ref, fg_ref, o_ref,
                    idx_even, idx_odd, buf, sem_idx, sem_rows, *, final):
    tm = buf.shape[2]
    i = pl.program_id(0)
    idx_smem = (idx_even, idx_odd)

    def fetch(dest_ref, sl):
        _copy_idx(dest_ref, idx_smem[sl], sem_idx)
        for k in range(MOE_TOPK):
            _gather_rows(idx_smem[sl], k * tm, tm, y_hbm, buf.at[sl, k], sem_rows.at[sl])

    def run(slot):
        if slot == 0:
            @pl.when(i == 0)
            def _():
                fetch(dest_cur, 0)

        @pl.when(i + 1 < pl.num_programs(0))
        def _():
            fetch(dest_nxt, 1 - slot)

        for k in range(MOE_TOPK):
            _wait_rows(tm, y_hbm, buf.at[slot, k], sem_rows.at[slot])
        route = route_ref[...]
        y = route[:, 2:3] * buf[slot, 0] + route[:, 3:4] * buf[slot, 1]
        out = x_ref[...] + g_ref[0] * y
        if final:
            out = out * lax.rsqrt(jnp.mean(out * out, axis=-1, keepdims=True) + RMS_EPS) * fg_ref[...]
        o_ref[...] = out

    for slot in range(2):
        pl.when(lax.rem(i, 2) == slot)(functools.partial(run, slot))


def _combine(dest, route, y_rows, x, gate, final_g, final, tm):
    b, s, d = x.shape
    t = b * s
    per_seq = s // tm
    steps = t // tm
    out = pl.pallas_call(
        functools.partial(_combine_kernel, final=final),
        grid=(steps,),
        in_specs=[
            pl.BlockSpec((1, 1, MOE_TOPK * tm), lambda i: (i, 0, 0)),
            pl.BlockSpec((1, 1, MOE_TOPK * tm), lambda i: (jnp.minimum(i + 1, steps - 1), 0, 0)),
            pl.BlockSpec((tm, LANES), lambda i: (i, 0)),
            pl.BlockSpec(memory_space=pl.ANY),
            pl.BlockSpec((tm, d), lambda i: (i, 0)),
            pl.BlockSpec((1, 1, d), lambda i: (i // per_seq, 0, 0)),
            pl.BlockSpec((1, d), lambda i: (0, 0)),
        ],
        out_specs=pl.BlockSpec((tm, d), lambda i: (i, 0)),
        out_shape=jax.ShapeDtypeStruct((t, d), F32),
        scratch_shapes=[
            pltpu.SMEM((MOE_TOPK * tm,), jnp.int32),
            pltpu.SMEM((MOE_TOPK * tm,), jnp.int32),
            pltpu.VMEM((2, MOE_TOPK, tm, d), F32),
            pltpu.SemaphoreType.DMA,
            pltpu.SemaphoreType.DMA((2,)),
        ],
        compiler_params=_params("arbitrary"),
        name="moe_combine",
    )(dest, dest, route, y_rows, x.reshape(t, d), gate, final_g.reshape(1, d))
    return out.reshape(b, s, d)


def _slot_tables(route, counts, bm, tm):
    t = route.shape[0]
    nb = MOE_TOPK * t // bm + MOE_EXPERTS
    counts = counts[0, :MOE_EXPERTS].astype(jnp.int32)
    padded = (counts + bm - 1) // bm * bm
    pad_end = jnp.cumsum(padded)
    pad_start = pad_end - padded
    expert = route[:, 0:MOE_TOPK].astype(jnp.int32)
    rank = route[:, 4:4 + MOE_TOPK].astype(jnp.int32)
    dest = pad_start[expert] + rank
    dest_tiles = dest.reshape(t // tm, tm, MOE_TOPK).transpose(0, 2, 1).reshape(t // tm, 1, MOE_TOPK * tm)
    blk_start = jnp.arange(nb, dtype=jnp.int32) * bm
    blk_e = jnp.minimum(jnp.sum((pad_end[None, :] <= blk_start[:, None]).astype(jnp.int32), axis=1), MOE_EXPERTS - 1)
    used = (pad_end[-1:] // bm).astype(jnp.int32)
    tail_blk = jnp.where(counts % bm != 0, pad_end // bm - 1, -1).astype(jnp.int32)
    return dest_tiles, blk_e, used, tail_blk, nb


def _moe_layer(x, g, sc, sh, gate, w_route, b_route, w_gate, w_up, w_down, final_g, final, bm=512, tm=256):
    b, s, d = x.shape
    tm = min(tm, s)
    h_rows, route, counts = _router(x, g, sc, sh, w_route, b_route)
    dest, blk_e, used, tail_blk, nb = _slot_tables(route, counts, bm, tm)
    slots = _dispatch(tail_blk, used, dest, h_rows, nb, bm, tm)
    y_rows = _expert_blocks(blk_e, used, slots, w_gate, w_up, w_down, bm)
    return _combine(dest, route, y_rows, x, gate, final_g, final, tm)


def kernel(x, c, ada_w, ada_b, norm_g, final_g, sb_w_in, sb_w_out, gla_w_in, gla_w_gate_up, gla_b_gate, gla_norm_g, gla_w_out, conv_w_in, conv_w, conv_b, conv_w_out, dil_w_in, dil_w_out, moe_w_grp, moe_b_grp, moe_w_exp, moe_b_exp, moe_w_gate, moe_w_up, moe_w_down):
    b, s, d = x.shape
    depth = ada_w.shape[0]
    mod = _ada_mod(c, ada_w, ada_b).reshape(depth, b, N_MOD, 1, d)
    for i in range(depth):
        sh1, sc1, g1, sh2, sc2, g2 = (mod[i, :, m] for m in range(N_MOD))
        kind, j = i % 4, i // 4
        if kind == 0:
            qkv = _norm_proj(x, norm_g[i, 0], sc1, sh1, sb_w_in[j].astype(BF16))
            x = _proj_res(_sb_attention(qkv), sb_w_out[j].astype(BF16), x, g1)
        elif kind == 1:
            dk = gla_w_gate_up.shape[-1] // GLA_HEADS
            dv = gla_norm_g.shape[-1]
            w_in = jnp.pad(gla_w_in[j], ((0, 0), (0, LANES - GLA_GATE_RANK))).astype(BF16)
            w_up = jnp.pad(gla_w_gate_up[j], ((0, LANES - GLA_GATE_RANK), (0, 0))).astype(BF16)
            proj = _norm_proj(x, norm_g[i, 0], sc1, sh1, w_in, tn=640)
            o = _gla_core(proj, w_up, gla_b_gate[j].reshape(1, -1), gla_norm_g[j].reshape(1, -1), dk, dv)
            x = _proj_res(o, gla_w_out[j].astype(BF16), x, g1)
        elif kind == 2:
            proj = _norm_proj(x, norm_g[i, 0], sc1, sh1, conv_w_in[j].astype(BF16))
            x = _conv_res(proj, conv_w[j], conv_b[j], conv_w_out[j].astype(BF16), x, g1)
        else:
            group_w = 3 * DIL_HEADS * DIL_HEAD_DIM
            outs, lses = [], []
            for g, (window, dilation) in enumerate(DIL_PATTERNS):
                w_g = dil_w_in[j][:, g * group_w:(g + 1) * group_w].astype(BF16)
                view = _norm_proj(x, norm_g[i, 0], sc1, sh1, w_g, dilation=dilation)
                o_g, l_g = _dil_group(view, dilation, window, s)
                outs.append(o_g)
                lses.append(l_g)
            x = _dil_res(outs, lses, dil_w_out[j].astype(BF16), x, g1)
        w_route = jnp.pad(jnp.concatenate([moe_w_grp[i], moe_w_exp[i]], axis=1),
                          ((0, 0), (0, LANES - MOE_GROUPS - MOE_EXPERTS)))
        b_route = jnp.pad(jnp.concatenate([moe_b_grp[i], moe_b_exp[i]]), (0, LANES - MOE_GROUPS - MOE_EXPERTS))
        x = _moe_layer(x, norm_g[i, 1], sc2, sh2, g2, w_route, b_route.reshape(1, LANES),
                       moe_w_gate[i].astype(BF16), moe_w_up[i].astype(BF16), moe_w_down[i].astype(BF16),
                       final_g, final=(i == depth - 1))
    return x
```

```python
import functools

import jax
import jax.numpy as jnp
from jax import lax
from jax.experimental import pallas as pl
from jax.experimental.pallas import tpu as pltpu

F32 = jnp.float32
BF16 = jnp.bfloat16

LANES = 128
RMS_EPS = 1e-6
N_MOD = 6

SB_HEADS = 16
GLA_HEADS = 4
GLA_GATE_RANK = 16
GLA_GATE_TAU = 16.0
GLA_CHUNK = 64
CONV_WIDTH = 3
DIL_PATTERNS = ((128, 1), (512, 4), (2048, 16))
DIL_HEADS = 8
DIL_HEAD_DIM = 64
DIL_QBLOCK = 128
DIL_TILE_TOKENS = 2048
MOE_GROUPS = 4
MOE_PER_GROUP = 8
MOE_EXPERTS = MOE_GROUPS * MOE_PER_GROUP
MOE_TOPK = 2

SB_LOG_ZERO = -104.0
SB_BAND = 2

VMEM_LIMIT = 48 * 1024 * 1024


def _params(*sem):
    return pltpu.CompilerParams(dimension_semantics=sem, vmem_limit_bytes=VMEM_LIMIT)


def _sigmoid(x):
    return 1.0 / (1.0 + jnp.exp(-x))


def _softplus(x):
    return jnp.maximum(x, 0.0) + jnp.log(1.0 + jnp.exp(-jnp.abs(x)))


def _split_bf16(x):
    hi = x.astype(BF16)
    lo = (x - hi.astype(F32)).astype(BF16)
    return hi, lo


def _dot(a, b):
    return jnp.dot(a, b, preferred_element_type=F32)


def _dot_nt(a, b):
    return lax.dot_general(a, b, (((1,), (1,)), ((), ())), preferred_element_type=F32)


def _dot_tn(a, b):
    return lax.dot_general(a, b, (((0,), (0,)), ((), ())), preferred_element_type=F32)


def _mod_norm(x, g, sc, sh):
    r = lax.rsqrt(jnp.mean(x * x, axis=-1, keepdims=True) + RMS_EPS)
    return (x * r) * g * (1.0 + sc) + sh


def _ada_kernel(c_ref, w_ref, b_ref, o_ref):
    c = c_ref[...]
    cond = c * _sigmoid(c)
    o_ref[0] = _dot(cond.astype(BF16), w_ref[0].astype(BF16)) + b_ref[0]


def _ada_mod(c, ada_w, ada_b):
    depth, d, n = ada_w.shape
    b = c.shape[0]
    tn = 1536
    return pl.pallas_call(
        _ada_kernel,
        grid=(depth, n // tn),
        in_specs=[
            pl.BlockSpec((b, d), lambda i, j: (0, 0)),
            pl.BlockSpec((1, d, tn), lambda i, j: (i, 0, j)),
            pl.BlockSpec((1, 1, tn), lambda i, j: (i, 0, j)),
        ],
        out_specs=pl.BlockSpec((1, b, tn), lambda i, j: (i, 0, j)),
        out_shape=jax.ShapeDtypeStruct((depth, b, n), F32),
        compiler_params=_params("parallel", "parallel"),
        name="ada_mod",
    )(c, ada_w, ada_b.reshape(depth, 1, n))


def _norm_proj_kernel(x_ref, g_ref, sc_ref, sh_ref, w_ref, o_ref, h_ref, *res_ref, dilation, tn):
    h_ref[...] = _mod_norm(x_ref[0], g_ref[...], sc_ref[0], sh_ref[0]).astype(BF16)
    tm = h_ref.shape[0]
    for j in range(w_ref.shape[1] // tn):
        res = _dot(h_ref[...], w_ref[:, j * tn:(j + 1) * tn])
        if dilation == 1:
            o_ref[0, :, j * tn:(j + 1) * tn] = res.astype(o_ref.dtype)
            continue
        for c in range(tn // LANES):
            res_ref[0][j * (tn // LANES) + c] = res[:, c * LANES:(c + 1) * LANES]
        for rho in range(dilation):
            for c in range(tn // LANES):
                rows = res_ref[0][j * (tn // LANES) + c, pl.ds(rho, tm // dilation, stride=dilation), :]
                col = (j * dilation + rho) * tn + c * LANES
                o_ref[0, :, col:col + LANES] = rows.astype(o_ref.dtype)


def _norm_proj(x, g, sc, sh, w, tm=512, tn=512, dilation=1):
    b, s, d = x.shape
    n = w.shape[1]
    tm = min(tm, s)
    assert s % tm == 0 and n % tn == 0 and tm % (16 * dilation) == 0
    scratch = [pltpu.VMEM((tm, d), BF16)]
    if dilation > 1:
        scratch.append(pltpu.VMEM((n // LANES, tm, LANES), F32))
    return pl.pallas_call(
        functools.partial(_norm_proj_kernel, dilation=dilation, tn=tn),
        grid=(b, s // tm),
        in_specs=[
            pl.BlockSpec((1, tm, d), lambda bi, i: (bi, i, 0)),
            pl.BlockSpec((1, d), lambda bi, i: (0, 0)),
            pl.BlockSpec((1, 1, d), lambda bi, i: (bi, 0, 0)),
            pl.BlockSpec((1, 1, d), lambda bi, i: (bi, 0, 0)),
            pl.BlockSpec((d, n), lambda bi, i: (0, 0)),
        ],
        out_specs=pl.BlockSpec((1, tm // dilation, dilation * n), lambda bi, i: (bi, i, 0)),
        out_shape=jax.ShapeDtypeStruct((b, s // dilation, dilation * n), BF16),
        scratch_shapes=scratch,
        compiler_params=_params("parallel", "parallel"),
        name="norm_proj",
    )(x, g.reshape(1, d), sc, sh, w)


def _proj_res_kernel(o_ref, w_ref, x_ref, g_ref, y_ref):
    y_ref[0] = x_ref[0] + g_ref[0] * _dot(o_ref[0], w_ref[...])


def _proj_res(o, w, x, gate, tm=512):
    b, s, d = x.shape
    k = o.shape[-1]
    tm = min(tm, s)
    return pl.pallas_call(
        _proj_res_kernel,
        grid=(b, s // tm),
        in_specs=[
            pl.BlockSpec((1, tm, k), lambda bi, i: (bi, i, 0)),
            pl.BlockSpec((k, d), lambda bi, i: (0, 0)),
            pl.BlockSpec((1, tm, d), lambda bi, i: (bi, i, 0)),
            pl.BlockSpec((1, 1, d), lambda bi, i: (bi, 0, 0)),
        ],
        out_specs=pl.BlockSpec((1, tm, d), lambda bi, i: (bi, i, 0)),
        out_shape=jax.ShapeDtypeStruct((b, s, d), F32),
        compiler_params=_params("parallel", "parallel"),
        name="proj_res",
    )(o, w, x, gate)


def _sb_kernel(q_ref, k_ref, v_ref, o_ref, acc_ref, c_ref, *, blk, hd):
    scale = hd ** -0.5
    fold_scale = (hd & (hd - 1)) == 0 and (hd.bit_length() - 1) % 2 == 0
    first = lax.broadcasted_iota(jnp.int32, (blk, LANES), 1) < hd
    row = lax.broadcasted_iota(jnp.int32, (blk, blk), 0)
    col = lax.broadcasted_iota(jnp.int32, (blk, blk), 1)
    minus_later = jnp.where(row > col, -1.0, 0.0).astype(BF16)
    causal = col < row
    subs = q_ref.shape[1] // blk
    first_block = pl.program_id(2) * subs

    def visit_many(work):
        kv = []
        for _, block, _, _ in work:
            start = pl.multiple_of(block * blk, blk)
            kv.append((k_ref[0, pl.ds(start, blk), :], v_ref[0, pl.ds(start, blk), :]))
        zs = [[_dot_nt(q_heads[h], kb) for h in range(2)] for (q_heads, _, _, _), (kb, _) in zip(work, kv)]
        log_betas, sps = [], []
        for (_, _, mask, _), z2 in zip(work, zs):
            lb2, sp2 = [], []
            for z in z2:
                if not fold_scale:
                    z = z * scale
                sp = _softplus(z)
                lb2.append(z - sp)
                sp2.append(sp if mask is None else jnp.where(mask, sp, 0.0))
            log_betas.append(lb2)
            sps.append(sp2)
        suffixes = [[_dot(sp.astype(BF16), minus_later) for sp in sp2] for sp2 in sps]
        return kv, log_betas, sps, suffixes

    def finish(work, staged, c_in):
        kv, log_betas, sps, suffixes = staged
        cs = {chain: list(c) for chain, c in c_in.items()}
        weights = []
        for n, (_, _, mask, chain) in enumerate(work):
            a2 = []
            for h in range(2):
                a = jnp.exp(log_betas[n][h] + suffixes[n][h] + cs[chain][h])
                a2.append(a if mask is None else jnp.where(mask, a, 0.0))
                cs[chain][h] = cs[chain][h] - jnp.sum(sps[n][h], axis=1, keepdims=True)
            weights.append(a2)
        accs = {}
        for n, (_, _, _, chain) in enumerate(work):
            outs = [_dot(weights[n][h].astype(BF16), kv[n][1]) for h in range(2)]
            more = jnp.where(first, outs[0], outs[1])
            accs[chain] = more if chain not in accs else accs[chain] + more
        return accs, cs

    def remainder(sub, q_heads, depth):
        def cond(carry):
            j, cmax = carry
            return jnp.logical_and(j >= 0, cmax > SB_LOG_ZERO)

        def body(carry):
            j, _ = carry
            work = [(q_heads, j, None, sub)]
            accs, cs = finish(work, visit_many(work), {sub: [c_ref[sub, 0], c_ref[sub, 1]]})
            acc_ref[sub] += accs[sub]
            for h in range(2):
                c_ref[sub, h] = cs[sub][h]
            return j - 1, jnp.max(c_ref[sub])

        lax.while_loop(cond, body, (first_block + sub - depth - 1, jnp.max(c_ref[sub])))

    def run(depth):
        heads = []
        for sub in range(subs):
            q = q_ref[0, sub * blk:(sub + 1) * blk, :]
            if fold_scale:
                q = (q.astype(F32) * scale).astype(BF16)
            zero = jnp.zeros_like(q)
            heads.append((jnp.where(first, q, zero), jnp.where(first, zero, q)))
        work = [(heads[sub], first_block + sub - back, causal if back == 0 else None, sub)
                for sub in range(subs) for back in range(depth + 1)]
        zero_c = [jnp.zeros((blk, 1), F32)] * 2
        accs, cs = finish(work, visit_many(work), {sub: zero_c for sub in range(subs)})
        for sub in range(subs):
            acc_ref[sub] = accs[sub]
            for h in range(2):
                c_ref[sub, h] = cs[sub][h]
        for sub in range(subs):
            remainder(sub, heads[sub], depth)
            o_ref[0, sub * blk:(sub + 1) * blk, :] = acc_ref[sub].astype(o_ref.dtype)

    pl.when(first_block >= SB_BAND)(functools.partial(run, SB_BAND))
    pl.when(first_block < SB_BAND)(functools.partial(run, 0))


def _sb_attention(qkv, blk=128, rows=512):
    b, s, n = qkv.shape
    width = n // 3
    hd = width // SB_HEADS
    assert 2 * hd == LANES
    pairs = width // LANES
    rows = min(rows, s)
    return pl.pallas_call(
        functools.partial(_sb_kernel, blk=blk, hd=hd),
        grid=(b, pairs, s // rows),
        in_specs=[
            pl.BlockSpec((1, rows, LANES), lambda bi, p, i: (bi, i, p)),
            pl.BlockSpec((1, s, LANES), lambda bi, p, i: (bi, 0, pairs + p)),
            pl.BlockSpec((1, s, LANES), lambda bi, p, i: (bi, 0, 2 * pairs + p)),
        ],
        out_specs=pl.BlockSpec((1, rows, LANES), lambda bi, p, i: (bi, i, p)),
        out_shape=jax.ShapeDtypeStruct((b, s, width), BF16),
        scratch_shapes=[pltpu.VMEM((rows // blk, blk, LANES), F32), pltpu.VMEM((rows // blk, 2, blk, 1), F32)],
        compiler_params=_params("parallel", "parallel", "arbitrary"),
        name="sb_attention",
    )(qkv, qkv, qkv)


def _gla_kernel(p_ref, wg_ref, bg_ref, ng_ref, o_ref, st_ref, *, rows, dk, dv):
    @pl.when(pl.program_id(1) == 0)
    def _():
        st_ref[...] = jnp.zeros_like(st_ref)

    heads = GLA_HEADS
    ch = GLA_CHUNK
    k0 = heads * dk
    v0 = 2 * heads * dk
    r0 = v0 + heads * dv
    g0 = r0 + heads * dv
    row = lax.broadcasted_iota(jnp.int32, (ch, ch), 0)
    col = lax.broadcasted_iota(jnp.int32, (ch, ch), 1)
    causal = col <= row
    lower = jnp.where(causal, 1.0, 0.0).astype(BF16)
    chunks = [slice(s * ch, (s + 1) * ch) for s in range(rows // ch)]
    g_pres = [_dot(p_ref[0, t, g0:g0 + LANES], wg_ref[...]) + bg_ref[...] for t in chunks]
    cums = []
    for g_pre in g_pres:
        log_a = (jnp.minimum(g_pre, 0.0) - jnp.log(1.0 + jnp.exp(-jnp.abs(g_pre)))) * (1.0 / GLA_GATE_TAU)
        hi, lo = _split_bf16(log_a)
        cums.append(_dot(lower, hi) + _dot(lower, lo))
    parts = []
    for t, cum in zip(chunks, cums):
        for h in range(heads):
            bh = cum[:, h * dk:(h + 1) * dk]
            b_last = bh[ch - 1:ch, :]
            q = p_ref[0, t, h * dk:(h + 1) * dk].astype(F32) * dk ** -0.5
            k = p_ref[0, t, k0 + h * dk:k0 + (h + 1) * dk].astype(F32)
            q_in = (q * jnp.exp(bh)).astype(BF16)
            k_in = (k * jnp.exp(-bh)).astype(BF16)
            k_state = (k * jnp.exp(b_last - bh)).astype(BF16)
            parts.append((t, h, q_in, k_in, k_state, jnp.exp(b_last)))
    scores = [jnp.where(causal, _dot_nt(q_in, k_in), 0.0).astype(BF16) for _, _, q_in, k_in, _, _ in parts]
    values = [p_ref[0, t, v0 + h * dv:v0 + (h + 1) * dv] for t, h, _, _, _, _ in parts]
    intra = [_dot(sc, v) for sc, v in zip(scores, values)]
    updates = [_dot_tn(v, k_state) for v, (_, _, _, _, k_state, _) in zip(values, parts)]
    states = [st_ref[h] for h in range(heads)]
    for n, (t, h, q_in, _, _, decay) in enumerate(parts):
        o = intra[n] + _dot_nt(q_in, states[h].astype(BF16))
        states[h] = decay * states[h] + updates[n]
        o = o * lax.rsqrt(jnp.mean(o * o, axis=-1, keepdims=True) + RMS_EPS) * ng_ref[...]
        r = p_ref[0, t, r0 + h * dv:r0 + (h + 1) * dv].astype(F32)
        o_ref[0, t, h * dv:(h + 1) * dv] = (o * (r * _sigmoid(r))).astype(o_ref.dtype)
    for h in range(heads):
        st_ref[h] = states[h]


def _gla_core(proj, w_gate_up, b_gate, norm_g, dk, dv, rows=256):
    b, s, n = proj.shape
    rows = min(rows, s)
    hk = GLA_HEADS * dk
    return pl.pallas_call(
        functools.partial(_gla_kernel, rows=rows, dk=dk, dv=dv),
        grid=(b, s // rows),
        in_specs=[
            pl.BlockSpec((1, rows, n), lambda bi, i: (bi, i, 0)),
            pl.BlockSpec((LANES, hk), lambda bi, i: (0, 0)),
            pl.BlockSpec((1, hk), lambda bi, i: (0, 0)),
            pl.BlockSpec((1, dv), lambda bi, i: (0, 0)),
        ],
        out_specs=pl.BlockSpec((1, rows, GLA_HEADS * dv), lambda bi, i: (bi, i, 0)),
        out_shape=jax.ShapeDtypeStruct((b, s, GLA_HEADS * dv), BF16),
        scratch_shapes=[pltpu.VMEM((GLA_HEADS, dv, dk), F32)],
        compiler_params=_params("parallel", "arbitrary"),
        name="gla_core",
    )(proj, w_gate_up, b_gate, norm_g)


def _conv_res_kernel(gb_ref, gc_ref, u_ref, hgc_ref, hu_ref, cw_ref, cb_ref, w_ref, x_ref, g_ref, y_ref):
    u2 = gc_ref[0].astype(F32) * u_ref[0].astype(F32)
    halo = hgc_ref[0].astype(F32) * hu_ref[0].astype(F32)
    halo = jnp.where(pl.program_id(1) > 0, halo, 0.0)
    rows = lax.broadcasted_iota(jnp.int32, u2.shape, 0)
    prev1 = jnp.where(rows == 0, halo[7:8], pltpu.roll(u2, 1, 0))
    prev2 = jnp.where(rows == 0, halo[6:7], jnp.where(rows == 1, halo[7:8], pltpu.roll(u2, 2, 0)))
    cw = cw_ref[...]
    y = cw[0:1] * prev2 + cw[1:2] * prev1 + cw[2:3] * u2 + cb_ref[...]
    o = (gb_ref[0].astype(F32) * y).astype(BF16)
    y_ref[0] = x_ref[0] + g_ref[0] * _dot(o, w_ref[...])


def _conv_res(proj, conv_w, conv_b, w_out, x, gate, tm=512):
    b, s, d = x.shape
    tm = min(tm, s)
    halo_blocks = tm // 8

    def halo_map(col):
        return lambda bi, i: (bi, jnp.maximum(i * halo_blocks - 1, 0), col)

    return pl.pallas_call(
        _conv_res_kernel,
        grid=(b, s // tm),
        in_specs=[
            pl.BlockSpec((1, tm, d), lambda bi, i: (bi, i, 0)),
            pl.BlockSpec((1, tm, d), lambda bi, i: (bi, i, 1)),
            pl.BlockSpec((1, tm, d), lambda bi, i: (bi, i, 2)),
            pl.BlockSpec((1, 8, d), halo_map(1)),
            pl.BlockSpec((1, 8, d), halo_map(2)),
            pl.BlockSpec((CONV_WIDTH, d), lambda bi, i: (0, 0)),
            pl.BlockSpec((1, d), lambda bi, i: (0, 0)),
            pl.BlockSpec((d, d), lambda bi, i: (0, 0)),
            pl.BlockSpec((1, tm, d), lambda bi, i: (bi, i, 0)),
            pl.BlockSpec((1, 1, d), lambda bi, i: (bi, 0, 0)),
        ],
        out_specs=pl.BlockSpec((1, tm, d), lambda bi, i: (bi, i, 0)),
        out_shape=jax.ShapeDtypeStruct((b, s, d), F32),
        compiler_params=_params("parallel", "parallel"),
        name="conv_res",
    )(proj, proj, proj, proj, proj, conv_w, conv_b.reshape(1, d), w_out, x, gate)


def _dil_kernel(q_ref, kc_ref, vc_ref, kp_ref, vp_ref, o_ref, l_ref, *, rows, hd, dilation):
    qb = DIL_QBLOCK
    n = pl.program_id(1)
    rho = pl.program_id(2)
    scale = hd ** -0.5
    fold_scale = (hd & (hd - 1)) == 0 and (hd.bit_length() - 1) % 2 == 0
    first = lax.broadcasted_iota(jnp.int32, (qb, LANES), 1) < hd
    row = lax.broadcasted_iota(jnp.int32, (qb, qb), 0)
    col = lax.broadcasted_iota(jnp.int32, (qb, qb), 1)
    in_prev = col >= row
    in_cur = col <= row
    neg = -jnp.inf

    tiles = []
    for s in range(rows // qb):
        t = slice(s * qb, (s + 1) * qb)
        tp = slice((s - 1) * qb, s * qb)
        prev_mask = jnp.logical_and(in_prev, n > 0) if s == 0 else in_prev
        for p in range(q_ref.shape[-1] // LANES):
            c = slice(p * LANES, (p + 1) * LANES)
            q = q_ref[0, t, c]
            if fold_scale:
                q = (q.astype(F32) * scale).astype(BF16)
            zero = jnp.zeros_like(q)
            q_heads = (jnp.where(first, q, zero), jnp.where(first, zero, q))
            if s == 0:
                k_prev, v_prev = kp_ref[0, :, c], vp_ref[0, :, c]
            else:
                k_prev, v_prev = kc_ref[0, tp, c], vc_ref[0, tp, c]
            tiles.append((s, p, q_heads, k_prev, kc_ref[0, t, c], v_prev, vc_ref[0, t, c], prev_mask))

    scores = []
    for _, _, q_heads, k_prev, k_cur, _, _, prev_mask in tiles:
        pair = []
        for h in range(2):
            z_prev, z_cur = _dot_nt(q_heads[h], k_prev), _dot_nt(q_heads[h], k_cur)
            if not fold_scale:
                z_prev, z_cur = z_prev * scale, z_cur * scale
            pair.append((jnp.where(prev_mask, z_prev, neg), jnp.where(in_cur, z_cur, neg)))
        scores.append(pair)

    probs = []
    for pair in scores:
        stats = []
        for z_prev, z_cur in pair:
            m = jnp.max(jnp.maximum(z_prev, z_cur), axis=1, keepdims=True)
            p_prev = jnp.exp(z_prev - m)
            p_cur = jnp.exp(z_cur - m)
            den = jnp.sum(p_prev + p_cur, axis=1, keepdims=True)
            stats.append((p_prev.astype(BF16), p_cur.astype(BF16), den, m))
        probs.append(stats)

    for (s, p, _, _, _, v_prev, v_cur, _), stats in zip(tiles, probs):
        outs, lses = [], []
        for p_prev, p_cur, den, m in stats:
            outs.append((_dot(p_prev, v_prev) + _dot(p_cur, v_cur)) / den)
            lses.append(jnp.broadcast_to(m + jnp.log(den), (qb, LANES)))
        if dilation == 1:
            t_out = slice(s * qb, (s + 1) * qb)
        else:
            t_out = pl.ds(s * qb * dilation + rho, qb, stride=dilation)
        o_ref[0, p, t_out, :] = jnp.where(first, outs[0], outs[1])
        l_ref[0, p, t_out, :] = jnp.where(first, lses[0], lses[1])


def _dil_group(view, dilation, window, seq):
    b, length, _ = view.shape
    width = DIL_HEADS * DIL_HEAD_DIM
    assert window == dilation * DIL_QBLOCK and seq % (dilation * DIL_QBLOCK) == 0
    rows = min(max(DIL_TILE_TOKENS // dilation, DIL_QBLOCK), 512, length)
    prev_blocks = rows // DIL_QBLOCK

    def cur(part):
        return lambda bi, i, rho: (bi, i, part * dilation + rho)

    def prev(part):
        return lambda bi, i, rho: (bi, jnp.maximum(i * prev_blocks - 1, 0), part * dilation + rho)

    pairs = width // LANES
    out_shape = jax.ShapeDtypeStruct((b, pairs, seq, LANES), F32)
    out_spec = pl.BlockSpec((1, pairs, rows * dilation, LANES), lambda bi, i, rho: (bi, 0, i, 0))
    return pl.pallas_call(
        functools.partial(_dil_kernel, rows=rows, hd=DIL_HEAD_DIM, dilation=dilation),
        grid=(b, length // rows, dilation),
        in_specs=[
            pl.BlockSpec((1, rows, width), cur(0)),
            pl.BlockSpec((1, rows, width), cur(1)),
            pl.BlockSpec((1, rows, width), cur(2)),
            pl.BlockSpec((1, DIL_QBLOCK, width), prev(1)),
            pl.BlockSpec((1, DIL_QBLOCK, width), prev(2)),
        ],
        out_specs=[out_spec, out_spec],
        out_shape=[out_shape, out_shape],
        compiler_params=_params("parallel", "parallel", "arbitrary"),
        name=f"dil_attn_r{dilation}",
    )(view, view, view, view, view)


def _dil_res_kernel(o0, o1, o2, l0, l1, l2, w_ref, x_ref, g_ref, y_ref):
    parts = []
    for p in range(o0.shape[1]):
        ls = (l0[0, p], l1[0, p], l2[0, p])
        m = jnp.maximum(jnp.maximum(ls[0], ls[1]), ls[2])
        es = [jnp.exp(l - m) for l in ls]
        parts.append((es[0] * o0[0, p] + es[1] * o1[0, p] + es[2] * o2[0, p]) / (es[0] + es[1] + es[2]))
    o = jnp.concatenate(parts, axis=1)
    y_ref[0] = x_ref[0] + g_ref[0] * _dot(o.astype(BF16), w_ref[...])


def _dil_res(outs, lses, w_out, x, gate, tm=512):
    b, s, d = x.shape
    k = w_out.shape[0]
    tm = min(tm, s)
    part = pl.BlockSpec((1, k // LANES, tm, LANES), lambda bi, i: (bi, 0, i, 0))
    return pl.pallas_call(
        _dil_res_kernel,
        grid=(b, s // tm),
        in_specs=[part] * 6 + [
            pl.BlockSpec((k, d), lambda bi, i: (0, 0)),
            pl.BlockSpec((1, tm, d), lambda bi, i: (bi, i, 0)),
            pl.BlockSpec((1, 1, d), lambda bi, i: (bi, 0, 0)),
        ],
        out_specs=pl.BlockSpec((1, tm, d), lambda bi, i: (bi, i, 0)),
        out_shape=jax.ShapeDtypeStruct((b, s, d), F32),
        compiler_params=_params("parallel", "parallel"),
        name="dil_res",
    )(*outs, *lses, w_out, x, gate)


def _router_kernel(x_ref, g_ref, sc_ref, sh_ref, wr_ref, br_ref, h_ref, route_ref, cnt_ref, run_ref):
    @pl.when(jnp.logical_and(pl.program_id(0) == 0, pl.program_id(1) == 0))
    def _():
        run_ref[...] = jnp.zeros_like(run_ref)

    h = _mod_norm(x_ref[0], g_ref[...], sc_ref[0], sh_ref[0])
    h_ref[...] = h
    h_hi, h_lo = _split_bf16(h)
    w_hi, w_lo = _split_bf16(wr_ref[...])
    logits = _dot(h_hi, w_hi) + (_dot(h_hi, w_lo) + _dot(h_lo, w_hi)) + br_ref[...]
    tm = logits.shape[0]
    lane = lax.broadcasted_iota(jnp.int32, logits.shape, 1).astype(F32)
    neg = -jnp.inf
    far = float(LANES)

    def first_argmax(vals):
        top = jnp.max(vals, axis=1, keepdims=True)
        return top, jnp.min(jnp.where(vals == top, lane, far), axis=1, keepdims=True)

    grp_logits = jnp.where(lane < MOE_GROUPS, logits, neg)
    grp_max, grp = first_argmax(grp_logits)
    p_grp = 1.0 / jnp.sum(jnp.exp(grp_logits - grp_max), axis=1, keepdims=True)
    base = MOE_GROUPS + grp * MOE_PER_GROUP
    in_grp = jnp.logical_and(lane >= base, lane < base + MOE_PER_GROUP)
    exp_logits = jnp.where(in_grp, logits, neg)
    m1, i1 = first_argmax(exp_logits)
    m2, i2 = first_argmax(jnp.where(lane == i1, neg, exp_logits))
    e2 = jnp.exp(m2 - m1)
    g1 = p_grp / (1.0 + e2)
    g2 = g1 * e2
    e_a = i1 - MOE_GROUPS
    e_b = i2 - MOE_GROUPS

    pick_a = lane == e_a
    pick_b = lane == e_b
    both = jnp.where(jnp.logical_or(pick_a, pick_b), 1.0, 0.0)
    row = lax.broadcasted_iota(jnp.int32, (tm, tm), 0)
    col = lax.broadcasted_iota(jnp.int32, (tm, tm), 1)
    earlier = jnp.where(col < row, 1.0, 0.0).astype(BF16)
    before = _dot(earlier, both.astype(BF16)) + run_ref[...]
    rank_a = jnp.sum(jnp.where(pick_a, before, 0.0), axis=1, keepdims=True)
    rank_b = jnp.sum(jnp.where(pick_b, before, 0.0), axis=1, keepdims=True)
    run_ref[...] += jnp.sum(both, axis=0, keepdims=True)
    cnt_ref[...] = run_ref[...]

    route = jnp.where(lane == 0, e_a, jnp.where(lane == 1, e_b, 0.0))
    route = jnp.where(lane == 2, g1, jnp.where(lane == 3, g2, route))
    route_ref[...] = jnp.where(lane == 4, rank_a, jnp.where(lane == 5, rank_b, route))


def _router(x, g, sc, sh, w_route, b_route, tm=512):
    b, s, d = x.shape
    tm = min(tm, s)
    steps = s // tm
    return pl.pallas_call(
        _router_kernel,
        grid=(b, steps),
        in_specs=[
            pl.BlockSpec((1, tm, d), lambda bi, i: (bi, i, 0)),
            pl.BlockSpec((1, d), lambda bi, i: (0, 0)),
            pl.BlockSpec((1, 1, d), lambda bi, i: (bi, 0, 0)),
            pl.BlockSpec((1, 1, d), lambda bi, i: (bi, 0, 0)),
            pl.BlockSpec((d, LANES), lambda bi, i: (0, 0)),
            pl.BlockSpec((1, LANES), lambda bi, i: (0, 0)),
        ],
        out_specs=[
            pl.BlockSpec((tm, d), lambda bi, i: (bi * steps + i, 0)),
            pl.BlockSpec((tm, LANES), lambda bi, i: (bi * steps + i, 0)),
            pl.BlockSpec((1, LANES), lambda bi, i: (0, 0)),
        ],
        out_shape=[
            jax.ShapeDtypeStruct((b * s, d), F32),
            jax.ShapeDtypeStruct((b * s, LANES), F32),
            jax.ShapeDtypeStruct((1, LANES), F32),
        ],
        scratch_shapes=[pltpu.VMEM((1, LANES), F32)],
        compiler_params=_params("arbitrary", "arbitrary"),
        name="moe_router",
    )(x, g.reshape(1, d), sc, sh, w_route, b_route)


def _copy_idx(src_ref, idx_smem, sem):
    cp = pltpu.make_async_copy(src_ref.at[0, 0], idx_smem, sem)
    cp.start()
    cp.wait()


def _dispatch_kernel(tail_ref, used_ref, dest_ref, h_ref, slots, idx_smem, zeros, sem_idx, sem_rows, sem_zero, *, nb):
    tm = h_ref.shape[0]
    bm = zeros.shape[0]

    @pl.when(pl.program_id(0) == 0)
    def _():
        zeros[...] = jnp.zeros_like(zeros)

        def clear(blk):
            return pltpu.make_async_copy(zeros, slots.at[pl.ds(blk * bm, bm)], sem_zero)

        def each_tail(fn):
            def body(e, carry):
                @pl.when(tail_ref[e] >= 0)
                def _():
                    fn(clear(tail_ref[e]))
                return carry
            lax.fori_loop(0, tail_ref.shape[0], body, 0)

        def each_unused(fn):
            def body(blk, carry):
                fn(clear(blk))
                return carry
            lax.fori_loop(used_ref[0], nb, body, 0)

        each_tail(lambda cp: cp.start())
        each_unused(lambda cp: cp.start())
        each_tail(lambda cp: cp.wait())
        each_unused(lambda cp: cp.wait())

    _copy_idx(dest_ref, idx_smem, sem_idx)

    def issue(r, carry):
        for k in range(MOE_TOPK):
            dst = slots.at[pl.ds(idx_smem[k * tm + r], 1)]
            pltpu.make_async_copy(h_ref.at[pl.ds(r, 1)], dst, sem_rows).start(priority=k)
        return carry

    lax.fori_loop(0, tm, issue, 0, unroll=8)
    for _ in range(MOE_TOPK):
        pltpu.make_async_copy(h_ref, slots.at[pl.ds(0, tm)], sem_rows).wait()


def _dispatch(tail_blk, used, dest, h_rows, nb, bm, tm):
    t, d = h_rows.shape
    grid_spec = pltpu.PrefetchScalarGridSpec(
        num_scalar_prefetch=2,
        grid=(t // tm,),
        in_specs=[
            pl.BlockSpec((1, 1, MOE_TOPK * tm), lambda i, tail, n: (i, 0, 0)),
            pl.BlockSpec((tm, d), lambda i, tail, n: (i, 0)),
        ],
        out_specs=pl.BlockSpec(memory_space=pl.ANY),
        scratch_shapes=[
            pltpu.SMEM((MOE_TOPK * tm,), jnp.int32),
            pltpu.VMEM((bm, d), F32),
            pltpu.SemaphoreType.DMA,
            pltpu.SemaphoreType.DMA,
            pltpu.SemaphoreType.DMA,
        ],
    )
    return pl.pallas_call(
        functools.partial(_dispatch_kernel, nb=nb),
        grid_spec=grid_spec,
        out_shape=jax.ShapeDtypeStruct((nb * bm, d), F32),
        compiler_params=_params("arbitrary"),
        name="moe_dispatch",
    )(tail_blk, used, dest, h_rows)


def _expert_kernel(blk_e_ref, used_ref, x_ref, wg_ref, wu_ref, wd_ref, y_ref):
    del blk_e_ref
    live = pl.program_id(0) < used_ref[0]

    @pl.when(live)
    def _():
        x = x_ref[...].astype(BF16)
        gate = _dot(x, wg_ref[0])
        up = _dot(x, wu_ref[0])
        act = (gate * _sigmoid(gate) * up).astype(BF16)
        y_ref[...] = _dot(act, wd_ref[0])

    @pl.when(jnp.logical_not(live))
    def _():
        y_ref[...] = jnp.zeros_like(y_ref)


def _expert_blocks(blk_e, used, slots, w_gate, w_up, w_down, bm):
    nb = blk_e.shape[0]
    _, d, f = w_gate.shape

    def block(i, e, n):
        return (jnp.minimum(i, n[0] - 1), 0)

    def weight(i, e, n):
        return (e[jnp.minimum(i, n[0] - 1)], 0, 0)

    grid_spec = pltpu.PrefetchScalarGridSpec(
        num_scalar_prefetch=2,
        grid=(nb,),
        in_specs=[
            pl.BlockSpec((bm, d), block),
            pl.BlockSpec((1, d, f), weight),
            pl.BlockSpec((1, d, f), weight),
            pl.BlockSpec((1, f, d), weight),
        ],
        out_specs=pl.BlockSpec((bm, d), lambda i, e, n: (i, 0)),
    )
    return pl.pallas_call(
        _expert_kernel,
        grid_spec=grid_spec,
        out_shape=jax.ShapeDtypeStruct((nb * bm, d), F32),
        compiler_params=_params("arbitrary"),
        name="moe_experts",
    )(blk_e, used, slots, w_gate, w_up, w_down)


def _gather_rows(idx_smem, offset, count, src_hbm, dst, sem):
    def issue(pair, carry):
        for lane in range(2):
            r = 2 * pair + lane
            src = src_hbm.at[pl.ds(idx_smem[offset + r], 1)]
            pltpu.make_async_copy(src, dst.at[pl.ds(r, 1)], sem).start(priority=lane)
        return carry

    lax.fori_loop(0, count // 2, issue, 0, unroll=4)


def _wait_rows(count, src_hbm, dst, sem):
    pltpu.make_async_copy(src_hbm.at[pl.ds(0, count)], dst, sem).wait()


def _combine_kernel(dest_cur, dest_nxt, route_ref, y_hbm, x_ref, g_ref, fg_ref, o_ref,
                    idx_even, idx_odd, buf, sem_idx, sem_rows, *, final):
    tm = buf.shape[2]
    i = pl.program_id(0)
    idx_smem = (idx_even, idx_odd)

    def fetch(dest_ref, sl):
        _copy_idx(dest_ref, idx_smem[sl], sem_idx)
        for k in range(MOE_TOPK):
            _gather_rows(idx_smem[sl], k * tm, tm, y_hbm, buf.at[sl, k], sem_rows.at[sl])

    def run(slot):
        if slot == 0:
            @pl.when(i == 0)
            def _():
                fetch(dest_cur, 0)

        @pl.when(i + 1 < pl.num_programs(0))
        def _():
            fetch(dest_nxt, 1 - slot)

        for k in range(MOE_TOPK):
            _wait_rows(tm, y_hbm, buf.at[slot, k], sem_rows.at[slot])
        route = route_ref[...]
        y = route[:, 2:3] * buf[slot, 0] + route[:, 3:4] * buf[slot, 1]
        out = x_ref[...] + g_ref[0] * y
        if final:
            out = out * lax.rsqrt(jnp.mean(out * out, axis=-1, keepdims=True) + RMS_EPS) * fg_ref[...]
        o_ref[...] = out

    for slot in range(2):
        pl.when(lax.rem(i, 2) == slot)(functools.partial(run, slot))


def _combine(dest, route, y_rows, x, gate, final_g, final, tm):
    b, s, d = x.shape
    t = b * s
    per_seq = s // tm
    steps = t // tm
    out = pl.pallas_call(
        functools.partial(_combine_kernel, final=final),
        grid=(steps,),
        in_specs=[
            pl.BlockSpec((1, 1, MOE_TOPK * tm), lambda i: (i, 0, 0)),
            pl.BlockSpec((1, 1, MOE_TOPK * tm), lambda i: (jnp.minimum(i + 1, steps - 1), 0, 0)),
            pl.BlockSpec((tm, LANES), lambda i: (i, 0)),
            pl.BlockSpec(memory_space=pl.ANY),
            pl.BlockSpec((tm, d), lambda i: (i, 0)),
            pl.BlockSpec((1, 1, d), lambda i: (i // per_seq, 0, 0)),
            pl.BlockSpec((1, d), lambda i: (0, 0)),
        ],
        out_specs=pl.BlockSpec((tm, d), lambda i: (i, 0)),
        out_shape=jax.ShapeDtypeStruct((t, d), F32),
        scratch_shapes=[
            pltpu.SMEM((MOE_TOPK * tm,), jnp.int32),
            pltpu.SMEM((MOE_TOPK * tm,), jnp.int32),
            pltpu.VMEM((2, MOE_TOPK, tm, d), F32),
            pltpu.SemaphoreType.DMA,
            pltpu.SemaphoreType.DMA((2,)),
        ],
        compiler_params=_params("arbitrary"),
        name="moe_combine",
    )(dest, dest, route, y_rows, x.reshape(t, d), gate, final_g.reshape(1, d))
    return out.reshape(b, s, d)


def _slot_tables(route, counts, bm, tm):
    t = route.shape[0]
    nb = MOE_TOPK * t // bm + MOE_EXPERTS
    counts = counts[0, :MOE_EXPERTS].astype(jnp.int32)
    padded = (counts + bm - 1) // bm * bm
    pad_end = jnp.cumsum(padded)
    pad_start = pad_end - padded
    expert = route[:, 0:MOE_TOPK].astype(jnp.int32)
    rank = route[:, 4:4 + MOE_TOPK].astype(jnp.int32)
    dest = pad_start[expert] + rank
    dest_tiles = dest.reshape(t // tm, tm, MOE_TOPK).transpose(0, 2, 1).reshape(t // tm, 1, MOE_TOPK * tm)
    blk_start = jnp.arange(nb, dtype=jnp.int32) * bm
    blk_e = jnp.minimum(jnp.sum((pad_end[None, :] <= blk_start[:, None]).astype(jnp.int32), axis=1), MOE_EXPERTS - 1)
    used = (pad_end[-1:] // bm).astype(jnp.int32)
    tail_blk = jnp.where(counts % bm != 0, pad_end // bm - 1, -1).astype(jnp.int32)
    return dest_tiles, blk_e, used, tail_blk, nb


def _moe_layer(x, g, sc, sh, gate, w_route, b_route, w_gate, w_up, w_down, final_g, final, bm=512, tm=256):
    b, s, d = x.shape
    tm = min(tm, s)
    h_rows, route, counts = _router(x, g, sc, sh, w_route, b_route)
    dest, blk_e, used, tail_blk, nb = _slot_tables(route, counts, bm, tm)
    slots = _dispatch(tail_blk, used, dest, h_rows, nb, bm, tm)
    y_rows = _expert_blocks(blk_e, used, slots, w_gate, w_up, w_down, bm)
    return _combine(dest, route, y_rows, x, gate, final_g, final, tm)


def kernel(x, c, ada_w, ada_b, norm_g, final_g, sb_w_in, sb_w_out, gla_w_in, gla_w_gate_up, gla_b_gate, gla_norm_g, gla_w_out, conv_w_in, conv_w, conv_b, conv_w_out, dil_w_in, dil_w_out, moe_w_grp, moe_b_grp, moe_w_exp, moe_b_exp, moe_w_gate, moe_w_up, moe_w_down):
    b, s, d = x.shape
    depth = ada_w.shape[0]
    mod = _ada_mod(c, ada_w, ada_b).reshape(depth, b, N_MOD, 1, d)
    for i in range(depth):
        sh1, sc1, g1, sh2, sc2, g2 = (mod[i, :, m] for m in range(N_MOD))
        kind, j = i % 4, i // 4
        if kind == 0:
            qkv = _norm_proj(x, norm_g[i, 0], sc1, sh1, sb_w_in[j].astype(BF16))
            x = _proj_res(_sb_attention(qkv), sb_w_out[j].astype(BF16), x, g1)
        elif kind == 1:
            dk = gla_w_gate_up.shape[-1] // GLA_HEADS
            dv = gla_norm_g.shape[-1]
            w_in = jnp.pad(gla_w_in[j], ((0, 0), (0, LANES - GLA_GATE_RANK))).astype(BF16)
            w_up = jnp.pad(gla_w_gate_up[j], ((0, LANES - GLA_GATE_RANK), (0, 0))).astype(BF16)
            proj = _norm_proj(x, norm_g[i, 0], sc1, sh1, w_in, tn=640)
            o = _gla_core(proj, w_up, gla_b_gate[j].reshape(1, -1), gla_norm_g[j].reshape(1, -1), dk, dv)
            x = _proj_res(o, gla_w_out[j].astype(BF16), x, g1)
        elif kind == 2:
            proj = _norm_proj(x, norm_g[i, 0], sc1, sh1, conv_w_in[j].astype(BF16))
            x = _conv_res(proj, conv_w[j], conv_b[j], conv_w_out[j].astype(BF16), x, g1)
        else:
            group_w = 3 * DIL_HEADS * DIL_HEAD_DIM
            outs, lses = [], []
            for g, (window, dilation) in enumerate(DIL_PATTERNS):
                w_g = dil_w_in[j][:, g * group_w:(g + 1) * group_w].astype(BF16)
                view = _norm_proj(x, norm_g[i, 0], sc1, sh1, w_g, dilation=dilation)
                o_g, l_g = _dil_group(view, dilation, window, s)
                outs.append(o_g)
                lses.append(l_g)
            x = _dil_res(outs, lses, dil_w_out[j].astype(BF16), x, g1)
        w_route = jnp.pad(jnp.concatenate([moe_w_grp[i], moe_w_exp[i]], axis=1),
                          ((0, 0), (0, LANES - MOE_GROUPS - MOE_EXPERTS)))
        b_route = jnp.pad(jnp.concatenate([moe_b_grp[i], moe_b_exp[i]]), (0, LANES - MOE_GROUPS - MOE_EXPERTS))
        x = _moe_layer(x, norm_g[i, 1], sc2, sh2, g2, w_route, b_route.reshape(1, LANES),
                       moe_w_gate[i].astype(BF16), moe_w_up[i].astype(BF16), moe_w_down[i].astype(BF16),
                       final_g, final=(i == depth - 1))
    return x
```

```python
import functools

import jax
import jax.numpy as jnp
from jax import lax
from jax.experimental import pallas as pl
from jax.experimental.pallas import tpu as pltpu
from jax.experimental.pallas import tpu_sc as plsc

F32 = jnp.float32
BF16 = jnp.bfloat16

LANES = 128
RMS_EPS = 1e-6
N_MOD = 6

SB_HEADS = 16
GLA_HEADS = 4
GLA_GATE_RANK = 16
GLA_GATE_TAU = 16.0
GLA_CHUNK = 64
CONV_WIDTH = 3
DIL_PATTERNS = ((128, 1), (512, 4), (2048, 16))
DIL_HEADS = 8
DIL_HEAD_DIM = 64
DIL_QBLOCK = 128
DIL_TILE_TOKENS = 2048
MOE_GROUPS = 4
MOE_PER_GROUP = 8
MOE_EXPERTS = MOE_GROUPS * MOE_PER_GROUP
MOE_TOPK = 2

SC_CORES = 2
SC_SUBCORES = 16
SC_WINDOW = 32

SB_LOG_ZERO = -104.0
SB_BAND = 2

VMEM_LIMIT = 48 * 1024 * 1024


def _params(*sem):
    return pltpu.CompilerParams(dimension_semantics=sem, vmem_limit_bytes=VMEM_LIMIT)


def _sigmoid(x):
    return 1.0 / (1.0 + jnp.exp(-x))


def _softplus(x):
    return jnp.maximum(x, 0.0) + jnp.log(1.0 + jnp.exp(-jnp.abs(x)))


def _split_bf16(x):
    hi = x.astype(BF16)
    lo = (x - hi.astype(F32)).astype(BF16)
    return hi, lo


def _dot(a, b):
    return jnp.dot(a, b, preferred_element_type=F32)


def _dot_nt(a, b):
    return lax.dot_general(a, b, (((1,), (1,)), ((), ())), preferred_element_type=F32)


def _dot_tn(a, b):
    return lax.dot_general(a, b, (((0,), (0,)), ((), ())), preferred_element_type=F32)


def _mod_norm(x, g, sc, sh):
    r = lax.rsqrt(jnp.mean(x * x, axis=-1, keepdims=True) + RMS_EPS)
    return (x * r) * g * (1.0 + sc) + sh


def _ada_kernel(c_ref, w_ref, b_ref, o_ref):
    c = c_ref[...]
    cond = c * _sigmoid(c)
    o_ref[0] = _dot(cond.astype(BF16), w_ref[0].astype(BF16)) + b_ref[0]


def _ada_mod(c, ada_w, ada_b):
    depth, d, n = ada_w.shape
    b = c.shape[0]
    tn = 1536
    return pl.pallas_call(
        _ada_kernel,
        grid=(depth, n // tn),
        in_specs=[
            pl.BlockSpec((b, d), lambda i, j: (0, 0)),
            pl.BlockSpec((1, d, tn), lambda i, j: (i, 0, j)),
            pl.BlockSpec((1, 1, tn), lambda i, j: (i, 0, j)),
        ],
        out_specs=pl.BlockSpec((1, b, tn), lambda i, j: (i, 0, j)),
        out_shape=jax.ShapeDtypeStruct((depth, b, n), F32),
        compiler_params=_params("parallel", "parallel"),
        name="ada_mod",
    )(c, ada_w, ada_b.reshape(depth, 1, n))


def _norm_proj_kernel(x_ref, g_ref, sc_ref, sh_ref, w_ref, o_ref, h_ref, *res_ref, dilation, tn):
    h_ref[...] = _mod_norm(x_ref[0], g_ref[...], sc_ref[0], sh_ref[0]).astype(BF16)
    tm = h_ref.shape[0]
    for j in range(w_ref.shape[1] // tn):
        res = _dot(h_ref[...], w_ref[:, j * tn:(j + 1) * tn])
        if dilation == 1:
            o_ref[0, :, j * tn:(j + 1) * tn] = res.astype(o_ref.dtype)
            continue
        for c in range(tn // LANES):
            res_ref[0][j * (tn // LANES) + c] = res[:, c * LANES:(c + 1) * LANES]
        for rho in range(dilation):
            for c in range(tn // LANES):
                rows = res_ref[0][j * (tn // LANES) + c, pl.ds(rho, tm // dilation, stride=dilation), :]
                col = (j * dilation + rho) * tn + c * LANES
                o_ref[0, :, col:col + LANES] = rows.astype(o_ref.dtype)


def _norm_proj(x, g, sc, sh, w, tm=512, tn=512, dilation=1):
    b, s, d = x.shape
    n = w.shape[1]
    tm = min(tm, s)
    assert s % tm == 0 and n % tn == 0 and tm % (16 * dilation) == 0
    scratch = [pltpu.VMEM((tm, d), BF16)]
    if dilation > 1:
        scratch.append(pltpu.VMEM((n // LANES, tm, LANES), F32))
    return pl.pallas_call(
        functools.partial(_norm_proj_kernel, dilation=dilation, tn=tn),
        grid=(b, s // tm),
        in_specs=[
            pl.BlockSpec((1, tm, d), lambda bi, i: (bi, i, 0)),
            pl.BlockSpec((1, d), lambda bi, i: (0, 0)),
            pl.BlockSpec((1, 1, d), lambda bi, i: (bi, 0, 0)),
            pl.BlockSpec((1, 1, d), lambda bi, i: (bi, 0, 0)),
            pl.BlockSpec((d, n), lambda bi, i: (0, 0)),
        ],
        out_specs=pl.BlockSpec((1, tm // dilation, dilation * n), lambda bi, i: (bi, i, 0)),
        out_shape=jax.ShapeDtypeStruct((b, s // dilation, dilation * n), BF16),
        scratch_shapes=scratch,
        compiler_params=_params("parallel", "parallel"),
        name="norm_proj",
    )(x, g.reshape(1, d), sc, sh, w)


def _proj_res_kernel(o_ref, w_ref, x_ref, g_ref, y_ref):
    y_ref[0] = x_ref[0] + g_ref[0] * _dot(o_ref[0], w_ref[...])


def _proj_res(o, w, x, gate, tm=512):
    b, s, d = x.shape
    k = o.shape[-1]
    tm = min(tm, s)
    return pl.pallas_call(
        _proj_res_kernel,
        grid=(b, s // tm),
        in_specs=[
            pl.BlockSpec((1, tm, k), lambda bi, i: (bi, i, 0)),
            pl.BlockSpec((k, d), lambda bi, i: (0, 0)),
            pl.BlockSpec((1, tm, d), lambda bi, i: (bi, i, 0)),
            pl.BlockSpec((1, 1, d), lambda bi, i: (bi, 0, 0)),
        ],
        out_specs=pl.BlockSpec((1, tm, d), lambda bi, i: (bi, i, 0)),
        out_shape=jax.ShapeDtypeStruct((b, s, d), F32),
        compiler_params=_params("parallel", "parallel"),
        name="proj_res",
    )(o, w, x, gate)


def _sb_kernel(q_ref, k_ref, v_ref, o_ref, acc_ref, c_ref, *, blk, hd):
    scale = hd ** -0.5
    fold_scale = (hd & (hd - 1)) == 0 and (hd.bit_length() - 1) % 2 == 0
    first = lax.broadcasted_iota(jnp.int32, (blk, LANES), 1) < hd
    row = lax.broadcasted_iota(jnp.int32, (blk, blk), 0)
    col = lax.broadcasted_iota(jnp.int32, (blk, blk), 1)
    minus_later = jnp.where(row > col, -1.0, 0.0).astype(BF16)
    causal = col < row
    subs = q_ref.shape[1] // blk
    first_block = pl.program_id(2) * subs

    def visit_many(work):
        kv = []
        for _, block, _, _ in work:
            start = pl.multiple_of(block * blk, blk)
            kv.append((k_ref[0, pl.ds(start, blk), :], v_ref[0, pl.ds(start, blk), :]))
        zs = [[_dot_nt(q_heads[h], kb) for h in range(2)] for (q_heads, _, _, _), (kb, _) in zip(work, kv)]
        log_betas, sps = [], []
        for (_, _, mask, _), z2 in zip(work, zs):
            lb2, sp2 = [], []
            for z in z2:
                if not fold_scale:
                    z = z * scale
                sp = _softplus(z)
                lb2.append(z - sp)
                sp2.append(sp if mask is None else jnp.where(mask, sp, 0.0))
            log_betas.append(lb2)
            sps.append(sp2)
        suffixes = [[_dot(sp.astype(BF16), minus_later) for sp in sp2] for sp2 in sps]
        return kv, log_betas, sps, suffixes

    def finish(work, staged, c_in):
        kv, log_betas, sps, suffixes = staged
        cs = {chain: list(c) for chain, c in c_in.items()}
        weights = []
        for n, (_, _, mask, chain) in enumerate(work):
            a2 = []
            for h in range(2):
                a = jnp.exp(log_betas[n][h] + suffixes[n][h] + cs[chain][h])
                a2.append(a if mask is None else jnp.where(mask, a, 0.0))
                cs[chain][h] = cs[chain][h] - jnp.sum(sps[n][h], axis=1, keepdims=True)
            weights.append(a2)
        accs = {}
        for n, (_, _, _, chain) in enumerate(work):
            outs = [_dot(weights[n][h].astype(BF16), kv[n][1]) for h in range(2)]
            more = jnp.where(first, outs[0], outs[1])
            accs[chain] = more if chain not in accs else accs[chain] + more
        return accs, cs

    def remainder(sub, q_heads, depth):
        def cond(carry):
            j, cmax = carry
            return jnp.logical_and(j >= 0, cmax > SB_LOG_ZERO)

        def body(carry):
            j, _ = carry
            work = [(q_heads, j, None, sub)]
            accs, cs = finish(work, visit_many(work), {sub: [c_ref[sub, 0], c_ref[sub, 1]]})
            acc_ref[sub] += accs[sub]
            for h in range(2):
                c_ref[sub, h] = cs[sub][h]
            return j - 1, jnp.max(c_ref[sub])

        lax.while_loop(cond, body, (first_block + sub - depth - 1, jnp.max(c_ref[sub])))

    def run(depth):
        heads = []
        for sub in range(subs):
            q = q_ref[0, sub * blk:(sub + 1) * blk, :]
            if fold_scale:
                q = (q.astype(F32) * scale).astype(BF16)
            zero = jnp.zeros_like(q)
            heads.append((jnp.where(first, q, zero), jnp.where(first, zero, q)))
        work = [(heads[sub], first_block + sub - back, causal if back == 0 else None, sub)
                for sub in range(subs) for back in range(depth + 1)]
        zero_c = [jnp.zeros((blk, 1), F32)] * 2
        accs, cs = finish(work, visit_many(work), {sub: zero_c for sub in range(subs)})
        for sub in range(subs):
            acc_ref[sub] = accs[sub]
            for h in range(2):
                c_ref[sub, h] = cs[sub][h]
        for sub in range(subs):
            remainder(sub, heads[sub], depth)
            o_ref[0, sub * blk:(sub + 1) * blk, :] = acc_ref[sub].astype(o_ref.dtype)

    pl.when(first_block >= SB_BAND)(functools.partial(run, SB_BAND))
    pl.when(first_block < SB_BAND)(functools.partial(run, 0))


def _sb_attention(qkv, blk=128, rows=512):
    b, s, n = qkv.shape
    width = n // 3
    hd = width // SB_HEADS
    assert 2 * hd == LANES
    pairs = width // LANES
    rows = min(rows, s)
    return pl.pallas_call(
        functools.partial(_sb_kernel, blk=blk, hd=hd),
        grid=(b, pairs, s // rows),
        in_specs=[
            pl.BlockSpec((1, rows, LANES), lambda bi, p, i: (bi, i, p)),
            pl.BlockSpec((1, s, LANES), lambda bi, p, i: (bi, 0, pairs + p)),
            pl.BlockSpec((1, s, LANES), lambda bi, p, i: (bi, 0, 2 * pairs + p)),
        ],
        out_specs=pl.BlockSpec((1, rows, LANES), lambda bi, p, i: (bi, i, p)),
        out_shape=jax.ShapeDtypeStruct((b, s, width), BF16),
        scratch_shapes=[pltpu.VMEM((rows // blk, blk, LANES), F32), pltpu.VMEM((rows // blk, 2, blk, 1), F32)],
        compiler_params=_params("parallel", "parallel", "arbitrary"),
        name="sb_attention",
    )(qkv, qkv, qkv)


def _gla_kernel(p_ref, wg_ref, bg_ref, ng_ref, o_ref, st_ref, *, rows, dk, dv):
    @pl.when(pl.program_id(1) == 0)
    def _():
        st_ref[...] = jnp.zeros_like(st_ref)

    heads = GLA_HEADS
    ch = GLA_CHUNK
    k0 = heads * dk
    v0 = 2 * heads * dk
    r0 = v0 + heads * dv
    g0 = r0 + heads * dv
    row = lax.broadcasted_iota(jnp.int32, (ch, ch), 0)
    col = lax.broadcasted_iota(jnp.int32, (ch, ch), 1)
    causal = col <= row
    lower = jnp.where(causal, 1.0, 0.0).astype(BF16)
    chunks = [slice(s * ch, (s + 1) * ch) for s in range(rows // ch)]
    g_pres = [_dot(p_ref[0, t, g0:g0 + LANES], wg_ref[...]) + bg_ref[...] for t in chunks]
    cums = []
    for g_pre in g_pres:
        log_a = (jnp.minimum(g_pre, 0.0) - jnp.log(1.0 + jnp.exp(-jnp.abs(g_pre)))) * (1.0 / GLA_GATE_TAU)
        hi, lo = _split_bf16(log_a)
        cums.append(_dot(lower, hi) + _dot(lower, lo))
    parts = []
    for t, cum in zip(chunks, cums):
        for h in range(heads):
            bh = cum[:, h * dk:(h + 1) * dk]
            b_last = bh[ch - 1:ch, :]
            q = p_ref[0, t, h * dk:(h + 1) * dk].astype(F32) * dk ** -0.5
            k = p_ref[0, t, k0 + h * dk:k0 + (h + 1) * dk].astype(F32)
            q_in = (q * jnp.exp(bh)).astype(BF16)
            k_in = (k * jnp.exp(-bh)).astype(BF16)
            k_state = (k * jnp.exp(b_last - bh)).astype(BF16)
            parts.append((t, h, q_in, k_in, k_state, jnp.exp(b_last)))
    scores = [jnp.where(causal, _dot_nt(q_in, k_in), 0.0).astype(BF16) for _, _, q_in, k_in, _, _ in parts]
    values = [p_ref[0, t, v0 + h * dv:v0 + (h + 1) * dv] for t, h, _, _, _, _ in parts]
    intra = [_dot(sc, v) for sc, v in zip(scores, values)]
    updates = [_dot_tn(v, k_state) for v, (_, _, _, _, k_state, _) in zip(values, parts)]
    states = [st_ref[h] for h in range(heads)]
    for n, (t, h, q_in, _, _, decay) in enumerate(parts):
        o = intra[n] + _dot_nt(q_in, states[h].astype(BF16))
        states[h] = decay * states[h] + updates[n]
        o = o * lax.rsqrt(jnp.mean(o * o, axis=-1, keepdims=True) + RMS_EPS) * ng_ref[...]
        r = p_ref[0, t, r0 + h * dv:r0 + (h + 1) * dv].astype(F32)
        o_ref[0, t, h * dv:(h + 1) * dv] = (o * (r * _sigmoid(r))).astype(o_ref.dtype)
    for h in range(heads):
        st_ref[h] = states[h]


def _gla_core(proj, w_gate_up, b_gate, norm_g, dk, dv, rows=256):
    b, s, n = proj.shape
    rows = min(rows, s)
    hk = GLA_HEADS * dk
    return pl.pallas_call(
        functools.partial(_gla_kernel, rows=rows, dk=dk, dv=dv),
        grid=(b, s // rows),
        in_specs=[
            pl.BlockSpec((1, rows, n), lambda bi, i: (bi, i, 0)),
            pl.BlockSpec((LANES, hk), lambda bi, i: (0, 0)),
            pl.BlockSpec((1, hk), lambda bi, i: (0, 0)),
            pl.BlockSpec((1, dv), lambda bi, i: (0, 0)),
        ],
        out_specs=pl.BlockSpec((1, rows, GLA_HEADS * dv), lambda bi, i: (bi, i, 0)),
        out_shape=jax.ShapeDtypeStruct((b, s, GLA_HEADS * dv), BF16),
        scratch_shapes=[pltpu.VMEM((GLA_HEADS, dv, dk), F32)],
        compiler_params=_params("parallel", "arbitrary"),
        name="gla_core",
    )(proj, w_gate_up, b_gate, norm_g)


def _conv_res_kernel(gb_ref, gc_ref, u_ref, hgc_ref, hu_ref, cw_ref, cb_ref, w_ref, x_ref, g_ref, y_ref):
    u2 = gc_ref[0].astype(F32) * u_ref[0].astype(F32)
    halo = hgc_ref[0].astype(F32) * hu_ref[0].astype(F32)
    halo = jnp.where(pl.program_id(1) > 0, halo, 0.0)
    rows = lax.broadcasted_iota(jnp.int32, u2.shape, 0)
    prev1 = jnp.where(rows == 0, halo[7:8], pltpu.roll(u2, 1, 0))
    prev2 = jnp.where(rows == 0, halo[6:7], jnp.where(rows == 1, halo[7:8], pltpu.roll(u2, 2, 0)))
    cw = cw_ref[...]
    y = cw[0:1] * prev2 + cw[1:2] * prev1 + cw[2:3] * u2 + cb_ref[...]
    o = (gb_ref[0].astype(F32) * y).astype(BF16)
    y_ref[0] = x_ref[0] + g_ref[0] * _dot(o, w_ref[...])


def _conv_res(proj, conv_w, conv_b, w_out, x, gate, tm=512):
    b, s, d = x.shape
    tm = min(tm, s)
    halo_blocks = tm // 8

    def halo_map(col):
        return lambda bi, i: (bi, jnp.maximum(i * halo_blocks - 1, 0), col)

    return pl.pallas_call(
        _conv_res_kernel,
        grid=(b, s // tm),
        in_specs=[
            pl.BlockSpec((1, tm, d), lambda bi, i: (bi, i, 0)),
            pl.BlockSpec((1, tm, d), lambda bi, i: (bi, i, 1)),
            pl.BlockSpec((1, tm, d), lambda bi, i: (bi, i, 2)),
            pl.BlockSpec((1, 8, d), halo_map(1)),
            pl.BlockSpec((1, 8, d), halo_map(2)),
            pl.BlockSpec((CONV_WIDTH, d), lambda bi, i: (0, 0)),
            pl.BlockSpec((1, d), lambda bi, i: (0, 0)),
            pl.BlockSpec((d, d), lambda bi, i: (0, 0)),
            pl.BlockSpec((1, tm, d), lambda bi, i: (bi, i, 0)),
            pl.BlockSpec((1, 1, d), lambda bi, i: (bi, 0, 0)),
        ],
        out_specs=pl.BlockSpec((1, tm, d), lambda bi, i: (bi, i, 0)),
        out_shape=jax.ShapeDtypeStruct((b, s, d), F32),
        compiler_params=_params("parallel", "parallel"),
        name="conv_res",
    )(proj, proj, proj, proj, proj, conv_w, conv_b.reshape(1, d), w_out, x, gate)


def _dil_kernel(q_ref, kc_ref, vc_ref, kp_ref, vp_ref, o_ref, l_ref, *, rows, hd, dilation):
    qb = DIL_QBLOCK
    n = pl.program_id(1)
    rho = pl.program_id(2)
    scale = hd ** -0.5
    fold_scale = (hd & (hd - 1)) == 0 and (hd.bit_length() - 1) % 2 == 0
    first = lax.broadcasted_iota(jnp.int32, (qb, LANES), 1) < hd
    row = lax.broadcasted_iota(jnp.int32, (qb, qb), 0)
    col = lax.broadcasted_iota(jnp.int32, (qb, qb), 1)
    in_prev = col >= row
    in_cur = col <= row
    neg = -jnp.inf

    tiles = []
    for s in range(rows // qb):
        t = slice(s * qb, (s + 1) * qb)
        tp = slice((s - 1) * qb, s * qb)
        prev_mask = jnp.logical_and(in_prev, n > 0) if s == 0 else in_prev
        for p in range(q_ref.shape[-1] // LANES):
            c = slice(p * LANES, (p + 1) * LANES)
            q = q_ref[0, t, c]
            if fold_scale:
                q = (q.astype(F32) * scale).astype(BF16)
            zero = jnp.zeros_like(q)
            q_heads = (jnp.where(first, q, zero), jnp.where(first, zero, q))
            if s == 0:
                k_prev, v_prev = kp_ref[0, :, c], vp_ref[0, :, c]
            else:
                k_prev, v_prev = kc_ref[0, tp, c], vc_ref[0, tp, c]
            tiles.append((s, p, q_heads, k_prev, kc_ref[0, t, c], v_prev, vc_ref[0, t, c], prev_mask))

    scores = []
    for _, _, q_heads, k_prev, k_cur, _, _, prev_mask in tiles:
        pair = []
        for h in range(2):
            z_prev, z_cur = _dot_nt(q_heads[h], k_prev), _dot_nt(q_heads[h], k_cur)
            if not fold_scale:
                z_prev, z_cur = z_prev * scale, z_cur * scale
            pair.append((jnp.where(prev_mask, z_prev, neg), jnp.where(in_cur, z_cur, neg)))
        scores.append(pair)

    probs = []
    for pair in scores:
        stats = []
        for z_prev, z_cur in pair:
            m = jnp.max(jnp.maximum(z_prev, z_cur), axis=1, keepdims=True)
            p_prev = jnp.exp(z_prev - m)
            p_cur = jnp.exp(z_cur - m)
            den = jnp.sum(p_prev + p_cur, axis=1, keepdims=True)
            stats.append((p_prev.astype(BF16), p_cur.astype(BF16), den, m))
        probs.append(stats)

    for (s, p, _, _, _, v_prev, v_cur, _), stats in zip(tiles, probs):
        outs, lses = [], []
        for p_prev, p_cur, den, m in stats:
            outs.append((_dot(p_prev, v_prev) + _dot(p_cur, v_cur)) / den)
            lses.append(jnp.broadcast_to(m + jnp.log(den), (qb, LANES)))
        if dilation == 1:
            t_out = slice(s * qb, (s + 1) * qb)
        else:
            t_out = pl.ds(s * qb * dilation + rho, qb, stride=dilation)
        o_ref[0, p, t_out, :] = jnp.where(first, outs[0], outs[1])
        l_ref[0, p, t_out, :] = jnp.where(first, lses[0], lses[1])


def _dil_group(view, dilation, window, seq):
    b, length, _ = view.shape
    width = DIL_HEADS * DIL_HEAD_DIM
    assert window == dilation * DIL_QBLOCK and seq % (dilation * DIL_QBLOCK) == 0
    rows = min(max(DIL_TILE_TOKENS // dilation, DIL_QBLOCK), 512, length)
    prev_blocks = rows // DIL_QBLOCK

    def cur(part):
        return lambda bi, i, rho: (bi, i, part * dilation + rho)

    def prev(part):
        return lambda bi, i, rho: (bi, jnp.maximum(i * prev_blocks - 1, 0), part * dilation + rho)

    pairs = width // LANES
    out_shape = jax.ShapeDtypeStruct((b, pairs, seq, LANES), F32)
    out_spec = pl.BlockSpec((1, pairs, rows * dilation, LANES), lambda bi, i, rho: (bi, 0, i, 0))
    return pl.pallas_call(
        functools.partial(_dil_kernel, rows=rows, hd=DIL_HEAD_DIM, dilation=dilation),
        grid=(b, length // rows, dilation),
        in_specs=[
            pl.BlockSpec((1, rows, width), cur(0)),
            pl.BlockSpec((1, rows, width), cur(1)),
            pl.BlockSpec((1, rows, width), cur(2)),
            pl.BlockSpec((1, DIL_QBLOCK, width), prev(1)),
            pl.BlockSpec((1, DIL_QBLOCK, width), prev(2)),
        ],
        out_specs=[out_spec, out_spec],
        out_shape=[out_shape, out_shape],
        compiler_params=_params("parallel", "parallel", "arbitrary"),
        name=f"dil_attn_r{dilation}",
    )(view, view, view, view, view)


def _dil_res_kernel(o0, o1, o2, l0, l1, l2, w_ref, x_ref, g_ref, y_ref):
    parts = []
    for p in range(o0.shape[1]):
        ls = (l0[0, p], l1[0, p], l2[0, p])
        m = jnp.maximum(jnp.maximum(ls[0], ls[1]), ls[2])
        es = [jnp.exp(l - m) for l in ls]
        parts.append((es[0] * o0[0, p] + es[1] * o1[0, p] + es[2] * o2[0, p]) / (es[0] + es[1] + es[2]))
    o = jnp.concatenate(parts, axis=1)
    y_ref[0] = x_ref[0] + g_ref[0] * _dot(o.astype(BF16), w_ref[...])


def _dil_res(outs, lses, w_out, x, gate, tm=512):
    b, s, d = x.shape
    k = w_out.shape[0]
    tm = min(tm, s)
    part = pl.BlockSpec((1, k // LANES, tm, LANES), lambda bi, i: (bi, 0, i, 0))
    return pl.pallas_call(
        _dil_res_kernel,
        grid=(b, s // tm),
        in_specs=[part] * 6 + [
            pl.BlockSpec((k, d), lambda bi, i: (0, 0)),
            pl.BlockSpec((1, tm, d), lambda bi, i: (bi, i, 0)),
            pl.BlockSpec((1, 1, d), lambda bi, i: (bi, 0, 0)),
        ],
        out_specs=pl.BlockSpec((1, tm, d), lambda bi, i: (bi, i, 0)),
        out_shape=jax.ShapeDtypeStruct((b, s, d), F32),
        compiler_params=_params("parallel", "parallel"),
        name="dil_res",
    )(*outs, *lses, w_out, x, gate)


def _router_kernel(x_ref, g_ref, sc_ref, sh_ref, wr_ref, br_ref, h_ref, route_ref, cnt_ref, run_ref):
    @pl.when(jnp.logical_and(pl.program_id(0) == 0, pl.program_id(1) == 0))
    def _():
        run_ref[...] = jnp.zeros_like(run_ref)

    h = _mod_norm(x_ref[0], g_ref[...], sc_ref[0], sh_ref[0])
    for s in range(h.shape[1] // LANES):
        h_ref[pl.ds(s, h.shape[0], stride=h.shape[1] // LANES), :] = h[:, s * LANES:(s + 1) * LANES]
    h_hi, h_lo = _split_bf16(h)
    w_hi, w_lo = _split_bf16(wr_ref[...])
    logits = _dot(h_hi, w_hi) + (_dot(h_hi, w_lo) + _dot(h_lo, w_hi)) + br_ref[...]
    tm = logits.shape[0]
    lane = lax.broadcasted_iota(jnp.int32, logits.shape, 1).astype(F32)
    neg = -jnp.inf
    far = float(LANES)

    def first_argmax(vals):
        top = jnp.max(vals, axis=1, keepdims=True)
        return top, jnp.min(jnp.where(vals == top, lane, far), axis=1, keepdims=True)

    grp_logits = jnp.where(lane < MOE_GROUPS, logits, neg)
    grp_max, grp = first_argmax(grp_logits)
    p_grp = 1.0 / jnp.sum(jnp.exp(grp_logits - grp_max), axis=1, keepdims=True)
    base = MOE_GROUPS + grp * MOE_PER_GROUP
    in_grp = jnp.logical_and(lane >= base, lane < base + MOE_PER_GROUP)
    exp_logits = jnp.where(in_grp, logits, neg)
    m1, i1 = first_argmax(exp_logits)
    m2, i2 = first_argmax(jnp.where(lane == i1, neg, exp_logits))
    e2 = jnp.exp(m2 - m1)
    g1 = p_grp / (1.0 + e2)
    g2 = g1 * e2
    e_a = i1 - MOE_GROUPS
    e_b = i2 - MOE_GROUPS

    pick_a = lane == e_a
    pick_b = lane == e_b
    both = jnp.where(jnp.logical_or(pick_a, pick_b), 1.0, 0.0)
    row = lax.broadcasted_iota(jnp.int32, (tm, tm), 0)
    col = lax.broadcasted_iota(jnp.int32, (tm, tm), 1)
    earlier = jnp.where(col < row, 1.0, 0.0).astype(BF16)
    before = _dot(earlier, both.astype(BF16)) + run_ref[...]
    rank_a = jnp.sum(jnp.where(pick_a, before, 0.0), axis=1, keepdims=True)
    rank_b = jnp.sum(jnp.where(pick_b, before, 0.0), axis=1, keepdims=True)
    run_ref[...] += jnp.sum(both, axis=0, keepdims=True)
    cnt_ref[...] = run_ref[...]

    route = jnp.where(lane == 0, e_a, jnp.where(lane == 1, e_b, 0.0))
    route = jnp.where(lane == 2, g1, jnp.where(lane == 3, g2, route))
    route_ref[...] = jnp.where(lane == 4, rank_a, jnp.where(lane == 5, rank_b, route))


def _router(x, g, sc, sh, w_route, b_route, tm=512):
    b, s, d = x.shape
    tm = min(tm, s)
    steps = s // tm
    return pl.pallas_call(
        _router_kernel,
        grid=(b, steps),
        in_specs=[
            pl.BlockSpec((1, tm, d), lambda bi, i: (bi, i, 0)),
            pl.BlockSpec((1, d), lambda bi, i: (0, 0)),
            pl.BlockSpec((1, 1, d), lambda bi, i: (bi, 0, 0)),
            pl.BlockSpec((1, 1, d), lambda bi, i: (bi, 0, 0)),
            pl.BlockSpec((d, LANES), lambda bi, i: (0, 0)),
            pl.BlockSpec((1, LANES), lambda bi, i: (0, 0)),
        ],
        out_specs=[
            pl.BlockSpec((tm * (d // LANES), LANES), lambda bi, i: (bi * steps + i, 0)),
            pl.BlockSpec((tm, LANES), lambda bi, i: (bi * steps + i, 0)),
            pl.BlockSpec((1, LANES), lambda bi, i: (0, 0)),
        ],
        out_shape=[
            jax.ShapeDtypeStruct((b * s * (d // LANES), LANES), F32),
            jax.ShapeDtypeStruct((b * s, LANES), F32),
            jax.ShapeDtypeStruct((1, LANES), F32),
        ],
        scratch_shapes=[pltpu.VMEM((1, LANES), F32)],
        compiler_params=_params("arbitrary", "arbitrary"),
        name="moe_router",
    )(x, g.reshape(1, d), sc, sh, w_route, b_route)


def _sc_dispatch(h_slabs, dest, n_slots):
    t, chunks, _ = h_slabs.shape
    workers = SC_CORES * SC_SUBCORES
    per_worker = t // workers
    steps = per_worker // SC_WINDOW
    assert t % (workers * SC_WINDOW * 2) == 0
    mesh = plsc.VectorSubcoreMesh(core_axis_name="core", subcore_axis_name="subcore")

    @functools.partial(
        pl.kernel, mesh=mesh,
        out_type=jax.ShapeDtypeStruct((n_slots, chunks, LANES), h_slabs.dtype),
        scratch_types=[
            pltpu.VMEM((2, MOE_TOPK, SC_WINDOW), jnp.int32),
            pltpu.VMEM((2, SC_WINDOW, chunks, LANES), h_slabs.dtype),
            pltpu.SemaphoreType.DMA((2,)),
            pltpu.SemaphoreType.DMA((2,)),
        ])
    def dispatch(h_hbm, dest_hbm, slots_hbm, idx_v, rows_v, load_sem, store_sem):
        worker = lax.axis_index("subcore") * SC_CORES + lax.axis_index("core")
        base = worker * per_worker

        @pl.loop(0, steps, step=2)
        def _(step):
            for buf in range(2):
                off = base + (step + buf) * SC_WINDOW
                pltpu.async_copy(h_hbm.at[pl.ds(off, SC_WINDOW)], rows_v.at[buf], load_sem.at[buf])
                for k in range(MOE_TOPK):
                    pltpu.sync_copy(dest_hbm.at[k, pl.ds(off, SC_WINDOW)], idx_v.at[buf, k])
            for buf in range(2):
                off = base + (step + buf) * SC_WINDOW
                pltpu.make_async_copy(h_hbm.at[pl.ds(off, SC_WINDOW)], rows_v.at[buf], load_sem.at[buf]).wait()
                copies = [pltpu.async_copy(rows_v.at[buf], slots_hbm.at[idx_v.at[buf, k]], store_sem.at[buf])
                          for k in range(MOE_TOPK)]
                for cp in copies:
                    cp.wait()

    return dispatch(h_slabs, dest)


def _zero_tail_kernel(tail_ref, valid_ref, s_in, s_out):
    del tail_ref
    rows = lax.broadcasted_iota(jnp.int32, s_in.shape, 0)
    s_out[...] = jnp.where(rows < valid_ref[pl.program_id(0)], s_in[...], 0.0)


def _zero_tails(tail_blk, valid_rows, slots, block_rows):
    grid_spec = pltpu.PrefetchScalarGridSpec(
        num_scalar_prefetch=2,
        grid=(tail_blk.shape[0],),
        in_specs=[pl.BlockSpec((block_rows, LANES), lambda e, tail, valid: (tail[e], 0))],
        out_specs=pl.BlockSpec((block_rows, LANES), lambda e, tail, valid: (tail[e], 0)),
    )
    return pl.pallas_call(
        _zero_tail_kernel,
        grid_spec=grid_spec,
        out_shape=jax.ShapeDtypeStruct(slots.shape, slots.dtype),
        input_output_aliases={2: 0},
        compiler_params=_params("arbitrary"),
        name="moe_zero_tails",
    )(tail_blk, valid_rows, slots)


def _expert_kernel(blk_e_ref, used_ref, x_ref, wg_ref, wu_ref, wd_ref, y_ref):
    del blk_e_ref
    live = pl.program_id(0) < used_ref[0]

    @pl.when(live)
    def _():
        bm = y_ref.shape[0]
        chunks = x_ref.shape[0] // bm
        x = jnp.concatenate([x_ref[pl.ds(s, bm, stride=chunks), :] for s in range(chunks)], axis=1).astype(BF16)
        gate = _dot(x, wg_ref[0])
        up = _dot(x, wu_ref[0])
        act = (gate * _sigmoid(gate) * up).astype(BF16)
        y_ref[...] = _dot(act, wd_ref[0])

    @pl.when(jnp.logical_not(live))
    def _():
        y_ref[...] = jnp.zeros_like(y_ref)


def _expert_blocks(blk_e, used, slots, w_gate, w_up, w_down, bm):
    nb = blk_e.shape[0]
    _, d, f = w_gate.shape
    chunks = d // LANES

    def block(i, e, n):
        return (jnp.minimum(i, n[0] - 1), 0)

    def weight(i, e, n):
        return (e[jnp.minimum(i, n[0] - 1)], 0, 0)

    grid_spec = pltpu.PrefetchScalarGridSpec(
        num_scalar_prefetch=2,
        grid=(nb,),
        in_specs=[
            pl.BlockSpec((bm * chunks, LANES), block),
            pl.BlockSpec((1, d, f), weight),
            pl.BlockSpec((1, d, f), weight),
            pl.BlockSpec((1, f, d), weight),
        ],
        out_specs=pl.BlockSpec((bm, d), lambda i, e, n: (i, 0)),
    )
    return pl.pallas_call(
        _expert_kernel,
        grid_spec=grid_spec,
        out_shape=jax.ShapeDtypeStruct((nb * bm, d), F32),
        compiler_params=_params("arbitrary"),
        name="moe_experts",
    )(blk_e, used, slots, w_gate, w_up, w_down)


def _copy_idx(src_ref, idx_smem, sem):
    cp = pltpu.make_async_copy(src_ref.at[0, 0], idx_smem, sem)
    cp.start()
    cp.wait()


def _gather_rows(idx_smem, offset, count, src_hbm, dst, sem):
    def issue(pair, carry):
        for lane in range(2):
            r = 2 * pair + lane
            src = src_hbm.at[pl.ds(idx_smem[offset + r], 1)]
            pltpu.make_async_copy(src, dst.at[pl.ds(r, 1)], sem).start(priority=lane)
        return carry

    lax.fori_loop(0, count // 2, issue, 0, unroll=4)


def _wait_rows(count, src_hbm, dst, sem):
    pltpu.make_async_copy(src_hbm.at[pl.ds(0, count)], dst, sem).wait()


def _combine_kernel(dest_cur, dest_nxt, route_ref, y_hbm, x_ref, g_ref, fg_ref, o_ref,
                    idx_even, idx_odd, buf, sem_idx, sem_rows, *, final):
    tm = buf.shape[2]
    i = pl.program_id(0)
    idx_smem = (idx_even, idx_odd)

    def fetch(dest_ref, sl):
        _copy_idx(dest_ref, idx_smem[sl], sem_idx)
        for k in range(MOE_TOPK):
            _gather_rows(idx_smem[sl], k * tm, tm, y_hbm, buf.at[sl, k], sem_rows.at[sl])

    def run(slot):
        if slot == 0:
            @pl.when(i == 0)
            def _():
                fetch(dest_cur, 0)

        @pl.when(i + 1 < pl.num_programs(0))
        def _():
            fetch(dest_nxt, 1 - slot)

        for k in range(MOE_TOPK):
            _wait_rows(tm, y_hbm, buf.at[slot, k], sem_rows.at[slot])
        route = route_ref[...]
        y = route[:, 2:3] * buf[slot, 0] + route[:, 3:4] * buf[slot, 1]
        out = x_ref[...] + g_ref[0] * y
        if final:
            out = out * lax.rsqrt(jnp.mean(out * out, axis=-1, keepdims=True) + RMS_EPS) * fg_ref[...]
        o_ref[...] = out

    for slot in range(2):
        pl.when(lax.rem(i, 2) == slot)(functools.partial(run, slot))


def _combine(dest, route, y_rows, x, gate, final_g, final, tm):
    b, s, d = x.shape
    t = b * s
    per_seq = s // tm
    steps = t // tm
    out = pl.pallas_call(
        functools.partial(_combine_kernel, final=final),
        grid=(steps,),
        in_specs=[
            pl.BlockSpec((1, 1, MOE_TOPK * tm), lambda i: (i, 0, 0)),
            pl.BlockSpec((1, 1, MOE_TOPK * tm), lambda i: (jnp.minimum(i + 1, steps - 1), 0, 0)),
            pl.BlockSpec((tm, LANES), lambda i: (i, 0)),
            pl.BlockSpec(memory_space=pl.ANY),
            pl.BlockSpec((tm, d), lambda i: (i, 0)),
            pl.BlockSpec((1, 1, d), lambda i: (i // per_seq, 0, 0)),
            pl.BlockSpec((1, d), lambda i: (0, 0)),
        ],
        out_specs=pl.BlockSpec((tm, d), lambda i: (i, 0)),
        out_shape=jax.ShapeDtypeStruct((t, d), F32),
        scratch_shapes=[
            pltpu.SMEM((MOE_TOPK * tm,), jnp.int32),
            pltpu.SMEM((MOE_TOPK * tm,), jnp.int32),
            pltpu.VMEM((2, MOE_TOPK, tm, d), F32),
            pltpu.SemaphoreType.DMA,
            pltpu.SemaphoreType.DMA((2,)),
        ],
        compiler_params=_params("arbitrary"),
        name="moe_combine",
    )(dest, dest, route, y_rows, x.reshape(t, d), gate, final_g.reshape(1, d))
    return out.reshape(b, s, d)


def _slot_tables(route, counts, bm, tm):
    t = route.shape[0]
    nb = MOE_TOPK * t // bm + MOE_EXPERTS
    counts = counts[0, :MOE_EXPERTS].astype(jnp.int32)
    padded = (counts + bm - 1) // bm * bm
    pad_end = jnp.cumsum(padded)
    pad_start = pad_end - padded
    expert = route[:, 0:MOE_TOPK].astype(jnp.int32)
    rank = route[:, 4:4 + MOE_TOPK].astype(jnp.int32)
    dest = pad_start[expert] + rank
    dest_tiles = dest.reshape(t // tm, tm, MOE_TOPK).transpose(0, 2, 1).reshape(t // tm, 1, MOE_TOPK * tm)
    blk_start = jnp.arange(nb, dtype=jnp.int32) * bm
    blk_e = jnp.minimum(jnp.sum((pad_end[None, :] <= blk_start[:, None]).astype(jnp.int32), axis=1), MOE_EXPERTS - 1)
    used = (pad_end[-1:] // bm).astype(jnp.int32)
    tail_blk = jnp.maximum(pad_end // bm - 1, 0).astype(jnp.int32)
    tail_fill = jnp.where(counts % bm != 0, counts % bm, bm).astype(jnp.int32)
    return dest_tiles, dest.T, blk_e, used, tail_blk, tail_fill, nb


def _moe_layer(x, g, sc, sh, gate, w_route, b_route, w_gate, w_up, w_down, final_g, final, bm=512, tm=256):
    b, s, d = x.shape
    t = b * s
    chunks = d // LANES
    tm = min(tm, s)
    h_rows, route, counts = _router(x, g, sc, sh, w_route, b_route)
    dest_tiles, dest, blk_e, used, tail_blk, tail_fill, nb = _slot_tables(route, counts, bm, tm)
    slots = _sc_dispatch(h_rows.reshape(t, chunks, LANES), dest, nb * bm)
    slots = _zero_tails(tail_blk, tail_fill * chunks, slots.reshape(nb * bm * chunks, LANES), bm * chunks)
    y_rows = _expert_blocks(blk_e, used, slots, w_gate, w_up, w_down, bm)
    return _combine(dest_tiles, route, y_rows, x, gate, final_g, final, tm)


def kernel(x, c, ada_w, ada_b, norm_g, final_g, sb_w_in, sb_w_out, gla_w_in, gla_w_gate_up, gla_b_gate, gla_norm_g, gla_w_out, conv_w_in, conv_w, conv_b, conv_w_out, dil_w_in, dil_w_out, moe_w_grp, moe_b_grp, moe_w_exp, moe_b_exp, moe_w_gate, moe_w_up, moe_w_down):
    b, s, d = x.shape
    depth = ada_w.shape[0]
    mod = _ada_mod(c, ada_w, ada_b).reshape(depth, b, N_MOD, 1, d)
    for i in range(depth):
        sh1, sc1, g1, sh2, sc2, g2 = (mod[i, :, m] for m in range(N_MOD))
        kind, j = i % 4, i // 4
        if kind == 0:
            qkv = _norm_proj(x, norm_g[i, 0], sc1, sh1, sb_w_in[j].astype(BF16))
            x = _proj_res(_sb_attention(qkv), sb_w_out[j].astype(BF16), x, g1)
        elif kind == 1:
            dk = gla_w_gate_up.shape[-1] // GLA_HEADS
            dv = gla_norm_g.shape[-1]
            w_in = jnp.pad(gla_w_in[j], ((0, 0), (0, LANES - GLA_GATE_RANK))).astype(BF16)
            w_up = jnp.pad(gla_w_gate_up[j], ((0, LANES - GLA_GATE_RANK), (0, 0))).astype(BF16)
            proj = _norm_proj(x, norm_g[i, 0], sc1, sh1, w_in, tn=640)
            o = _gla_core(proj, w_up, gla_b_gate[j].reshape(1, -1), gla_norm_g[j].reshape(1, -1), dk, dv)
            x = _proj_res(o, gla_w_out[j].astype(BF16), x, g1)
        elif kind == 2:
            proj = _norm_proj(x, norm_g[i, 0], sc1, sh1, conv_w_in[j].astype(BF16))
            x = _conv_res(proj, conv_w[j], conv_b[j], conv_w_out[j].astype(BF16), x, g1)
        else:
            group_w = 3 * DIL_HEADS * DIL_HEAD_DIM
            outs, lses = [], []
            for g, (window, dilation) in enumerate(DIL_PATTERNS):
                w_g = dil_w_in[j][:, g * group_w:(g + 1) * group_w].astype(BF16)
                view = _norm_proj(x, norm_g[i, 0], sc1, sh1, w_g, dilation=dilation)
                o_g, l_g = _dil_group(view, dilation, window, s)
                outs.append(o_g)
                lses.append(l_g)
            x = _dil_res(outs, lses, dil_w_out[j].astype(BF16), x, g1)
        w_route = jnp.pad(jnp.concatenate([moe_w_grp[i], moe_w_exp[i]], axis=1),
                          ((0, 0), (0, LANES - MOE_GROUPS - MOE_EXPERTS)))
        b_route = jnp.pad(jnp.concatenate([moe_b_grp[i], moe_b_exp[i]]), (0, LANES - MOE_GROUPS - MOE_EXPERTS))
        x = _moe_layer(x, norm_g[i, 1], sc2, sh2, g2, w_route, b_route.reshape(1, LANES),
                       moe_w_gate[i].astype(BF16), moe_w_up[i].astype(BF16), moe_w_down[i].astype(BF16),
                       final_g, final=(i == depth - 1))
    return x
```

```python
import functools

import jax
import jax.numpy as jnp
from jax import lax
from jax.experimental import pallas as pl
from jax.experimental.pallas import tpu as pltpu
from jax.experimental.pallas import tpu_sc as plsc

F32 = jnp.float32
BF16 = jnp.bfloat16

LANES = 128
RMS_EPS = 1e-6
N_MOD = 6

SB_HEADS = 16
GLA_HEADS = 4
GLA_GATE_RANK = 16
GLA_GATE_TAU = 16.0
GLA_CHUNK = 64
CONV_WIDTH = 3
DIL_PATTERNS = ((128, 1), (512, 4), (2048, 16))
DIL_HEADS = 8
DIL_HEAD_DIM = 64
DIL_QBLOCK = 128
DIL_TILE_TOKENS = 2048
MOE_GROUPS = 4
MOE_PER_GROUP = 8
MOE_EXPERTS = MOE_GROUPS * MOE_PER_GROUP
MOE_TOPK = 2

SC_CORES = 2
SC_SUBCORES = 16
SC_WINDOW = 32

SB_LOG_ZERO = -104.0
SB_BAND = 2

VMEM_LIMIT = 48 * 1024 * 1024


def _params(*sem):
    return pltpu.CompilerParams(dimension_semantics=sem, vmem_limit_bytes=VMEM_LIMIT)


def _sigmoid(x):
    return 1.0 / (1.0 + jnp.exp(-x))


def _softplus(x):
    return jnp.maximum(x, 0.0) + jnp.log(1.0 + jnp.exp(-jnp.abs(x)))


def _split_bf16(x):
    hi = x.astype(BF16)
    lo = (x - hi.astype(F32)).astype(BF16)
    return hi, lo


def _dot(a, b):
    return jnp.dot(a, b, preferred_element_type=F32)


def _dot_nt(a, b):
    return lax.dot_general(a, b, (((1,), (1,)), ((), ())), preferred_element_type=F32)


def _dot_tn(a, b):
    return lax.dot_general(a, b, (((0,), (0,)), ((), ())), preferred_element_type=F32)


def _mod_norm(x, g, sc, sh):
    r = lax.rsqrt(jnp.mean(x * x, axis=-1, keepdims=True) + RMS_EPS)
    return (x * r) * g * (1.0 + sc) + sh


def _ada_kernel(c_ref, w_ref, b_ref, o_ref):
    c = c_ref[...]
    cond = c * _sigmoid(c)
    o_ref[0] = _dot(cond.astype(BF16), w_ref[0].astype(BF16)) + b_ref[0]


def _ada_mod(c, ada_w, ada_b):
    depth, d, n = ada_w.shape
    b = c.shape[0]
    tn = 1536
    return pl.pallas_call(
        _ada_kernel,
        grid=(depth, n // tn),
        in_specs=[
            pl.BlockSpec((b, d), lambda i, j: (0, 0)),
            pl.BlockSpec((1, d, tn), lambda i, j: (i, 0, j)),
            pl.BlockSpec((1, 1, tn), lambda i, j: (i, 0, j)),
        ],
        out_specs=pl.BlockSpec((1, b, tn), lambda i, j: (i, 0, j)),
        out_shape=jax.ShapeDtypeStruct((depth, b, n), F32),
        compiler_params=_params("parallel", "parallel"),
        name="ada_mod",
    )(c, ada_w, ada_b.reshape(depth, 1, n))


def _norm_proj_kernel(x_ref, g_ref, sc_ref, sh_ref, w_ref, o_ref, h_ref, *res_ref, dilation, tn):
    h_ref[...] = _mod_norm(x_ref[0], g_ref[...], sc_ref[0], sh_ref[0]).astype(BF16)
    tm = h_ref.shape[0]
    for j in range(w_ref.shape[1] // tn):
        res = _dot(h_ref[...], w_ref[:, j * tn:(j + 1) * tn])
        if dilation == 1:
            o_ref[0, :, j * tn:(j + 1) * tn] = res.astype(o_ref.dtype)
            continue
        for c in range(tn // LANES):
            res_ref[0][j * (tn // LANES) + c] = res[:, c * LANES:(c + 1) * LANES]
        for rho in range(dilation):
            for c in range(tn // LANES):
                rows = res_ref[0][j * (tn // LANES) + c, pl.ds(rho, tm // dilation, stride=dilation), :]
                col = (j * dilation + rho) * tn + c * LANES
                o_ref[0, :, col:col + LANES] = rows.astype(o_ref.dtype)


def _norm_proj(x, g, sc, sh, w, tm=512, tn=512, dilation=1):
    b, s, d = x.shape
    n = w.shape[1]
    tm = min(tm, s)
    assert s % tm == 0 and n % tn == 0 and tm % (16 * dilation) == 0
    scratch = [pltpu.VMEM((tm, d), BF16)]
    if dilation > 1:
        scratch.append(pltpu.VMEM((n // LANES, tm, LANES), F32))
    return pl.pallas_call(
        functools.partial(_norm_proj_kernel, dilation=dilation, tn=tn),
        grid=(b, s // tm),
        in_specs=[
            pl.BlockSpec((1, tm, d), lambda bi, i: (bi, i, 0)),
            pl.BlockSpec((1, d), lambda bi, i: (0, 0)),
            pl.BlockSpec((1, 1, d), lambda bi, i: (bi, 0, 0)),
            pl.BlockSpec((1, 1, d), lambda bi, i: (bi, 0, 0)),
            pl.BlockSpec((d, n), lambda bi, i: (0, 0)),
        ],
        out_specs=pl.BlockSpec((1, tm // dilation, dilation * n), lambda bi, i: (bi, i, 0)),
        out_shape=jax.ShapeDtypeStruct((b, s // dilation, dilation * n), BF16),
        scratch_shapes=scratch,
        compiler_params=_params("parallel", "parallel"),
        name="norm_proj",
    )(x, g.reshape(1, d), sc, sh, w)


def _proj_res_kernel(o_ref, w_ref, x_ref, g_ref, y_ref):
    y_ref[0] = x_ref[0] + g_ref[0] * _dot(o_ref[0], w_ref[...])


def _proj_res(o, w, x, gate, tm=512):
    b, s, d = x.shape
    k = o.shape[-1]
    tm = min(tm, s)
    return pl.pallas_call(
        _proj_res_kernel,
        grid=(b, s // tm),
        in_specs=[
            pl.BlockSpec((1, tm, k), lambda bi, i: (bi, i, 0)),
            pl.BlockSpec((k, d), lambda bi, i: (0, 0)),
            pl.BlockSpec((1, tm, d), lambda bi, i: (bi, i, 0)),
            pl.BlockSpec((1, 1, d), lambda bi, i: (bi, 0, 0)),
        ],
        out_specs=pl.BlockSpec((1, tm, d), lambda bi, i: (bi, i, 0)),
        out_shape=jax.ShapeDtypeStruct((b, s, d), F32),
        compiler_params=_params("parallel", "parallel"),
        name="proj_res",
    )(o, w, x, gate)


def _sb_kernel(q_ref, k_ref, v_ref, o_ref, acc_ref, c_ref, *, blk, hd):
    scale = hd ** -0.5
    fold_scale = (hd & (hd - 1)) == 0 and (hd.bit_length() - 1) % 2 == 0
    first = lax.broadcasted_iota(jnp.int32, (blk, LANES), 1) < hd
    row = lax.broadcasted_iota(jnp.int32, (blk, blk), 0)
    col = lax.broadcasted_iota(jnp.int32, (blk, blk), 1)
    minus_later = jnp.where(row > col, -1.0, 0.0).astype(BF16)
    causal = col < row
    subs = q_ref.shape[1] // blk
    first_block = pl.program_id(2) * subs

    def visit_many(work):
        kv = []
        for _, block, _, _ in work:
            start = pl.multiple_of(block * blk, blk)
            kv.append((k_ref[0, pl.ds(start, blk), :], v_ref[0, pl.ds(start, blk), :]))
        zs = [[_dot_nt(q_heads[h], kb) for h in range(2)] for (q_heads, _, _, _), (kb, _) in zip(work, kv)]
        log_betas, sps = [], []
        for (_, _, mask, _), z2 in zip(work, zs):
            lb2, sp2 = [], []
            for z in z2:
                if not fold_scale:
                    z = z * scale
                sp = _softplus(z)
                lb2.append(z - sp)
                sp2.append(sp if mask is None else jnp.where(mask, sp, 0.0))
            log_betas.append(lb2)
            sps.append(sp2)
        suffixes = [[_dot(sp.astype(BF16), minus_later) for sp in sp2] for sp2 in sps]
        return kv, log_betas, sps, suffixes

    def finish(work, staged, c_in):
        kv, log_betas, sps, suffixes = staged
        cs = {chain: list(c) for chain, c in c_in.items()}
        weights = []
        for n, (_, _, mask, chain) in enumerate(work):
            a2 = []
            for h in range(2):
                a = jnp.exp(log_betas[n][h] + suffixes[n][h] + cs[chain][h])
                a2.append(a if mask is None else jnp.where(mask, a, 0.0))
                cs[chain][h] = cs[chain][h] - jnp.sum(sps[n][h], axis=1, keepdims=True)
            weights.append(a2)
        accs = {}
        for n, (_, _, _, chain) in enumerate(work):
            outs = [_dot(weights[n][h].astype(BF16), kv[n][1]) for h in range(2)]
            more = jnp.where(first, outs[0], outs[1])
            accs[chain] = more if chain not in accs else accs[chain] + more
        return accs, cs

    def remainder(sub, q_heads, depth):
        def cond(carry):
            j, cmax = carry
            return jnp.logical_and(j >= 0, cmax > SB_LOG_ZERO)

        def body(carry):
            j, _ = carry
            work = [(q_heads, j, None, sub)]
            accs, cs = finish(work, visit_many(work), {sub: [c_ref[sub, 0], c_ref[sub, 1]]})
            acc_ref[sub] += accs[sub]
            for h in range(2):
                c_ref[sub, h] = cs[sub][h]
            return j - 1, jnp.max(c_ref[sub])

        lax.while_loop(cond, body, (first_block + sub - depth - 1, jnp.max(c_ref[sub])))

    def run(depth):
        heads = []
        for sub in range(subs):
            q = q_ref[0, sub * blk:(sub + 1) * blk, :]
            if fold_scale:
                q = (q.astype(F32) * scale).astype(BF16)
            zero = jnp.zeros_like(q)
            heads.append((jnp.where(first, q, zero), jnp.where(first, zero, q)))
        work = [(heads[sub], first_block + sub - back, causal if back == 0 else None, sub)
                for sub in range(subs) for back in range(depth + 1)]
        zero_c = [jnp.zeros((blk, 1), F32)] * 2
        accs, cs = finish(work, visit_many(work), {sub: zero_c for sub in range(subs)})
        for sub in range(subs):
            acc_ref[sub] = accs[sub]
            for h in range(2):
                c_ref[sub, h] = cs[sub][h]
        for sub in range(subs):
            remainder(sub, heads[sub], depth)
            o_ref[0, sub * blk:(sub + 1) * blk, :] = acc_ref[sub].astype(o_ref.dtype)

    pl.when(first_block >= SB_BAND)(functools.partial(run, SB_BAND))
    pl.when(first_block < SB_BAND)(functools.partial(run, 0))


def _sb_attention(qkv, blk=128, rows=512):
    b, s, n = qkv.shape
    width = n // 3
    hd = width // SB_HEADS
    assert 2 * hd == LANES
    pairs = width // LANES
    rows = min(rows, s)
    return pl.pallas_call(
        functools.partial(_sb_kernel, blk=blk, hd=hd),
        grid=(b, pairs, s // rows),
        in_specs=[
            pl.BlockSpec((1, rows, LANES), lambda bi, p, i: (bi, i, p)),
            pl.BlockSpec((1, s, LANES), lambda bi, p, i: (bi, 0, pairs + p)),
            pl.BlockSpec((1, s, LANES), lambda bi, p, i: (bi, 0, 2 * pairs + p)),
        ],
        out_specs=pl.BlockSpec((1, rows, LANES), lambda bi, p, i: (bi, i, p)),
        out_shape=jax.ShapeDtypeStruct((b, s, width), BF16),
        scratch_shapes=[pltpu.VMEM((rows // blk, blk, LANES), F32), pltpu.VMEM((rows // blk, 2, blk, 1), F32)],
        compiler_params=_params("parallel", "parallel", "arbitrary"),
        name="sb_attention",
    )(qkv, qkv, qkv)


def _gla_kernel(p_ref, wg_ref, bg_ref, ng_ref, o_ref, st_ref, *, rows, dk, dv):
    @pl.when(pl.program_id(1) == 0)
    def _():
        st_ref[...] = jnp.zeros_like(st_ref)

    heads = GLA_HEADS
    ch = GLA_CHUNK
    k0 = heads * dk
    v0 = 2 * heads * dk
    r0 = v0 + heads * dv
    g0 = r0 + heads * dv
    row = lax.broadcasted_iota(jnp.int32, (ch, ch), 0)
    col = lax.broadcasted_iota(jnp.int32, (ch, ch), 1)
    causal = col <= row
    lower = jnp.where(causal, 1.0, 0.0).astype(BF16)
    chunks = [slice(s * ch, (s + 1) * ch) for s in range(rows // ch)]
    g_pres = [_dot(p_ref[0, t, g0:g0 + LANES], wg_ref[...]) + bg_ref[...] for t in chunks]
    cums = []
    for g_pre in g_pres:
        log_a = (jnp.minimum(g_pre, 0.0) - jnp.log(1.0 + jnp.exp(-jnp.abs(g_pre)))) * (1.0 / GLA_GATE_TAU)
        hi, lo = _split_bf16(log_a)
        cums.append(_dot(lower, hi) + _dot(lower, lo))
    parts = []
    for t, cum in zip(chunks, cums):
        for h in range(heads):
            bh = cum[:, h * dk:(h + 1) * dk]
            b_last = bh[ch - 1:ch, :]
            q = p_ref[0, t, h * dk:(h + 1) * dk].astype(F32) * dk ** -0.5
            k = p_ref[0, t, k0 + h * dk:k0 + (h + 1) * dk].astype(F32)
            q_in = (q * jnp.exp(bh)).astype(BF16)
            k_in = (k * jnp.exp(-bh)).astype(BF16)
            k_state = (k * jnp.exp(b_last - bh)).astype(BF16)
            parts.append((t, h, q_in, k_in, k_state, jnp.exp(b_last)))
    scores = [jnp.where(causal, _dot_nt(q_in, k_in), 0.0).astype(BF16) for _, _, q_in, k_in, _, _ in parts]
    values = [p_ref[0, t, v0 + h * dv:v0 + (h + 1) * dv] for t, h, _, _, _, _ in parts]
    intra = [_dot(sc, v) for sc, v in zip(scores, values)]
    updates = [_dot_tn(v, k_state) for v, (_, _, _, _, k_state, _) in zip(values, parts)]
    states = [st_ref[h] for h in range(heads)]
    for n, (t, h, q_in, _, _, decay) in enumerate(parts):
        o = intra[n] + _dot_nt(q_in, states[h].astype(BF16))
        states[h] = decay * states[h] + updates[n]
        o = o * lax.rsqrt(jnp.mean(o * o, axis=-1, keepdims=True) + RMS_EPS) * ng_ref[...]
        r = p_ref[0, t, r0 + h * dv:r0 + (h + 1) * dv].astype(F32)
        o_ref[0, t, h * dv:(h + 1) * dv] = (o * (r * _sigmoid(r))).astype(o_ref.dtype)
    for h in range(heads):
        st_ref[h] = states[h]


def _gla_core(proj, w_gate_up, b_gate, norm_g, dk, dv, rows=256):
    b, s, n = proj.shape
    rows = min(rows, s)
    hk = GLA_HEADS * dk
    return pl.pallas_call(
        functools.partial(_gla_kernel, rows=rows, dk=dk, dv=dv),
        grid=(b, s // rows),
        in_specs=[
            pl.BlockSpec((1, rows, n), lambda bi, i: (bi, i, 0)),
            pl.BlockSpec((LANES, hk), lambda bi, i: (0, 0)),
            pl.BlockSpec((1, hk), lambda bi, i: (0, 0)),
            pl.BlockSpec((1, dv), lambda bi, i: (0, 0)),
        ],
        out_specs=pl.BlockSpec((1, rows, GLA_HEADS * dv), lambda bi, i: (bi, i, 0)),
        out_shape=jax.ShapeDtypeStruct((b, s, GLA_HEADS * dv), BF16),
        scratch_shapes=[pltpu.VMEM((GLA_HEADS, dv, dk), F32)],
        compiler_params=_params("parallel", "arbitrary"),
        name="gla_core",
    )(proj, w_gate_up, b_gate, norm_g)


def _conv_res_kernel(gb_ref, gc_ref, u_ref, hgc_ref, hu_ref, cw_ref, cb_ref, w_ref, x_ref, g_ref, y_ref):
    u2 = gc_ref[0].astype(F32) * u_ref[0].astype(F32)
    halo = hgc_ref[0].astype(F32) * hu_ref[0].astype(F32)
    halo = jnp.where(pl.program_id(1) > 0, halo, 0.0)
    rows = lax.broadcasted_iota(jnp.int32, u2.shape, 0)
    prev1 = jnp.where(rows == 0, halo[7:8], pltpu.roll(u2, 1, 0))
    prev2 = jnp.where(rows == 0, halo[6:7], jnp.where(rows == 1, halo[7:8], pltpu.roll(u2, 2, 0)))
    cw = cw_ref[...]
    y = cw[0:1] * prev2 + cw[1:2] * prev1 + cw[2:3] * u2 + cb_ref[...]
    o = (gb_ref[0].astype(F32) * y).astype(BF16)
    y_ref[0] = x_ref[0] + g_ref[0] * _dot(o, w_ref[...])


def _conv_res(proj, conv_w, conv_b, w_out, x, gate, tm=512):
    b, s, d = x.shape
    tm = min(tm, s)
    halo_blocks = tm // 8

    def halo_map(col):
        return lambda bi, i: (bi, jnp.maximum(i * halo_blocks - 1, 0), col)

    return pl.pallas_call(
        _conv_res_kernel,
        grid=(b, s // tm),
        in_specs=[
            pl.BlockSpec((1, tm, d), lambda bi, i: (bi, i, 0)),
            pl.BlockSpec((1, tm, d), lambda bi, i: (bi, i, 1)),
            pl.BlockSpec((1, tm, d), lambda bi, i: (bi, i, 2)),
            pl.BlockSpec((1, 8, d), halo_map(1)),
            pl.BlockSpec((1, 8, d), halo_map(2)),
            pl.BlockSpec((CONV_WIDTH, d), lambda bi, i: (0, 0)),
            pl.BlockSpec((1, d), lambda bi, i: (0, 0)),
            pl.BlockSpec((d, d), lambda bi, i: (0, 0)),
            pl.BlockSpec((1, tm, d), lambda bi, i: (bi, i, 0)),
            pl.BlockSpec((1, 1, d), lambda bi, i: (bi, 0, 0)),
        ],
        out_specs=pl.BlockSpec((1, tm, d), lambda bi, i: (bi, i, 0)),
        out_shape=jax.ShapeDtypeStruct((b, s, d), F32),
        compiler_params=_params("parallel", "parallel"),
        name="conv_res",
    )(proj, proj, proj, proj, proj, conv_w, conv_b.reshape(1, d), w_out, x, gate)


def _dil_kernel(q_ref, kc_ref, vc_ref, kp_ref, vp_ref, o_ref, l_ref, *, rows, hd, dilation):
    qb = DIL_QBLOCK
    n = pl.program_id(1)
    rho = pl.program_id(2)
    scale = hd ** -0.5
    fold_scale = (hd & (hd - 1)) == 0 and (hd.bit_length() - 1) % 2 == 0
    first = lax.broadcasted_iota(jnp.int32, (qb, LANES), 1) < hd
    row = lax.broadcasted_iota(jnp.int32, (qb, qb), 0)
    col = lax.broadcasted_iota(jnp.int32, (qb, qb), 1)
    in_prev = col >= row
    in_cur = col <= row
    neg = -jnp.inf

    tiles = []
    for s in range(rows // qb):
        t = slice(s * qb, (s + 1) * qb)
        tp = slice((s - 1) * qb, s * qb)
        prev_mask = jnp.logical_and(in_prev, n > 0) if s == 0 else in_prev
        for p in range(q_ref.shape[-1] // LANES):
            c = slice(p * LANES, (p + 1) * LANES)
            q = q_ref[0, t, c]
            if fold_scale:
                q = (q.astype(F32) * scale).astype(BF16)
            zero = jnp.zeros_like(q)
            q_heads = (jnp.where(first, q, zero), jnp.where(first, zero, q))
            if s == 0:
                k_prev, v_prev = kp_ref[0, :, c], vp_ref[0, :, c]
            else:
                k_prev, v_prev = kc_ref[0, tp, c], vc_ref[0, tp, c]
            tiles.append((s, p, q_heads, k_prev, kc_ref[0, t, c], v_prev, vc_ref[0, t, c], prev_mask))

    scores = []
    for _, _, q_heads, k_prev, k_cur, _, _, prev_mask in tiles:
        pair = []
        for h in range(2):
            z_prev, z_cur = _dot_nt(q_heads[h], k_prev), _dot_nt(q_heads[h], k_cur)
            if not fold_scale:
                z_prev, z_cur = z_prev * scale, z_cur * scale
            pair.append((jnp.where(prev_mask, z_prev, neg), jnp.where(in_cur, z_cur, neg)))
        scores.append(pair)

    probs = []
    for pair in scores:
        stats = []
        for z_prev, z_cur in pair:
            m = jnp.max(jnp.maximum(z_prev, z_cur), axis=1, keepdims=True)
            p_prev = jnp.exp(z_prev - m)
            p_cur = jnp.exp(z_cur - m)
            den = jnp.sum(p_prev + p_cur, axis=1, keepdims=True)
            stats.append((p_prev.astype(BF16), p_cur.astype(BF16), den, m))
        probs.append(stats)

    for (s, p, _, _, _, v_prev, v_cur, _), stats in zip(tiles, probs):
        outs, lses = [], []
        for p_prev, p_cur, den, m in stats:
            outs.append((_dot(p_prev, v_prev) + _dot(p_cur, v_cur)) / den)
            lses.append(jnp.broadcast_to(m + jnp.log(den), (qb, LANES)))
        if dilation == 1:
            t_out = slice(s * qb, (s + 1) * qb)
        else:
            t_out = pl.ds(s * qb * dilation + rho, qb, stride=dilation)
        o_ref[0, p, t_out, :] = jnp.where(first, outs[0], outs[1])
        l_ref[0, p, t_out, :] = jnp.where(first, lses[0], lses[1])


def _dil_group(view, dilation, window, seq):
    b, length, _ = view.shape
    width = DIL_HEADS * DIL_HEAD_DIM
    assert window == dilation * DIL_QBLOCK and seq % (dilation * DIL_QBLOCK) == 0
    rows = min(max(DIL_TILE_TOKENS // dilation, DIL_QBLOCK), 512, length)
    prev_blocks = rows // DIL_QBLOCK

    def cur(part):
        return lambda bi, i, rho: (bi, i, part * dilation + rho)

    def prev(part):
        return lambda bi, i, rho: (bi, jnp.maximum(i * prev_blocks - 1, 0), part * dilation + rho)

    pairs = width // LANES
    out_shape = jax.ShapeDtypeStruct((b, pairs, seq, LANES), F32)
    out_spec = pl.BlockSpec((1, pairs, rows * dilation, LANES), lambda bi, i, rho: (bi, 0, i, 0))
    return pl.pallas_call(
        functools.partial(_dil_kernel, rows=rows, hd=DIL_HEAD_DIM, dilation=dilation),
        grid=(b, length // rows, dilation),
        in_specs=[
            pl.BlockSpec((1, rows, width), cur(0)),
            pl.BlockSpec((1, rows, width), cur(1)),
            pl.BlockSpec((1, rows, width), cur(2)),
            pl.BlockSpec((1, DIL_QBLOCK, width), prev(1)),
            pl.BlockSpec((1, DIL_QBLOCK, width), prev(2)),
        ],
        out_specs=[out_spec, out_spec],
        out_shape=[out_shape, out_shape],
        compiler_params=_params("parallel", "parallel", "arbitrary"),
        name=f"dil_attn_r{dilation}",
    )(view, view, view, view, view)


def _dil_res_kernel(o0, o1, o2, l0, l1, l2, w_ref, x_ref, g_ref, y_ref):
    parts = []
    for p in range(o0.shape[1]):
        ls = (l0[0, p], l1[0, p], l2[0, p])
        m = jnp.maximum(jnp.maximum(ls[0], ls[1]), ls[2])
        es = [jnp.exp(l - m) for l in ls]
        parts.append((es[0] * o0[0, p] + es[1] * o1[0, p] + es[2] * o2[0, p]) / (es[0] + es[1] + es[2]))
    o = jnp.concatenate(parts, axis=1)
    y_ref[0] = x_ref[0] + g_ref[0] * _dot(o.astype(BF16), w_ref[...])


def _dil_res(outs, lses, w_out, x, gate, tm=512):
    b, s, d = x.shape
    k = w_out.shape[0]
    tm = min(tm, s)
    part = pl.BlockSpec((1, k // LANES, tm, LANES), lambda bi, i: (bi, 0, i, 0))
    return pl.pallas_call(
        _dil_res_kernel,
        grid=(b, s // tm),
        in_specs=[part] * 6 + [
            pl.BlockSpec((k, d), lambda bi, i: (0, 0)),
            pl.BlockSpec((1, tm, d), lambda bi, i: (bi, i, 0)),
            pl.BlockSpec((1, 1, d), lambda bi, i: (bi, 0, 0)),
        ],
        out_specs=pl.BlockSpec((1, tm, d), lambda bi, i: (bi, i, 0)),
        out_shape=jax.ShapeDtypeStruct((b, s, d), F32),
        compiler_params=_params("parallel", "parallel"),
        name="dil_res",
    )(*outs, *lses, w_out, x, gate)


def _router_kernel(x_ref, g_ref, sc_ref, sh_ref, wr_ref, br_ref, h_ref, route_ref, cnt_ref, run_ref):
    @pl.when(jnp.logical_and(pl.program_id(0) == 0, pl.program_id(1) == 0))
    def _():
        run_ref[...] = jnp.zeros_like(run_ref)

    h = _mod_norm(x_ref[0], g_ref[...], sc_ref[0], sh_ref[0])
    for s in range(h.shape[1] // LANES):
        h_ref[pl.ds(s, h.shape[0], stride=h.shape[1] // LANES), :] = h[:, s * LANES:(s + 1) * LANES]
    h_hi, h_lo = _split_bf16(h)
    w_hi, w_lo = _split_bf16(wr_ref[...])
    logits = _dot(h_hi, w_hi) + (_dot(h_hi, w_lo) + _dot(h_lo, w_hi)) + br_ref[...]
    tm = logits.shape[0]
    lane = lax.broadcasted_iota(jnp.int32, logits.shape, 1).astype(F32)
    neg = -jnp.inf
    far = float(LANES)

    def first_argmax(vals):
        top = jnp.max(vals, axis=1, keepdims=True)
        return top, jnp.min(jnp.where(vals == top, lane, far), axis=1, keepdims=True)

    grp_logits = jnp.where(lane < MOE_GROUPS, logits, neg)
    grp_max, grp = first_argmax(grp_logits)
    p_grp = 1.0 / jnp.sum(jnp.exp(grp_logits - grp_max), axis=1, keepdims=True)
    base = MOE_GROUPS + grp * MOE_PER_GROUP
    in_grp = jnp.logical_and(lane >= base, lane < base + MOE_PER_GROUP)
    exp_logits = jnp.where(in_grp, logits, neg)
    m1, i1 = first_argmax(exp_logits)
    m2, i2 = first_argmax(jnp.where(lane == i1, neg, exp_logits))
    e2 = jnp.exp(m2 - m1)
    g1 = p_grp / (1.0 + e2)
    g2 = g1 * e2
    e_a = i1 - MOE_GROUPS
    e_b = i2 - MOE_GROUPS

    pick_a = lane == e_a
    pick_b = lane == e_b
    both = jnp.where(jnp.logical_or(pick_a, pick_b), 1.0, 0.0)
    row = lax.broadcasted_iota(jnp.int32, (tm, tm), 0)
    col = lax.broadcasted_iota(jnp.int32, (tm, tm), 1)
    earlier = jnp.where(col < row, 1.0, 0.0).astype(BF16)
    before = _dot(earlier, both.astype(BF16)) + run_ref[...]
    rank_a = jnp.sum(jnp.where(pick_a, before, 0.0), axis=1, keepdims=True)
    rank_b = jnp.sum(jnp.where(pick_b, before, 0.0), axis=1, keepdims=True)
    run_ref[...] += jnp.sum(both, axis=0, keepdims=True)
    cnt_ref[...] = run_ref[...]

    route = jnp.where(lane == 0, e_a, jnp.where(lane == 1, e_b, 0.0))
    route = jnp.where(lane == 2, g1, jnp.where(lane == 3, g2, route))
    route_ref[...] = jnp.where(lane == 4, rank_a, jnp.where(lane == 5, rank_b, route))


def _router(x, g, sc, sh, w_route, b_route, tm=512):
    b, s, d = x.shape
    tm = min(tm, s)
    steps = s // tm
    return pl.pallas_call(
        _router_kernel,
        grid=(b, steps),
        in_specs=[
            pl.BlockSpec((1, tm, d), lambda bi, i: (bi, i, 0)),
            pl.BlockSpec((1, d), lambda bi, i: (0, 0)),
            pl.BlockSpec((1, 1, d), lambda bi, i: (bi, 0, 0)),
            pl.BlockSpec((1, 1, d), lambda bi, i: (bi, 0, 0)),
            pl.BlockSpec((d, LANES), lambda bi, i: (0, 0)),
            pl.BlockSpec((1, LANES), lambda bi, i: (0, 0)),
        ],
        out_specs=[
            pl.BlockSpec((tm * (d // LANES), LANES), lambda bi, i: (bi * steps + i, 0)),
            pl.BlockSpec((tm, LANES), lambda bi, i: (bi * steps + i, 0)),
            pl.BlockSpec((1, LANES), lambda bi, i: (0, 0)),
        ],
        out_shape=[
            jax.ShapeDtypeStruct((b * s * (d // LANES), LANES), F32),
            jax.ShapeDtypeStruct((b * s, LANES), F32),
            jax.ShapeDtypeStruct((1, LANES), F32),
        ],
        scratch_shapes=[pltpu.VMEM((1, LANES), F32)],
        compiler_params=_params("arbitrary", "arbitrary"),
        name="moe_router",
    )(x, g.reshape(1, d), sc, sh, w_route, b_route)


def _sc_dispatch(h_slabs, dest, n_slots):
    t, chunks, _ = h_slabs.shape
    workers = SC_CORES * SC_SUBCORES
    per_worker = t // workers
    steps = per_worker // SC_WINDOW
    assert t % (workers * SC_WINDOW * 2) == 0
    mesh = plsc.VectorSubcoreMesh(core_axis_name="core", subcore_axis_name="subcore")

    @functools.partial(
        pl.kernel, mesh=mesh,
        out_type=jax.ShapeDtypeStruct((n_slots, chunks, LANES), h_slabs.dtype),
        scratch_types=[
            pltpu.VMEM((2, MOE_TOPK, SC_WINDOW), jnp.int32),
            pltpu.VMEM((2, SC_WINDOW, chunks, LANES), h_slabs.dtype),
            pltpu.SemaphoreType.DMA((2,)),
            pltpu.SemaphoreType.DMA((2,)),
        ])
    def dispatch(h_hbm, dest_hbm, slots_hbm, idx_v, rows_v, load_sem, store_sem):
        worker = lax.axis_index("subcore") * SC_CORES + lax.axis_index("core")
        base = worker * per_worker

        @pl.loop(0, steps, step=2)
        def _(step):
            for buf in range(2):
                off = base + (step + buf) * SC_WINDOW
                pltpu.async_copy(h_hbm.at[pl.ds(off, SC_WINDOW)], rows_v.at[buf], load_sem.at[buf])
                for k in range(MOE_TOPK):
                    pltpu.sync_copy(dest_hbm.at[k, pl.ds(off, SC_WINDOW)], idx_v.at[buf, k])
            for buf in range(2):
                off = base + (step + buf) * SC_WINDOW
                pltpu.make_async_copy(h_hbm.at[pl.ds(off, SC_WINDOW)], rows_v.at[buf], load_sem.at[buf]).wait()
                copies = [pltpu.async_copy(rows_v.at[buf], slots_hbm.at[idx_v.at[buf, k]], store_sem.at[buf])
                          for k in range(MOE_TOPK)]
                for cp in copies:
                    cp.wait()

    return dispatch(h_slabs, dest)


def _zero_tail_kernel(tail_ref, valid_ref, s_in, s_out):
    del tail_ref
    rows = lax.broadcasted_iota(jnp.int32, s_in.shape, 0)
    s_out[...] = jnp.where(rows < valid_ref[pl.program_id(0)], s_in[...], 0.0)


def _zero_tails(tail_blk, valid_rows, slots, block_rows):
    grid_spec = pltpu.PrefetchScalarGridSpec(
        num_scalar_prefetch=2,
        grid=(tail_blk.shape[0],),
        in_specs=[pl.BlockSpec((block_rows, LANES), lambda e, tail, valid: (tail[e], 0))],
        out_specs=pl.BlockSpec((block_rows, LANES), lambda e, tail, valid: (tail[e], 0)),
    )
    return pl.pallas_call(
        _zero_tail_kernel,
        grid_spec=grid_spec,
        out_shape=jax.ShapeDtypeStruct(slots.shape, slots.dtype),
        input_output_aliases={2: 0},
        compiler_params=_params("arbitrary"),
        name="moe_zero_tails",
    )(tail_blk, valid_rows, slots)


def _expert_kernel(blk_e_ref, used_ref, x_ref, wg_ref, wu_ref, wd_ref, y_ref):
    del blk_e_ref
    live = pl.program_id(0) < used_ref[0]

    @pl.when(live)
    def _():
        chunks = wg_ref.shape[1] // LANES
        bm = x_ref.shape[0] // chunks
        x = jnp.concatenate([x_ref[pl.ds(s, bm, stride=chunks), :] for s in range(chunks)], axis=1).astype(BF16)
        gate = _dot(x, wg_ref[0])
        up = _dot(x, wu_ref[0])
        act = (gate * _sigmoid(gate) * up).astype(BF16)
        y = _dot(act, wd_ref[0])
        for s in range(chunks):
            y_ref[pl.ds(s, bm, stride=chunks), :] = y[:, s * LANES:(s + 1) * LANES]

    @pl.when(jnp.logical_not(live))
    def _():
        y_ref[...] = jnp.zeros_like(y_ref)


def _expert_blocks(blk_e, used, slots, w_gate, w_up, w_down, bm):
    nb = blk_e.shape[0]
    _, d, f = w_gate.shape
    chunks = d // LANES

    def block(i, e, n):
        return (jnp.minimum(i, n[0] - 1), 0)

    def weight(i, e, n):
        return (e[jnp.minimum(i, n[0] - 1)], 0, 0)

    grid_spec = pltpu.PrefetchScalarGridSpec(
        num_scalar_prefetch=2,
        grid=(nb,),
        in_specs=[
            pl.BlockSpec((bm * chunks, LANES), block),
            pl.BlockSpec((1, d, f), weight),
            pl.BlockSpec((1, d, f), weight),
            pl.BlockSpec((1, f, d), weight),
        ],
        out_specs=pl.BlockSpec((bm * chunks, LANES), lambda i, e, n: (i, 0)),
    )
    return pl.pallas_call(
        _expert_kernel,
        grid_spec=grid_spec,
        out_shape=jax.ShapeDtypeStruct((nb * bm * chunks, LANES), F32),
        compiler_params=_params("arbitrary"),
        name="moe_experts",
    )(blk_e, used, slots, w_gate, w_up, w_down)


def _sc_gather(y_slabs, idx):
    n = idx.shape[0]
    _, chunks, _ = y_slabs.shape
    workers = SC_CORES * SC_SUBCORES
    per_worker = n // workers
    steps = per_worker // SC_WINDOW
    assert n % (workers * SC_WINDOW * 2) == 0
    mesh = plsc.VectorSubcoreMesh(core_axis_name="core", subcore_axis_name="subcore")

    @functools.partial(
        pl.kernel, mesh=mesh,
        out_type=jax.ShapeDtypeStruct((n, chunks, LANES), y_slabs.dtype),
        scratch_types=[
            pltpu.VMEM((2, SC_WINDOW), jnp.int32),
            pltpu.VMEM((2, SC_WINDOW, chunks, LANES), y_slabs.dtype),
            pltpu.SemaphoreType.DMA((2,)),
        ])
    def gather(y_hbm, idx_hbm, out_hbm, idx_v, rows_v, sem):
        worker = lax.axis_index("subcore") * SC_CORES + lax.axis_index("core")
        base = worker * per_worker

        @pl.loop(0, steps, step=2)
        def _(step):
            for buf in range(2):
                off = base + (step + buf) * SC_WINDOW
                pltpu.sync_copy(idx_hbm.at[pl.ds(off, SC_WINDOW)], idx_v.at[buf])
                pltpu.async_copy(y_hbm.at[idx_v.at[buf]], rows_v.at[buf], sem.at[buf])
            for buf in range(2):
                off = base + (step + buf) * SC_WINDOW
                pltpu.make_async_copy(y_hbm.at[idx_v.at[buf]], rows_v.at[buf], sem.at[buf]).wait()
                pltpu.sync_copy(rows_v.at[buf], out_hbm.at[pl.ds(off, SC_WINDOW)])

    return gather(y_slabs, idx)


def _combine_kernel(g0_ref, g1_ref, route_ref, x_ref, gate_ref, fg_ref, o_ref, *, final):
    tm, d = x_ref.shape
    chunks = d // LANES

    def rows(ref):
        return jnp.concatenate([ref[pl.ds(s, tm, stride=chunks), :] for s in range(chunks)], axis=1)

    route = route_ref[...]
    y = route[:, 2:3] * rows(g0_ref) + route[:, 3:4] * rows(g1_ref)
    out = x_ref[...] + gate_ref[0] * y
    if final:
        out = out * lax.rsqrt(jnp.mean(out * out, axis=-1, keepdims=True) + RMS_EPS) * fg_ref[...]
    o_ref[...] = out


def _combine(dest, route, y_slabs, x, gate, final_g, final, tm, pieces=4):
    b, s, d = x.shape
    t = b * s
    chunks = d // LANES
    per_seq = s // tm
    piece = t // pieces
    tiles = piece // tm
    out = x.reshape(t, d)
    fg = final_g.reshape(1, d)
    for c in range(pieces):
        idx = dest[:, c * piece:(c + 1) * piece].reshape(MOE_TOPK * piece)
        g = _sc_gather(y_slabs, idx).reshape(MOE_TOPK * piece * chunks, LANES)
        first = c * tiles
        out = pl.pallas_call(
            functools.partial(_combine_kernel, final=final),
            grid=(tiles,),
            in_specs=[
                pl.BlockSpec((tm * chunks, LANES), lambda i: (i, 0)),
                pl.BlockSpec((tm * chunks, LANES), lambda i: (tiles + i, 0)),
                pl.BlockSpec((tm, LANES), lambda i, first=first: (first + i, 0)),
                pl.BlockSpec((tm, d), lambda i, first=first: (first + i, 0)),
                pl.BlockSpec((1, 1, d), lambda i, first=first: ((first + i) // per_seq, 0, 0)),
                pl.BlockSpec((1, d), lambda i: (0, 0)),
            ],
            out_specs=pl.BlockSpec((tm, d), lambda i, first=first: (first + i, 0)),
            out_shape=jax.ShapeDtypeStruct((t, d), F32),
            input_output_aliases={3: 0},
            compiler_params=_params("parallel"),
            name="moe_combine",
        )(g, g, route, out, gate, fg)
    return out.reshape(b, s, d)


def _slot_tables(route, counts, bm):
    t = route.shape[0]
    nb = MOE_TOPK * t // bm + MOE_EXPERTS
    counts = counts[0, :MOE_EXPERTS].astype(jnp.int32)
    padded = (counts + bm - 1) // bm * bm
    pad_end = jnp.cumsum(padded)
    pad_start = pad_end - padded
    expert = route[:, 0:MOE_TOPK].astype(jnp.int32)
    rank = route[:, 4:4 + MOE_TOPK].astype(jnp.int32)
    dest = pad_start[expert] + rank
    blk_start = jnp.arange(nb, dtype=jnp.int32) * bm
    blk_e = jnp.minimum(jnp.sum((pad_end[None, :] <= blk_start[:, None]).astype(jnp.int32), axis=1), MOE_EXPERTS - 1)
    used = (pad_end[-1:] // bm).astype(jnp.int32)
    tail_blk = jnp.maximum(pad_end // bm - 1, 0).astype(jnp.int32)
    tail_fill = jnp.where(counts % bm != 0, counts % bm, bm).astype(jnp.int32)
    return dest.T, blk_e, used, tail_blk, tail_fill, nb


def _moe_layer(x, g, sc, sh, gate, w_route, b_route, w_gate, w_up, w_down, final_g, final, bm=512, tm=256):
    b, s, d = x.shape
    t = b * s
    chunks = d // LANES
    tm = min(tm, s)
    h_rows, route, counts = _router(x, g, sc, sh, w_route, b_route)
    dest, blk_e, used, tail_blk, tail_fill, nb = _slot_tables(route, counts, bm)
    slots = _sc_dispatch(h_rows.reshape(t, chunks, LANES), dest, nb * bm)
    slots = _zero_tails(tail_blk, tail_fill * chunks, slots.reshape(nb * bm * chunks, LANES), bm * chunks)
    y_slabs = _expert_blocks(blk_e, used, slots, w_gate, w_up, w_down, bm).reshape(nb * bm, chunks, LANES)
    return _combine(dest, route, y_slabs, x, gate, final_g, final, tm)


def kernel(x, c, ada_w, ada_b, norm_g, final_g, sb_w_in, sb_w_out, gla_w_in, gla_w_gate_up, gla_b_gate, gla_norm_g, gla_w_out, conv_w_in, conv_w, conv_b, conv_w_out, dil_w_in, dil_w_out, moe_w_grp, moe_b_grp, moe_w_exp, moe_b_exp, moe_w_gate, moe_w_up, moe_w_down):
    b, s, d = x.shape
    depth = ada_w.shape[0]
    mod = _ada_mod(c, ada_w, ada_b).reshape(depth, b, N_MOD, 1, d)
    for i in range(depth):
        sh1, sc1, g1, sh2, sc2, g2 = (mod[i, :, m] for m in range(N_MOD))
        kind, j = i % 4, i // 4
        if kind == 0:
            qkv = _norm_proj(x, norm_g[i, 0], sc1, sh1, sb_w_in[j].astype(BF16))
            x = _proj_res(_sb_attention(qkv), sb_w_out[j].astype(BF16), x, g1)
        elif kind == 1:
            dk = gla_w_gate_up.shape[-1] // GLA_HEADS
            dv = gla_norm_g.shape[-1]
            w_in = jnp.pad(gla_w_in[j], ((0, 0), (0, LANES - GLA_GATE_RANK))).astype(BF16)
            w_up = jnp.pad(gla_w_gate_up[j], ((0, LANES - GLA_GATE_RANK), (0, 0))).astype(BF16)
            proj = _norm_proj(x, norm_g[i, 0], sc1, sh1, w_in, tn=640)
            o = _gla_core(proj, w_up, gla_b_gate[j].reshape(1, -1), gla_norm_g[j].reshape(1, -1), dk, dv)
            x = _proj_res(o, gla_w_out[j].astype(BF16), x, g1)
        elif kind == 2:
            proj = _norm_proj(x, norm_g[i, 0], sc1, sh1, conv_w_in[j].astype(BF16))
            x = _conv_res(proj, conv_w[j], conv_b[j], conv_w_out[j].astype(BF16), x, g1)
        else:
            group_w = 3 * DIL_HEADS * DIL_HEAD_DIM
            outs, lses = [], []
            for g, (window, dilation) in enumerate(DIL_PATTERNS):
                w_g = dil_w_in[j][:, g * group_w:(g + 1) * group_w].astype(BF16)
                view = _norm_proj(x, norm_g[i, 0], sc1, sh1, w_g, dilation=dilation)
                o_g, l_g = _dil_group(view, dilation, window, s)
                outs.append(o_g)
                lses.append(l_g)
            x = _dil_res(outs, lses, dil_w_out[j].astype(BF16), x, g1)
        w_route = jnp.pad(jnp.concatenate([moe_w_grp[i], moe_w_exp[i]], axis=1),
                          ((0, 0), (0, LANES - MOE_GROUPS - MOE_EXPERTS)))
        b_route = jnp.pad(jnp.concatenate([moe_b_grp[i], moe_b_exp[i]]), (0, LANES - MOE_GROUPS - MOE_EXPERTS))
        x = _moe_layer(x, norm_g[i, 1], sc2, sh2, g2, w_route, b_route.reshape(1, LANES),
                       moe_w_gate[i].astype(BF16), moe_w_up[i].astype(BF16), moe_w_down[i].astype(BF16),
                       final_g, final=(i == depth - 1))
    return x
```

```python
import functools

import jax
import jax.numpy as jnp
from jax import lax
from jax.experimental import pallas as pl
from jax.experimental.pallas import tpu as pltpu
from jax.experimental.pallas import tpu_sc as plsc

F32 = jnp.float32
BF16 = jnp.bfloat16

LANES = 128
RMS_EPS = 1e-6
N_MOD = 6

SB_HEADS = 16
GLA_HEADS = 4
GLA_GATE_RANK = 16
GLA_GATE_TAU = 16.0
GLA_CHUNK = 64
CONV_WIDTH = 3
DIL_PATTERNS = ((128, 1), (512, 4), (2048, 16))
DIL_HEADS = 8
DIL_HEAD_DIM = 64
DIL_QBLOCK = 128
DIL_TILE_TOKENS = 2048
MOE_GROUPS = 4
MOE_PER_GROUP = 8
MOE_EXPERTS = MOE_GROUPS * MOE_PER_GROUP
MOE_TOPK = 2

SC_CORES = 2
SC_SUBCORES = 16
SC_WINDOW = 64
PACKED_LANES = 2 * LANES

SB_LOG_ZERO = -104.0
SB_BAND = 2

VMEM_LIMIT = 48 * 1024 * 1024


def _params(*sem):
    return pltpu.CompilerParams(dimension_semantics=sem, vmem_limit_bytes=VMEM_LIMIT)


def _sigmoid(x):
    return 1.0 / (1.0 + jnp.exp(-x))


def _softplus(x):
    return jnp.maximum(x, 0.0) + jnp.log(1.0 + jnp.exp(-jnp.abs(x)))


def _split_bf16(x):
    hi = x.astype(BF16)
    lo = (x - hi.astype(F32)).astype(BF16)
    return hi, lo


def _dot(a, b):
    return jnp.dot(a, b, preferred_element_type=F32)


def _dot_nt(a, b):
    return lax.dot_general(a, b, (((1,), (1,)), ((), ())), preferred_element_type=F32)


def _dot_tn(a, b):
    return lax.dot_general(a, b, (((0,), (0,)), ((), ())), preferred_element_type=F32)


def _unpack_slabs(ref, rows):
    k = ref.shape[0] // rows
    words = jnp.concatenate([ref[pl.ds(s, rows, stride=k), :] for s in range(k)], axis=1)
    halves = [pltpu.unpack_elementwise(words, index=i, packed_dtype=BF16, unpacked_dtype=F32) for i in range(2)]
    return jnp.concatenate(halves, axis=1)


def _pack_slabs(ref, values):
    rows, d = values.shape
    half = d // 2
    words = pltpu.pack_elementwise([values[:, :half], values[:, half:]], packed_dtype=BF16)
    for s in range(half // LANES):
        ref[pl.ds(s, rows, stride=half // LANES), :] = words[:, s * LANES:(s + 1) * LANES]


def _mod_norm(x, g, sc, sh):
    r = lax.rsqrt(jnp.mean(x * x, axis=-1, keepdims=True) + RMS_EPS)
    return (x * r) * g * (1.0 + sc) + sh


def _ada_kernel(c_ref, w_ref, b_ref, o_ref):
    c = c_ref[...]
    cond = c * _sigmoid(c)
    o_ref[0] = _dot(cond.astype(BF16), w_ref[0].astype(BF16)) + b_ref[0]


def _ada_mod(c, ada_w, ada_b):
    depth, d, n = ada_w.shape
    b = c.shape[0]
    tn = 1536
    return pl.pallas_call(
        _ada_kernel,
        grid=(depth, n // tn),
        in_specs=[
            pl.BlockSpec((b, d), lambda i, j: (0, 0)),
            pl.BlockSpec((1, d, tn), lambda i, j: (i, 0, j)),
            pl.BlockSpec((1, 1, tn), lambda i, j: (i, 0, j)),
        ],
        out_specs=pl.BlockSpec((1, b, tn), lambda i, j: (i, 0, j)),
        out_shape=jax.ShapeDtypeStruct((depth, b, n), F32),
        compiler_params=_params("parallel", "parallel"),
        name="ada_mod",
    )(c, ada_w, ada_b.reshape(depth, 1, n))


def _norm_proj_kernel(x_ref, g_ref, sc_ref, sh_ref, w_ref, o_ref, h_ref, *res_ref, dilation, tn):
    h_ref[...] = _mod_norm(x_ref[0], g_ref[...], sc_ref[0], sh_ref[0]).astype(BF16)
    tm = h_ref.shape[0]
    for j in range(w_ref.shape[1] // tn):
        res = _dot(h_ref[...], w_ref[:, j * tn:(j + 1) * tn])
        if dilation == 1:
            o_ref[0, :, j * tn:(j + 1) * tn] = res.astype(o_ref.dtype)
            continue
        for c in range(tn // LANES):
            res_ref[0][j * (tn // LANES) + c] = res[:, c * LANES:(c + 1) * LANES]
        for rho in range(dilation):
            for c in range(tn // LANES):
                rows = res_ref[0][j * (tn // LANES) + c, pl.ds(rho, tm // dilation, stride=dilation), :]
                col = (j * dilation + rho) * tn + c * LANES
                o_ref[0, :, col:col + LANES] = rows.astype(o_ref.dtype)


def _norm_proj(x, g, sc, sh, w, tm=512, tn=512, dilation=1):
    b, s, d = x.shape
    n = w.shape[1]
    tm = min(tm, s)
    assert s % tm == 0 and n % tn == 0 and tm % (16 * dilation) == 0
    scratch = [pltpu.VMEM((tm, d), BF16)]
    if dilation > 1:
        scratch.append(pltpu.VMEM((n // LANES, tm, LANES), F32))
    return pl.pallas_call(
        functools.partial(_norm_proj_kernel, dilation=dilation, tn=tn),
        grid=(b, s // tm),
        in_specs=[
            pl.BlockSpec((1, tm, d), lambda bi, i: (bi, i, 0)),
            pl.BlockSpec((1, d), lambda bi, i: (0, 0)),
            pl.BlockSpec((1, 1, d), lambda bi, i: (bi, 0, 0)),
            pl.BlockSpec((1, 1, d), lambda bi, i: (bi, 0, 0)),
            pl.BlockSpec((d, n), lambda bi, i: (0, 0)),
        ],
        out_specs=pl.BlockSpec((1, tm // dilation, dilation * n), lambda bi, i: (bi, i, 0)),
        out_shape=jax.ShapeDtypeStruct((b, s // dilation, dilation * n), BF16),
        scratch_shapes=scratch,
        compiler_params=_params("parallel", "parallel"),
        name="norm_proj",
    )(x, g.reshape(1, d), sc, sh, w)


def _proj_res_kernel(o_ref, w_ref, x_ref, g_ref, y_ref):
    y_ref[0] = x_ref[0] + g_ref[0] * _dot(o_ref[0], w_ref[...])


def _proj_res(o, w, x, gate, tm=512):
    b, s, d = x.shape
    k = o.shape[-1]
    tm = min(tm, s)
    return pl.pallas_call(
        _proj_res_kernel,
        grid=(b, s // tm),
        in_specs=[
            pl.BlockSpec((1, tm, k), lambda bi, i: (bi, i, 0)),
            pl.BlockSpec((k, d), lambda bi, i: (0, 0)),
            pl.BlockSpec((1, tm, d), lambda bi, i: (bi, i, 0)),
            pl.BlockSpec((1, 1, d), lambda bi, i: (bi, 0, 0)),
        ],
        out_specs=pl.BlockSpec((1, tm, d), lambda bi, i: (bi, i, 0)),
        out_shape=jax.ShapeDtypeStruct((b, s, d), F32),
        compiler_params=_params("parallel", "parallel"),
        name="proj_res",
    )(o, w, x, gate)


def _sb_kernel(q_ref, k_ref, v_ref, o_ref, acc_ref, c_ref, *, blk, hd):
    scale = hd ** -0.5
    fold_scale = (hd & (hd - 1)) == 0 and (hd.bit_length() - 1) % 2 == 0
    first = lax.broadcasted_iota(jnp.int32, (blk, LANES), 1) < hd
    row = lax.broadcasted_iota(jnp.int32, (blk, blk), 0)
    col = lax.broadcasted_iota(jnp.int32, (blk, blk), 1)
    minus_later = jnp.where(row > col, -1.0, 0.0).astype(BF16)
    causal = col < row
    subs = q_ref.shape[1] // blk
    first_block = pl.program_id(2) * subs

    def visit_many(work):
        kv = []
        for _, block, _, _ in work:
            start = pl.multiple_of(block * blk, blk)
            kv.append((k_ref[0, pl.ds(start, blk), :], v_ref[0, pl.ds(start, blk), :]))
        zs = [[_dot_nt(q_heads[h], kb) for h in range(2)] for (q_heads, _, _, _), (kb, _) in zip(work, kv)]
        log_betas, sps = [], []
        for (_, _, mask, _), z2 in zip(work, zs):
            lb2, sp2 = [], []
            for z in z2:
                if not fold_scale:
                    z = z * scale
                sp = _softplus(z)
                lb2.append(z - sp)
                sp2.append(sp if mask is None else jnp.where(mask, sp, 0.0))
            log_betas.append(lb2)
            sps.append(sp2)
        suffixes = [[_dot(sp.astype(BF16), minus_later) for sp in sp2] for sp2 in sps]
        return kv, log_betas, sps, suffixes

    def finish(work, staged, c_in):
        kv, log_betas, sps, suffixes = staged
        cs = {chain: list(c) for chain, c in c_in.items()}
        weights = []
        for n, (_, _, mask, chain) in enumerate(work):
            a2 = []
            for h in range(2):
                a = jnp.exp(log_betas[n][h] + suffixes[n][h] + cs[chain][h])
                a2.append(a if mask is None else jnp.where(mask, a, 0.0))
                cs[chain][h] = cs[chain][h] - jnp.sum(sps[n][h], axis=1, keepdims=True)
            weights.append(a2)
        accs = {}
        for n, (_, _, _, chain) in enumerate(work):
            outs = [_dot(weights[n][h].astype(BF16), kv[n][1]) for h in range(2)]
            more = jnp.where(first, outs[0], outs[1])
            accs[chain] = more if chain not in accs else accs[chain] + more
        return accs, cs

    def remainder(sub, q_heads, depth):
        def cond(carry):
            j, cmax = carry
            return jnp.logical_and(j >= 0, cmax > SB_LOG_ZERO)

        def body(carry):
            j, _ = carry
            work = [(q_heads, j, None, sub)]
            accs, cs = finish(work, visit_many(work), {sub: [c_ref[sub, 0], c_ref[sub, 1]]})
            acc_ref[sub] += accs[sub]
            for h in range(2):
                c_ref[sub, h] = cs[sub][h]
            return j - 1, jnp.max(c_ref[sub])

        lax.while_loop(cond, body, (first_block + sub - depth - 1, jnp.max(c_ref[sub])))

    def run(depth):
        heads = []
        for sub in range(subs):
            q = q_ref[0, sub * blk:(sub + 1) * blk, :]
            if fold_scale:
                q = (q.astype(F32) * scale).astype(BF16)
            zero = jnp.zeros_like(q)
            heads.append((jnp.where(first, q, zero), jnp.where(first, zero, q)))
        work = [(heads[sub], first_block + sub - back, causal if back == 0 else None, sub)
                for sub in range(subs) for back in range(depth + 1)]
        zero_c = [jnp.zeros((blk, 1), F32)] * 2
        accs, cs = finish(work, visit_many(work), {sub: zero_c for sub in range(subs)})
        for sub in range(subs):
            acc_ref[sub] = accs[sub]
            for h in range(2):
                c_ref[sub, h] = cs[sub][h]
        for sub in range(subs):
            remainder(sub, heads[sub], depth)
            o_ref[0, sub * blk:(sub + 1) * blk, :] = acc_ref[sub].astype(o_ref.dtype)

    pl.when(first_block >= SB_BAND)(functools.partial(run, SB_BAND))
    pl.when(first_block < SB_BAND)(functools.partial(run, 0))


def _sb_attention(qkv, blk=128, rows=512):
    b, s, n = qkv.shape
    width = n // 3
    hd = width // SB_HEADS
    assert 2 * hd == LANES
    pairs = width // LANES
    rows = min(rows, s)
    return pl.pallas_call(
        functools.partial(_sb_kernel, blk=blk, hd=hd),
        grid=(b, pairs, s // rows),
        in_specs=[
            pl.BlockSpec((1, rows, LANES), lambda bi, p, i: (bi, i, p)),
            pl.BlockSpec((1, s, LANES), lambda bi, p, i: (bi, 0, pairs + p)),
            pl.BlockSpec((1, s, LANES), lambda bi, p, i: (bi, 0, 2 * pairs + p)),
        ],
        out_specs=pl.BlockSpec((1, rows, LANES), lambda bi, p, i: (bi, i, p)),
        out_shape=jax.ShapeDtypeStruct((b, s, width), BF16),
        scratch_shapes=[pltpu.VMEM((rows // blk, blk, LANES), F32), pltpu.VMEM((rows // blk, 2, blk, 1), F32)],
        compiler_params=_params("parallel", "parallel", "arbitrary"),
        name="sb_attention",
    )(qkv, qkv, qkv)


def _gla_kernel(p_ref, wg_ref, bg_ref, ng_ref, o_ref, st_ref, *, rows, dk, dv):
    @pl.when(pl.program_id(1) == 0)
    def _():
        st_ref[...] = jnp.zeros_like(st_ref)

    heads = GLA_HEADS
    ch = GLA_CHUNK
    k0 = heads * dk
    v0 = 2 * heads * dk
    r0 = v0 + heads * dv
    g0 = r0 + heads * dv
    row = lax.broadcasted_iota(jnp.int32, (ch, ch), 0)
    col = lax.broadcasted_iota(jnp.int32, (ch, ch), 1)
    causal = col <= row
    lower = jnp.where(causal, 1.0, 0.0).astype(BF16)
    chunks = [slice(s * ch, (s + 1) * ch) for s in range(rows // ch)]
    g_pres = [_dot(p_ref[0, t, g0:g0 + LANES], wg_ref[...]) + bg_ref[...] for t in chunks]
    cums = []
    for g_pre in g_pres:
        log_a = (jnp.minimum(g_pre, 0.0) - jnp.log(1.0 + jnp.exp(-jnp.abs(g_pre)))) * (1.0 / GLA_GATE_TAU)
        hi, lo = _split_bf16(log_a)
        cums.append(_dot(lower, hi) + _dot(lower, lo))
    parts = []
    for t, cum in zip(chunks, cums):
        for h in range(heads):
            bh = cum[:, h * dk:(h + 1) * dk]
            b_last = bh[ch - 1:ch, :]
            q = p_ref[0, t, h * dk:(h + 1) * dk].astype(F32) * dk ** -0.5
            k = p_ref[0, t, k0 + h * dk:k0 + (h + 1) * dk].astype(F32)
            q_in = (q * jnp.exp(bh)).astype(BF16)
            k_in = (k * jnp.exp(-bh)).astype(BF16)
            k_state = (k * jnp.exp(b_last - bh)).astype(BF16)
            parts.append((t, h, q_in, k_in, k_state, jnp.exp(b_last)))
    scores = [jnp.where(causal, _dot_nt(q_in, k_in), 0.0).astype(BF16) for _, _, q_in, k_in, _, _ in parts]
    values = [p_ref[0, t, v0 + h * dv:v0 + (h + 1) * dv] for t, h, _, _, _, _ in parts]
    intra = [_dot(sc, v) for sc, v in zip(scores, values)]
    updates = [_dot_tn(v, k_state) for v, (_, _, _, _, k_state, _) in zip(values, parts)]
    states = [st_ref[h] for h in range(heads)]
    for n, (t, h, q_in, _, _, decay) in enumerate(parts):
        o = intra[n] + _dot_nt(q_in, states[h].astype(BF16))
        states[h] = decay * states[h] + updates[n]
        o = o * lax.rsqrt(jnp.mean(o * o, axis=-1, keepdims=True) + RMS_EPS) * ng_ref[...]
        r = p_ref[0, t, r0 + h * dv:r0 + (h + 1) * dv].astype(F32)
        o_ref[0, t, h * dv:(h + 1) * dv] = (o * (r * _sigmoid(r))).astype(o_ref.dtype)
    for h in range(heads):
        st_ref[h] = states[h]


def _gla_core(proj, w_gate_up, b_gate, norm_g, dk, dv, rows=256):
    b, s, n = proj.shape
    rows = min(rows, s)
    hk = GLA_HEADS * dk
    return pl.pallas_call(
        functools.partial(_gla_kernel, rows=rows, dk=dk, dv=dv),
        grid=(b, s // rows),
        in_specs=[
            pl.BlockSpec((1, rows, n), lambda bi, i: (bi, i, 0)),
            pl.BlockSpec((LANES, hk), lambda bi, i: (0, 0)),
            pl.BlockSpec((1, hk), lambda bi, i: (0, 0)),
            pl.BlockSpec((1, dv), lambda bi, i: (0, 0)),
        ],
        out_specs=pl.BlockSpec((1, rows, GLA_HEADS * dv), lambda bi, i: (bi, i, 0)),
        out_shape=jax.ShapeDtypeStruct((b, s, GLA_HEADS * dv), BF16),
        scratch_shapes=[pltpu.VMEM((GLA_HEADS, dv, dk), F32)],
        compiler_params=_params("parallel", "arbitrary"),
        name="gla_core",
    )(proj, w_gate_up, b_gate, norm_g)


def _conv_res_kernel(gb_ref, gc_ref, u_ref, hgc_ref, hu_ref, cw_ref, cb_ref, w_ref, x_ref, g_ref, y_ref):
    u2 = gc_ref[0].astype(F32) * u_ref[0].astype(F32)
    halo = hgc_ref[0].astype(F32) * hu_ref[0].astype(F32)
    halo = jnp.where(pl.program_id(1) > 0, halo, 0.0)
    rows = lax.broadcasted_iota(jnp.int32, u2.shape, 0)
    prev1 = jnp.where(rows == 0, halo[7:8], pltpu.roll(u2, 1, 0))
    prev2 = jnp.where(rows == 0, halo[6:7], jnp.where(rows == 1, halo[7:8], pltpu.roll(u2, 2, 0)))
    cw = cw_ref[...]
    y = cw[0:1] * prev2 + cw[1:2] * prev1 + cw[2:3] * u2 + cb_ref[...]
    o = (gb_ref[0].astype(F32) * y).astype(BF16)
    y_ref[0] = x_ref[0] + g_ref[0] * _dot(o, w_ref[...])


def _conv_res(proj, conv_w, conv_b, w_out, x, gate, tm=512):
    b, s, d = x.shape
    tm = min(tm, s)
    halo_blocks = tm // 8

    def halo_map(col):
        return lambda bi, i: (bi, jnp.maximum(i * halo_blocks - 1, 0), col)

    return pl.pallas_call(
        _conv_res_kernel,
        grid=(b, s // tm),
        in_specs=[
            pl.BlockSpec((1, tm, d), lambda bi, i: (bi, i, 0)),
            pl.BlockSpec((1, tm, d), lambda bi, i: (bi, i, 1)),
            pl.BlockSpec((1, tm, d), lambda bi, i: (bi, i, 2)),
            pl.BlockSpec((1, 8, d), halo_map(1)),
            pl.BlockSpec((1, 8, d), halo_map(2)),
            pl.BlockSpec((CONV_WIDTH, d), lambda bi, i: (0, 0)),
            pl.BlockSpec((1, d), lambda bi, i: (0, 0)),
            pl.BlockSpec((d, d), lambda bi, i: (0, 0)),
            pl.BlockSpec((1, tm, d), lambda bi, i: (bi, i, 0)),
            pl.BlockSpec((1, 1, d), lambda bi, i: (bi, 0, 0)),
        ],
        out_specs=pl.BlockSpec((1, tm, d), lambda bi, i: (bi, i, 0)),
        out_shape=jax.ShapeDtypeStruct((b, s, d), F32),
        compiler_params=_params("parallel", "parallel"),
        name="conv_res",
    )(proj, proj, proj, proj, proj, conv_w, conv_b.reshape(1, d), w_out, x, gate)


def _dil_kernel(q_ref, kc_ref, vc_ref, kp_ref, vp_ref, o_ref, l_ref, *, rows, hd, dilation):
    qb = DIL_QBLOCK
    n = pl.program_id(1)
    rho = pl.program_id(2)
    scale = hd ** -0.5
    fold_scale = (hd & (hd - 1)) == 0 and (hd.bit_length() - 1) % 2 == 0
    first = lax.broadcasted_iota(jnp.int32, (qb, LANES), 1) < hd
    row = lax.broadcasted_iota(jnp.int32, (qb, qb), 0)
    col = lax.broadcasted_iota(jnp.int32, (qb, qb), 1)
    in_prev = col >= row
    in_cur = col <= row
    neg = -jnp.inf

    tiles = []
    for s in range(rows // qb):
        t = slice(s * qb, (s + 1) * qb)
        tp = slice((s - 1) * qb, s * qb)
        prev_mask = jnp.logical_and(in_prev, n > 0) if s == 0 else in_prev
        for p in range(q_ref.shape[-1] // LANES):
            c = slice(p * LANES, (p + 1) * LANES)
            q = q_ref[0, t, c]
            if fold_scale:
                q = (q.astype(F32) * scale).astype(BF16)
            zero = jnp.zeros_like(q)
            q_heads = (jnp.where(first, q, zero), jnp.where(first, zero, q))
            if s == 0:
                k_prev, v_prev = kp_ref[0, :, c], vp_ref[0, :, c]
            else:
                k_prev, v_prev = kc_ref[0, tp, c], vc_ref[0, tp, c]
            tiles.append((s, p, q_heads, k_prev, kc_ref[0, t, c], v_prev, vc_ref[0, t, c], prev_mask))

    scores = []
    for _, _, q_heads, k_prev, k_cur, _, _, prev_mask in tiles:
        pair = []
        for h in range(2):
            z_prev, z_cur = _dot_nt(q_heads[h], k_prev), _dot_nt(q_heads[h], k_cur)
            if not fold_scale:
                z_prev, z_cur = z_prev * scale, z_cur * scale
            pair.append((jnp.where(prev_mask, z_prev, neg), jnp.where(in_cur, z_cur, neg)))
        scores.append(pair)

    probs = []
    for pair in scores:
        stats = []
        for z_prev, z_cur in pair:
            m = jnp.max(jnp.maximum(z_prev, z_cur), axis=1, keepdims=True)
            p_prev = jnp.exp(z_prev - m)
            p_cur = jnp.exp(z_cur - m)
            den = jnp.sum(p_prev + p_cur, axis=1, keepdims=True)
            stats.append((p_prev.astype(BF16), p_cur.astype(BF16), den, m))
        probs.append(stats)

    for (s, p, _, _, _, v_prev, v_cur, _), stats in zip(tiles, probs):
        outs, lses = [], []
        for p_prev, p_cur, den, m in stats:
            outs.append((_dot(p_prev, v_prev) + _dot(p_cur, v_cur)) / den)
            lses.append(jnp.broadcast_to(m + jnp.log(den), (qb, LANES)))
        if dilation == 1:
            t_out = slice(s * qb, (s + 1) * qb)
        else:
            t_out = pl.ds(s * qb * dilation + rho, qb, stride=dilation)
        o_ref[0, p, t_out, :] = jnp.where(first, outs[0], outs[1])
        l_ref[0, p, t_out, :] = jnp.where(first, lses[0], lses[1])


def _dil_group(view, dilation, window, seq):
    b, length, _ = view.shape
    width = DIL_HEADS * DIL_HEAD_DIM
    assert window == dilation * DIL_QBLOCK and seq % (dilation * DIL_QBLOCK) == 0
    rows = min(max(DIL_TILE_TOKENS // dilation, DIL_QBLOCK), 512, length)
    prev_blocks = rows // DIL_QBLOCK

    def cur(part):
        return lambda bi, i, rho: (bi, i, part * dilation + rho)

    def prev(part):
        return lambda bi, i, rho: (bi, jnp.maximum(i * prev_blocks - 1, 0), part * dilation + rho)

    pairs = width // LANES
    out_shape = jax.ShapeDtypeStruct((b, pairs, seq, LANES), F32)
    out_spec = pl.BlockSpec((1, pairs, rows * dilation, LANES), lambda bi, i, rho: (bi, 0, i, 0))
    return pl.pallas_call(
        functools.partial(_dil_kernel, rows=rows, hd=DIL_HEAD_DIM, dilation=dilation),
        grid=(b, length // rows, dilation),
        in_specs=[
            pl.BlockSpec((1, rows, width), cur(0)),
            pl.BlockSpec((1, rows, width), cur(1)),
            pl.BlockSpec((1, rows, width), cur(2)),
            pl.BlockSpec((1, DIL_QBLOCK, width), prev(1)),
            pl.BlockSpec((1, DIL_QBLOCK, width), prev(2)),
        ],
        out_specs=[out_spec, out_spec],
        out_shape=[out_shape, out_shape],
        compiler_params=_params("parallel", "parallel", "arbitrary"),
        name=f"dil_attn_r{dilation}",
    )(view, view, view, view, view)


def _dil_res_kernel(o0, o1, o2, l0, l1, l2, w_ref, x_ref, g_ref, y_ref):
    parts = []
    for p in range(o0.shape[1]):
        ls = (l0[0, p], l1[0, p], l2[0, p])
        m = jnp.maximum(jnp.maximum(ls[0], ls[1]), ls[2])
        es = [jnp.exp(l - m) for l in ls]
        parts.append((es[0] * o0[0, p] + es[1] * o1[0, p] + es[2] * o2[0, p]) / (es[0] + es[1] + es[2]))
    o = jnp.concatenate(parts, axis=1)
    y_ref[0] = x_ref[0] + g_ref[0] * _dot(o.astype(BF16), w_ref[...])


def _dil_res(outs, lses, w_out, x, gate, tm=512):
    b, s, d = x.shape
    k = w_out.shape[0]
    tm = min(tm, s)
    part = pl.BlockSpec((1, k // LANES, tm, LANES), lambda bi, i: (bi, 0, i, 0))
    return pl.pallas_call(
        _dil_res_kernel,
        grid=(b, s // tm),
        in_specs=[part] * 6 + [
            pl.BlockSpec((k, d), lambda bi, i: (0, 0)),
            pl.BlockSpec((1, tm, d), lambda bi, i: (bi, i, 0)),
            pl.BlockSpec((1, 1, d), lambda bi, i: (bi, 0, 0)),
        ],
        out_specs=pl.BlockSpec((1, tm, d), lambda bi, i: (bi, i, 0)),
        out_shape=jax.ShapeDtypeStruct((b, s, d), F32),
        compiler_params=_params("parallel", "parallel"),
        name="dil_res",
    )(*outs, *lses, w_out, x, gate)


def _router_kernel(x_ref, g_ref, sc_ref, sh_ref, wr_ref, br_ref, h_ref, route_ref, cnt_ref, run_ref):
    @pl.when(jnp.logical_and(pl.program_id(0) == 0, pl.program_id(1) == 0))
    def _():
        run_ref[...] = jnp.zeros_like(run_ref)

    h = _mod_norm(x_ref[0], g_ref[...], sc_ref[0], sh_ref[0])
    _pack_slabs(h_ref, h)
    h_hi, h_lo = _split_bf16(h)
    w_hi, w_lo = _split_bf16(wr_ref[...])
    logits = _dot(h_hi, w_hi) + (_dot(h_hi, w_lo) + _dot(h_lo, w_hi)) + br_ref[...]
    tm = logits.shape[0]
    lane = lax.broadcasted_iota(jnp.int32, logits.shape, 1).astype(F32)
    neg = -jnp.inf
    far = float(LANES)

    def first_argmax(vals):
        top = jnp.max(vals, axis=1, keepdims=True)
        return top, jnp.min(jnp.where(vals == top, lane, far), axis=1, keepdims=True)

    grp_logits = jnp.where(lane < MOE_GROUPS, logits, neg)
    grp_max, grp = first_argmax(grp_logits)
    p_grp = 1.0 / jnp.sum(jnp.exp(grp_logits - grp_max), axis=1, keepdims=True)
    base = MOE_GROUPS + grp * MOE_PER_GROUP
    in_grp = jnp.logical_and(lane >= base, lane < base + MOE_PER_GROUP)
    exp_logits = jnp.where(in_grp, logits, neg)
    m1, i1 = first_argmax(exp_logits)
    m2, i2 = first_argmax(jnp.where(lane == i1, neg, exp_logits))
    e2 = jnp.exp(m2 - m1)
    g1 = p_grp / (1.0 + e2)
    g2 = g1 * e2
    e_a = i1 - MOE_GROUPS
    e_b = i2 - MOE_GROUPS

    pick_a = lane == e_a
    pick_b = lane == e_b
    both = jnp.where(jnp.logical_or(pick_a, pick_b), 1.0, 0.0)
    row = lax.broadcasted_iota(jnp.int32, (tm, tm), 0)
    col = lax.broadcasted_iota(jnp.int32, (tm, tm), 1)
    earlier = jnp.where(col < row, 1.0, 0.0).astype(BF16)
    before = _dot(earlier, both.astype(BF16)) + run_ref[...]
    rank_a = jnp.sum(jnp.where(pick_a, before, 0.0), axis=1, keepdims=True)
    rank_b = jnp.sum(jnp.where(pick_b, before, 0.0), axis=1, keepdims=True)
    run_ref[...] += jnp.sum(both, axis=0, keepdims=True)
    cnt_ref[...] = run_ref[...]

    route = jnp.where(lane == 0, e_a, jnp.where(lane == 1, e_b, 0.0))
    route = jnp.where(lane == 2, g1, jnp.where(lane == 3, g2, route))
    route_ref[...] = jnp.where(lane == 4, rank_a, jnp.where(lane == 5, rank_b, route))


def _router(x, g, sc, sh, w_route, b_route, tm=512):
    b, s, d = x.shape
    tm = min(tm, s)
    steps = s // tm
    return pl.pallas_call(
        _router_kernel,
        grid=(b, steps),
        in_specs=[
            pl.BlockSpec((1, tm, d), lambda bi, i: (bi, i, 0)),
            pl.BlockSpec((1, d), lambda bi, i: (0, 0)),
            pl.BlockSpec((1, 1, d), lambda bi, i: (bi, 0, 0)),
            pl.BlockSpec((1, 1, d), lambda bi, i: (bi, 0, 0)),
            pl.BlockSpec((d, LANES), lambda bi, i: (0, 0)),
            pl.BlockSpec((1, LANES), lambda bi, i: (0, 0)),
        ],
        out_specs=[
            pl.BlockSpec((tm * (d // PACKED_LANES), LANES), lambda bi, i: (bi * steps + i, 0)),
            pl.BlockSpec((tm, LANES), lambda bi, i: (bi * steps + i, 0)),
            pl.BlockSpec((1, LANES), lambda bi, i: (0, 0)),
        ],
        out_shape=[
            jax.ShapeDtypeStruct((b * s * (d // PACKED_LANES), LANES), jnp.int32),
            jax.ShapeDtypeStruct((b * s, LANES), F32),
            jax.ShapeDtypeStruct((1, LANES), F32),
        ],
        scratch_shapes=[pltpu.VMEM((1, LANES), F32)],
        compiler_params=_params("arbitrary", "arbitrary"),
        name="moe_router",
    )(x, g.reshape(1, d), sc, sh, w_route, b_route)


def _sc_dispatch(h_slabs, dest, n_slots):
    t, chunks, _ = h_slabs.shape
    workers = SC_CORES * SC_SUBCORES
    per_worker = t // workers
    steps = per_worker // SC_WINDOW
    assert t % (workers * SC_WINDOW * 2) == 0
    mesh = plsc.VectorSubcoreMesh(core_axis_name="core", subcore_axis_name="subcore")

    @functools.partial(
        pl.kernel, mesh=mesh,
        out_type=jax.ShapeDtypeStruct((n_slots, chunks, LANES), h_slabs.dtype),
        scratch_types=[
            pltpu.VMEM((2, MOE_TOPK, SC_WINDOW), jnp.int32),
            pltpu.VMEM((2, SC_WINDOW, chunks, LANES), h_slabs.dtype),
            pltpu.SemaphoreType.DMA((2,)),
            pltpu.SemaphoreType.DMA((2,)),
        ])
    def dispatch(h_hbm, dest_hbm, slots_hbm, idx_v, rows_v, load_sem, store_sem):
        worker = lax.axis_index("subcore") * SC_CORES + lax.axis_index("core")
        base = worker * per_worker

        @pl.loop(0, steps, step=2)
        def _(step):
            for buf in range(2):
                off = base + (step + buf) * SC_WINDOW
                pltpu.async_copy(h_hbm.at[pl.ds(off, SC_WINDOW)], rows_v.at[buf], load_sem.at[buf])
                for k in range(MOE_TOPK):
                    pltpu.sync_copy(dest_hbm.at[k, pl.ds(off, SC_WINDOW)], idx_v.at[buf, k])
            for buf in range(2):
                off = base + (step + buf) * SC_WINDOW
                pltpu.make_async_copy(h_hbm.at[pl.ds(off, SC_WINDOW)], rows_v.at[buf], load_sem.at[buf]).wait()
                copies = [pltpu.async_copy(rows_v.at[buf], slots_hbm.at[idx_v.at[buf, k]], store_sem.at[buf])
                          for k in range(MOE_TOPK)]
                for cp in copies:
                    cp.wait()

    return dispatch(h_slabs, dest)


def _zero_tail_kernel(tail_ref, valid_ref, s_in, s_out):
    del tail_ref
    rows = lax.broadcasted_iota(jnp.int32, s_in.shape, 0)
    s_out[...] = jnp.where(rows < valid_ref[pl.program_id(0)], s_in[...], 0)


def _zero_tails(tail_blk, valid_rows, slots, block_rows):
    grid_spec = pltpu.PrefetchScalarGridSpec(
        num_scalar_prefetch=2,
        grid=(tail_blk.shape[0],),
        in_specs=[pl.BlockSpec((block_rows, LANES), lambda e, tail, valid: (tail[e], 0))],
        out_specs=pl.BlockSpec((block_rows, LANES), lambda e, tail, valid: (tail[e], 0)),
    )
    return pl.pallas_call(
        _zero_tail_kernel,
        grid_spec=grid_spec,
        out_shape=jax.ShapeDtypeStruct(slots.shape, slots.dtype),
        input_output_aliases={2: 0},
        compiler_params=_params("arbitrary"),
        name="moe_zero_tails",
    )(tail_blk, valid_rows, slots)


def _expert_kernel(blk_e_ref, used_ref, x_ref, wg_ref, wu_ref, wd_ref, y_ref):
    del blk_e_ref
    live = pl.program_id(0) < used_ref[0]

    @pl.when(live)
    def _():
        bm = x_ref.shape[0] // (wg_ref.shape[1] // PACKED_LANES)
        x = _unpack_slabs(x_ref, bm).astype(BF16)
        gate = _dot(x, wg_ref[0])
        up = _dot(x, wu_ref[0])
        act = (gate * _sigmoid(gate) * up).astype(BF16)
        _pack_slabs(y_ref, _dot(act, wd_ref[0]))

    @pl.when(jnp.logical_not(live))
    def _():
        y_ref[...] = jnp.zeros_like(y_ref)


def _expert_blocks(blk_e, used, slots, w_gate, w_up, w_down, bm):
    nb = blk_e.shape[0]
    _, d, f = w_gate.shape
    chunks = d // PACKED_LANES

    def block(i, e, n):
        return (jnp.minimum(i, n[0] - 1), 0)

    def weight(i, e, n):
        return (e[jnp.minimum(i, n[0] - 1)], 0, 0)

    grid_spec = pltpu.PrefetchScalarGridSpec(
        num_scalar_prefetch=2,
        grid=(nb,),
        in_specs=[
            pl.BlockSpec((bm * chunks, LANES), block),
            pl.BlockSpec((1, d, f), weight),
            pl.BlockSpec((1, d, f), weight),
            pl.BlockSpec((1, f, d), weight),
        ],
        out_specs=pl.BlockSpec((bm * chunks, LANES), lambda i, e, n: (i, 0)),
    )
    return pl.pallas_call(
        _expert_kernel,
        grid_spec=grid_spec,
        out_shape=jax.ShapeDtypeStruct((nb * bm * chunks, LANES), jnp.int32),
        compiler_params=_params("arbitrary"),
        name="moe_experts",
    )(blk_e, used, slots, w_gate, w_up, w_down)


def _sc_gather(y_slabs, idx):
    n = idx.shape[0]
    _, chunks, _ = y_slabs.shape
    workers = SC_CORES * SC_SUBCORES
    per_worker = n // workers
    steps = per_worker // SC_WINDOW
    assert n % (workers * SC_WINDOW * 2) == 0
    mesh = plsc.VectorSubcoreMesh(core_axis_name="core", subcore_axis_name="subcore")

    @functools.partial(
        pl.kernel, mesh=mesh,
        out_type=jax.ShapeDtypeStruct((n, chunks, LANES), y_slabs.dtype),
        scratch_types=[
            pltpu.VMEM((2, SC_WINDOW), jnp.int32),
            pltpu.VMEM((2, SC_WINDOW, chunks, LANES), y_slabs.dtype),
            pltpu.SemaphoreType.DMA((2,)),
        ])
    def gather(y_hbm, idx_hbm, out_hbm, idx_v, rows_v, sem):
        worker = lax.axis_index("subcore") * SC_CORES + lax.axis_index("core")
        base = worker * per_worker

        @pl.loop(0, steps, step=2)
        def _(step):
            for buf in range(2):
                off = base + (step + buf) * SC_WINDOW
                pltpu.sync_copy(idx_hbm.at[pl.ds(off, SC_WINDOW)], idx_v.at[buf])
                pltpu.async_copy(y_hbm.at[idx_v.at[buf]], rows_v.at[buf], sem.at[buf])
            for buf in range(2):
                off = base + (step + buf) * SC_WINDOW
                pltpu.make_async_copy(y_hbm.at[idx_v.at[buf]], rows_v.at[buf], sem.at[buf]).wait()
                pltpu.sync_copy(rows_v.at[buf], out_hbm.at[pl.ds(off, SC_WINDOW)])

    return gather(y_slabs, idx)


def _combine_kernel(g0_ref, g1_ref, route_ref, x_ref, gate_ref, fg_ref, o_ref, *, final):
    tm = x_ref.shape[0]
    route = route_ref[...]
    y = route[:, 2:3] * _unpack_slabs(g0_ref, tm) + route[:, 3:4] * _unpack_slabs(g1_ref, tm)
    out = x_ref[...] + gate_ref[0] * y
    if final:
        out = out * lax.rsqrt(jnp.mean(out * out, axis=-1, keepdims=True) + RMS_EPS) * fg_ref[...]
    o_ref[...] = out


def _combine(dest, route, y_slabs, x, gate, final_g, final, tm, pieces=4):
    b, s, d = x.shape
    t = b * s
    chunks = d // PACKED_LANES
    per_seq = s // tm
    piece = t // pieces
    tiles = piece // tm
    out = x.reshape(t, d)
    fg = final_g.reshape(1, d)
    for c in range(pieces):
        idx = dest[:, c * piece:(c + 1) * piece].reshape(MOE_TOPK * piece)
        g = _sc_gather(y_slabs, idx).reshape(MOE_TOPK * piece * chunks, LANES)
        first = c * tiles
        out = pl.pallas_call(
            functools.partial(_combine_kernel, final=final),
            grid=(tiles,),
            in_specs=[
                pl.BlockSpec((tm * chunks, LANES), lambda i: (i, 0)),
                pl.BlockSpec((tm * chunks, LANES), lambda i: (tiles + i, 0)),
                pl.BlockSpec((tm, LANES), lambda i, first=first: (first + i, 0)),
                pl.BlockSpec((tm, d), lambda i, first=first: (first + i, 0)),
                pl.BlockSpec((1, 1, d), lambda i, first=first: ((first + i) // per_seq, 0, 0)),
                pl.BlockSpec((1, d), lambda i: (0, 0)),
            ],
            out_specs=pl.BlockSpec((tm, d), lambda i, first=first: (first + i, 0)),
            out_shape=jax.ShapeDtypeStruct((t, d), F32),
            input_output_aliases={3: 0},
            compiler_params=_params("parallel"),
            name="moe_combine",
        )(g, g, route, out, gate, fg)
    return out.reshape(b, s, d)


def _slot_tables(route, counts, bm):
    t = route.shape[0]
    nb = MOE_TOPK * t // bm + MOE_EXPERTS
    counts = counts[0, :MOE_EXPERTS].astype(jnp.int32)
    padded = (counts + bm - 1) // bm * bm
    pad_end = jnp.cumsum(padded)
    pad_start = pad_end - padded
    expert = route[:, 0:MOE_TOPK].astype(jnp.int32)
    rank = route[:, 4:4 + MOE_TOPK].astype(jnp.int32)
    dest = pad_start[expert] + rank
    blk_start = jnp.arange(nb, dtype=jnp.int32) * bm
    blk_e = jnp.minimum(jnp.sum((pad_end[None, :] <= blk_start[:, None]).astype(jnp.int32), axis=1), MOE_EXPERTS - 1)
    used = (pad_end[-1:] // bm).astype(jnp.int32)
    tail_blk = jnp.maximum(pad_end // bm - 1, 0).astype(jnp.int32)
    tail_fill = jnp.where(counts % bm != 0, counts % bm, bm).astype(jnp.int32)
    return dest.T, blk_e, used, tail_blk, tail_fill, nb


def _moe_layer(x, g, sc, sh, gate, w_route, b_route, w_gate, w_up, w_down, final_g, final, bm=512, tm=256):
    b, s, d = x.shape
    t = b * s
    chunks = d // PACKED_LANES
    tm = min(tm, s)
    h_rows, route, counts = _router(x, g, sc, sh, w_route, b_route)
    dest, blk_e, used, tail_blk, tail_fill, nb = _slot_tables(route, counts, bm)
    slots = _sc_dispatch(h_rows.reshape(t, chunks, LANES), dest, nb * bm)
    slots = _zero_tails(tail_blk, tail_fill * chunks, slots.reshape(nb * bm * chunks, LANES), bm * chunks)
    y_slabs = _expert_blocks(blk_e, used, slots, w_gate, w_up, w_down, bm).reshape(nb * bm, chunks, LANES)
    return _combine(dest, route, y_slabs, x, gate, final_g, final, tm)


def kernel(x, c, ada_w, ada_b, norm_g, final_g, sb_w_in, sb_w_out, gla_w_in, gla_w_gate_up, gla_b_gate, gla_norm_g, gla_w_out, conv_w_in, conv_w, conv_b, conv_w_out, dil_w_in, dil_w_out, moe_w_grp, moe_b_grp, moe_w_exp, moe_b_exp, moe_w_gate, moe_w_up, moe_w_down):
    b, s, d = x.shape
    depth = ada_w.shape[0]
    mod = _ada_mod(c, ada_w, ada_b).reshape(depth, b, N_MOD, 1, d)
    for i in range(depth):
        sh1, sc1, g1, sh2, sc2, g2 = (mod[i, :, m] for m in range(N_MOD))
        kind, j = i % 4, i // 4
        if kind == 0:
            qkv = _norm_proj(x, norm_g[i, 0], sc1, sh1, sb_w_in[j].astype(BF16))
            x = _proj_res(_sb_attention(qkv), sb_w_out[j].astype(BF16), x, g1)
        elif kind == 1:
            dk = gla_w_gate_up.shape[-1] // GLA_HEADS
            dv = gla_norm_g.shape[-1]
            w_in = jnp.pad(gla_w_in[j], ((0, 0), (0, LANES - GLA_GATE_RANK))).astype(BF16)
            w_up = jnp.pad(gla_w_gate_up[j], ((0, LANES - GLA_GATE_RANK), (0, 0))).astype(BF16)
            proj = _norm_proj(x, norm_g[i, 0], sc1, sh1, w_in, tn=640)
            o = _gla_core(proj, w_up, gla_b_gate[j].reshape(1, -1), gla_norm_g[j].reshape(1, -1), dk, dv)
            x = _proj_res(o, gla_w_out[j].astype(BF16), x, g1)
        elif kind == 2:
            proj = _norm_proj(x, norm_g[i, 0], sc1, sh1, conv_w_in[j].astype(BF16))
            x = _conv_res(proj, conv_w[j], conv_b[j], conv_w_out[j].astype(BF16), x, g1)
        else:
            group_w = 3 * DIL_HEADS * DIL_HEAD_DIM
            outs, lses = [], []
            for g, (window, dilation) in enumerate(DIL_PATTERNS):
                w_g = dil_w_in[j][:, g * group_w:(g + 1) * group_w].astype(BF16)
                view = _norm_proj(x, norm_g[i, 0], sc1, sh1, w_g, dilation=dilation)
                o_g, l_g = _dil_group(view, dilation, window, s)
                outs.append(o_g)
                lses.append(l_g)
            x = _dil_res(outs, lses, dil_w_out[j].astype(BF16), x, g1)
        w_route = jnp.pad(jnp.concatenate([moe_w_grp[i], moe_w_exp[i]], axis=1),
                          ((0, 0), (0, LANES - MOE_GROUPS - MOE_EXPERTS)))
        b_route = jnp.pad(jnp.concatenate([moe_b_grp[i], moe_b_exp[i]]), (0, LANES - MOE_GROUPS - MOE_EXPERTS))
        x = _moe_layer(x, norm_g[i, 1], sc2, sh2, g2, w_route, b_route.reshape(1, LANES),
                       moe_w_gate[i].astype(BF16), moe_w_up[i].astype(BF16), moe_w_down[i].astype(BF16),
                       final_g, final=(i == depth - 1))
    return x
```

```python
import functools

import jax
import jax.numpy as jnp
from jax import lax
from jax.experimental import pallas as pl
from jax.experimental.pallas import tpu as pltpu
from jax.experimental.pallas import tpu_sc as plsc

F32 = jnp.float32
BF16 = jnp.bfloat16

LANES = 128
RMS_EPS = 1e-6
N_MOD = 6

SB_HEADS = 16
GLA_HEADS = 4
GLA_GATE_RANK = 16
GLA_GATE_TAU = 16.0
GLA_CHUNK = 64
CONV_WIDTH = 3
DIL_PATTERNS = ((128, 1), (512, 4), (2048, 16))
DIL_HEADS = 8
DIL_HEAD_DIM = 64
DIL_QBLOCK = 128
DIL_TILE_TOKENS = 2048
MOE_GROUPS = 4
MOE_PER_GROUP = 8
MOE_EXPERTS = MOE_GROUPS * MOE_PER_GROUP
MOE_TOPK = 2

SC_CORES = 2
SC_SUBCORES = 16
SC_WINDOW = 64
PACKED_LANES = 2 * LANES

SB_LOG_ZERO = -104.0
SB_BAND = 2

VMEM_LIMIT = 48 * 1024 * 1024


def _params(*sem):
    return pltpu.CompilerParams(dimension_semantics=sem, vmem_limit_bytes=VMEM_LIMIT)


def _sigmoid(x):
    return 1.0 / (1.0 + jnp.exp(-x))


def _softplus(x):
    return jnp.maximum(x, 0.0) + jnp.log(1.0 + jnp.exp(-jnp.abs(x)))


def _split_bf16(x):
    hi = x.astype(BF16)
    lo = (x - hi.astype(F32)).astype(BF16)
    return hi, lo


def _dot(a, b):
    return jnp.dot(a, b, preferred_element_type=F32)


def _dot_nt(a, b):
    return lax.dot_general(a, b, (((1,), (1,)), ((), ())), preferred_element_type=F32)


def _dot_tn(a, b):
    return lax.dot_general(a, b, (((0,), (0,)), ((), ())), preferred_element_type=F32)


def _unpack_slabs(ref, rows):
    k = ref.shape[0] // rows
    words = jnp.concatenate([ref[pl.ds(s, rows, stride=k), :] for s in range(k)], axis=1)
    halves = [pltpu.unpack_elementwise(words, index=i, packed_dtype=BF16, unpacked_dtype=F32) for i in range(2)]
    return jnp.concatenate(halves, axis=1)


def _pack_slabs(ref, values):
    rows, d = values.shape
    half = d // 2
    words = pltpu.pack_elementwise([values[:, :half], values[:, half:]], packed_dtype=BF16)
    for s in range(half // LANES):
        ref[pl.ds(s, rows, stride=half // LANES), :] = words[:, s * LANES:(s + 1) * LANES]


def _mod_norm(x, g, sc, sh):
    r = lax.rsqrt(jnp.mean(x * x, axis=-1, keepdims=True) + RMS_EPS)
    return (x * r) * g * (1.0 + sc) + sh


def _ada_kernel(c_ref, w_ref, b_ref, o_ref):
    c = c_ref[...]
    cond = c * _sigmoid(c)
    o_ref[0] = _dot(cond.astype(BF16), w_ref[0].astype(BF16)) + b_ref[0]


def _ada_mod(c, ada_w, ada_b):
    depth, d, n = ada_w.shape
    b = c.shape[0]
    tn = 1536
    return pl.pallas_call(
        _ada_kernel,
        grid=(depth, n // tn),
        in_specs=[
            pl.BlockSpec((b, d), lambda i, j: (0, 0)),
            pl.BlockSpec((1, d, tn), lambda i, j: (i, 0, j)),
            pl.BlockSpec((1, 1, tn), lambda i, j: (i, 0, j)),
        ],
        out_specs=pl.BlockSpec((1, b, tn), lambda i, j: (i, 0, j)),
        out_shape=jax.ShapeDtypeStruct((depth, b, n), F32),
        compiler_params=_params("parallel", "parallel"),
        name="ada_mod",
    )(c, ada_w, ada_b.reshape(depth, 1, n))


def _norm_proj_kernel(x_ref, g_ref, sc_ref, sh_ref, w_ref, o_ref, h_ref, *res_ref, dilation, tn):
    h_ref[...] = _mod_norm(x_ref[0], g_ref[...], sc_ref[0], sh_ref[0]).astype(BF16)
    tm = h_ref.shape[0]
    for j in range(w_ref.shape[1] // tn):
        res = _dot(h_ref[...], w_ref[:, j * tn:(j + 1) * tn])
        if dilation == 1:
            o_ref[0, :, j * tn:(j + 1) * tn] = res.astype(o_ref.dtype)
            continue
        for c in range(tn // LANES):
            res_ref[0][j * (tn // LANES) + c] = res[:, c * LANES:(c + 1) * LANES]
        for rho in range(dilation):
            for c in range(tn // LANES):
                rows = res_ref[0][j * (tn // LANES) + c, pl.ds(rho, tm // dilation, stride=dilation), :]
                col = (j * dilation + rho) * tn + c * LANES
                o_ref[0, :, col:col + LANES] = rows.astype(o_ref.dtype)


def _norm_proj(x, g, sc, sh, w, tm=512, tn=512, dilation=1):
    b, s, d = x.shape
    n = w.shape[1]
    tm = min(tm, s)
    assert s % tm == 0 and n % tn == 0 and tm % (16 * dilation) == 0
    scratch = [pltpu.VMEM((tm, d), BF16)]
    if dilation > 1:
        scratch.append(pltpu.VMEM((n // LANES, tm, LANES), F32))
    return pl.pallas_call(
        functools.partial(_norm_proj_kernel, dilation=dilation, tn=tn),
        grid=(b, s // tm),
        in_specs=[
            pl.BlockSpec((1, tm, d), lambda bi, i: (bi, i, 0)),
            pl.BlockSpec((1, d), lambda bi, i: (0, 0)),
            pl.BlockSpec((1, 1, d), lambda bi, i: (bi, 0, 0)),
            pl.BlockSpec((1, 1, d), lambda bi, i: (bi, 0, 0)),
            pl.BlockSpec((d, n), lambda bi, i: (0, 0)),
        ],
        out_specs=pl.BlockSpec((1, tm // dilation, dilation * n), lambda bi, i: (bi, i, 0)),
        out_shape=jax.ShapeDtypeStruct((b, s // dilation, dilation * n), BF16),
        scratch_shapes=scratch,
        compiler_params=_params("parallel", "parallel"),
        name="norm_proj",
    )(x, g.reshape(1, d), sc, sh, w)


def _proj_res_kernel(o_ref, w_ref, x_ref, g_ref, y_ref):
    y_ref[0] = x_ref[0] + g_ref[0] * _dot(o_ref[0], w_ref[...])


def _proj_res(o, w, x, gate, tm=512):
    b, s, d = x.shape
    k = o.shape[-1]
    tm = min(tm, s)
    return pl.pallas_call(
        _proj_res_kernel,
        grid=(b, s // tm),
        in_specs=[
            pl.BlockSpec((1, tm, k), lambda bi, i: (bi, i, 0)),
            pl.BlockSpec((k, d), lambda bi, i: (0, 0)),
            pl.BlockSpec((1, tm, d), lambda bi, i: (bi, i, 0)),
            pl.BlockSpec((1, 1, d), lambda bi, i: (bi, 0, 0)),
        ],
        out_specs=pl.BlockSpec((1, tm, d), lambda bi, i: (bi, i, 0)),
        out_shape=jax.ShapeDtypeStruct((b, s, d), F32),
        compiler_params=_params("parallel", "parallel"),
        name="proj_res",
    )(o, w, x, gate)


def _sb_kernel(q_ref, k_ref, v_ref, o_ref, acc_ref, c_ref, *, blk, hd):
    scale = hd ** -0.5
    fold_scale = (hd & (hd - 1)) == 0 and (hd.bit_length() - 1) % 2 == 0
    first = lax.broadcasted_iota(jnp.int32, (blk, LANES), 1) < hd
    row = lax.broadcasted_iota(jnp.int32, (blk, blk), 0)
    col = lax.broadcasted_iota(jnp.int32, (blk, blk), 1)
    minus_later = jnp.where(row > col, -1.0, 0.0).astype(BF16)
    causal = col < row
    subs = q_ref.shape[1] // blk
    first_block = pl.program_id(2) * subs

    def visit_many(work):
        kv = []
        for _, block, _, _ in work:
            start = pl.multiple_of(block * blk, blk)
            kv.append((k_ref[0, pl.ds(start, blk), :], v_ref[0, pl.ds(start, blk), :]))
        zs = [[_dot_nt(q_heads[h], kb) for h in range(2)] for (q_heads, _, _, _), (kb, _) in zip(work, kv)]
        log_betas, sps = [], []
        for (_, _, mask, _), z2 in zip(work, zs):
            lb2, sp2 = [], []
            for z in z2:
                if not fold_scale:
                    z = z * scale
                sp = _softplus(z)
                lb2.append(z - sp)
                sp2.append(sp if mask is None else jnp.where(mask, sp, 0.0))
            log_betas.append(lb2)
            sps.append(sp2)
        suffixes = [[_dot(sp.astype(BF16), minus_later) for sp in sp2] for sp2 in sps]
        return kv, log_betas, sps, suffixes

    def finish(work, staged, c_in):
        kv, log_betas, sps, suffixes = staged
        cs = {chain: list(c) for chain, c in c_in.items()}
        weights = []
        for n, (_, _, mask, chain) in enumerate(work):
            a2 = []
            for h in range(2):
                a = jnp.exp(log_betas[n][h] + suffixes[n][h] + cs[chain][h])
                a2.append(a if mask is None else jnp.where(mask, a, 0.0))
                cs[chain][h] = cs[chain][h] - jnp.sum(sps[n][h], axis=1, keepdims=True)
            weights.append(a2)
        accs = {}
        for n, (_, _, _, chain) in enumerate(work):
            outs = [_dot(weights[n][h].astype(BF16), kv[n][1]) for h in range(2)]
            more = jnp.where(first, outs[0], outs[1])
            accs[chain] = more if chain not in accs else accs[chain] + more
        return accs, cs

    def remainder(sub, q_heads, depth):
        def cond(carry):
            j, cmax = carry
            return jnp.logical_and(j >= 0, cmax > SB_LOG_ZERO)

        def body(carry):
            j, _ = carry
            work = [(q_heads, j, None, sub)]
            accs, cs = finish(work, visit_many(work), {sub: [c_ref[sub, 0], c_ref[sub, 1]]})
            acc_ref[sub] += accs[sub]
            for h in range(2):
                c_ref[sub, h] = cs[sub][h]
            return j - 1, jnp.max(c_ref[sub])

        lax.while_loop(cond, body, (first_block + sub - depth - 1, jnp.max(c_ref[sub])))

    def run(depth):
        heads = []
        for sub in range(subs):
            q = q_ref[0, sub * blk:(sub + 1) * blk, :]
            if fold_scale:
                q = (q.astype(F32) * scale).astype(BF16)
            zero = jnp.zeros_like(q)
            heads.append((jnp.where(first, q, zero), jnp.where(first, zero, q)))
        work = [(heads[sub], first_block + sub - back, causal if back == 0 else None, sub)
                for sub in range(subs) for back in range(depth + 1)]
        zero_c = [jnp.zeros((blk, 1), F32)] * 2
        accs, cs = finish(work, visit_many(work), {sub: zero_c for sub in range(subs)})
        for sub in range(subs):
            acc_ref[sub] = accs[sub]
            for h in range(2):
                c_ref[sub, h] = cs[sub][h]
        for sub in range(subs):
            remainder(sub, heads[sub], depth)
            o_ref[0, sub * blk:(sub + 1) * blk, :] = acc_ref[sub].astype(o_ref.dtype)

    pl.when(first_block >= SB_BAND)(functools.partial(run, SB_BAND))
    pl.when(first_block < SB_BAND)(functools.partial(run, 0))


def _sb_attention(qkv, blk=128, rows=512):
    b, s, n = qkv.shape
    width = n // 3
    hd = width // SB_HEADS
    assert 2 * hd == LANES
    pairs = width // LANES
    rows = min(rows, s)
    return pl.pallas_call(
        functools.partial(_sb_kernel, blk=blk, hd=hd),
        grid=(b, pairs, s // rows),
        in_specs=[
            pl.BlockSpec((1, rows, LANES), lambda bi, p, i: (bi, i, p)),
            pl.BlockSpec((1, s, LANES), lambda bi, p, i: (bi, 0, pairs + p)),
            pl.BlockSpec((1, s, LANES), lambda bi, p, i: (bi, 0, 2 * pairs + p)),
        ],
        out_specs=pl.BlockSpec((1, rows, LANES), lambda bi, p, i: (bi, i, p)),
        out_shape=jax.ShapeDtypeStruct((b, s, width), BF16),
        scratch_shapes=[pltpu.VMEM((rows // blk, blk, LANES), F32), pltpu.VMEM((rows // blk, 2, blk, 1), F32)],
        compiler_params=_params("parallel", "parallel", "arbitrary"),
        name="sb_attention",
    )(qkv, qkv, qkv)


def _gla_kernel(p_ref, wg_ref, bg_ref, ng_ref, o_ref, st_ref, *, rows, dk, dv):
    @pl.when(pl.program_id(1) == 0)
    def _():
        st_ref[...] = jnp.zeros_like(st_ref)

    heads = GLA_HEADS
    ch = GLA_CHUNK
    k0 = heads * dk
    v0 = 2 * heads * dk
    r0 = v0 + heads * dv
    g0 = r0 + heads * dv
    row = lax.broadcasted_iota(jnp.int32, (ch, ch), 0)
    col = lax.broadcasted_iota(jnp.int32, (ch, ch), 1)
    causal = col <= row
    lower = jnp.where(causal, 1.0, 0.0).astype(BF16)
    chunks = [slice(s * ch, (s + 1) * ch) for s in range(rows // ch)]
    g_pres = [_dot(p_ref[0, t, g0:g0 + LANES], wg_ref[...]) + bg_ref[...] for t in chunks]
    cums = []
    for g_pre in g_pres:
        log_a = (jnp.minimum(g_pre, 0.0) - jnp.log(1.0 + jnp.exp(-jnp.abs(g_pre)))) * (1.0 / GLA_GATE_TAU)
        hi, lo = _split_bf16(log_a)
        cums.append(_dot(lower, hi) + _dot(lower, lo))
    parts = []
    for t, cum in zip(chunks, cums):
        for h in range(heads):
            bh = cum[:, h * dk:(h + 1) * dk]
            b_last = bh[ch - 1:ch, :]
            q = p_ref[0, t, h * dk:(h + 1) * dk].astype(F32) * dk ** -0.5
            k = p_ref[0, t, k0 + h * dk:k0 + (h + 1) * dk].astype(F32)
            q_in = (q * jnp.exp(bh)).astype(BF16)
            k_in = (k * jnp.exp(-bh)).astype(BF16)
            k_state = (k * jnp.exp(b_last - bh)).astype(BF16)
            parts.append((t, h, q_in, k_in, k_state, jnp.exp(b_last)))
    scores = [jnp.where(causal, _dot_nt(q_in, k_in), 0.0).astype(BF16) for _, _, q_in, k_in, _, _ in parts]
    values = [p_ref[0, t, v0 + h * dv:v0 + (h + 1) * dv] for t, h, _, _, _, _ in parts]
    intra = [_dot(sc, v) for sc, v in zip(scores, values)]
    updates = [_dot_tn(v, k_state) for v, (_, _, _, _, k_state, _) in zip(values, parts)]
    states = [st_ref[h] for h in range(heads)]
    for n, (t, h, q_in, _, _, decay) in enumerate(parts):
        o = intra[n] + _dot_nt(q_in, states[h].astype(BF16))
        states[h] = decay * states[h] + updates[n]
        o = o * lax.rsqrt(jnp.mean(o * o, axis=-1, keepdims=True) + RMS_EPS) * ng_ref[...]
        r = p_ref[0, t, r0 + h * dv:r0 + (h + 1) * dv].astype(F32)
        o_ref[0, t, h * dv:(h + 1) * dv] = (o * (r * _sigmoid(r))).astype(o_ref.dtype)
    for h in range(heads):
        st_ref[h] = states[h]


def _gla_core(proj, w_gate_up, b_gate, norm_g, dk, dv, rows=256):
    b, s, n = proj.shape
    rows = min(rows, s)
    hk = GLA_HEADS * dk
    return pl.pallas_call(
        functools.partial(_gla_kernel, rows=rows, dk=dk, dv=dv),
        grid=(b, s // rows),
        in_specs=[
            pl.BlockSpec((1, rows, n), lambda bi, i: (bi, i, 0)),
            pl.BlockSpec((LANES, hk), lambda bi, i: (0, 0)),
            pl.BlockSpec((1, hk), lambda bi, i: (0, 0)),
            pl.BlockSpec((1, dv), lambda bi, i: (0, 0)),
        ],
        out_specs=pl.BlockSpec((1, rows, GLA_HEADS * dv), lambda bi, i: (bi, i, 0)),
        out_shape=jax.ShapeDtypeStruct((b, s, GLA_HEADS * dv), BF16),
        scratch_shapes=[pltpu.VMEM((GLA_HEADS, dv, dk), F32)],
        compiler_params=_params("parallel", "arbitrary"),
        name="gla_core",
    )(proj, w_gate_up, b_gate, norm_g)


def _conv_res_kernel(gb_ref, gc_ref, u_ref, hgc_ref, hu_ref, cw_ref, cb_ref, w_ref, x_ref, g_ref, y_ref):
    u2 = gc_ref[0].astype(F32) * u_ref[0].astype(F32)
    halo = hgc_ref[0].astype(F32) * hu_ref[0].astype(F32)
    halo = jnp.where(pl.program_id(1) > 0, halo, 0.0)
    rows = lax.broadcasted_iota(jnp.int32, u2.shape, 0)
    prev1 = jnp.where(rows == 0, halo[7:8], pltpu.roll(u2, 1, 0))
    prev2 = jnp.where(rows == 0, halo[6:7], jnp.where(rows == 1, halo[7:8], pltpu.roll(u2, 2, 0)))
    cw = cw_ref[...]
    y = cw[0:1] * prev2 + cw[1:2] * prev1 + cw[2:3] * u2 + cb_ref[...]
    o = (gb_ref[0].astype(F32) * y).astype(BF16)
    y_ref[0] = x_ref[0] + g_ref[0] * _dot(o, w_ref[...])


def _conv_res(proj, conv_w, conv_b, w_out, x, gate, tm=512):
    b, s, d = x.shape
    tm = min(tm, s)
    halo_blocks = tm // 8

    def halo_map(col):
        return lambda bi, i: (bi, jnp.maximum(i * halo_blocks - 1, 0), col)

    return pl.pallas_call(
        _conv_res_kernel,
        grid=(b, s // tm),
        in_specs=[
            pl.BlockSpec((1, tm, d), lambda bi, i: (bi, i, 0)),
            pl.BlockSpec((1, tm, d), lambda bi, i: (bi, i, 1)),
            pl.BlockSpec((1, tm, d), lambda bi, i: (bi, i, 2)),
            pl.BlockSpec((1, 8, d), halo_map(1)),
            pl.BlockSpec((1, 8, d), halo_map(2)),
            pl.BlockSpec((CONV_WIDTH, d), lambda bi, i: (0, 0)),
            pl.BlockSpec((1, d), lambda bi, i: (0, 0)),
            pl.BlockSpec((d, d), lambda bi, i: (0, 0)),
            pl.BlockSpec((1, tm, d), lambda bi, i: (bi, i, 0)),
            pl.BlockSpec((1, 1, d), lambda bi, i: (bi, 0, 0)),
        ],
        out_specs=pl.BlockSpec((1, tm, d), lambda bi, i: (bi, i, 0)),
        out_shape=jax.ShapeDtypeStruct((b, s, d), F32),
        compiler_params=_params("parallel", "parallel"),
        name="conv_res",
    )(proj, proj, proj, proj, proj, conv_w, conv_b.reshape(1, d), w_out, x, gate)


def _dil_kernel(q_ref, kc_ref, vc_ref, kp_ref, vp_ref, o_ref, l_ref, *, rows, hd, dilation):
    qb = DIL_QBLOCK
    n = pl.program_id(1)
    rho = pl.program_id(2)
    scale = hd ** -0.5
    fold_scale = (hd & (hd - 1)) == 0 and (hd.bit_length() - 1) % 2 == 0
    first = lax.broadcasted_iota(jnp.int32, (qb, LANES), 1) < hd
    row = lax.broadcasted_iota(jnp.int32, (qb, qb), 0)
    col = lax.broadcasted_iota(jnp.int32, (qb, qb), 1)
    in_prev = col >= row
    in_cur = col <= row
    neg = -jnp.inf

    tiles = []
    for s in range(rows // qb):
        t = slice(s * qb, (s + 1) * qb)
        tp = slice((s - 1) * qb, s * qb)
        prev_mask = jnp.logical_and(in_prev, n > 0) if s == 0 else in_prev
        for p in range(q_ref.shape[-1] // LANES):
            c = slice(p * LANES, (p + 1) * LANES)
            q = q_ref[0, t, c]
            if fold_scale:
                q = (q.astype(F32) * scale).astype(BF16)
            zero = jnp.zeros_like(q)
            q_heads = (jnp.where(first, q, zero), jnp.where(first, zero, q))
            if s == 0:
                k_prev, v_prev = kp_ref[0, :, c], vp_ref[0, :, c]
            else:
                k_prev, v_prev = kc_ref[0, tp, c], vc_ref[0, tp, c]
            tiles.append((s, p, q_heads, k_prev, kc_ref[0, t, c], v_prev, vc_ref[0, t, c], prev_mask))

    scores = []
    for _, _, q_heads, k_prev, k_cur, _, _, prev_mask in tiles:
        pair = []
        for h in range(2):
            z_prev, z_cur = _dot_nt(q_heads[h], k_prev), _dot_nt(q_heads[h], k_cur)
            if not fold_scale:
                z_prev, z_cur = z_prev * scale, z_cur * scale
            pair.append((jnp.where(prev_mask, z_prev, neg), jnp.where(in_cur, z_cur, neg)))
        scores.append(pair)

    probs = []
    for pair in scores:
        stats = []
        for z_prev, z_cur in pair:
            m = jnp.max(jnp.maximum(z_prev, z_cur), axis=1, keepdims=True)
            p_prev = jnp.exp(z_prev - m)
            p_cur = jnp.exp(z_cur - m)
            den = jnp.sum(p_prev + p_cur, axis=1, keepdims=True)
            stats.append((p_prev.astype(BF16), p_cur.astype(BF16), den, m))
        probs.append(stats)

    for (s, p, _, _, _, v_prev, v_cur, _), stats in zip(tiles, probs):
        outs, lses = [], []
        for p_prev, p_cur, den, m in stats:
            outs.append((_dot(p_prev, v_prev) + _dot(p_cur, v_cur)) / den)
            lses.append(jnp.broadcast_to(m + jnp.log(den), (qb, LANES)))
        if dilation == 1:
            t_out = slice(s * qb, (s + 1) * qb)
        else:
            t_out = pl.ds(s * qb * dilation + rho, qb, stride=dilation)
        o_ref[0, p, t_out, :] = jnp.where(first, outs[0], outs[1])
        l_ref[0, p, t_out, :] = jnp.where(first, lses[0], lses[1])


def _dil_group(view, dilation, window, seq):
    b, length, _ = view.shape
    width = DIL_HEADS * DIL_HEAD_DIM
    assert window == dilation * DIL_QBLOCK and seq % (dilation * DIL_QBLOCK) == 0
    rows = min(max(DIL_TILE_TOKENS // dilation, DIL_QBLOCK), 512, length)
    prev_blocks = rows // DIL_QBLOCK

    def cur(part):
        return lambda bi, i, rho: (bi, i, part * dilation + rho)

    def prev(part):
        return lambda bi, i, rho: (bi, jnp.maximum(i * prev_blocks - 1, 0), part * dilation + rho)

    pairs = width // LANES
    out_shape = jax.ShapeDtypeStruct((b, pairs, seq, LANES), F32)
    out_spec = pl.BlockSpec((1, pairs, rows * dilation, LANES), lambda bi, i, rho: (bi, 0, i, 0))
    return pl.pallas_call(
        functools.partial(_dil_kernel, rows=rows, hd=DIL_HEAD_DIM, dilation=dilation),
        grid=(b, length // rows, dilation),
        in_specs=[
            pl.BlockSpec((1, rows, width), cur(0)),
            pl.BlockSpec((1, rows, width), cur(1)),
            pl.BlockSpec((1, rows, width), cur(2)),
            pl.BlockSpec((1, DIL_QBLOCK, width), prev(1)),
            pl.BlockSpec((1, DIL_QBLOCK, width), prev(2)),
        ],
        out_specs=[out_spec, out_spec],
        out_shape=[out_shape, out_shape],
        compiler_params=_params("parallel", "parallel", "arbitrary"),
        name=f"dil_attn_r{dilation}",
    )(view, view, view, view, view)


def _dil_res_kernel(o0, o1, o2, l0, l1, l2, w_ref, x_ref, g_ref, y_ref):
    parts = []
    for p in range(o0.shape[1]):
        ls = (l0[0, p], l1[0, p], l2[0, p])
        m = jnp.maximum(jnp.maximum(ls[0], ls[1]), ls[2])
        es = [jnp.exp(l - m) for l in ls]
        parts.append((es[0] * o0[0, p] + es[1] * o1[0, p] + es[2] * o2[0, p]) / (es[0] + es[1] + es[2]))
    o = jnp.concatenate(parts, axis=1)
    y_ref[0] = x_ref[0] + g_ref[0] * _dot(o.astype(BF16), w_ref[...])


def _dil_res(outs, lses, w_out, x, gate, tm=512):
    b, s, d = x.shape
    k = w_out.shape[0]
    tm = min(tm, s)
    part = pl.BlockSpec((1, k // LANES, tm, LANES), lambda bi, i: (bi, 0, i, 0))
    return pl.pallas_call(
        _dil_res_kernel,
        grid=(b, s // tm),
        in_specs=[part] * 6 + [
            pl.BlockSpec((k, d), lambda bi, i: (0, 0)),
            pl.BlockSpec((1, tm, d), lambda bi, i: (bi, i, 0)),
            pl.BlockSpec((1, 1, d), lambda bi, i: (bi, 0, 0)),
        ],
        out_specs=pl.BlockSpec((1, tm, d), lambda bi, i: (bi, i, 0)),
        out_shape=jax.ShapeDtypeStruct((b, s, d), F32),
        compiler_params=_params("parallel", "parallel"),
        name="dil_res",
    )(*outs, *lses, w_out, x, gate)


def _router_kernel(x_ref, g_ref, sc_ref, sh_ref, wr_ref, br_ref, h_ref, route_ref, cnt_ref, run_ref, earlier_ref):
    tm = x_ref.shape[1]

    @pl.when(jnp.logical_and(pl.program_id(0) == 0, pl.program_id(1) == 0))
    def _():
        run_ref[...] = jnp.zeros_like(run_ref)
        row = lax.broadcasted_iota(jnp.int32, (tm, tm), 0)
        col = lax.broadcasted_iota(jnp.int32, (tm, tm), 1)
        earlier_ref[...] = jnp.where(col < row, 1.0, 0.0).astype(BF16)

    h = _mod_norm(x_ref[0], g_ref[...], sc_ref[0], sh_ref[0])
    _pack_slabs(h_ref, h)
    logits = _dot(h.astype(BF16), wr_ref[...].astype(BF16)) + br_ref[...]
    lane = lax.broadcasted_iota(jnp.int32, logits.shape, 1).astype(F32)
    neg = -jnp.inf
    far = float(LANES)

    def first_argmax(vals):
        top = jnp.max(vals, axis=1, keepdims=True)
        return top, jnp.min(jnp.where(vals == top, lane, far), axis=1, keepdims=True)

    grp_logits = jnp.where(lane < MOE_GROUPS, logits, neg)
    grp_max, grp = first_argmax(grp_logits)
    p_grp = 1.0 / jnp.sum(jnp.exp(grp_logits - grp_max), axis=1, keepdims=True)
    base = MOE_GROUPS + grp * MOE_PER_GROUP
    in_grp = jnp.logical_and(lane >= base, lane < base + MOE_PER_GROUP)
    exp_logits = jnp.where(in_grp, logits, neg)
    m1, i1 = first_argmax(exp_logits)
    m2, i2 = first_argmax(jnp.where(lane == i1, neg, exp_logits))
    e2 = jnp.exp(m2 - m1)
    g1 = p_grp / (1.0 + e2)
    g2 = g1 * e2
    e_a = i1 - MOE_GROUPS
    e_b = i2 - MOE_GROUPS

    pick_a = lane == e_a
    pick_b = lane == e_b
    both = jnp.where(jnp.logical_or(pick_a, pick_b), 1.0, 0.0)
    before = _dot(earlier_ref[...], both.astype(BF16)) + run_ref[...]
    rank_a = jnp.sum(jnp.where(pick_a, before, 0.0), axis=1, keepdims=True)
    rank_b = jnp.sum(jnp.where(pick_b, before, 0.0), axis=1, keepdims=True)
    run_ref[...] += jnp.sum(both, axis=0, keepdims=True)
    cnt_ref[...] = run_ref[...]

    route = jnp.where(lane == 0, e_a, jnp.where(lane == 1, e_b, 0.0))
    route = jnp.where(lane == 2, g1, jnp.where(lane == 3, g2, route))
    route_ref[...] = jnp.where(lane == 4, rank_a, jnp.where(lane == 5, rank_b, route))


def _router(x, g, sc, sh, w_route, b_route, tm=512):
    b, s, d = x.shape
    tm = min(tm, s)
    steps = s // tm
    return pl.pallas_call(
        _router_kernel,
        grid=(b, steps),
        in_specs=[
            pl.BlockSpec((1, tm, d), lambda bi, i: (bi, i, 0)),
            pl.BlockSpec((1, d), lambda bi, i: (0, 0)),
            pl.BlockSpec((1, 1, d), lambda bi, i: (bi, 0, 0)),
            pl.BlockSpec((1, 1, d), lambda bi, i: (bi, 0, 0)),
            pl.BlockSpec((d, LANES), lambda bi, i: (0, 0)),
            pl.BlockSpec((1, LANES), lambda bi, i: (0, 0)),
        ],
        out_specs=[
            pl.BlockSpec((tm * (d // PACKED_LANES), LANES), lambda bi, i: (bi * steps + i, 0)),
            pl.BlockSpec((tm, LANES), lambda bi, i: (bi * steps + i, 0)),
            pl.BlockSpec((1, LANES), lambda bi, i: (0, 0)),
        ],
        out_shape=[
            jax.ShapeDtypeStruct((b * s * (d // PACKED_LANES), LANES), jnp.int32),
            jax.ShapeDtypeStruct((b * s, LANES), F32),
            jax.ShapeDtypeStruct((1, LANES), F32),
        ],
        scratch_shapes=[pltpu.VMEM((1, LANES), F32), pltpu.VMEM((tm, tm), BF16)],
        compiler_params=_params("arbitrary", "arbitrary"),
        name="moe_router",
    )(x, g.reshape(1, d), sc, sh, w_route, b_route)


def _sc_dispatch(h_slabs, dest, n_slots):
    t, chunks, _ = h_slabs.shape
    workers = SC_CORES * SC_SUBCORES
    per_worker = t // workers
    steps = per_worker // SC_WINDOW
    assert t % (workers * SC_WINDOW * 2) == 0
    mesh = plsc.VectorSubcoreMesh(core_axis_name="core", subcore_axis_name="subcore")

    @functools.partial(
        pl.kernel, mesh=mesh,
        out_type=jax.ShapeDtypeStruct((n_slots, chunks, LANES), h_slabs.dtype),
        scratch_types=[
            pltpu.VMEM((2, MOE_TOPK, SC_WINDOW), jnp.int32),
            pltpu.VMEM((2, SC_WINDOW, chunks, LANES), h_slabs.dtype),
            pltpu.SemaphoreType.DMA((2,)),
            pltpu.SemaphoreType.DMA((2,)),
        ])
    def dispatch(h_hbm, dest_hbm, slots_hbm, idx_v, rows_v, load_sem, store_sem):
        worker = lax.axis_index("subcore") * SC_CORES + lax.axis_index("core")
        base = worker * per_worker

        @pl.loop(0, steps, step=2)
        def _(step):
            for buf in range(2):
                off = base + (step + buf) * SC_WINDOW
                pltpu.async_copy(h_hbm.at[pl.ds(off, SC_WINDOW)], rows_v.at[buf], load_sem.at[buf])
                for k in range(MOE_TOPK):
                    pltpu.sync_copy(dest_hbm.at[k, pl.ds(off, SC_WINDOW)], idx_v.at[buf, k])
            for buf in range(2):
                off = base + (step + buf) * SC_WINDOW
                pltpu.make_async_copy(h_hbm.at[pl.ds(off, SC_WINDOW)], rows_v.at[buf], load_sem.at[buf]).wait()
                copies = [pltpu.async_copy(rows_v.at[buf], slots_hbm.at[idx_v.at[buf, k]], store_sem.at[buf])
                          for k in range(MOE_TOPK)]
                for cp in copies:
                    cp.wait()

    return dispatch(h_slabs, dest)


def _zero_tail_kernel(tail_ref, valid_ref, s_in, s_out):
    del tail_ref
    rows = lax.broadcasted_iota(jnp.int32, s_in.shape, 0)
    s_out[...] = jnp.where(rows < valid_ref[pl.program_id(0)], s_in[...], 0)


def _zero_tails(tail_blk, valid_rows, slots, block_rows):
    grid_spec = pltpu.PrefetchScalarGridSpec(
        num_scalar_prefetch=2,
        grid=(tail_blk.shape[0],),
        in_specs=[pl.BlockSpec((block_rows, LANES), lambda e, tail, valid: (tail[e], 0))],
        out_specs=pl.BlockSpec((block_rows, LANES), lambda e, tail, valid: (tail[e], 0)),
    )
    return pl.pallas_call(
        _zero_tail_kernel,
        grid_spec=grid_spec,
        out_shape=jax.ShapeDtypeStruct(slots.shape, slots.dtype),
        input_output_aliases={2: 0},
        compiler_params=_params("arbitrary"),
        name="moe_zero_tails",
    )(tail_blk, valid_rows, slots)


def _expert_kernel(blk_e_ref, used_ref, x_ref, wg_ref, wu_ref, wd_ref, y_ref, wg_bf, wu_bf, wd_bf):
    i = pl.program_id(0)
    live = i < used_ref[0]
    changed = jnp.logical_or(i == 0, blk_e_ref[i] != blk_e_ref[jnp.maximum(i - 1, 0)])

    @pl.when(jnp.logical_and(live, changed))
    def _():
        wg_bf[...] = wg_ref[0, 0].astype(BF16)
        wu_bf[...] = wu_ref[0, 0].astype(BF16)
        wd_bf[...] = wd_ref[0, 0].astype(BF16)

    @pl.when(live)
    def _():
        bm = x_ref.shape[0] // (wg_bf.shape[0] // PACKED_LANES)
        x = _unpack_slabs(x_ref, bm).astype(BF16)
        gate = _dot(x, wg_bf[...])
        up = _dot(x, wu_bf[...])
        act = (gate * _sigmoid(gate) * up).astype(BF16)
        _pack_slabs(y_ref, _dot(act, wd_bf[...]))

    @pl.when(jnp.logical_not(live))
    def _():
        y_ref[...] = jnp.zeros_like(y_ref)


def _expert_blocks(blk_e, used, slots, w_gate, w_up, w_down, layer, bm):
    nb = blk_e.shape[0]
    _, _, d, f = w_gate.shape
    chunks = d // PACKED_LANES

    def block(i, e, n):
        return (jnp.minimum(i, n[0] - 1), 0)

    def weight(i, e, n):
        return (layer, e[jnp.minimum(i, n[0] - 1)], 0, 0)

    grid_spec = pltpu.PrefetchScalarGridSpec(
        num_scalar_prefetch=2,
        grid=(nb,),
        in_specs=[
            pl.BlockSpec((bm * chunks, LANES), block),
            pl.BlockSpec((1, 1, d, f), weight),
            pl.BlockSpec((1, 1, d, f), weight),
            pl.BlockSpec((1, 1, f, d), weight),
        ],
        out_specs=pl.BlockSpec((bm * chunks, LANES), lambda i, e, n: (i, 0)),
        scratch_shapes=[pltpu.VMEM((d, f), BF16), pltpu.VMEM((d, f), BF16), pltpu.VMEM((f, d), BF16)],
    )
    return pl.pallas_call(
        _expert_kernel,
        grid_spec=grid_spec,
        out_shape=jax.ShapeDtypeStruct((nb * bm * chunks, LANES), jnp.int32),
        compiler_params=_params("arbitrary"),
        name="moe_experts",
    )(blk_e, used, slots, w_gate, w_up, w_down)


def _sc_gather(y_slabs, idx):
    n = idx.shape[0]
    _, chunks, _ = y_slabs.shape
    workers = SC_CORES * SC_SUBCORES
    per_worker = n // workers
    steps = per_worker // SC_WINDOW
    assert n % (workers * SC_WINDOW * 2) == 0
    mesh = plsc.VectorSubcoreMesh(core_axis_name="core", subcore_axis_name="subcore")

    @functools.partial(
        pl.kernel, mesh=mesh,
        out_type=jax.ShapeDtypeStruct((n, chunks, LANES), y_slabs.dtype),
        scratch_types=[
            pltpu.VMEM((2, SC_WINDOW), jnp.int32),
            pltpu.VMEM((2, SC_WINDOW, chunks, LANES), y_slabs.dtype),
            pltpu.SemaphoreType.DMA((2,)),
        ])
    def gather(y_hbm, idx_hbm, out_hbm, idx_v, rows_v, sem):
        worker = lax.axis_index("subcore") * SC_CORES + lax.axis_index("core")
        base = worker * per_worker

        @pl.loop(0, steps, step=2)
        def _(step):
            for buf in range(2):
                off = base + (step + buf) * SC_WINDOW
                pltpu.sync_copy(idx_hbm.at[pl.ds(off, SC_WINDOW)], idx_v.at[buf])
                pltpu.async_copy(y_hbm.at[idx_v.at[buf]], rows_v.at[buf], sem.at[buf])
            for buf in range(2):
                off = base + (step + buf) * SC_WINDOW
                pltpu.make_async_copy(y_hbm.at[idx_v.at[buf]], rows_v.at[buf], sem.at[buf]).wait()
                pltpu.sync_copy(rows_v.at[buf], out_hbm.at[pl.ds(off, SC_WINDOW)])

    return gather(y_slabs, idx)


def _combine_kernel(g0_ref, g1_ref, route_ref, x_ref, gate_ref, fg_ref, o_ref, *, final):
    tm = x_ref.shape[0]
    route = route_ref[...]
    y = route[:, 2:3] * _unpack_slabs(g0_ref, tm) + route[:, 3:4] * _unpack_slabs(g1_ref, tm)
    out = x_ref[...] + gate_ref[0] * y
    if final:
        out = out * lax.rsqrt(jnp.mean(out * out, axis=-1, keepdims=True) + RMS_EPS) * fg_ref[...]
    o_ref[...] = out


def _combine(dest, route, y_slabs, x, gate, final_g, final, tm, pieces=4):
    b, s, d = x.shape
    t = b * s
    chunks = d // PACKED_LANES
    per_seq = s // tm
    piece = t // pieces
    tiles = piece // tm
    out = x.reshape(t, d)
    fg = final_g.reshape(1, d)
    for c in range(pieces):
        idx = dest[:, c * piece:(c + 1) * piece].reshape(MOE_TOPK * piece)
        g = _sc_gather(y_slabs, idx).reshape(MOE_TOPK * piece * chunks, LANES)
        first = c * tiles
        out = pl.pallas_call(
            functools.partial(_combine_kernel, final=final),
            grid=(tiles,),
            in_specs=[
                pl.BlockSpec((tm * chunks, LANES), lambda i: (i, 0)),
                pl.BlockSpec((tm * chunks, LANES), lambda i: (tiles + i, 0)),
                pl.BlockSpec((tm, LANES), lambda i, first=first: (first + i, 0)),
                pl.BlockSpec((tm, d), lambda i, first=first: (first + i, 0)),
                pl.BlockSpec((1, 1, d), lambda i, first=first: ((first + i) // per_seq, 0, 0)),
                pl.BlockSpec((1, d), lambda i: (0, 0)),
            ],
            out_specs=pl.BlockSpec((tm, d), lambda i, first=first: (first + i, 0)),
            out_shape=jax.ShapeDtypeStruct((t, d), F32),
            input_output_aliases={3: 0},
            compiler_params=_params("parallel"),
            name="moe_combine",
        )(g, g, route, out, gate, fg)
    return out.reshape(b, s, d)


def _slot_tables(route, counts, bm):
    t = route.shape[0]
    nb = MOE_TOPK * t // bm + MOE_EXPERTS
    counts = counts[0, :MOE_EXPERTS].astype(jnp.int32)
    padded = (counts + bm - 1) // bm * bm
    pad_end = jnp.cumsum(padded)
    pad_start = pad_end - padded
    expert = route[:, 0:MOE_TOPK].astype(jnp.int32)
    rank = route[:, 4:4 + MOE_TOPK].astype(jnp.int32)
    dest = pad_start[expert] + rank
    blk_start = jnp.arange(nb, dtype=jnp.int32) * bm
    blk_e = jnp.minimum(jnp.sum((pad_end[None, :] <= blk_start[:, None]).astype(jnp.int32), axis=1), MOE_EXPERTS - 1)
    used = (pad_end[-1:] // bm).astype(jnp.int32)
    tail_blk = jnp.maximum(pad_end // bm - 1, 0).astype(jnp.int32)
    tail_fill = jnp.where(counts % bm != 0, counts % bm, bm).astype(jnp.int32)
    return dest.T, blk_e, used, tail_blk, tail_fill, nb


def _moe_layer(x, g, sc, sh, gate, w_route, b_route, w_gate, w_up, w_down, layer, final_g, final, bm=512, tm=256):
    b, s, d = x.shape
    t = b * s
    chunks = d // PACKED_LANES
    tm = min(tm, s)
    h_rows, route, counts = _router(x, g, sc, sh, w_route, b_route)
    dest, blk_e, used, tail_blk, tail_fill, nb = _slot_tables(route, counts, bm)
    slots = _sc_dispatch(h_rows.reshape(t, chunks, LANES), dest, nb * bm)
    slots = _zero_tails(tail_blk, tail_fill * chunks, slots.reshape(nb * bm * chunks, LANES), bm * chunks)
    y_slabs = _expert_blocks(blk_e, used, slots, w_gate, w_up, w_down, layer, bm).reshape(nb * bm, chunks, LANES)
    return _combine(dest, route, y_slabs, x, gate, final_g, final, tm)


def kernel(x, c, ada_w, ada_b, norm_g, final_g, sb_w_in, sb_w_out, gla_w_in, gla_w_gate_up, gla_b_gate, gla_norm_g, gla_w_out, conv_w_in, conv_w, conv_b, conv_w_out, dil_w_in, dil_w_out, moe_w_grp, moe_b_grp, moe_w_exp, moe_b_exp, moe_w_gate, moe_w_up, moe_w_down):
    b, s, d = x.shape
    depth = ada_w.shape[0]
    mod = _ada_mod(c, ada_w, ada_b).reshape(depth, b, N_MOD, 1, d)
    for i in range(depth):
        sh1, sc1, g1, sh2, sc2, g2 = (mod[i, :, m] for m in range(N_MOD))
        kind, j = i % 4, i // 4
        if kind == 0:
            qkv = _norm_proj(x, norm_g[i, 0], sc1, sh1, sb_w_in[j].astype(BF16))
            x = _proj_res(_sb_attention(qkv), sb_w_out[j].astype(BF16), x, g1)
        elif kind == 1:
            dk = gla_w_gate_up.shape[-1] // GLA_HEADS
            dv = gla_norm_g.shape[-1]
            w_in = jnp.pad(gla_w_in[j], ((0, 0), (0, LANES - GLA_GATE_RANK))).astype(BF16)
            w_up = jnp.pad(gla_w_gate_up[j], ((0, LANES - GLA_GATE_RANK), (0, 0))).astype(BF16)
            proj = _norm_proj(x, norm_g[i, 0], sc1, sh1, w_in, tn=640)
            o = _gla_core(proj, w_up, gla_b_gate[j].reshape(1, -1), gla_norm_g[j].reshape(1, -1), dk, dv)
            x = _proj_res(o, gla_w_out[j].astype(BF16), x, g1)
        elif kind == 2:
            proj = _norm_proj(x, norm_g[i, 0], sc1, sh1, conv_w_in[j].astype(BF16))
            x = _conv_res(proj, conv_w[j], conv_b[j], conv_w_out[j].astype(BF16), x, g1)
        else:
            group_w = 3 * DIL_HEADS * DIL_HEAD_DIM
            outs, lses = [], []
            for g, (window, dilation) in enumerate(DIL_PATTERNS):
                w_g = dil_w_in[j][:, g * group_w:(g + 1) * group_w].astype(BF16)
                view = _norm_proj(x, norm_g[i, 0], sc1, sh1, w_g, dilation=dilation)
                o_g, l_g = _dil_group(view, dilation, window, s)
                outs.append(o_g)
                lses.append(l_g)
            x = _dil_res(outs, lses, dil_w_out[j].astype(BF16), x, g1)
        w_route = jnp.pad(jnp.concatenate([moe_w_grp[i], moe_w_exp[i]], axis=1),
                          ((0, 0), (0, LANES - MOE_GROUPS - MOE_EXPERTS)))
        b_route = jnp.pad(jnp.concatenate([moe_b_grp[i], moe_b_exp[i]]), (0, LANES - MOE_GROUPS - MOE_EXPERTS))
        x = _moe_layer(x, norm_g[i, 1], sc2, sh2, g2, w_route, b_route.reshape(1, LANES),
                       moe_w_gate, moe_w_up, moe_w_down, i, final_g, final=(i == depth - 1))
    return x
```

```python
import functools

import jax
import jax.numpy as jnp
from jax import lax
from jax.experimental import pallas as pl
from jax.experimental.pallas import tpu as pltpu
from jax.experimental.pallas import tpu_sc as plsc

F32 = jnp.float32
BF16 = jnp.bfloat16

LANES = 128
RMS_EPS = 1e-6
N_MOD = 6

SB_HEADS = 16
GLA_HEADS = 4
GLA_GATE_RANK = 16
GLA_GATE_TAU = 16.0
GLA_CHUNK = 64
CONV_WIDTH = 3
DIL_PATTERNS = ((128, 1), (512, 4), (2048, 16))
DIL_HEADS = 8
DIL_HEAD_DIM = 64
DIL_QBLOCK = 128
DIL_TILE_TOKENS = 2048
MOE_GROUPS = 4
MOE_PER_GROUP = 8
MOE_EXPERTS = MOE_GROUPS * MOE_PER_GROUP
MOE_TOPK = 2

SC_CORES = 2
SC_SUBCORES = 16
SC_WINDOW = 64
PACKED_LANES = 2 * LANES

SB_LOG_ZERO = -104.0
SB_BAND = 2

VMEM_LIMIT = 48 * 1024 * 1024


def _params(*sem):
    return pltpu.CompilerParams(dimension_semantics=sem, vmem_limit_bytes=VMEM_LIMIT)


def _sigmoid(x):
    return 1.0 / (1.0 + jnp.exp(-x))


def _softplus(x):
    return jnp.maximum(x, 0.0) + jnp.log(1.0 + jnp.exp(-jnp.abs(x)))


def _split_bf16(x):
    hi = x.astype(BF16)
    lo = (x - hi.astype(F32)).astype(BF16)
    return hi, lo


def _dot(a, b):
    return jnp.dot(a, b, preferred_element_type=F32)


def _dot_nt(a, b):
    return lax.dot_general(a, b, (((1,), (1,)), ((), ())), preferred_element_type=F32)


def _dot_tn(a, b):
    return lax.dot_general(a, b, (((0,), (0,)), ((), ())), preferred_element_type=F32)


def _unpack_slabs(ref, rows):
    k = ref.shape[0] // rows
    words = jnp.concatenate([ref[pl.ds(s, rows, stride=k), :] for s in range(k)], axis=1)
    halves = [pltpu.unpack_elementwise(words, index=i, packed_dtype=BF16, unpacked_dtype=F32) for i in range(2)]
    return jnp.concatenate(halves, axis=1)


def _pack_slabs(ref, values):
    rows, d = values.shape
    half = d // 2
    words = pltpu.pack_elementwise([values[:, :half], values[:, half:]], packed_dtype=BF16)
    for s in range(half // LANES):
        ref[pl.ds(s, rows, stride=half // LANES), :] = words[:, s * LANES:(s + 1) * LANES]


def _mod_norm(x, g, sc, sh):
    r = lax.rsqrt(jnp.mean(x * x, axis=-1, keepdims=True) + RMS_EPS)
    return (x * r) * g * (1.0 + sc) + sh


def _ada_kernel(c_ref, w_ref, b_ref, o_ref):
    c = c_ref[...]
    cond = c * _sigmoid(c)
    o_ref[0] = _dot(cond.astype(BF16), w_ref[0].astype(BF16)) + b_ref[0]


def _ada_mod(c, ada_w, ada_b):
    depth, d, n = ada_w.shape
    b = c.shape[0]
    tn = 1536
    return pl.pallas_call(
        _ada_kernel,
        grid=(depth, n // tn),
        in_specs=[
            pl.BlockSpec((b, d), lambda i, j: (0, 0)),
            pl.BlockSpec((1, d, tn), lambda i, j: (i, 0, j)),
            pl.BlockSpec((1, 1, tn), lambda i, j: (i, 0, j)),
        ],
        out_specs=pl.BlockSpec((1, b, tn), lambda i, j: (i, 0, j)),
        out_shape=jax.ShapeDtypeStruct((depth, b, n), F32),
        compiler_params=_params("parallel", "parallel"),
        name="ada_mod",
    )(c, ada_w, ada_b.reshape(depth, 1, n))


def _norm_proj_kernel(x_ref, g_ref, sc_ref, sh_ref, w_ref, o_ref, h_ref, *res_ref, dilation, tn):
    h_ref[...] = _mod_norm(x_ref[0], g_ref[...], sc_ref[0], sh_ref[0]).astype(BF16)
    tm = h_ref.shape[0]
    for j in range(w_ref.shape[1] // tn):
        res = _dot(h_ref[...], w_ref[:, j * tn:(j + 1) * tn])
        if dilation == 1:
            o_ref[0, :, j * tn:(j + 1) * tn] = res.astype(o_ref.dtype)
            continue
        for c in range(tn // LANES):
            res_ref[0][j * (tn // LANES) + c] = res[:, c * LANES:(c + 1) * LANES]
        for rho in range(dilation):
            for c in range(tn // LANES):
                rows = res_ref[0][j * (tn // LANES) + c, pl.ds(rho, tm // dilation, stride=dilation), :]
                col = (j * dilation + rho) * tn + c * LANES
                o_ref[0, :, col:col + LANES] = rows.astype(o_ref.dtype)


def _norm_proj(x, g, sc, sh, w, tm=512, tn=512, dilation=1):
    b, s, d = x.shape
    n = w.shape[1]
    tm = min(tm, s)
    assert s % tm == 0 and n % tn == 0 and tm % (16 * dilation) == 0
    scratch = [pltpu.VMEM((tm, d), BF16)]
    if dilation > 1:
        scratch.append(pltpu.VMEM((n // LANES, tm, LANES), F32))
    return pl.pallas_call(
        functools.partial(_norm_proj_kernel, dilation=dilation, tn=tn),
        grid=(b, s // tm),
        in_specs=[
            pl.BlockSpec((1, tm, d), lambda bi, i: (bi, i, 0)),
            pl.BlockSpec((1, d), lambda bi, i: (0, 0)),
            pl.BlockSpec((1, 1, d), lambda bi, i: (bi, 0, 0)),
            pl.BlockSpec((1, 1, d), lambda bi, i: (bi, 0, 0)),
            pl.BlockSpec((d, n), lambda bi, i: (0, 0)),
        ],
        out_specs=pl.BlockSpec((1, tm // dilation, dilation * n), lambda bi, i: (bi, i, 0)),
        out_shape=jax.ShapeDtypeStruct((b, s // dilation, dilation * n), BF16),
        scratch_shapes=scratch,
        compiler_params=_params("parallel", "parallel"),
        name="norm_proj",
    )(x, g.reshape(1, d), sc, sh, w)


def _proj_res_kernel(o_ref, w_ref, x_ref, g_ref, y_ref):
    y_ref[0] = x_ref[0] + g_ref[0] * _dot(o_ref[0], w_ref[...])


def _proj_res(o, w, x, gate, tm=512):
    b, s, d = x.shape
    k = o.shape[-1]
    tm = min(tm, s)
    return pl.pallas_call(
        _proj_res_kernel,
        grid=(b, s // tm),
        in_specs=[
            pl.BlockSpec((1, tm, k), lambda bi, i: (bi, i, 0)),
            pl.BlockSpec((k, d), lambda bi, i: (0, 0)),
            pl.BlockSpec((1, tm, d), lambda bi, i: (bi, i, 0)),
            pl.BlockSpec((1, 1, d), lambda bi, i: (bi, 0, 0)),
        ],
        out_specs=pl.BlockSpec((1, tm, d), lambda bi, i: (bi, i, 0)),
        out_shape=jax.ShapeDtypeStruct((b, s, d), F32),
        compiler_params=_params("parallel", "parallel"),
        name="proj_res",
    )(o, w, x, gate)


def _sb_kernel(q_ref, k_ref, v_ref, o_ref, acc_ref, c_ref, *, blk, hd):
    scale = hd ** -0.5
    fold_scale = (hd & (hd - 1)) == 0 and (hd.bit_length() - 1) % 2 == 0
    first = lax.broadcasted_iota(jnp.int32, (blk, LANES), 1) < hd
    row = lax.broadcasted_iota(jnp.int32, (blk, blk), 0)
    col = lax.broadcasted_iota(jnp.int32, (blk, blk), 1)
    minus_later = jnp.where(row > col, -1.0, 0.0).astype(BF16)
    causal = col < row
    subs = q_ref.shape[1] // blk
    first_block = pl.program_id(2) * subs

    def visit_many(work):
        kv = []
        for _, block, _, _ in work:
            start = pl.multiple_of(block * blk, blk)
            kv.append((k_ref[0, pl.ds(start, blk), :], v_ref[0, pl.ds(start, blk), :]))
        zs = [[_dot_nt(q_heads[h], kb) for h in range(2)] for (q_heads, _, _, _), (kb, _) in zip(work, kv)]
        log_betas, sps = [], []
        for (_, _, mask, _), z2 in zip(work, zs):
            lb2, sp2 = [], []
            for z in z2:
                if not fold_scale:
                    z = z * scale
                sp = _softplus(z)
                lb2.append(z - sp)
                sp2.append(sp if mask is None else jnp.where(mask, sp, 0.0))
            log_betas.append(lb2)
            sps.append(sp2)
        suffixes = [[_dot(sp.astype(BF16), minus_later) for sp in sp2] for sp2 in sps]
        return kv, log_betas, sps, suffixes

    def finish(work, staged, c_in):
        kv, log_betas, sps, suffixes = staged
        cs = {chain: list(c) for chain, c in c_in.items()}
        weights = []
        for n, (_, _, mask, chain) in enumerate(work):
            a2 = []
            for h in range(2):
                a = jnp.exp(log_betas[n][h] + suffixes[n][h] + cs[chain][h])
                a2.append(a if mask is None else jnp.where(mask, a, 0.0))
                cs[chain][h] = cs[chain][h] - jnp.sum(sps[n][h], axis=1, keepdims=True)
            weights.append(a2)
        accs = {}
        for n, (_, _, _, chain) in enumerate(work):
            outs = [_dot(weights[n][h].astype(BF16), kv[n][1]) for h in range(2)]
            more = jnp.where(first, outs[0], outs[1])
            accs[chain] = more if chain not in accs else accs[chain] + more
        return accs, cs

    def remainder(sub, q_heads, depth):
        def cond(carry):
            j, cmax = carry
            return jnp.logical_and(j >= 0, cmax > SB_LOG_ZERO)

        def body(carry):
            j, _ = carry
            work = [(q_heads, j, None, sub)]
            accs, cs = finish(work, visit_many(work), {sub: [c_ref[sub, 0], c_ref[sub, 1]]})
            acc_ref[sub] += accs[sub]
            for h in range(2):
                c_ref[sub, h] = cs[sub][h]
            return j - 1, jnp.max(c_ref[sub])

        lax.while_loop(cond, body, (first_block + sub - depth - 1, jnp.max(c_ref[sub])))

    def run(depth):
        heads = []
        for sub in range(subs):
            q = q_ref[0, sub * blk:(sub + 1) * blk, :]
            if fold_scale:
                q = (q.astype(F32) * scale).astype(BF16)
            zero = jnp.zeros_like(q)
            heads.append((jnp.where(first, q, zero), jnp.where(first, zero, q)))
        work = [(heads[sub], first_block + sub - back, causal if back == 0 else None, sub)
                for sub in range(subs) for back in range(depth + 1)]
        zero_c = [jnp.zeros((blk, 1), F32)] * 2
        accs, cs = finish(work, visit_many(work), {sub: zero_c for sub in range(subs)})
        for sub in range(subs):
            acc_ref[sub] = accs[sub]
            for h in range(2):
                c_ref[sub, h] = cs[sub][h]
        @pl.when(jnp.max(c_ref[...]) > SB_LOG_ZERO)
        def _():
            for sub in range(subs):
                remainder(sub, heads[sub], depth)

        for sub in range(subs):
            o_ref[0, sub * blk:(sub + 1) * blk, :] = acc_ref[sub].astype(o_ref.dtype)

    pl.when(first_block >= SB_BAND)(functools.partial(run, SB_BAND))
    pl.when(first_block < SB_BAND)(functools.partial(run, 0))


def _sb_attention(qkv, blk=128, rows=512):
    b, s, n = qkv.shape
    width = n // 3
    hd = width // SB_HEADS
    assert 2 * hd == LANES
    pairs = width // LANES
    rows = min(rows, s)
    return pl.pallas_call(
        functools.partial(_sb_kernel, blk=blk, hd=hd),
        grid=(b, pairs, s // rows),
        in_specs=[
            pl.BlockSpec((1, rows, LANES), lambda bi, p, i: (bi, i, p)),
            pl.BlockSpec((1, s, LANES), lambda bi, p, i: (bi, 0, pairs + p)),
            pl.BlockSpec((1, s, LANES), lambda bi, p, i: (bi, 0, 2 * pairs + p)),
        ],
        out_specs=pl.BlockSpec((1, rows, LANES), lambda bi, p, i: (bi, i, p)),
        out_shape=jax.ShapeDtypeStruct((b, s, width), BF16),
        scratch_shapes=[pltpu.VMEM((rows // blk, blk, LANES), F32), pltpu.VMEM((rows // blk, 2, blk, 1), F32)],
        compiler_params=_params("parallel", "parallel", "arbitrary"),
        name="sb_attention",
    )(qkv, qkv, qkv)


def _gla_kernel(p_ref, wg_ref, bg_ref, ng_ref, o_ref, st_ref, *, rows, dk, dv):
    @pl.when(pl.program_id(1) == 0)
    def _():
        st_ref[...] = jnp.zeros_like(st_ref)

    heads = GLA_HEADS
    ch = GLA_CHUNK
    k0 = heads * dk
    v0 = 2 * heads * dk
    r0 = v0 + heads * dv
    g0 = r0 + heads * dv
    row = lax.broadcasted_iota(jnp.int32, (ch, ch), 0)
    col = lax.broadcasted_iota(jnp.int32, (ch, ch), 1)
    causal = col <= row
    lower = jnp.where(causal, 1.0, 0.0).astype(BF16)
    chunks = [slice(s * ch, (s + 1) * ch) for s in range(rows // ch)]
    g_pres = [_dot(p_ref[0, t, g0:g0 + LANES], wg_ref[...]) + bg_ref[...] for t in chunks]
    cums = []
    for g_pre in g_pres:
        log_a = (jnp.minimum(g_pre, 0.0) - jnp.log(1.0 + jnp.exp(-jnp.abs(g_pre)))) * (1.0 / GLA_GATE_TAU)
        hi, lo = _split_bf16(log_a)
        cums.append(_dot(lower, hi) + _dot(lower, lo))
    parts = []
    for t, cum in zip(chunks, cums):
        for h in range(heads):
            bh = cum[:, h * dk:(h + 1) * dk]
            b_last = bh[ch - 1:ch, :]
            q = p_ref[0, t, h * dk:(h + 1) * dk].astype(F32) * dk ** -0.5
            k = p_ref[0, t, k0 + h * dk:k0 + (h + 1) * dk].astype(F32)
            q_in = (q * jnp.exp(bh)).astype(BF16)
            k_in = (k * jnp.exp(-bh)).astype(BF16)
            k_state = (k * jnp.exp(b_last - bh)).astype(BF16)
            parts.append((t, h, q_in, k_in, k_state, jnp.exp(b_last)))
    scores = [jnp.where(causal, _dot_nt(q_in, k_in), 0.0).astype(BF16) for _, _, q_in, k_in, _, _ in parts]
    values = [p_ref[0, t, v0 + h * dv:v0 + (h + 1) * dv] for t, h, _, _, _, _ in parts]
    intra = [_dot(sc, v) for sc, v in zip(scores, values)]
    updates = [_dot_tn(v, k_state) for v, (_, _, _, _, k_state, _) in zip(values, parts)]
    states = [st_ref[h] for h in range(heads)]
    for n, (t, h, q_in, _, _, decay) in enumerate(parts):
        o = intra[n] + _dot_nt(q_in, states[h].astype(BF16))
        states[h] = decay * states[h] + updates[n]
        o = o * lax.rsqrt(jnp.mean(o * o, axis=-1, keepdims=True) + RMS_EPS) * ng_ref[...]
        r = p_ref[0, t, r0 + h * dv:r0 + (h + 1) * dv].astype(F32)
        o_ref[0, t, h * dv:(h + 1) * dv] = (o * (r * _sigmoid(r))).astype(o_ref.dtype)
    for h in range(heads):
        st_ref[h] = states[h]


def _gla_core(proj, w_gate_up, b_gate, norm_g, dk, dv, rows=256):
    b, s, n = proj.shape
    rows = min(rows, s)
    hk = GLA_HEADS * dk
    return pl.pallas_call(
        functools.partial(_gla_kernel, rows=rows, dk=dk, dv=dv),
        grid=(b, s // rows),
        in_specs=[
            pl.BlockSpec((1, rows, n), lambda bi, i: (bi, i, 0)),
            pl.BlockSpec((LANES, hk), lambda bi, i: (0, 0)),
            pl.BlockSpec((1, hk), lambda bi, i: (0, 0)),
            pl.BlockSpec((1, dv), lambda bi, i: (0, 0)),
        ],
        out_specs=pl.BlockSpec((1, rows, GLA_HEADS * dv), lambda bi, i: (bi, i, 0)),
        out_shape=jax.ShapeDtypeStruct((b, s, GLA_HEADS * dv), BF16),
        scratch_shapes=[pltpu.VMEM((GLA_HEADS, dv, dk), F32)],
        compiler_params=_params("parallel", "arbitrary"),
        name="gla_core",
    )(proj, w_gate_up, b_gate, norm_g)


def _conv_res_kernel(gb_ref, gc_ref, u_ref, hgc_ref, hu_ref, cw_ref, cb_ref, w_ref, x_ref, g_ref, y_ref):
    u2 = gc_ref[0].astype(F32) * u_ref[0].astype(F32)
    halo = hgc_ref[0].astype(F32) * hu_ref[0].astype(F32)
    halo = jnp.where(pl.program_id(1) > 0, halo, 0.0)
    rows = lax.broadcasted_iota(jnp.int32, u2.shape, 0)
    prev1 = jnp.where(rows == 0, halo[7:8], pltpu.roll(u2, 1, 0))
    prev2 = jnp.where(rows == 0, halo[6:7], jnp.where(rows == 1, halo[7:8], pltpu.roll(u2, 2, 0)))
    cw = cw_ref[...]
    y = cw[0:1] * prev2 + cw[1:2] * prev1 + cw[2:3] * u2 + cb_ref[...]
    o = (gb_ref[0].astype(F32) * y).astype(BF16)
    y_ref[0] = x_ref[0] + g_ref[0] * _dot(o, w_ref[...])


def _conv_res(proj, conv_w, conv_b, w_out, x, gate, tm=512):
    b, s, d = x.shape
    tm = min(tm, s)
    halo_blocks = tm // 8

    def halo_map(col):
        return lambda bi, i: (bi, jnp.maximum(i * halo_blocks - 1, 0), col)

    return pl.pallas_call(
        _conv_res_kernel,
        grid=(b, s // tm),
        in_specs=[
            pl.BlockSpec((1, tm, d), lambda bi, i: (bi, i, 0)),
            pl.BlockSpec((1, tm, d), lambda bi, i: (bi, i, 1)),
            pl.BlockSpec((1, tm, d), lambda bi, i: (bi, i, 2)),
            pl.BlockSpec((1, 8, d), halo_map(1)),
            pl.BlockSpec((1, 8, d), halo_map(2)),
            pl.BlockSpec((CONV_WIDTH, d), lambda bi, i: (0, 0)),
            pl.BlockSpec((1, d), lambda bi, i: (0, 0)),
            pl.BlockSpec((d, d), lambda bi, i: (0, 0)),
            pl.BlockSpec((1, tm, d), lambda bi, i: (bi, i, 0)),
            pl.BlockSpec((1, 1, d), lambda bi, i: (bi, 0, 0)),
        ],
        out_specs=pl.BlockSpec((1, tm, d), lambda bi, i: (bi, i, 0)),
        out_shape=jax.ShapeDtypeStruct((b, s, d), F32),
        compiler_params=_params("parallel", "parallel"),
        name="conv_res",
    )(proj, proj, proj, proj, proj, conv_w, conv_b.reshape(1, d), w_out, x, gate)


def _dil_kernel(q_ref, kc_ref, vc_ref, kp_ref, vp_ref, o_ref, l_ref, *, rows, hd, dilation, group):
    qb = DIL_QBLOCK
    n = pl.program_id(1)
    width = q_ref.shape[-1] // group
    scale = hd ** -0.5
    fold_scale = (hd & (hd - 1)) == 0 and (hd.bit_length() - 1) % 2 == 0
    first = lax.broadcasted_iota(jnp.int32, (qb, LANES), 1) < hd
    row = lax.broadcasted_iota(jnp.int32, (qb, qb), 0)
    col = lax.broadcasted_iota(jnp.int32, (qb, qb), 1)
    in_prev = col >= row
    in_cur = col <= row
    neg = -jnp.inf

    tiles = []
    for u, s in [(u, s) for u in range(group) for s in range(rows // qb)]:
        t = slice(s * qb, (s + 1) * qb)
        tp = slice((s - 1) * qb, s * qb)
        prev_mask = jnp.logical_and(in_prev, n > 0) if s == 0 else in_prev
        for p in range(width // LANES):
            c = slice(u * width + p * LANES, u * width + (p + 1) * LANES)
            q = q_ref[0, t, c]
            if fold_scale:
                q = (q.astype(F32) * scale).astype(BF16)
            zero = jnp.zeros_like(q)
            q_heads = (jnp.where(first, q, zero), jnp.where(first, zero, q))
            if s == 0:
                k_prev, v_prev = kp_ref[0, :, c], vp_ref[0, :, c]
            else:
                k_prev, v_prev = kc_ref[0, tp, c], vc_ref[0, tp, c]
            tiles.append(((u, s), p, q_heads, k_prev, kc_ref[0, t, c], v_prev, vc_ref[0, t, c], prev_mask))

    scores = []
    for _, _, q_heads, k_prev, k_cur, _, _, prev_mask in tiles:
        pair = []
        for h in range(2):
            z_prev, z_cur = _dot_nt(q_heads[h], k_prev), _dot_nt(q_heads[h], k_cur)
            if not fold_scale:
                z_prev, z_cur = z_prev * scale, z_cur * scale
            pair.append((jnp.where(prev_mask, z_prev, neg), jnp.where(in_cur, z_cur, neg)))
        scores.append(pair)

    probs = []
    for pair in scores:
        stats = []
        for z_prev, z_cur in pair:
            m = jnp.max(jnp.maximum(z_prev, z_cur), axis=1, keepdims=True)
            p_prev = jnp.exp(z_prev - m)
            p_cur = jnp.exp(z_cur - m)
            den = jnp.sum(p_prev + p_cur, axis=1, keepdims=True)
            stats.append((p_prev.astype(BF16), p_cur.astype(BF16), den, m))
        probs.append(stats)

    for ((u, s), p, _, _, _, v_prev, v_cur, _), stats in zip(tiles, probs):
        outs, lses = [], []
        for p_prev, p_cur, den, m in stats:
            outs.append((_dot(p_prev, v_prev) + _dot(p_cur, v_cur)) / den)
            lses.append(jnp.broadcast_to(m + jnp.log(den), (qb, LANES)))
        if dilation == 1:
            t_out = slice(s * qb, (s + 1) * qb)
        else:
            rho = pl.program_id(2) * group + u
            t_out = pl.ds(s * qb * dilation + rho, qb, stride=dilation)
        o_ref[0, p, t_out, :] = jnp.where(first, outs[0], outs[1])
        l_ref[0, p, t_out, :] = jnp.where(first, lses[0], lses[1])


def _dil_group(view, dilation, window, seq):
    b, length, _ = view.shape
    width = DIL_HEADS * DIL_HEAD_DIM
    assert window == dilation * DIL_QBLOCK and seq % (dilation * DIL_QBLOCK) == 0
    rows = min(max(DIL_TILE_TOKENS // dilation, DIL_QBLOCK), 512, length)
    group = min(dilation, max(512 // rows, 1))
    prev_blocks = rows // DIL_QBLOCK
    per_part = dilation // group

    def cur(part):
        return lambda bi, i, j: (bi, i, part * per_part + j)

    def prev(part):
        return lambda bi, i, j: (bi, jnp.maximum(i * prev_blocks - 1, 0), part * per_part + j)

    pairs = width // LANES
    out_shape = jax.ShapeDtypeStruct((b, pairs, seq, LANES), F32)
    out_spec = pl.BlockSpec((1, pairs, rows * dilation, LANES), lambda bi, i, j: (bi, 0, i, 0))
    return pl.pallas_call(
        functools.partial(_dil_kernel, rows=rows, hd=DIL_HEAD_DIM, dilation=dilation, group=group),
        grid=(b, length // rows, per_part),
        in_specs=[
            pl.BlockSpec((1, rows, group * width), cur(0)),
            pl.BlockSpec((1, rows, group * width), cur(1)),
            pl.BlockSpec((1, rows, group * width), cur(2)),
            pl.BlockSpec((1, DIL_QBLOCK, group * width), prev(1)),
            pl.BlockSpec((1, DIL_QBLOCK, group * width), prev(2)),
        ],
        out_specs=[out_spec, out_spec],
        out_shape=[out_shape, out_shape],
        compiler_params=_params("parallel", "parallel", "arbitrary"),
        name=f"dil_attn_r{dilation}",
    )(view, view, view, view, view)


def _dil_res_kernel(o0, o1, o2, l0, l1, l2, w_ref, x_ref, g_ref, y_ref):
    parts = []
    for p in range(o0.shape[1]):
        ls = (l0[0, p], l1[0, p], l2[0, p])
        m = jnp.maximum(jnp.maximum(ls[0], ls[1]), ls[2])
        es = [jnp.exp(l - m) for l in ls]
        parts.append((es[0] * o0[0, p] + es[1] * o1[0, p] + es[2] * o2[0, p]) / (es[0] + es[1] + es[2]))
    o = jnp.concatenate(parts, axis=1)
    y_ref[0] = x_ref[0] + g_ref[0] * _dot(o.astype(BF16), w_ref[...])


def _dil_res(outs, lses, w_out, x, gate, tm=512):
    b, s, d = x.shape
    k = w_out.shape[0]
    tm = min(tm, s)
    part = pl.BlockSpec((1, k // LANES, tm, LANES), lambda bi, i: (bi, 0, i, 0))
    return pl.pallas_call(
        _dil_res_kernel,
        grid=(b, s // tm),
        in_specs=[part] * 6 + [
            pl.BlockSpec((k, d), lambda bi, i: (0, 0)),
            pl.BlockSpec((1, tm, d), lambda bi, i: (bi, i, 0)),
            pl.BlockSpec((1, 1, d), lambda bi, i: (bi, 0, 0)),
        ],
        out_specs=pl.BlockSpec((1, tm, d), lambda bi, i: (bi, i, 0)),
        out_shape=jax.ShapeDtypeStruct((b, s, d), F32),
        compiler_params=_params("parallel", "parallel"),
        name="dil_res",
    )(*outs, *lses, w_out, x, gate)


def _router_kernel(x_ref, g_ref, sc_ref, sh_ref, wr_ref, br_ref, h_ref, route_ref, cnt_ref, run_ref, earlier_ref):
    tm = x_ref.shape[1]

    @pl.when(jnp.logical_and(pl.program_id(0) == 0, pl.program_id(1) == 0))
    def _():
        run_ref[...] = jnp.zeros_like(run_ref)
        row = lax.broadcasted_iota(jnp.int32, (tm, tm), 0)
        col = lax.broadcasted_iota(jnp.int32, (tm, tm), 1)
        earlier_ref[...] = jnp.where(col < row, 1.0, 0.0).astype(BF16)

    h = _mod_norm(x_ref[0], g_ref[...], sc_ref[0], sh_ref[0])
    _pack_slabs(h_ref, h)
    logits = _dot(h.astype(BF16), wr_ref[...].astype(BF16)) + br_ref[...]
    lane = lax.broadcasted_iota(jnp.int32, logits.shape, 1).astype(F32)
    neg = -jnp.inf
    far = float(LANES)

    def first_argmax(vals):
        top = jnp.max(vals, axis=1, keepdims=True)
        return top, jnp.min(jnp.where(vals == top, lane, far), axis=1, keepdims=True)

    grp_logits = jnp.where(lane < MOE_GROUPS, logits, neg)
    grp_max, grp = first_argmax(grp_logits)
    p_grp = 1.0 / jnp.sum(jnp.exp(grp_logits - grp_max), axis=1, keepdims=True)
    base = MOE_GROUPS + grp * MOE_PER_GROUP
    in_grp = jnp.logical_and(lane >= base, lane < base + MOE_PER_GROUP)
    exp_logits = jnp.where(in_grp, logits, neg)
    m1, i1 = first_argmax(exp_logits)
    m2, i2 = first_argmax(jnp.where(lane == i1, neg, exp_logits))
    e2 = jnp.exp(m2 - m1)
    g1 = p_grp / (1.0 + e2)
    g2 = g1 * e2
    e_a = i1 - MOE_GROUPS
    e_b = i2 - MOE_GROUPS

    pick_a = lane == e_a
    pick_b = lane == e_b
    both = jnp.where(jnp.logical_or(pick_a, pick_b), 1.0, 0.0)
    before = _dot(earlier_ref[...], both.astype(BF16)) + run_ref[...]
    rank_a = jnp.sum(jnp.where(pick_a, before, 0.0), axis=1, keepdims=True)
    rank_b = jnp.sum(jnp.where(pick_b, before, 0.0), axis=1, keepdims=True)
    run_ref[...] += jnp.sum(both, axis=0, keepdims=True)
    cnt_ref[...] = run_ref[...]

    route = jnp.where(lane == 0, e_a, jnp.where(lane == 1, e_b, 0.0))
    route = jnp.where(lane == 2, g1, jnp.where(lane == 3, g2, route))
    route_ref[...] = jnp.where(lane == 4, rank_a, jnp.where(lane == 5, rank_b, route))


def _router(x, g, sc, sh, w_route, b_route, tm=512):
    b, s, d = x.shape
    tm = min(tm, s)
    steps = s // tm
    return pl.pallas_call(
        _router_kernel,
        grid=(b, steps),
        in_specs=[
            pl.BlockSpec((1, tm, d), lambda bi, i: (bi, i, 0)),
            pl.BlockSpec((1, d), lambda bi, i: (0, 0)),
            pl.BlockSpec((1, 1, d), lambda bi, i: (bi, 0, 0)),
            pl.BlockSpec((1, 1, d), lambda bi, i: (bi, 0, 0)),
            pl.BlockSpec((d, LANES), lambda bi, i: (0, 0)),
            pl.BlockSpec((1, LANES), lambda bi, i: (0, 0)),
        ],
        out_specs=[
            pl.BlockSpec((tm * (d // PACKED_LANES), LANES), lambda bi, i: (bi * steps + i, 0)),
            pl.BlockSpec((tm, LANES), lambda bi, i: (bi * steps + i, 0)),
            pl.BlockSpec((1, LANES), lambda bi, i: (0, 0)),
        ],
        out_shape=[
            jax.ShapeDtypeStruct((b * s * (d // PACKED_LANES), LANES), jnp.int32),
            jax.ShapeDtypeStruct((b * s, LANES), F32),
            jax.ShapeDtypeStruct((1, LANES), F32),
        ],
        scratch_shapes=[pltpu.VMEM((1, LANES), F32), pltpu.VMEM((tm, tm), BF16)],
        compiler_params=_params("arbitrary", "arbitrary"),
        name="moe_router",
    )(x, g.reshape(1, d), sc, sh, w_route, b_route)


def _sc_dispatch(h_slabs, dest, n_slots):
    t, chunks, _ = h_slabs.shape
    workers = SC_CORES * SC_SUBCORES
    per_worker = t // workers
    steps = per_worker // SC_WINDOW
    assert t % (workers * SC_WINDOW * 2) == 0
    mesh = plsc.VectorSubcoreMesh(core_axis_name="core", subcore_axis_name="subcore")

    @functools.partial(
        pl.kernel, mesh=mesh,
        out_type=jax.ShapeDtypeStruct((n_slots, chunks, LANES), h_slabs.dtype),
        scratch_types=[
            pltpu.VMEM((2, MOE_TOPK, SC_WINDOW), jnp.int32),
            pltpu.VMEM((2, SC_WINDOW, chunks, LANES), h_slabs.dtype),
            pltpu.SemaphoreType.DMA((2,)),
            pltpu.SemaphoreType.DMA((2,)),
        ])
    def dispatch(h_hbm, dest_hbm, slots_hbm, idx_v, rows_v, load_sem, store_sem):
        worker = lax.axis_index("subcore") * SC_CORES + lax.axis_index("core")
        base = worker * per_worker

        @pl.loop(0, steps, step=2)
        def _(step):
            for buf in range(2):
                off = base + (step + buf) * SC_WINDOW
                pltpu.async_copy(h_hbm.at[pl.ds(off, SC_WINDOW)], rows_v.at[buf], load_sem.at[buf])
                for k in range(MOE_TOPK):
                    pltpu.sync_copy(dest_hbm.at[k, pl.ds(off, SC_WINDOW)], idx_v.at[buf, k])
            for buf in range(2):
                off = base + (step + buf) * SC_WINDOW
                pltpu.make_async_copy(h_hbm.at[pl.ds(off, SC_WINDOW)], rows_v.at[buf], load_sem.at[buf]).wait()
                copies = [pltpu.async_copy(rows_v.at[buf], slots_hbm.at[idx_v.at[buf, k]], store_sem.at[buf])
                          for k in range(MOE_TOPK)]
                for cp in copies:
                    cp.wait()

    return dispatch(h_slabs, dest)


def _zero_tail_kernel(tail_ref, valid_ref, s_in, s_out):
    del tail_ref
    rows = lax.broadcasted_iota(jnp.int32, s_in.shape, 0)
    s_out[...] = jnp.where(rows < valid_ref[pl.program_id(0)], s_in[...], 0)


def _zero_tails(tail_blk, valid_rows, slots, block_rows):
    grid_spec = pltpu.PrefetchScalarGridSpec(
        num_scalar_prefetch=2,
        grid=(tail_blk.shape[0],),
        in_specs=[pl.BlockSpec((block_rows, LANES), lambda e, tail, valid: (tail[e], 0))],
        out_specs=pl.BlockSpec((block_rows, LANES), lambda e, tail, valid: (tail[e], 0)),
    )
    return pl.pallas_call(
        _zero_tail_kernel,
        grid_spec=grid_spec,
        out_shape=jax.ShapeDtypeStruct(slots.shape, slots.dtype),
        input_output_aliases={2: 0},
        compiler_params=_params("arbitrary"),
        name="moe_zero_tails",
    )(tail_blk, valid_rows, slots)


def _expert_kernel(blk_e_ref, used_ref, x_ref, wg_ref, wu_ref, wd_ref, y_ref, wg_bf, wu_bf, wd_bf):
    i = pl.program_id(0)
    live = i < used_ref[0]
    changed = jnp.logical_or(i == 0, blk_e_ref[i] != blk_e_ref[jnp.maximum(i - 1, 0)])

    @pl.when(jnp.logical_and(live, changed))
    def _():
        wg_bf[...] = wg_ref[0, 0].astype(BF16)
        wu_bf[...] = wu_ref[0, 0].astype(BF16)
        wd_bf[...] = wd_ref[0, 0].astype(BF16)

    @pl.when(live)
    def _():
        bm = x_ref.shape[0] // (wg_bf.shape[0] // PACKED_LANES)
        x = _unpack_slabs(x_ref, bm).astype(BF16)
        gate = _dot(x, wg_bf[...])
        up = _dot(x, wu_bf[...])
        act = (gate * _sigmoid(gate) * up).astype(BF16)
        _pack_slabs(y_ref, _dot(act, wd_bf[...]))

    @pl.when(jnp.logical_not(live))
    def _():
        y_ref[...] = jnp.zeros_like(y_ref)


def _expert_blocks(blk_e, used, slots, w_gate, w_up, w_down, layer, bm):
    nb = blk_e.shape[0]
    _, _, d, f = w_gate.shape
    chunks = d // PACKED_LANES

    def block(i, e, n):
        return (jnp.minimum(i, n[0] - 1), 0)

    def weight(i, e, n):
        return (layer, e[jnp.minimum(i, n[0] - 1)], 0, 0)

    grid_spec = pltpu.PrefetchScalarGridSpec(
        num_scalar_prefetch=2,
        grid=(nb,),
        in_specs=[
            pl.BlockSpec((bm * chunks, LANES), block),
            pl.BlockSpec((1, 1, d, f), weight),
            pl.BlockSpec((1, 1, d, f), weight),
            pl.BlockSpec((1, 1, f, d), weight),
        ],
        out_specs=pl.BlockSpec((bm * chunks, LANES), lambda i, e, n: (i, 0)),
        scratch_shapes=[pltpu.VMEM((d, f), BF16), pltpu.VMEM((d, f), BF16), pltpu.VMEM((f, d), BF16)],
    )
    return pl.pallas_call(
        _expert_kernel,
        grid_spec=grid_spec,
        out_shape=jax.ShapeDtypeStruct((nb * bm * chunks, LANES), jnp.int32),
        compiler_params=_params("arbitrary"),
        name="moe_experts",
    )(blk_e, used, slots, w_gate, w_up, w_down)


def _sc_gather(y_slabs, idx):
    n = idx.shape[0]
    _, chunks, _ = y_slabs.shape
    workers = SC_CORES * SC_SUBCORES
    per_worker = n // workers
    steps = per_worker // SC_WINDOW
    assert n % (workers * SC_WINDOW * 2) == 0
    mesh = plsc.VectorSubcoreMesh(core_axis_name="core", subcore_axis_name="subcore")

    @functools.partial(
        pl.kernel, mesh=mesh,
        out_type=jax.ShapeDtypeStruct((n, chunks, LANES), y_slabs.dtype),
        scratch_types=[
            pltpu.VMEM((2, SC_WINDOW), jnp.int32),
            pltpu.VMEM((2, SC_WINDOW, chunks, LANES), y_slabs.dtype),
            pltpu.SemaphoreType.DMA((2,)),
        ])
    def gather(y_hbm, idx_hbm, out_hbm, idx_v, rows_v, sem):
        worker = lax.axis_index("subcore") * SC_CORES + lax.axis_index("core")
        base = worker * per_worker

        @pl.loop(0, steps, step=2)
        def _(step):
            for buf in range(2):
                off = base + (step + buf) * SC_WINDOW
                pltpu.sync_copy(idx_hbm.at[pl.ds(off, SC_WINDOW)], idx_v.at[buf])
                pltpu.async_copy(y_hbm.at[idx_v.at[buf]], rows_v.at[buf], sem.at[buf])
            for buf in range(2):
                off = base + (step + buf) * SC_WINDOW
                pltpu.make_async_copy(y_hbm.at[idx_v.at[buf]], rows_v.at[buf], sem.at[buf]).wait()
                pltpu.sync_copy(rows_v.at[buf], out_hbm.at[pl.ds(off, SC_WINDOW)])

    return gather(y_slabs, idx)


def _combine_kernel(g0_ref, g1_ref, route_ref, x_ref, gate_ref, fg_ref, o_ref, *, final):
    tm = x_ref.shape[0]
    route = route_ref[...]
    y = route[:, 2:3] * _unpack_slabs(g0_ref, tm) + route[:, 3:4] * _unpack_slabs(g1_ref, tm)
    out = x_ref[...] + gate_ref[0] * y
    if final:
        out = out * lax.rsqrt(jnp.mean(out * out, axis=-1, keepdims=True) + RMS_EPS) * fg_ref[...]
    o_ref[...] = out


def _combine(dest, route, y_slabs, x, gate, final_g, final, tm, pieces=4):
    b, s, d = x.shape
    t = b * s
    chunks = d // PACKED_LANES
    per_seq = s // tm
    piece = t // pieces
    tiles = piece // tm
    out = x.reshape(t, d)
    fg = final_g.reshape(1, d)
    for c in range(pieces):
        idx = dest[:, c * piece:(c + 1) * piece].reshape(MOE_TOPK * piece)
        g = _sc_gather(y_slabs, idx).reshape(MOE_TOPK * piece * chunks, LANES)
        first = c * tiles
        out = pl.pallas_call(
            functools.partial(_combine_kernel, final=final),
            grid=(tiles,),
            in_specs=[
                pl.BlockSpec((tm * chunks, LANES), lambda i: (i, 0)),
                pl.BlockSpec((tm * chunks, LANES), lambda i: (tiles + i, 0)),
                pl.BlockSpec((tm, LANES), lambda i, first=first: (first + i, 0)),
                pl.BlockSpec((tm, d), lambda i, first=first: (first + i, 0)),
                pl.BlockSpec((1, 1, d), lambda i, first=first: ((first + i) // per_seq, 0, 0)),
                pl.BlockSpec((1, d), lambda i: (0, 0)),
            ],
            out_specs=pl.BlockSpec((tm, d), lambda i, first=first: (first + i, 0)),
            out_shape=jax.ShapeDtypeStruct((t, d), F32),
            input_output_aliases={3: 0},
            compiler_params=_params("parallel"),
            name="moe_combine",
        )(g, g, route, out, gate, fg)
    return out.reshape(b, s, d)


def _slot_tables(route, counts, bm):
    t = route.shape[0]
    nb = MOE_TOPK * t // bm + MOE_EXPERTS
    counts = counts[0, :MOE_EXPERTS].astype(jnp.int32)
    padded = (counts + bm - 1) // bm * bm
    pad_end = jnp.cumsum(padded)
    pad_start = pad_end - padded
    expert = route[:, 0:MOE_TOPK].astype(jnp.int32)
    rank = route[:, 4:4 + MOE_TOPK].astype(jnp.int32)
    dest = pad_start[expert] + rank
    blk_start = jnp.arange(nb, dtype=jnp.int32) * bm
    blk_e = jnp.minimum(jnp.sum((pad_end[None, :] <= blk_start[:, None]).astype(jnp.int32), axis=1), MOE_EXPERTS - 1)
    used = (pad_end[-1:] // bm).astype(jnp.int32)
    tail_blk = jnp.maximum(pad_end // bm - 1, 0).astype(jnp.int32)
    tail_fill = jnp.where(counts % bm != 0, counts % bm, bm).astype(jnp.int32)
    return dest.T, blk_e, used, tail_blk, tail_fill, nb


def _moe_layer(x, g, sc, sh, gate, w_route, b_route, w_gate, w_up, w_down, layer, final_g, final, bm=512, tm=1024):
    b, s, d = x.shape
    t = b * s
    chunks = d // PACKED_LANES
    tm = min(tm, s)
    h_rows, route, counts = _router(x, g, sc, sh, w_route, b_route)
    dest, blk_e, used, tail_blk, tail_fill, nb = _slot_tables(route, counts, bm)
    slots = _sc_dispatch(h_rows.reshape(t, chunks, LANES), dest, nb * bm)
    slots = _zero_tails(tail_blk, tail_fill * chunks, slots.reshape(nb * bm * chunks, LANES), bm * chunks)
    y_slabs = _expert_blocks(blk_e, used, slots, w_gate, w_up, w_down, layer, bm).reshape(nb * bm, chunks, LANES)
    return _combine(dest, route, y_slabs, x, gate, final_g, final, tm)


def kernel(x, c, ada_w, ada_b, norm_g, final_g, sb_w_in, sb_w_out, gla_w_in, gla_w_gate_up, gla_b_gate, gla_norm_g, gla_w_out, conv_w_in, conv_w, conv_b, conv_w_out, dil_w_in, dil_w_out, moe_w_grp, moe_b_grp, moe_w_exp, moe_b_exp, moe_w_gate, moe_w_up, moe_w_down):
    b, s, d = x.shape
    depth = ada_w.shape[0]
    mod = _ada_mod(c, ada_w, ada_b).reshape(depth, b, N_MOD, 1, d)
    for i in range(depth):
        sh1, sc1, g1, sh2, sc2, g2 = (mod[i, :, m] for m in range(N_MOD))
        kind, j = i % 4, i // 4
        if kind == 0:
            qkv = _norm_proj(x, norm_g[i, 0], sc1, sh1, sb_w_in[j].astype(BF16))
            x = _proj_res(_sb_attention(qkv), sb_w_out[j].astype(BF16), x, g1)
        elif kind == 1:
            dk = gla_w_gate_up.shape[-1] // GLA_HEADS
            dv = gla_norm_g.shape[-1]
            w_in = jnp.pad(gla_w_in[j], ((0, 0), (0, LANES - GLA_GATE_RANK))).astype(BF16)
            w_up = jnp.pad(gla_w_gate_up[j], ((0, LANES - GLA_GATE_RANK), (0, 0))).astype(BF16)
            proj = _norm_proj(x, norm_g[i, 0], sc1, sh1, w_in, tn=640)
            o = _gla_core(proj, w_up, gla_b_gate[j].reshape(1, -1), gla_norm_g[j].reshape(1, -1), dk, dv)
            x = _proj_res(o, gla_w_out[j].astype(BF16), x, g1)
        elif kind == 2:
            proj = _norm_proj(x, norm_g[i, 0], sc1, sh1, conv_w_in[j].astype(BF16))
            x = _conv_res(proj, conv_w[j], conv_b[j], conv_w_out[j].astype(BF16), x, g1)
        else:
            group_w = 3 * DIL_HEADS * DIL_HEAD_DIM
            outs, lses = [], []
            for g, (window, dilation) in enumerate(DIL_PATTERNS):
                w_g = dil_w_in[j][:, g * group_w:(g + 1) * group_w].astype(BF16)
                view = _norm_proj(x, norm_g[i, 0], sc1, sh1, w_g, dilation=dilation)
                o_g, l_g = _dil_group(view, dilation, window, s)
                outs.append(o_g)
                lses.append(l_g)
            x = _dil_res(outs, lses, dil_w_out[j].astype(BF16), x, g1)
        w_route = jnp.pad(jnp.concatenate([moe_w_grp[i], moe_w_exp[i]], axis=1),
                          ((0, 0), (0, LANES - MOE_GROUPS - MOE_EXPERTS)))
        b_route = jnp.pad(jnp.concatenate([moe_b_grp[i], moe_b_exp[i]]), (0, LANES - MOE_GROUPS - MOE_EXPERTS))
        x = _moe_layer(x, norm_g[i, 1], sc2, sh2, g2, w_route, b_route.reshape(1, LANES),
                       moe_w_gate, moe_w_up, moe_w_down, i, final_g, final=(i == depth - 1))
    return x
```

```python
import functools

import jax
import jax.numpy as jnp
from jax import lax
from jax.experimental import pallas as pl
from jax.experimental.pallas import tpu as pltpu
from jax.experimental.pallas import tpu_sc as plsc

F32 = jnp.float32
BF16 = jnp.bfloat16

LANES = 128
MXU_COLUMNS = 256
RMS_EPS = 1e-6
N_MOD = 6

SB_HEADS = 16
GLA_HEADS = 4
GLA_GATE_RANK = 16
GLA_GATE_TAU = 16.0
GLA_CHUNK = 64
CONV_WIDTH = 3
DIL_PATTERNS = ((128, 1), (512, 4), (2048, 16))
DIL_HEADS = 8
DIL_HEAD_DIM = 64
DIL_QBLOCK = 128
DIL_TILE_TOKENS = 2048
MOE_GROUPS = 4
MOE_PER_GROUP = 8
MOE_EXPERTS = MOE_GROUPS * MOE_PER_GROUP
MOE_TOPK = 2

SC_CORES = 2
SC_SUBCORES = 16
SC_WINDOW = 64
PACKED_LANES = 2 * LANES

SB_LOG_ZERO = -104.0
SB_BAND = 2

VMEM_LIMIT = 48 * 1024 * 1024


def _params(*sem):
    return pltpu.CompilerParams(dimension_semantics=sem, vmem_limit_bytes=VMEM_LIMIT)


def _sigmoid(x):
    return 1.0 / (1.0 + jnp.exp(-x))


def _softplus(x):
    return jnp.maximum(x, 0.0) + jnp.log(1.0 + jnp.exp(-jnp.abs(x)))


def _split_bf16(x):
    hi = x.astype(BF16)
    lo = (x - hi.astype(F32)).astype(BF16)
    return hi, lo


def _dot(a, b):
    return jnp.dot(a, b, preferred_element_type=F32)


def _dot_nt(a, b):
    return lax.dot_general(a, b, (((1,), (1,)), ((), ())), preferred_element_type=F32)


def _dot_tn(a, b):
    return lax.dot_general(a, b, (((0,), (0,)), ((), ())), preferred_element_type=F32)


def _unpack_slabs(ref, rows):
    k = ref.shape[0] // rows
    words = jnp.concatenate([ref[pl.ds(s, rows, stride=k), :] for s in range(k)], axis=1)
    halves = [pltpu.unpack_elementwise(words, index=i, packed_dtype=BF16, unpacked_dtype=F32) for i in range(2)]
    return jnp.concatenate(halves, axis=1)


def _pack_slabs(ref, values):
    rows, d = values.shape
    half = d // 2
    words = pltpu.pack_elementwise([values[:, :half], values[:, half:]], packed_dtype=BF16)
    for s in range(half // LANES):
        ref[pl.ds(s, rows, stride=half // LANES), :] = words[:, s * LANES:(s + 1) * LANES]


def _mod_norm(x, g, sc, sh):
    r = lax.rsqrt(jnp.mean(x * x, axis=-1, keepdims=True) + RMS_EPS)
    return (x * r) * g * (1.0 + sc) + sh


def _ada_kernel(c_ref, w_ref, b_ref, o_ref):
    c = c_ref[...]
    cond = c * _sigmoid(c)
    o_ref[0] = _dot(cond.astype(BF16), w_ref[0].astype(BF16)) + b_ref[0]


def _ada_mod(c, ada_w, ada_b):
    depth, d, n = ada_w.shape
    b = c.shape[0]
    tn = 1536
    return pl.pallas_call(
        _ada_kernel,
        grid=(depth, n // tn),
        in_specs=[
            pl.BlockSpec((b, d), lambda i, j: (0, 0)),
            pl.BlockSpec((1, d, tn), lambda i, j: (i, 0, j)),
            pl.BlockSpec((1, 1, tn), lambda i, j: (i, 0, j)),
        ],
        out_specs=pl.BlockSpec((1, b, tn), lambda i, j: (i, 0, j)),
        out_shape=jax.ShapeDtypeStruct((depth, b, n), F32),
        compiler_params=_params("parallel", "parallel"),
        name="ada_mod",
    )(c, ada_w, ada_b.reshape(depth, 1, n))


def _norm_proj_kernel(x_ref, g_ref, sc_ref, sh_ref, w_ref, o_ref, h_ref, *res_ref, dilation, tn):
    h_ref[...] = _mod_norm(x_ref[0], g_ref[...], sc_ref[0], sh_ref[0]).astype(BF16)
    tm = h_ref.shape[0]
    for j in range(w_ref.shape[1] // tn):
        res = _dot(h_ref[...], w_ref[:, j * tn:(j + 1) * tn])
        if dilation == 1:
            o_ref[0, :, j * tn:(j + 1) * tn] = res.astype(o_ref.dtype)
            continue
        for c in range(tn // LANES):
            res_ref[0][j * (tn // LANES) + c] = res[:, c * LANES:(c + 1) * LANES]
        for rho in range(dilation):
            for c in range(tn // LANES):
                rows = res_ref[0][j * (tn // LANES) + c, pl.ds(rho, tm // dilation, stride=dilation), :]
                col = (j * dilation + rho) * tn + c * LANES
                o_ref[0, :, col:col + LANES] = rows.astype(o_ref.dtype)


def _norm_proj(x, g, sc, sh, w, tm=512, tn=512, dilation=1):
    b, s, d = x.shape
    n = w.shape[1]
    tm = min(tm, s)
    assert s % tm == 0 and n % tn == 0 and tm % (16 * dilation) == 0
    scratch = [pltpu.VMEM((tm, d), BF16)]
    if dilation > 1:
        scratch.append(pltpu.VMEM((n // LANES, tm, LANES), F32))
    return pl.pallas_call(
        functools.partial(_norm_proj_kernel, dilation=dilation, tn=tn),
        grid=(b, s // tm),
        in_specs=[
            pl.BlockSpec((1, tm, d), lambda bi, i: (bi, i, 0)),
            pl.BlockSpec((1, d), lambda bi, i: (0, 0)),
            pl.BlockSpec((1, 1, d), lambda bi, i: (bi, 0, 0)),
            pl.BlockSpec((1, 1, d), lambda bi, i: (bi, 0, 0)),
            pl.BlockSpec((d, n), lambda bi, i: (0, 0)),
        ],
        out_specs=pl.BlockSpec((1, tm // dilation, dilation * n), lambda bi, i: (bi, i, 0)),
        out_shape=jax.ShapeDtypeStruct((b, s // dilation, dilation * n), BF16),
        scratch_shapes=scratch,
        compiler_params=_params("parallel", "parallel"),
        name="norm_proj",
    )(x, g.reshape(1, d), sc, sh, w)


def _proj_res_kernel(o_ref, w_ref, x_ref, g_ref, y_ref):
    y_ref[0] = x_ref[0] + g_ref[0] * _dot(o_ref[0], w_ref[...])


def _proj_res(o, w, x, gate, tm=512):
    b, s, d = x.shape
    k = o.shape[-1]
    tm = min(tm, s)
    return pl.pallas_call(
        _proj_res_kernel,
        grid=(b, s // tm),
        in_specs=[
            pl.BlockSpec((1, tm, k), lambda bi, i: (bi, i, 0)),
            pl.BlockSpec((k, d), lambda bi, i: (0, 0)),
            pl.BlockSpec((1, tm, d), lambda bi, i: (bi, i, 0)),
            pl.BlockSpec((1, 1, d), lambda bi, i: (bi, 0, 0)),
        ],
        out_specs=pl.BlockSpec((1, tm, d), lambda bi, i: (bi, i, 0)),
        out_shape=jax.ShapeDtypeStruct((b, s, d), F32),
        compiler_params=_params("parallel", "parallel"),
        name="proj_res",
    )(o, w, x, gate)


def _sb_kernel(q_ref, k_ref, v_ref, o_ref, acc_ref, c_ref, *, blk, hd):
    scale = hd ** -0.5
    fold_scale = (hd & (hd - 1)) == 0 and (hd.bit_length() - 1) % 2 == 0
    first = lax.broadcasted_iota(jnp.int32, (blk, LANES), 1) < hd
    row = lax.broadcasted_iota(jnp.int32, (blk, blk), 0)
    col = lax.broadcasted_iota(jnp.int32, (blk, blk), 1)
    minus_later = jnp.where(row > col, -1.0, 0.0).astype(BF16)
    causal = col < row
    subs = q_ref.shape[1] // blk
    first_block = pl.program_id(2) * subs

    def visit_many(work):
        kv = []
        for _, block, _, _ in work:
            start = pl.multiple_of(block * blk, blk)
            kv.append((k_ref[0, pl.ds(start, blk), :], v_ref[0, pl.ds(start, blk), :]))
        zs = [[_dot_nt(q_heads[h], kb) for h in range(2)] for (q_heads, _, _, _), (kb, _) in zip(work, kv)]
        log_betas, sps = [], []
        for (_, _, mask, _), z2 in zip(work, zs):
            lb2, sp2 = [], []
            for z in z2:
                if not fold_scale:
                    z = z * scale
                sp = _softplus(z)
                lb2.append(z - sp)
                sp2.append(sp if mask is None else jnp.where(mask, sp, 0.0))
            log_betas.append(lb2)
            sps.append(sp2)
        suffixes = [[_dot(sp.astype(BF16), minus_later) for sp in sp2] for sp2 in sps]
        return kv, log_betas, sps, suffixes

    def finish(work, staged, c_in):
        kv, log_betas, sps, suffixes = staged
        cs = {chain: list(c) for chain, c in c_in.items()}
        weights = []
        for n, (_, _, mask, chain) in enumerate(work):
            a2 = []
            for h in range(2):
                a = jnp.exp(log_betas[n][h] + suffixes[n][h] + cs[chain][h])
                a2.append(a if mask is None else jnp.where(mask, a, 0.0))
                cs[chain][h] = cs[chain][h] - jnp.sum(sps[n][h], axis=1, keepdims=True)
            weights.append(a2)
        accs = {}
        for n, (_, _, _, chain) in enumerate(work):
            outs = [_dot(weights[n][h].astype(BF16), kv[n][1]) for h in range(2)]
            more = jnp.where(first, outs[0], outs[1])
            accs[chain] = more if chain not in accs else accs[chain] + more
        return accs, cs

    def remainder(sub, q_heads, depth):
        def cond(carry):
            j, cmax = carry
            return jnp.logical_and(j >= 0, cmax > SB_LOG_ZERO)

        def body(carry):
            j, _ = carry
            work = [(q_heads, j, None, sub)]
            accs, cs = finish(work, visit_many(work), {sub: [c_ref[sub, 0], c_ref[sub, 1]]})
            acc_ref[sub] += accs[sub]
            for h in range(2):
                c_ref[sub, h] = cs[sub][h]
            return j - 1, jnp.max(c_ref[sub])

        lax.while_loop(cond, body, (first_block + sub - depth - 1, jnp.max(c_ref[sub])))

    def run(depth):
        heads = []
        for sub in range(subs):
            q = q_ref[0, sub * blk:(sub + 1) * blk, :]
            if fold_scale:
                q = (q.astype(F32) * scale).astype(BF16)
            zero = jnp.zeros_like(q)
            heads.append((jnp.where(first, q, zero), jnp.where(first, zero, q)))
        work = [(heads[sub], first_block + sub - back, causal if back == 0 else None, sub)
                for sub in range(subs) for back in range(depth + 1)]
        zero_c = [jnp.zeros((blk, 1), F32)] * 2
        accs, cs = finish(work, visit_many(work), {sub: zero_c for sub in range(subs)})
        for sub in range(subs):
            acc_ref[sub] = accs[sub]
            for h in range(2):
                c_ref[sub, h] = cs[sub][h]
        @pl.when(jnp.max(c_ref[...]) > SB_LOG_ZERO)
        def _():
            for sub in range(subs):
                remainder(sub, heads[sub], depth)

        for sub in range(subs):
            o_ref[0, sub * blk:(sub + 1) * blk, :] = acc_ref[sub].astype(o_ref.dtype)

    pl.when(first_block >= SB_BAND)(functools.partial(run, SB_BAND))
    pl.when(first_block < SB_BAND)(functools.partial(run, 0))


def _sb_attention(qkv, blk=128, rows=512):
    b, s, n = qkv.shape
    width = n // 3
    hd = width // SB_HEADS
    assert 2 * hd == LANES
    pairs = width // LANES
    rows = min(rows, s)
    return pl.pallas_call(
        functools.partial(_sb_kernel, blk=blk, hd=hd),
        grid=(b, pairs, s // rows),
        in_specs=[
            pl.BlockSpec((1, rows, LANES), lambda bi, p, i: (bi, i, p)),
            pl.BlockSpec((1, s, LANES), lambda bi, p, i: (bi, 0, pairs + p)),
            pl.BlockSpec((1, s, LANES), lambda bi, p, i: (bi, 0, 2 * pairs + p)),
        ],
        out_specs=pl.BlockSpec((1, rows, LANES), lambda bi, p, i: (bi, i, p)),
        out_shape=jax.ShapeDtypeStruct((b, s, width), BF16),
        scratch_shapes=[pltpu.VMEM((rows // blk, blk, LANES), F32), pltpu.VMEM((rows // blk, 2, blk, 1), F32)],
        compiler_params=_params("parallel", "parallel", "arbitrary"),
        name="sb_attention",
    )(qkv, qkv, qkv)


def _gla_kernel(p_ref, wg_ref, bg_ref, ng_ref, o_ref, st_ref, *, rows, dk, dv):
    @pl.when(pl.program_id(1) == 0)
    def _():
        st_ref[...] = jnp.zeros_like(st_ref)

    heads = GLA_HEADS
    ch = GLA_CHUNK
    k0 = heads * dk
    v0 = 2 * heads * dk
    r0 = v0 + heads * dv
    g0 = r0 + heads * dv
    row = lax.broadcasted_iota(jnp.int32, (ch, ch), 0)
    col = lax.broadcasted_iota(jnp.int32, (ch, ch), 1)
    causal = col <= row
    lower = jnp.where(causal, 1.0, 0.0).astype(BF16)
    chunks = [slice(s * ch, (s + 1) * ch) for s in range(rows // ch)]
    g_pres = [_dot(p_ref[0, t, g0:g0 + LANES], wg_ref[...]) + bg_ref[...] for t in chunks]
    cums = []
    for g_pre in g_pres:
        log_a = (jnp.minimum(g_pre, 0.0) - jnp.log(1.0 + jnp.exp(-jnp.abs(g_pre)))) * (1.0 / GLA_GATE_TAU)
        hi, lo = _split_bf16(log_a)
        cums.append(_dot(lower, hi) + _dot(lower, lo))
    parts = []
    for t, cum in zip(chunks, cums):
        for h in range(heads):
            bh = cum[:, h * dk:(h + 1) * dk]
            b_last = bh[ch - 1:ch, :]
            q = p_ref[0, t, h * dk:(h + 1) * dk].astype(F32) * dk ** -0.5
            k = p_ref[0, t, k0 + h * dk:k0 + (h + 1) * dk].astype(F32)
            q_in = (q * jnp.exp(bh)).astype(BF16)
            k_in = (k * jnp.exp(-bh)).astype(BF16)
            k_state = (k * jnp.exp(b_last - bh)).astype(BF16)
            parts.append((t, h, q_in, k_in, k_state, jnp.exp(b_last)))
    scores = [jnp.where(causal, _dot_nt(q_in, k_in), 0.0).astype(BF16) for _, _, q_in, k_in, _, _ in parts]
    values = [p_ref[0, t, v0 + h * dv:v0 + (h + 1) * dv] for t, h, _, _, _, _ in parts]
    intra = [_dot(sc, v) for sc, v in zip(scores, values)]
    updates = [_dot_tn(v, k_state) for v, (_, _, _, _, k_state, _) in zip(values, parts)]
    states = [st_ref[h] for h in range(heads)]
    for n, (t, h, q_in, _, _, decay) in enumerate(parts):
        o = intra[n] + _dot_nt(q_in, states[h].astype(BF16))
        states[h] = decay * states[h] + updates[n]
        o = o * lax.rsqrt(jnp.mean(o * o, axis=-1, keepdims=True) + RMS_EPS) * ng_ref[...]
        r = p_ref[0, t, r0 + h * dv:r0 + (h + 1) * dv].astype(F32)
        o_ref[0, t, h * dv:(h + 1) * dv] = (o * (r * _sigmoid(r))).astype(o_ref.dtype)
    for h in range(heads):
        st_ref[h] = states[h]


def _gla_core(proj, w_gate_up, b_gate, norm_g, dk, dv, rows=256):
    b, s, n = proj.shape
    rows = min(rows, s)
    hk = GLA_HEADS * dk
    return pl.pallas_call(
        functools.partial(_gla_kernel, rows=rows, dk=dk, dv=dv),
        grid=(b, s // rows),
        in_specs=[
            pl.BlockSpec((1, rows, n), lambda bi, i: (bi, i, 0)),
            pl.BlockSpec((LANES, hk), lambda bi, i: (0, 0)),
            pl.BlockSpec((1, hk), lambda bi, i: (0, 0)),
            pl.BlockSpec((1, dv), lambda bi, i: (0, 0)),
        ],
        out_specs=pl.BlockSpec((1, rows, GLA_HEADS * dv), lambda bi, i: (bi, i, 0)),
        out_shape=jax.ShapeDtypeStruct((b, s, GLA_HEADS * dv), BF16),
        scratch_shapes=[pltpu.VMEM((GLA_HEADS, dv, dk), F32)],
        compiler_params=_params("parallel", "arbitrary"),
        name="gla_core",
    )(proj, w_gate_up, b_gate, norm_g)


def _conv_res_kernel(gb_ref, gc_ref, u_ref, hgc_ref, hu_ref, cw_ref, cb_ref, w_ref, x_ref, g_ref, y_ref):
    u2 = gc_ref[0].astype(F32) * u_ref[0].astype(F32)
    halo = hgc_ref[0].astype(F32) * hu_ref[0].astype(F32)
    halo = jnp.where(pl.program_id(1) > 0, halo, 0.0)
    rows = lax.broadcasted_iota(jnp.int32, u2.shape, 0)
    prev1 = jnp.where(rows == 0, halo[7:8], pltpu.roll(u2, 1, 0))
    prev2 = jnp.where(rows == 0, halo[6:7], jnp.where(rows == 1, halo[7:8], pltpu.roll(u2, 2, 0)))
    cw = cw_ref[...]
    y = cw[0:1] * prev2 + cw[1:2] * prev1 + cw[2:3] * u2 + cb_ref[...]
    o = (gb_ref[0].astype(F32) * y).astype(BF16)
    y_ref[0] = x_ref[0] + g_ref[0] * _dot(o, w_ref[...])


def _conv_res(proj, conv_w, conv_b, w_out, x, gate, tm=512):
    b, s, d = x.shape
    tm = min(tm, s)
    halo_blocks = tm // 8

    def halo_map(col):
        return lambda bi, i: (bi, jnp.maximum(i * halo_blocks - 1, 0), col)

    return pl.pallas_call(
        _conv_res_kernel,
        grid=(b, s // tm),
        in_specs=[
            pl.BlockSpec((1, tm, d), lambda bi, i: (bi, i, 0)),
            pl.BlockSpec((1, tm, d), lambda bi, i: (bi, i, 1)),
            pl.BlockSpec((1, tm, d), lambda bi, i: (bi, i, 2)),
            pl.BlockSpec((1, 8, d), halo_map(1)),
            pl.BlockSpec((1, 8, d), halo_map(2)),
            pl.BlockSpec((CONV_WIDTH, d), lambda bi, i: (0, 0)),
            pl.BlockSpec((1, d), lambda bi, i: (0, 0)),
            pl.BlockSpec((d, d), lambda bi, i: (0, 0)),
            pl.BlockSpec((1, tm, d), lambda bi, i: (bi, i, 0)),
            pl.BlockSpec((1, 1, d), lambda bi, i: (bi, 0, 0)),
        ],
        out_specs=pl.BlockSpec((1, tm, d), lambda bi, i: (bi, i, 0)),
        out_shape=jax.ShapeDtypeStruct((b, s, d), F32),
        compiler_params=_params("parallel", "parallel"),
        name="conv_res",
    )(proj, proj, proj, proj, proj, conv_w, conv_b.reshape(1, d), w_out, x, gate)


def _dil_kernel(q_ref, kc_ref, vc_ref, kp_ref, vp_ref, o_ref, l_ref, *, rows, hd, dilation, group):
    qb = DIL_QBLOCK
    n = pl.program_id(1)
    width = q_ref.shape[-1] // group
    scale = hd ** -0.5
    fold_scale = (hd & (hd - 1)) == 0 and (hd.bit_length() - 1) % 2 == 0
    first = lax.broadcasted_iota(jnp.int32, (qb, LANES), 1) < hd
    row = lax.broadcasted_iota(jnp.int32, (qb, qb), 0)
    col = lax.broadcasted_iota(jnp.int32, (qb, qb), 1)
    in_prev = col >= row
    in_cur = col <= row
    neg = -jnp.inf

    tiles = []
    for u, s in [(u, s) for u in range(group) for s in range(rows // qb)]:
        t = slice(s * qb, (s + 1) * qb)
        tp = slice((s - 1) * qb, s * qb)
        prev_mask = jnp.logical_and(in_prev, n > 0) if s == 0 else in_prev
        for p in range(width // LANES):
            c = slice(u * width + p * LANES, u * width + (p + 1) * LANES)
            q = q_ref[0, t, c]
            if fold_scale:
                q = (q.astype(F32) * scale).astype(BF16)
            zero = jnp.zeros_like(q)
            q_heads = (jnp.where(first, q, zero), jnp.where(first, zero, q))
            if s == 0:
                k_prev, v_prev = kp_ref[0, :, c], vp_ref[0, :, c]
            else:
                k_prev, v_prev = kc_ref[0, tp, c], vc_ref[0, tp, c]
            tiles.append(((u, s), p, q_heads, k_prev, kc_ref[0, t, c], v_prev, vc_ref[0, t, c], prev_mask))

    scores = []
    for _, _, q_heads, k_prev, k_cur, _, _, prev_mask in tiles:
        pair = []
        for h in range(2):
            z_prev, z_cur = _dot_nt(q_heads[h], k_prev), _dot_nt(q_heads[h], k_cur)
            if not fold_scale:
                z_prev, z_cur = z_prev * scale, z_cur * scale
            pair.append((jnp.where(prev_mask, z_prev, neg), jnp.where(in_cur, z_cur, neg)))
        scores.append(pair)

    probs = []
    for pair in scores:
        stats = []
        for z_prev, z_cur in pair:
            m = jnp.max(jnp.maximum(z_prev, z_cur), axis=1, keepdims=True)
            p_prev = jnp.exp(z_prev - m)
            p_cur = jnp.exp(z_cur - m)
            den = jnp.sum(p_prev + p_cur, axis=1, keepdims=True)
            stats.append((p_prev.astype(BF16), p_cur.astype(BF16), den, m))
        probs.append(stats)

    lane = lax.broadcasted_iota(jnp.int32, (qb, LANES), 1)
    lse_rows = {}
    for ((u, s), p, _, _, _, v_prev, v_cur, _), stats in zip(tiles, probs):
        if dilation == 1:
            t_out = slice(s * qb, (s + 1) * qb)
        else:
            rho = pl.program_id(2) * group + u
            t_out = pl.ds(s * qb * dilation + rho, qb, stride=dilation)
        outs = []
        lse = lse_rows.get((u, s), jnp.zeros((qb, LANES), F32))
        for h, (p_prev, p_cur, den, m) in enumerate(stats):
            outs.append((_dot(p_prev, v_prev) + _dot(p_cur, v_cur)) / den)
            lse = jnp.where(lane == 2 * p + h, m + jnp.log(den), lse)
        lse_rows[(u, s)] = lse
        o_ref[0, p, t_out, :] = jnp.where(first, outs[0], outs[1])
        if p == width // LANES - 1:
            l_ref[0, t_out, :] = lse


def _dil_group(view, dilation, window, seq):
    b, length, _ = view.shape
    width = DIL_HEADS * DIL_HEAD_DIM
    assert window == dilation * DIL_QBLOCK and seq % (dilation * DIL_QBLOCK) == 0
    rows = min(max(DIL_TILE_TOKENS // dilation, DIL_QBLOCK), 512, length)
    group = min(dilation, max(512 // rows, 1))
    prev_blocks = rows // DIL_QBLOCK
    per_part = dilation // group

    def cur(part):
        return lambda bi, i, j: (bi, i, part * per_part + j)

    def prev(part):
        return lambda bi, i, j: (bi, jnp.maximum(i * prev_blocks - 1, 0), part * per_part + j)

    pairs = width // LANES
    out_shape = jax.ShapeDtypeStruct((b, pairs, seq, LANES), F32)
    out_spec = pl.BlockSpec((1, pairs, rows * dilation, LANES), lambda bi, i, j: (bi, 0, i, 0))
    lse_shape = jax.ShapeDtypeStruct((b, seq, LANES), F32)
    lse_spec = pl.BlockSpec((1, rows * dilation, LANES), lambda bi, i, j: (bi, i, 0))
    return pl.pallas_call(
        functools.partial(_dil_kernel, rows=rows, hd=DIL_HEAD_DIM, dilation=dilation, group=group),
        grid=(b, length // rows, per_part),
        in_specs=[
            pl.BlockSpec((1, rows, group * width), cur(0)),
            pl.BlockSpec((1, rows, group * width), cur(1)),
            pl.BlockSpec((1, rows, group * width), cur(2)),
            pl.BlockSpec((1, DIL_QBLOCK, group * width), prev(1)),
            pl.BlockSpec((1, DIL_QBLOCK, group * width), prev(2)),
        ],
        out_specs=[out_spec, lse_spec],
        out_shape=[out_shape, lse_shape],
        compiler_params=_params("parallel", "parallel", "arbitrary"),
        name=f"dil_attn_r{dilation}",
    )(view, view, view, view, view)


def _dil_res_kernel(o0, o1, o2, l0, l1, l2, w_ref, x_ref, g_ref, y_ref):
    ls = (l0[0], l1[0], l2[0])
    m = jnp.maximum(jnp.maximum(ls[0], ls[1]), ls[2])
    es = [jnp.exp(l - m) for l in ls]
    total = es[0] + es[1] + es[2]
    ws = [e / total for e in es]
    hd = LANES // 2
    first = lax.broadcasted_iota(jnp.int32, (x_ref.shape[1], LANES), 1) < hd
    parts = []
    for p in range(o0.shape[1]):
        w = [jnp.where(first, wg[:, 2 * p:2 * p + 1], wg[:, 2 * p + 1:2 * p + 2]) for wg in ws]
        parts.append(w[0] * o0[0, p] + w[1] * o1[0, p] + w[2] * o2[0, p])
    o = jnp.concatenate(parts, axis=1)
    y_ref[0] = x_ref[0] + g_ref[0] * _dot(o.astype(BF16), w_ref[...])


def _dil_res(outs, lses, w_out, x, gate, tm=512):
    b, s, d = x.shape
    k = w_out.shape[0]
    tm = min(tm, s)
    part = pl.BlockSpec((1, k // LANES, tm, LANES), lambda bi, i: (bi, 0, i, 0))
    lse = pl.BlockSpec((1, tm, LANES), lambda bi, i: (bi, i, 0))
    return pl.pallas_call(
        _dil_res_kernel,
        grid=(b, s // tm),
        in_specs=[part] * 3 + [lse] * 3 + [
            pl.BlockSpec((k, d), lambda bi, i: (0, 0)),
            pl.BlockSpec((1, tm, d), lambda bi, i: (bi, i, 0)),
            pl.BlockSpec((1, 1, d), lambda bi, i: (bi, 0, 0)),
        ],
        out_specs=pl.BlockSpec((1, tm, d), lambda bi, i: (bi, i, 0)),
        out_shape=jax.ShapeDtypeStruct((b, s, d), F32),
        compiler_params=_params("parallel", "parallel"),
        name="dil_res",
    )(*outs, *lses, w_out, x, gate)


def _router_kernel(x_ref, g_ref, sc_ref, sh_ref, wr_ref, br_ref, h_ref, route_ref, cnt_ref, run_ref, earlier_ref):
    tm = x_ref.shape[1]

    @pl.when(jnp.logical_and(pl.program_id(0) == 0, pl.program_id(1) == 0))
    def _():
        run_ref[...] = jnp.zeros_like(run_ref)
        row = lax.broadcasted_iota(jnp.int32, (tm, tm), 0)
        col = lax.broadcasted_iota(jnp.int32, (tm, tm), 1)
        earlier_ref[...] = jnp.where(col < row, 1.0, 0.0).astype(BF16)

    h = _mod_norm(x_ref[0], g_ref[...], sc_ref[0], sh_ref[0])
    _pack_slabs(h_ref, h)
    logits = _dot(h.astype(BF16), wr_ref[...].astype(BF16)) + br_ref[...]
    lane = lax.broadcasted_iota(jnp.int32, logits.shape, 1).astype(F32)
    neg = -jnp.inf
    far = float(LANES)

    def first_argmax(vals):
        top = jnp.max(vals, axis=1, keepdims=True)
        return top, jnp.min(jnp.where(vals == top, lane, far), axis=1, keepdims=True)

    grp_logits = jnp.where(lane < MOE_GROUPS, logits, neg)
    grp_max, grp = first_argmax(grp_logits)
    p_grp = 1.0 / jnp.sum(jnp.exp(grp_logits - grp_max), axis=1, keepdims=True)
    base = MOE_GROUPS + grp * MOE_PER_GROUP
    in_grp = jnp.logical_and(lane >= base, lane < base + MOE_PER_GROUP)
    exp_logits = jnp.where(in_grp, logits, neg)
    m1, i1 = first_argmax(exp_logits)
    m2, i2 = first_argmax(jnp.where(lane == i1, neg, exp_logits))
    e2 = jnp.exp(m2 - m1)
    g1 = p_grp / (1.0 + e2)
    g2 = g1 * e2
    e_a = i1 - MOE_GROUPS
    e_b = i2 - MOE_GROUPS

    pick_a = lane == e_a
    pick_b = lane == e_b
    both = jnp.where(jnp.logical_or(pick_a, pick_b), 1.0, 0.0)
    before = _dot(earlier_ref[...], both.astype(BF16)) + run_ref[...]
    rank_a = jnp.sum(jnp.where(pick_a, before, 0.0), axis=1, keepdims=True)
    rank_b = jnp.sum(jnp.where(pick_b, before, 0.0), axis=1, keepdims=True)
    run_ref[...] += jnp.sum(both, axis=0, keepdims=True)
    cnt_ref[...] = run_ref[...]

    route = jnp.where(lane == 0, e_a, jnp.where(lane == 1, e_b, 0.0))
    route = jnp.where(lane == 2, g1, jnp.where(lane == 3, g2, route))
    route_ref[...] = jnp.where(lane == 4, rank_a, jnp.where(lane == 5, rank_b, route))


def _router(x, g, sc, sh, w_route, b_route, tm=512):
    b, s, d = x.shape
    tm = min(tm, s)
    steps = s // tm
    return pl.pallas_call(
        _router_kernel,
        grid=(b, steps),
        in_specs=[
            pl.BlockSpec((1, tm, d), lambda bi, i: (bi, i, 0)),
            pl.BlockSpec((1, d), lambda bi, i: (0, 0)),
            pl.BlockSpec((1, 1, d), lambda bi, i: (bi, 0, 0)),
            pl.BlockSpec((1, 1, d), lambda bi, i: (bi, 0, 0)),
            pl.BlockSpec((d, LANES), lambda bi, i: (0, 0)),
            pl.BlockSpec((1, LANES), lambda bi, i: (0, 0)),
        ],
        out_specs=[
            pl.BlockSpec((tm * (d // PACKED_LANES), LANES), lambda bi, i: (bi * steps + i, 0)),
            pl.BlockSpec((tm, LANES), lambda bi, i: (bi * steps + i, 0)),
            pl.BlockSpec((1, LANES), lambda bi, i: (0, 0)),
        ],
        out_shape=[
            jax.ShapeDtypeStruct((b * s * (d // PACKED_LANES), LANES), jnp.int32),
            jax.ShapeDtypeStruct((b * s, LANES), F32),
            jax.ShapeDtypeStruct((1, LANES), F32),
        ],
        scratch_shapes=[pltpu.VMEM((1, LANES), F32), pltpu.VMEM((tm, tm), BF16)],
        compiler_params=_params("arbitrary", "arbitrary"),
        name="moe_router",
    )(x, g.reshape(1, d), sc, sh, w_route, b_route)


def _sc_dispatch(h_slabs, dest, n_slots):
    t, chunks, _ = h_slabs.shape
    workers = SC_CORES * SC_SUBCORES
    per_worker = t // workers
    steps = per_worker // SC_WINDOW
    assert t % (workers * SC_WINDOW * 2) == 0
    mesh = plsc.VectorSubcoreMesh(core_axis_name="core", subcore_axis_name="subcore")

    @functools.partial(
        pl.kernel, mesh=mesh,
        out_type=jax.ShapeDtypeStruct((n_slots, chunks, LANES), h_slabs.dtype),
        scratch_types=[
            pltpu.VMEM((2, MOE_TOPK, SC_WINDOW), jnp.int32),
            pltpu.VMEM((2, SC_WINDOW, chunks, LANES), h_slabs.dtype),
            pltpu.SemaphoreType.DMA((2,)),
            pltpu.SemaphoreType.DMA((2,)),
        ])
    def dispatch(h_hbm, dest_hbm, slots_hbm, idx_v, rows_v, load_sem, store_sem):
        worker = lax.axis_index("subcore") * SC_CORES + lax.axis_index("core")
        base = worker * per_worker

        @pl.loop(0, steps, step=2)
        def _(step):
            for buf in range(2):
                off = base + (step + buf) * SC_WINDOW
                pltpu.async_copy(h_hbm.at[pl.ds(off, SC_WINDOW)], rows_v.at[buf], load_sem.at[buf])
                for k in range(MOE_TOPK):
                    pltpu.sync_copy(dest_hbm.at[k, pl.ds(off, SC_WINDOW)], idx_v.at[buf, k])
            for buf in range(2):
                off = base + (step + buf) * SC_WINDOW
                pltpu.make_async_copy(h_hbm.at[pl.ds(off, SC_WINDOW)], rows_v.at[buf], load_sem.at[buf]).wait()
                copies = [pltpu.async_copy(rows_v.at[buf], slots_hbm.at[idx_v.at[buf, k]], store_sem.at[buf])
                          for k in range(MOE_TOPK)]
                for cp in copies:
                    cp.wait()

    return dispatch(h_slabs, dest)


def _zero_tail_kernel(tail_ref, valid_ref, s_in, s_out):
    del tail_ref
    rows = lax.broadcasted_iota(jnp.int32, s_in.shape, 0)
    s_out[...] = jnp.where(rows < valid_ref[pl.program_id(0)], s_in[...], 0)


def _zero_tails(tail_blk, valid_rows, slots, block_rows):
    grid_spec = pltpu.PrefetchScalarGridSpec(
        num_scalar_prefetch=2,
        grid=(tail_blk.shape[0],),
        in_specs=[pl.BlockSpec((block_rows, LANES), lambda e, tail, valid: (tail[e], 0))],
        out_specs=pl.BlockSpec((block_rows, LANES), lambda e, tail, valid: (tail[e], 0)),
    )
    return pl.pallas_call(
        _zero_tail_kernel,
        grid_spec=grid_spec,
        out_shape=jax.ShapeDtypeStruct(slots.shape, slots.dtype),
        input_output_aliases={2: 0},
        compiler_params=_params("arbitrary"),
        name="moe_zero_tails",
    )(tail_blk, valid_rows, slots)


def _expert_kernel(blk_e_ref, used_ref, x_ref, wg_ref, wu_ref, wd_ref, y_ref, wg_bf, wu_bf, wd_bf):
    i = pl.program_id(0)
    live = i < used_ref[0]
    changed = jnp.logical_or(i == 0, blk_e_ref[i] != blk_e_ref[jnp.maximum(i - 1, 0)])

    @pl.when(jnp.logical_and(live, changed))
    def _():
        wg_bf[...] = wg_ref[0, 0].astype(BF16)
        wu_bf[...] = wu_ref[0, 0].astype(BF16)
        wd_bf[...] = wd_ref[0, 0].astype(BF16)

    @pl.when(live)
    def _():
        bm = x_ref.shape[0] // (wg_bf.shape[0] // PACKED_LANES)
        x = _unpack_slabs(x_ref, bm).astype(BF16)
        gate = _dot(x, wg_bf[...])
        up = _dot(x, wu_bf[...])
        act = (gate * _sigmoid(gate) * up).astype(BF16)
        _pack_slabs(y_ref, _dot(act, wd_bf[...]))

    @pl.when(jnp.logical_not(live))
    def _():
        y_ref[...] = jnp.zeros_like(y_ref)


def _expert_blocks(blk_e, used, slots, w_gate, w_up, w_down, layer, bm):
    nb = blk_e.shape[0]
    _, _, d, f = w_gate.shape
    chunks = d // PACKED_LANES

    def block(i, e, n):
        return (jnp.minimum(i, n[0] - 1), 0)

    def weight(i, e, n):
        return (layer, e[jnp.minimum(i, n[0] - 1)], 0, 0)

    grid_spec = pltpu.PrefetchScalarGridSpec(
        num_scalar_prefetch=2,
        grid=(nb,),
        in_specs=[
            pl.BlockSpec((bm * chunks, LANES), block),
            pl.BlockSpec((1, 1, d, f), weight),
            pl.BlockSpec((1, 1, d, f), weight),
            pl.BlockSpec((1, 1, f, d), weight),
        ],
        out_specs=pl.BlockSpec((bm * chunks, LANES), lambda i, e, n: (i, 0)),
        scratch_shapes=[pltpu.VMEM((d, f), BF16), pltpu.VMEM((d, f), BF16), pltpu.VMEM((f, d), BF16)],
    )
    return pl.pallas_call(
        _expert_kernel,
        grid_spec=grid_spec,
        out_shape=jax.ShapeDtypeStruct((nb * bm * chunks, LANES), jnp.int32),
        compiler_params=_params("arbitrary"),
        name="moe_experts",
    )(blk_e, used, slots, w_gate, w_up, w_down)


def _sc_gather(y_slabs, idx):
    n = idx.shape[0]
    _, chunks, _ = y_slabs.shape
    workers = SC_CORES * SC_SUBCORES
    per_worker = n // workers
    steps = per_worker // SC_WINDOW
    assert n % (workers * SC_WINDOW * 2) == 0
    mesh = plsc.VectorSubcoreMesh(core_axis_name="core", subcore_axis_name="subcore")

    @functools.partial(
        pl.kernel, mesh=mesh,
        out_type=jax.ShapeDtypeStruct((n, chunks, LANES), y_slabs.dtype),
        scratch_types=[
            pltpu.VMEM((2, SC_WINDOW), jnp.int32),
            pltpu.VMEM((2, SC_WINDOW, chunks, LANES), y_slabs.dtype),
            pltpu.SemaphoreType.DMA((2,)),
        ])
    def gather(y_hbm, idx_hbm, out_hbm, idx_v, rows_v, sem):
        worker = lax.axis_index("subcore") * SC_CORES + lax.axis_index("core")
        base = worker * per_worker

        @pl.loop(0, steps, step=2)
        def _(step):
            for buf in range(2):
                off = base + (step + buf) * SC_WINDOW
                pltpu.sync_copy(idx_hbm.at[pl.ds(off, SC_WINDOW)], idx_v.at[buf])
                pltpu.async_copy(y_hbm.at[idx_v.at[buf]], rows_v.at[buf], sem.at[buf])
            for buf in range(2):
                off = base + (step + buf) * SC_WINDOW
                pltpu.make_async_copy(y_hbm.at[idx_v.at[buf]], rows_v.at[buf], sem.at[buf]).wait()
                pltpu.sync_copy(rows_v.at[buf], out_hbm.at[pl.ds(off, SC_WINDOW)])

    return gather(y_slabs, idx)


def _combine_kernel(g0_ref, g1_ref, route_ref, x_ref, gate_ref, fg_ref, o_ref, *, final):
    tm = x_ref.shape[0]
    route = route_ref[...]
    y = route[:, 2:3] * _unpack_slabs(g0_ref, tm) + route[:, 3:4] * _unpack_slabs(g1_ref, tm)
    out = x_ref[...] + gate_ref[0] * y
    if final:
        out = out * lax.rsqrt(jnp.mean(out * out, axis=-1, keepdims=True) + RMS_EPS) * fg_ref[...]
    o_ref[...] = out


def _combine(dest, route, y_slabs, x, gate, final_g, final, tm, pieces=8):
    b, s, d = x.shape
    t = b * s
    chunks = d // PACKED_LANES
    per_seq = s // tm
    piece = t // pieces
    tiles = piece // tm
    out = x.reshape(t, d)
    fg = final_g.reshape(1, d)
    for c in range(pieces):
        idx = dest[:, c * piece:(c + 1) * piece].reshape(MOE_TOPK * piece)
        g = _sc_gather(y_slabs, idx).reshape(MOE_TOPK * piece * chunks, LANES)
        first = c * tiles
        out = pl.pallas_call(
            functools.partial(_combine_kernel, final=final),
            grid=(tiles,),
            in_specs=[
                pl.BlockSpec((tm * chunks, LANES), lambda i: (i, 0)),
                pl.BlockSpec((tm * chunks, LANES), lambda i: (tiles + i, 0)),
                pl.BlockSpec((tm, LANES), lambda i, first=first: (first + i, 0)),
                pl.BlockSpec((tm, d), lambda i, first=first: (first + i, 0)),
                pl.BlockSpec((1, 1, d), lambda i, first=first: ((first + i) // per_seq, 0, 0)),
                pl.BlockSpec((1, d), lambda i: (0, 0)),
            ],
            out_specs=pl.BlockSpec((tm, d), lambda i, first=first: (first + i, 0)),
            out_shape=jax.ShapeDtypeStruct((t, d), F32),
            input_output_aliases={3: 0},
            compiler_params=_params("parallel"),
            name="moe_combine",
        )(g, g, route, out, gate, fg)
    return out.reshape(b, s, d)


def _slot_tables(route, counts, bm):
    t = route.shape[0]
    nb = MOE_TOPK * t // bm + MOE_EXPERTS
    counts = counts[0, :MOE_EXPERTS].astype(jnp.int32)
    padded = (counts + bm - 1) // bm * bm
    pad_end = jnp.cumsum(padded)
    pad_start = pad_end - padded
    expert = route[:, 0:MOE_TOPK].astype(jnp.int32)
    rank = route[:, 4:4 + MOE_TOPK].astype(jnp.int32)
    dest = pad_start[expert] + rank
    blk_start = jnp.arange(nb, dtype=jnp.int32) * bm
    blk_e = jnp.minimum(jnp.sum((pad_end[None, :] <= blk_start[:, None]).astype(jnp.int32), axis=1), MOE_EXPERTS - 1)
    used = (pad_end[-1:] // bm).astype(jnp.int32)
    tail_blk = jnp.maximum(pad_end // bm - 1, 0).astype(jnp.int32)
    tail_fill = jnp.where(counts % bm != 0, counts % bm, bm).astype(jnp.int32)
    return dest.T, blk_e, used, tail_blk, tail_fill, nb


def _moe_layer(x, g, sc, sh, gate, w_route, b_route, w_gate, w_up, w_down, layer, final_g, final, bm=512, tm=1024):
    b, s, d = x.shape
    t = b * s
    chunks = d // PACKED_LANES
    tm = min(tm, s)
    h_rows, route, counts = _router(x, g, sc, sh, w_route, b_route)
    dest, blk_e, used, tail_blk, tail_fill, nb = _slot_tables(route, counts, bm)
    slots = _sc_dispatch(h_rows.reshape(t, chunks, LANES), dest, nb * bm)
    slots = _zero_tails(tail_blk, tail_fill * chunks, slots.reshape(nb * bm * chunks, LANES), bm * chunks)
    y_slabs = _expert_blocks(blk_e, used, slots, w_gate, w_up, w_down, layer, bm).reshape(nb * bm, chunks, LANES)
    return _combine(dest, route, y_slabs, x, gate, final_g, final, tm)


def kernel(x, c, ada_w, ada_b, norm_g, final_g, sb_w_in, sb_w_out, gla_w_in, gla_w_gate_up, gla_b_gate, gla_norm_g, gla_w_out, conv_w_in, conv_w, conv_b, conv_w_out, dil_w_in, dil_w_out, moe_w_grp, moe_b_grp, moe_w_exp, moe_b_exp, moe_w_gate, moe_w_up, moe_w_down):
    b, s, d = x.shape
    depth = ada_w.shape[0]
    mod = _ada_mod(c, ada_w, ada_b).reshape(depth, b, N_MOD, 1, d)
    for i in range(depth):
        sh1, sc1, g1, sh2, sc2, g2 = (mod[i, :, m] for m in range(N_MOD))
        kind, j = i % 4, i // 4
        if kind == 0:
            qkv = _norm_proj(x, norm_g[i, 0], sc1, sh1, sb_w_in[j].astype(BF16))
            x = _proj_res(_sb_attention(qkv), sb_w_out[j].astype(BF16), x, g1)
        elif kind == 1:
            dk = gla_w_gate_up.shape[-1] // GLA_HEADS
            dv = gla_norm_g.shape[-1]
            n_in = gla_w_in.shape[-1]
            w_in = jnp.pad(gla_w_in[j], ((0, 0), (0, -n_in % MXU_COLUMNS))).astype(BF16)
            w_up = jnp.pad(gla_w_gate_up[j], ((0, LANES - GLA_GATE_RANK), (0, 0))).astype(BF16)
            proj = _norm_proj(x, norm_g[i, 0], sc1, sh1, w_in, tn=MXU_COLUMNS)
            o = _gla_core(proj, w_up, gla_b_gate[j].reshape(1, -1), gla_norm_g[j].reshape(1, -1), dk, dv)
            x = _proj_res(o, gla_w_out[j].astype(BF16), x, g1)
        elif kind == 2:
            proj = _norm_proj(x, norm_g[i, 0], sc1, sh1, conv_w_in[j].astype(BF16))
            x = _conv_res(proj, conv_w[j], conv_b[j], conv_w_out[j].astype(BF16), x, g1)
        else:
            group_w = 3 * DIL_HEADS * DIL_HEAD_DIM
            outs, lses = [], []
            for g, (window, dilation) in enumerate(DIL_PATTERNS):
                w_g = dil_w_in[j][:, g * group_w:(g + 1) * group_w].astype(BF16)
                view = _norm_proj(x, norm_g[i, 0], sc1, sh1, w_g, dilation=dilation)
                o_g, l_g = _dil_group(view, dilation, window, s)
                outs.append(o_g)
                lses.append(l_g)
            x = _dil_res(outs, lses, dil_w_out[j].astype(BF16), x, g1)
        w_route = jnp.pad(jnp.concatenate([moe_w_grp[i], moe_w_exp[i]], axis=1),
                          ((0, 0), (0, LANES - MOE_GROUPS - MOE_EXPERTS)))
        b_route = jnp.pad(jnp.concatenate([moe_b_grp[i], moe_b_exp[i]]), (0, LANES - MOE_GROUPS - MOE_EXPERTS))
        x = _moe_layer(x, norm_g[i, 1], sc2, sh2, g2, w_route, b_route.reshape(1, LANES),
                       moe_w_gate, moe_w_up, moe_w_down, i, final_g, final=(i == depth - 1))
    return x
```

```python
import functools

import jax
import jax.numpy as jnp
from jax import lax
from jax.experimental import pallas as pl
from jax.experimental.pallas import tpu as pltpu
from jax.experimental.pallas import tpu_sc as plsc

F32 = jnp.float32
BF16 = jnp.bfloat16

LANES = 128
MXU_COLUMNS = 256
RMS_EPS = 1e-6
N_MOD = 6

SB_HEADS = 16
GLA_HEADS = 4
GLA_GATE_RANK = 16
GLA_GATE_TAU = 16.0
GLA_CHUNK = 64
CONV_WIDTH = 3
DIL_PATTERNS = ((128, 1), (512, 4), (2048, 16))
DIL_HEADS = 8
DIL_HEAD_DIM = 64
DIL_QBLOCK = 128
DIL_TILE_TOKENS = 2048
MOE_GROUPS = 4
MOE_PER_GROUP = 8
MOE_EXPERTS = MOE_GROUPS * MOE_PER_GROUP
MOE_TOPK = 2

SC_CORES = 2
SC_SUBCORES = 16
SC_WINDOW = 64
PACKED_LANES = 2 * LANES

SB_LOG_ZERO = -104.0
SB_BAND = 2

VMEM_LIMIT = 48 * 1024 * 1024


def _params(*sem):
    return pltpu.CompilerParams(dimension_semantics=sem, vmem_limit_bytes=VMEM_LIMIT)


def _sigmoid(x):
    return 1.0 / (1.0 + jnp.exp(-x))


def _softplus(x):
    return jnp.maximum(x, 0.0) + jnp.log(1.0 + jnp.exp(-jnp.abs(x)))


def _split_bf16(x):
    hi = x.astype(BF16)
    lo = (x - hi.astype(F32)).astype(BF16)
    return hi, lo


def _dot(a, b):
    return jnp.dot(a, b, preferred_element_type=F32)


def _dot_nt(a, b):
    return lax.dot_general(a, b, (((1,), (1,)), ((), ())), preferred_element_type=F32)


def _dot_tn(a, b):
    return lax.dot_general(a, b, (((0,), (0,)), ((), ())), preferred_element_type=F32)


def _unpack_slabs(ref, rows):
    k = ref.shape[0] // rows
    words = jnp.concatenate([ref[pl.ds(s, rows, stride=k), :] for s in range(k)], axis=1)
    halves = [pltpu.unpack_elementwise(words, index=i, packed_dtype=BF16, unpacked_dtype=F32) for i in range(2)]
    return jnp.concatenate(halves, axis=1)


def _pack_slabs(ref, values):
    rows, d = values.shape
    half = d // 2
    words = pltpu.pack_elementwise([values[:, :half], values[:, half:]], packed_dtype=BF16)
    for s in range(half // LANES):
        ref[pl.ds(s, rows, stride=half // LANES), :] = words[:, s * LANES:(s + 1) * LANES]


def _mod_norm(x, g, sc, sh):
    r = lax.rsqrt(jnp.mean(x * x, axis=-1, keepdims=True) + RMS_EPS)
    return (x * r) * g * (1.0 + sc) + sh


def _ada_kernel(c_ref, w_ref, b_ref, o_ref):
    c = c_ref[...]
    cond = c * _sigmoid(c)
    o_ref[0] = _dot(cond.astype(BF16), w_ref[0].astype(BF16)) + b_ref[0]


def _ada_mod(c, ada_w, ada_b):
    depth, d, n = ada_w.shape
    b = c.shape[0]
    tn = 1536
    return pl.pallas_call(
        _ada_kernel,
        grid=(depth, n // tn),
        in_specs=[
            pl.BlockSpec((b, d), lambda i, j: (0, 0)),
            pl.BlockSpec((1, d, tn), lambda i, j: (i, 0, j)),
            pl.BlockSpec((1, 1, tn), lambda i, j: (i, 0, j)),
        ],
        out_specs=pl.BlockSpec((1, b, tn), lambda i, j: (i, 0, j)),
        out_shape=jax.ShapeDtypeStruct((depth, b, n), F32),
        compiler_params=_params("parallel", "parallel"),
        name="ada_mod",
    )(c, ada_w, ada_b.reshape(depth, 1, n))


def _norm_proj_kernel(x_ref, g_ref, sc_ref, sh_ref, w_ref, o_ref, h_ref, *res_ref, dilation, tn):
    h_ref[...] = _mod_norm(x_ref[0], g_ref[...], sc_ref[0], sh_ref[0]).astype(BF16)
    tm = h_ref.shape[0]
    for j in range(w_ref.shape[1] // tn):
        res = _dot(h_ref[...], w_ref[:, j * tn:(j + 1) * tn])
        if dilation == 1:
            o_ref[0, :, j * tn:(j + 1) * tn] = res.astype(o_ref.dtype)
            continue
        for c in range(tn // LANES):
            res_ref[0][j * (tn // LANES) + c] = res[:, c * LANES:(c + 1) * LANES]
        for rho in range(dilation):
            for c in range(tn // LANES):
                rows = res_ref[0][j * (tn // LANES) + c, pl.ds(rho, tm // dilation, stride=dilation), :]
                col = (j * dilation + rho) * tn + c * LANES
                o_ref[0, :, col:col + LANES] = rows.astype(o_ref.dtype)


def _norm_proj(x, g, sc, sh, w, tm=512, tn=512, dilation=1):
    b, s, d = x.shape
    n = w.shape[1]
    tm = min(tm, s)
    assert s % tm == 0 and n % tn == 0 and tm % (16 * dilation) == 0
    scratch = [pltpu.VMEM((tm, d), BF16)]
    if dilation > 1:
        scratch.append(pltpu.VMEM((n // LANES, tm, LANES), F32))
    return pl.pallas_call(
        functools.partial(_norm_proj_kernel, dilation=dilation, tn=tn),
        grid=(b, s // tm),
        in_specs=[
            pl.BlockSpec((1, tm, d), lambda bi, i: (bi, i, 0)),
            pl.BlockSpec((1, d), lambda bi, i: (0, 0)),
            pl.BlockSpec((1, 1, d), lambda bi, i: (bi, 0, 0)),
            pl.BlockSpec((1, 1, d), lambda bi, i: (bi, 0, 0)),
            pl.BlockSpec((d, n), lambda bi, i: (0, 0)),
        ],
        out_specs=pl.BlockSpec((1, tm // dilation, dilation * n), lambda bi, i: (bi, i, 0)),
        out_shape=jax.ShapeDtypeStruct((b, s // dilation, dilation * n), BF16),
        scratch_shapes=scratch,
        compiler_params=_params("parallel", "parallel"),
        name="norm_proj",
    )(x, g.reshape(1, d), sc, sh, w)


def _proj_res_kernel(o_ref, w_ref, x_ref, g_ref, y_ref):
    y_ref[0] = x_ref[0] + g_ref[0] * _dot(o_ref[0], w_ref[...])


def _proj_res(o, w, x, gate, tm=512):
    b, s, d = x.shape
    k = o.shape[-1]
    tm = min(tm, s)
    return pl.pallas_call(
        _proj_res_kernel,
        grid=(b, s // tm),
        in_specs=[
            pl.BlockSpec((1, tm, k), lambda bi, i: (bi, i, 0)),
            pl.BlockSpec((k, d), lambda bi, i: (0, 0)),
            pl.BlockSpec((1, tm, d), lambda bi, i: (bi, i, 0)),
            pl.BlockSpec((1, 1, d), lambda bi, i: (bi, 0, 0)),
        ],
        out_specs=pl.BlockSpec((1, tm, d), lambda bi, i: (bi, i, 0)),
        out_shape=jax.ShapeDtypeStruct((b, s, d), F32),
        compiler_params=_params("parallel", "parallel"),
        name="proj_res",
    )(o, w, x, gate)


def _sb_kernel(q_ref, k_ref, v_ref, o_ref, acc_ref, c_ref, *, blk, hd):
    scale = hd ** -0.5
    fold_scale = (hd & (hd - 1)) == 0 and (hd.bit_length() - 1) % 2 == 0
    first = lax.broadcasted_iota(jnp.int32, (blk, LANES), 1) < hd
    row = lax.broadcasted_iota(jnp.int32, (blk, blk), 0)
    col = lax.broadcasted_iota(jnp.int32, (blk, blk), 1)
    minus_later = jnp.where(row > col, -1.0, 0.0).astype(BF16)
    causal = col < row
    subs = q_ref.shape[1] // blk
    first_block = pl.program_id(2) * subs

    def visit_many(work):
        kv = []
        for _, block, _, _ in work:
            start = pl.multiple_of(block * blk, blk)
            kv.append((k_ref[0, pl.ds(start, blk), :], v_ref[0, pl.ds(start, blk), :]))
        zs = [[_dot_nt(q_heads[h], kb) for h in range(2)] for (q_heads, _, _, _), (kb, _) in zip(work, kv)]
        log_betas, sps = [], []
        for (_, _, mask, _), z2 in zip(work, zs):
            lb2, sp2 = [], []
            for z in z2:
                if not fold_scale:
                    z = z * scale
                sp = _softplus(z)
                lb2.append(z - sp)
                sp2.append(sp if mask is None else jnp.where(mask, sp, 0.0))
            log_betas.append(lb2)
            sps.append(sp2)
        suffixes = [[_dot(sp.astype(BF16), minus_later) for sp in sp2] for sp2 in sps]
        return kv, log_betas, sps, suffixes

    def finish(work, staged, c_in):
        kv, log_betas, sps, suffixes = staged
        cs = {chain: list(c) for chain, c in c_in.items()}
        weights = []
        for n, (_, _, mask, chain) in enumerate(work):
            a2 = []
            for h in range(2):
                a = jnp.exp(log_betas[n][h] + suffixes[n][h] + cs[chain][h])
                a2.append(a if mask is None else jnp.where(mask, a, 0.0))
                cs[chain][h] = cs[chain][h] - jnp.sum(sps[n][h], axis=1, keepdims=True)
            weights.append(a2)
        accs = {}
        for n, (_, _, _, chain) in enumerate(work):
            outs = [_dot(weights[n][h].astype(BF16), kv[n][1]) for h in range(2)]
            more = jnp.where(first, outs[0], outs[1])
            accs[chain] = more if chain not in accs else accs[chain] + more
        return accs, cs

    def remainder(sub, q_heads, depth):
        def cond(carry):
            j, cmax = carry
            return jnp.logical_and(j >= 0, cmax > SB_LOG_ZERO)

        def body(carry):
            j, _ = carry
            work = [(q_heads, j, None, sub)]
            accs, cs = finish(work, visit_many(work), {sub: [c_ref[sub, 0], c_ref[sub, 1]]})
            acc_ref[sub] += accs[sub]
            for h in range(2):
                c_ref[sub, h] = cs[sub][h]
            return j - 1, jnp.max(c_ref[sub])

        lax.while_loop(cond, body, (first_block + sub - depth - 1, jnp.max(c_ref[sub])))

    def run(depths):
        heads = []
        for sub in range(subs):
            q = q_ref[0, sub * blk:(sub + 1) * blk, :]
            if fold_scale:
                q = (q.astype(F32) * scale).astype(BF16)
            zero = jnp.zeros_like(q)
            heads.append((jnp.where(first, q, zero), jnp.where(first, zero, q)))
        work = [(heads[sub], first_block + sub - back, causal if back == 0 else None, sub)
                for sub in range(subs) for back in range(depths[sub] + 1)]
        zero_c = [jnp.zeros((blk, 1), F32)] * 2
        accs, cs = finish(work, visit_many(work), {sub: zero_c for sub in range(subs)})
        for sub in range(subs):
            acc_ref[sub] = accs[sub]
            for h in range(2):
                c_ref[sub, h] = cs[sub][h]
        @pl.when(jnp.max(c_ref[...]) > SB_LOG_ZERO)
        def _():
            for sub in range(subs):
                remainder(sub, heads[sub], depths[sub])

        for sub in range(subs):
            o_ref[0, sub * blk:(sub + 1) * blk, :] = acc_ref[sub].astype(o_ref.dtype)

    assert subs >= SB_BAND
    pl.when(first_block > 0)(functools.partial(run, [SB_BAND] * subs))
    pl.when(first_block == 0)(functools.partial(run, [min(SB_BAND, sub) for sub in range(subs)]))


def _sb_attention(qkv, blk=128, rows=1024):
    b, s, n = qkv.shape
    width = n // 3
    hd = width // SB_HEADS
    assert 2 * hd == LANES
    pairs = width // LANES
    rows = min(rows, s)
    return pl.pallas_call(
        functools.partial(_sb_kernel, blk=blk, hd=hd),
        grid=(b, pairs, s // rows),
        in_specs=[
            pl.BlockSpec((1, rows, LANES), lambda bi, p, i: (bi, i, p)),
            pl.BlockSpec((1, s, LANES), lambda bi, p, i: (bi, 0, pairs + p)),
            pl.BlockSpec((1, s, LANES), lambda bi, p, i: (bi, 0, 2 * pairs + p)),
        ],
        out_specs=pl.BlockSpec((1, rows, LANES), lambda bi, p, i: (bi, i, p)),
        out_shape=jax.ShapeDtypeStruct((b, s, width), BF16),
        scratch_shapes=[pltpu.VMEM((rows // blk, blk, LANES), F32), pltpu.VMEM((rows // blk, 2, blk, 1), F32)],
        compiler_params=_params("parallel", "parallel", "arbitrary"),
        name="sb_attention",
    )(qkv, qkv, qkv)


def _gla_kernel(p_ref, wg_ref, bg_ref, ng_ref, o_ref, st_ref, *, rows, dk, dv):
    @pl.when(pl.program_id(1) == 0)
    def _():
        st_ref[...] = jnp.zeros_like(st_ref)

    heads = GLA_HEADS
    ch = GLA_CHUNK
    k0 = heads * dk
    v0 = 2 * heads * dk
    r0 = v0 + heads * dv
    g0 = r0 + heads * dv
    row = lax.broadcasted_iota(jnp.int32, (ch, ch), 0)
    col = lax.broadcasted_iota(jnp.int32, (ch, ch), 1)
    causal = col <= row
    lower = jnp.where(causal, 1.0, 0.0).astype(BF16)
    chunks = [slice(s * ch, (s + 1) * ch) for s in range(rows // ch)]
    g_pres = [_dot(p_ref[0, t, g0:g0 + LANES], wg_ref[...]) + bg_ref[...] for t in chunks]
    cums = []
    for g_pre in g_pres:
        log_a = (jnp.minimum(g_pre, 0.0) - jnp.log(1.0 + jnp.exp(-jnp.abs(g_pre)))) * (1.0 / GLA_GATE_TAU)
        hi, lo = _split_bf16(log_a)
        cums.append(_dot(lower, hi) + _dot(lower, lo))
    parts = []
    for t, cum in zip(chunks, cums):
        for h in range(heads):
            bh = cum[:, h * dk:(h + 1) * dk]
            b_last = bh[ch - 1:ch, :]
            q = p_ref[0, t, h * dk:(h + 1) * dk].astype(F32) * dk ** -0.5
            k = p_ref[0, t, k0 + h * dk:k0 + (h + 1) * dk].astype(F32)
            q_in = (q * jnp.exp(bh)).astype(BF16)
            k_in = (k * jnp.exp(-bh)).astype(BF16)
            k_state = (k * jnp.exp(b_last - bh)).astype(BF16)
            parts.append((t, h, q_in, k_in, k_state, jnp.exp(b_last)))
    scores = [jnp.where(causal, _dot_nt(q_in, k_in), 0.0).astype(BF16) for _, _, q_in, k_in, _, _ in parts]
    values = [p_ref[0, t, v0 + h * dv:v0 + (h + 1) * dv] for t, h, _, _, _, _ in parts]
    intra = [_dot(sc, v) for sc, v in zip(scores, values)]
    updates = [_dot_tn(v, k_state) for v, (_, _, _, _, k_state, _) in zip(values, parts)]
    states = [st_ref[h] for h in range(heads)]
    for n, (t, h, q_in, _, _, decay) in enumerate(parts):
        o = intra[n] + _dot_nt(q_in, states[h].astype(BF16))
        states[h] = decay * states[h] + updates[n]
        o = o * lax.rsqrt(jnp.mean(o * o, axis=-1, keepdims=True) + RMS_EPS) * ng_ref[...]
        r = p_ref[0, t, r0 + h * dv:r0 + (h + 1) * dv].astype(F32)
        o_ref[0, t, h * dv:(h + 1) * dv] = (o * (r * _sigmoid(r))).astype(o_ref.dtype)
    for h in range(heads):
        st_ref[h] = states[h]


def _gla_core(proj, w_gate_up, b_gate, norm_g, dk, dv, rows=256):
    b, s, n = proj.shape
    rows = min(rows, s)
    hk = GLA_HEADS * dk
    return pl.pallas_call(
        functools.partial(_gla_kernel, rows=rows, dk=dk, dv=dv),
        grid=(b, s // rows),
        in_specs=[
            pl.BlockSpec((1, rows, n), lambda bi, i: (bi, i, 0)),
            pl.BlockSpec((LANES, hk), lambda bi, i: (0, 0)),
            pl.BlockSpec((1, hk), lambda bi, i: (0, 0)),
            pl.BlockSpec((1, dv), lambda bi, i: (0, 0)),
        ],
        out_specs=pl.BlockSpec((1, rows, GLA_HEADS * dv), lambda bi, i: (bi, i, 0)),
        out_shape=jax.ShapeDtypeStruct((b, s, GLA_HEADS * dv), BF16),
        scratch_shapes=[pltpu.VMEM((GLA_HEADS, dv, dk), F32)],
        compiler_params=_params("parallel", "arbitrary"),
        name="gla_core",
    )(proj, w_gate_up, b_gate, norm_g)


def _conv_res_kernel(gb_ref, gc_ref, u_ref, hgc_ref, hu_ref, cw_ref, cb_ref, w_ref, x_ref, g_ref, y_ref):
    u2 = gc_ref[0].astype(F32) * u_ref[0].astype(F32)
    halo = hgc_ref[0].astype(F32) * hu_ref[0].astype(F32)
    halo = jnp.where(pl.program_id(1) > 0, halo, 0.0)
    rows = lax.broadcasted_iota(jnp.int32, u2.shape, 0)
    prev1 = jnp.where(rows == 0, halo[7:8], pltpu.roll(u2, 1, 0))
    prev2 = jnp.where(rows == 0, halo[6:7], jnp.where(rows == 1, halo[7:8], pltpu.roll(u2, 2, 0)))
    cw = cw_ref[...]
    y = cw[0:1] * prev2 + cw[1:2] * prev1 + cw[2:3] * u2 + cb_ref[...]
    o = (gb_ref[0].astype(F32) * y).astype(BF16)
    y_ref[0] = x_ref[0] + g_ref[0] * _dot(o, w_ref[...])


def _conv_res(proj, conv_w, conv_b, w_out, x, gate, tm=512):
    b, s, d = x.shape
    tm = min(tm, s)
    halo_blocks = tm // 8

    def halo_map(col):
        return lambda bi, i: (bi, jnp.maximum(i * halo_blocks - 1, 0), col)

    return pl.pallas_call(
        _conv_res_kernel,
        grid=(b, s // tm),
        in_specs=[
            pl.BlockSpec((1, tm, d), lambda bi, i: (bi, i, 0)),
            pl.BlockSpec((1, tm, d), lambda bi, i: (bi, i, 1)),
            pl.BlockSpec((1, tm, d), lambda bi, i: (bi, i, 2)),
            pl.BlockSpec((1, 8, d), halo_map(1)),
            pl.BlockSpec((1, 8, d), halo_map(2)),
            pl.BlockSpec((CONV_WIDTH, d), lambda bi, i: (0, 0)),
            pl.BlockSpec((1, d), lambda bi, i: (0, 0)),
            pl.BlockSpec((d, d), lambda bi, i: (0, 0)),
            pl.BlockSpec((1, tm, d), lambda bi, i: (bi, i, 0)),
            pl.BlockSpec((1, 1, d), lambda bi, i: (bi, 0, 0)),
        ],
        out_specs=pl.BlockSpec((1, tm, d), lambda bi, i: (bi, i, 0)),
        out_shape=jax.ShapeDtypeStruct((b, s, d), F32),
        compiler_params=_params("parallel", "parallel"),
        name="conv_res",
    )(proj, proj, proj, proj, proj, conv_w, conv_b.reshape(1, d), w_out, x, gate)


def _dil_kernel(q_ref, kc_ref, vc_ref, kp_ref, vp_ref, o_ref, l_ref, *, rows, hd, dilation, group):
    qb = DIL_QBLOCK
    n = pl.program_id(1)
    width = q_ref.shape[-1] // group
    scale = hd ** -0.5
    fold_scale = (hd & (hd - 1)) == 0 and (hd.bit_length() - 1) % 2 == 0
    first = lax.broadcasted_iota(jnp.int32, (qb, LANES), 1) < hd
    row = lax.broadcasted_iota(jnp.int32, (qb, qb), 0)
    col = lax.broadcasted_iota(jnp.int32, (qb, qb), 1)
    in_prev = col >= row
    in_cur = col <= row
    neg = -jnp.inf

    tiles = []
    for u, s in [(u, s) for u in range(group) for s in range(rows // qb)]:
        t = slice(s * qb, (s + 1) * qb)
        tp = slice((s - 1) * qb, s * qb)
        prev_mask = jnp.logical_and(in_prev, n > 0) if s == 0 else in_prev
        for p in range(width // LANES):
            c = slice(u * width + p * LANES, u * width + (p + 1) * LANES)
            q = q_ref[0, t, c]
            if fold_scale:
                q = (q.astype(F32) * scale).astype(BF16)
            zero = jnp.zeros_like(q)
            q_heads = (jnp.where(first, q, zero), jnp.where(first, zero, q))
            if s == 0:
                k_prev, v_prev = kp_ref[0, :, c], vp_ref[0, :, c]
            else:
                k_prev, v_prev = kc_ref[0, tp, c], vc_ref[0, tp, c]
            tiles.append(((u, s), p, q_heads, k_prev, kc_ref[0, t, c], v_prev, vc_ref[0, t, c], prev_mask))

    scores = []
    for _, _, q_heads, k_prev, k_cur, _, _, prev_mask in tiles:
        pair = []
        for h in range(2):
            z_prev, z_cur = _dot_nt(q_heads[h], k_prev), _dot_nt(q_heads[h], k_cur)
            if not fold_scale:
                z_prev, z_cur = z_prev * scale, z_cur * scale
            pair.append((jnp.where(prev_mask, z_prev, neg), jnp.where(in_cur, z_cur, neg)))
        scores.append(pair)

    probs = []
    for pair in scores:
        stats = []
        for z_prev, z_cur in pair:
            m = jnp.max(jnp.maximum(z_prev, z_cur), axis=1, keepdims=True)
            p_prev = jnp.exp(z_prev - m)
            p_cur = jnp.exp(z_cur - m)
            den = jnp.sum(p_prev + p_cur, axis=1, keepdims=True)
            stats.append((p_prev.astype(BF16), p_cur.astype(BF16), den, m))
        probs.append(stats)

    lane = lax.broadcasted_iota(jnp.int32, (qb, LANES), 1)
    lse_rows = {}
    for ((u, s), p, _, _, _, v_prev, v_cur, _), stats in zip(tiles, probs):
        if dilation == 1:
            t_out = slice(s * qb, (s + 1) * qb)
        else:
            rho = pl.program_id(2) * group + u
            t_out = pl.ds(s * qb * dilation + rho, qb, stride=dilation)
        outs = []
        lse = lse_rows.get((u, s), jnp.zeros((qb, LANES), F32))
        for h, (p_prev, p_cur, den, m) in enumerate(stats):
            outs.append((_dot(p_prev, v_prev) + _dot(p_cur, v_cur)) / den)
            lse = jnp.where(lane == 2 * p + h, m + jnp.log(den), lse)
        lse_rows[(u, s)] = lse
        o_ref[0, p, t_out, :] = jnp.where(first, outs[0], outs[1])
        if p == width // LANES - 1:
            l_ref[0, t_out, :] = lse


def _dil_group(view, dilation, window, seq):
    b, length, _ = view.shape
    width = DIL_HEADS * DIL_HEAD_DIM
    assert window == dilation * DIL_QBLOCK and seq % (dilation * DIL_QBLOCK) == 0
    rows = min(max(DIL_TILE_TOKENS // dilation, DIL_QBLOCK), 512, length)
    group = min(dilation, max(512 // rows, 1))
    prev_blocks = rows // DIL_QBLOCK
    per_part = dilation // group

    def cur(part):
        return lambda bi, i, j: (bi, i, part * per_part + j)

    def prev(part):
        return lambda bi, i, j: (bi, jnp.maximum(i * prev_blocks - 1, 0), part * per_part + j)

    pairs = width // LANES
    out_shape = jax.ShapeDtypeStruct((b, pairs, seq, LANES), F32)
    out_spec = pl.BlockSpec((1, pairs, rows * dilation, LANES), lambda bi, i, j: (bi, 0, i, 0))
    lse_shape = jax.ShapeDtypeStruct((b, seq, LANES), F32)
    lse_spec = pl.BlockSpec((1, rows * dilation, LANES), lambda bi, i, j: (bi, i, 0))
    return pl.pallas_call(
        functools.partial(_dil_kernel, rows=rows, hd=DIL_HEAD_DIM, dilation=dilation, group=group),
        grid=(b, length // rows, per_part),
        in_specs=[
            pl.BlockSpec((1, rows, group * width), cur(0)),
            pl.BlockSpec((1, rows, group * width), cur(1)),
            pl.BlockSpec((1, rows, group * width), cur(2)),
            pl.BlockSpec((1, DIL_QBLOCK, group * width), prev(1)),
            pl.BlockSpec((1, DIL_QBLOCK, group * width), prev(2)),
        ],
        out_specs=[out_spec, lse_spec],
        out_shape=[out_shape, lse_shape],
        compiler_params=_params("parallel", "parallel", "arbitrary"),
        name=f"dil_attn_r{dilation}",
    )(view, view, view, view, view)


def _dil_res_kernel(o0, o1, o2, l0, l1, l2, w_ref, x_ref, g_ref, y_ref):
    ls = (l0[0], l1[0], l2[0])
    m = jnp.maximum(jnp.maximum(ls[0], ls[1]), ls[2])
    es = [jnp.exp(l - m) for l in ls]
    total = es[0] + es[1] + es[2]
    ws = [e / total for e in es]
    hd = LANES // 2
    first = lax.broadcasted_iota(jnp.int32, (x_ref.shape[1], LANES), 1) < hd
    parts = []
    for p in range(o0.shape[1]):
        w = [jnp.where(first, wg[:, 2 * p:2 * p + 1], wg[:, 2 * p + 1:2 * p + 2]) for wg in ws]
        parts.append(w[0] * o0[0, p] + w[1] * o1[0, p] + w[2] * o2[0, p])
    o = jnp.concatenate(parts, axis=1)
    y_ref[0] = x_ref[0] + g_ref[0] * _dot(o.astype(BF16), w_ref[...])


def _dil_res(outs, lses, w_out, x, gate, tm=512):
    b, s, d = x.shape
    k = w_out.shape[0]
    tm = min(tm, s)
    part = pl.BlockSpec((1, k // LANES, tm, LANES), lambda bi, i: (bi, 0, i, 0))
    lse = pl.BlockSpec((1, tm, LANES), lambda bi, i: (bi, i, 0))
    return pl.pallas_call(
        _dil_res_kernel,
        grid=(b, s // tm),
        in_specs=[part] * 3 + [lse] * 3 + [
            pl.BlockSpec((k, d), lambda bi, i: (0, 0)),
            pl.BlockSpec((1, tm, d), lambda bi, i: (bi, i, 0)),
            pl.BlockSpec((1, 1, d), lambda bi, i: (bi, 0, 0)),
        ],
        out_specs=pl.BlockSpec((1, tm, d), lambda bi, i: (bi, i, 0)),
        out_shape=jax.ShapeDtypeStruct((b, s, d), F32),
        compiler_params=_params("parallel", "parallel"),
        name="dil_res",
    )(*outs, *lses, w_out, x, gate)


def _router_kernel(x_ref, g_ref, sc_ref, sh_ref, wr_ref, br_ref, h_ref, route_ref, cnt_ref, run_ref, earlier_ref):
    tm = x_ref.shape[1]

    @pl.when(jnp.logical_and(pl.program_id(0) == 0, pl.program_id(1) == 0))
    def _():
        run_ref[...] = jnp.zeros_like(run_ref)
        row = lax.broadcasted_iota(jnp.int32, (tm, tm), 0)
        col = lax.broadcasted_iota(jnp.int32, (tm, tm), 1)
        earlier_ref[...] = jnp.where(col < row, 1.0, 0.0).astype(BF16)

    h = _mod_norm(x_ref[0], g_ref[...], sc_ref[0], sh_ref[0])
    _pack_slabs(h_ref, h)
    logits = _dot(h.astype(BF16), wr_ref[...].astype(BF16)) + br_ref[...]
    lane = lax.broadcasted_iota(jnp.int32, logits.shape, 1).astype(F32)
    neg = -jnp.inf
    far = float(LANES)

    def first_argmax(vals):
        top = jnp.max(vals, axis=1, keepdims=True)
        return top, jnp.min(jnp.where(vals == top, lane, far), axis=1, keepdims=True)

    grp_logits = jnp.where(lane < MOE_GROUPS, logits, neg)
    grp_max, grp = first_argmax(grp_logits)
    p_grp = 1.0 / jnp.sum(jnp.exp(grp_logits - grp_max), axis=1, keepdims=True)
    base = MOE_GROUPS + grp * MOE_PER_GROUP
    in_grp = jnp.logical_and(lane >= base, lane < base + MOE_PER_GROUP)
    exp_logits = jnp.where(in_grp, logits, neg)
    m1, i1 = first_argmax(exp_logits)
    m2, i2 = first_argmax(jnp.where(lane == i1, neg, exp_logits))
    e2 = jnp.exp(m2 - m1)
    g1 = p_grp / (1.0 + e2)
    g2 = g1 * e2
    e_a = i1 - MOE_GROUPS
    e_b = i2 - MOE_GROUPS

    pick_a = lane == e_a
    pick_b = lane == e_b
    both = jnp.where(jnp.logical_or(pick_a, pick_b), 1.0, 0.0)
    before = _dot(earlier_ref[...], both.astype(BF16)) + run_ref[...]
    rank_a = jnp.sum(jnp.where(pick_a, before, 0.0), axis=1, keepdims=True)
    rank_b = jnp.sum(jnp.where(pick_b, before, 0.0), axis=1, keepdims=True)
    run_ref[...] += jnp.sum(both, axis=0, keepdims=True)
    cnt_ref[...] = run_ref[...]

    route = jnp.where(lane == 0, e_a, jnp.where(lane == 1, e_b, 0.0))
    route = jnp.where(lane == 2, g1, jnp.where(lane == 3, g2, route))
    route_ref[...] = jnp.where(lane == 4, rank_a, jnp.where(lane == 5, rank_b, route))


def _router(x, g, sc, sh, w_route, b_route, tm=512):
    b, s, d = x.shape
    tm = min(tm, s)
    steps = s // tm
    return pl.pallas_call(
        _router_kernel,
        grid=(b, steps),
        in_specs=[
            pl.BlockSpec((1, tm, d), lambda bi, i: (bi, i, 0)),
            pl.BlockSpec((1, d), lambda bi, i: (0, 0)),
            pl.BlockSpec((1, 1, d), lambda bi, i: (bi, 0, 0)),
            pl.BlockSpec((1, 1, d), lambda bi, i: (bi, 0, 0)),
            pl.BlockSpec((d, LANES), lambda bi, i: (0, 0)),
            pl.BlockSpec((1, LANES), lambda bi, i: (0, 0)),
        ],
        out_specs=[
            pl.BlockSpec((tm * (d // PACKED_LANES), LANES), lambda bi, i: (bi * steps + i, 0)),
            pl.BlockSpec((tm, LANES), lambda bi, i: (bi * steps + i, 0)),
            pl.BlockSpec((1, LANES), lambda bi, i: (0, 0)),
        ],
        out_shape=[
            jax.ShapeDtypeStruct((b * s * (d // PACKED_LANES), LANES), jnp.int32),
            jax.ShapeDtypeStruct((b * s, LANES), F32),
            jax.ShapeDtypeStruct((1, LANES), F32),
        ],
        scratch_shapes=[pltpu.VMEM((1, LANES), F32), pltpu.VMEM((tm, tm), BF16)],
        compiler_params=_params("arbitrary", "arbitrary"),
        name="moe_router",
    )(x, g.reshape(1, d), sc, sh, w_route, b_route)


def _sc_dispatch(h_slabs, dest, n_slots):
    t, chunks, _ = h_slabs.shape
    workers = SC_CORES * SC_SUBCORES
    per_worker = t // workers
    steps = per_worker // SC_WINDOW
    assert t % (workers * SC_WINDOW * 2) == 0
    mesh = plsc.VectorSubcoreMesh(core_axis_name="core", subcore_axis_name="subcore")

    @functools.partial(
        pl.kernel, mesh=mesh,
        out_type=jax.ShapeDtypeStruct((n_slots, chunks, LANES), h_slabs.dtype),
        scratch_types=[
            pltpu.VMEM((2, MOE_TOPK, SC_WINDOW), jnp.int32),
            pltpu.VMEM((2, SC_WINDOW, chunks, LANES), h_slabs.dtype),
            pltpu.SemaphoreType.DMA((2,)),
            pltpu.SemaphoreType.DMA((2,)),
        ])
    def dispatch(h_hbm, dest_hbm, slots_hbm, idx_v, rows_v, load_sem, store_sem):
        worker = lax.axis_index("subcore") * SC_CORES + lax.axis_index("core")
        base = worker * per_worker

        @pl.loop(0, steps, step=2)
        def _(step):
            for buf in range(2):
                off = base + (step + buf) * SC_WINDOW
                pltpu.async_copy(h_hbm.at[pl.ds(off, SC_WINDOW)], rows_v.at[buf], load_sem.at[buf])
                for k in range(MOE_TOPK):
                    pltpu.sync_copy(dest_hbm.at[k, pl.ds(off, SC_WINDOW)], idx_v.at[buf, k])
            for buf in range(2):
                off = base + (step + buf) * SC_WINDOW
                pltpu.make_async_copy(h_hbm.at[pl.ds(off, SC_WINDOW)], rows_v.at[buf], load_sem.at[buf]).wait()
                copies = [pltpu.async_copy(rows_v.at[buf], slots_hbm.at[idx_v.at[buf, k]], store_sem.at[buf])
                          for k in range(MOE_TOPK)]
                for cp in copies:
                    cp.wait()

    return dispatch(h_slabs, dest)


def _zero_tail_kernel(tail_ref, valid_ref, s_in, s_out):
    del tail_ref
    rows = lax.broadcasted_iota(jnp.int32, s_in.shape, 0)
    s_out[...] = jnp.where(rows < valid_ref[pl.program_id(0)], s_in[...], 0)


def _zero_tails(tail_blk, valid_rows, slots, block_rows):
    grid_spec = pltpu.PrefetchScalarGridSpec(
        num_scalar_prefetch=2,
        grid=(tail_blk.shape[0],),
        in_specs=[pl.BlockSpec((block_rows, LANES), lambda e, tail, valid: (tail[e], 0))],
        out_specs=pl.BlockSpec((block_rows, LANES), lambda e, tail, valid: (tail[e], 0)),
    )
    return pl.pallas_call(
        _zero_tail_kernel,
        grid_spec=grid_spec,
        out_shape=jax.ShapeDtypeStruct(slots.shape, slots.dtype),
        input_output_aliases={2: 0},
        compiler_params=_params("arbitrary"),
        name="moe_zero_tails",
    )(tail_blk, valid_rows, slots)


def _expert_kernel(blk_e_ref, used_ref, x_ref, wg_ref, wu_ref, wd_ref, y_ref, wg_bf, wu_bf, wd_bf):
    i = pl.program_id(0)
    live = i < used_ref[0]
    changed = jnp.logical_or(i == 0, blk_e_ref[i] != blk_e_ref[jnp.maximum(i - 1, 0)])

    @pl.when(jnp.logical_and(live, changed))
    def _():
        wg_bf[...] = wg_ref[0, 0].astype(BF16)
        wu_bf[...] = wu_ref[0, 0].astype(BF16)
        wd_bf[...] = wd_ref[0, 0].astype(BF16)

    @pl.when(live)
    def _():
        bm = x_ref.shape[0] // (wg_bf.shape[0] // PACKED_LANES)
        x = _unpack_slabs(x_ref, bm).astype(BF16)
        gate = _dot(x, wg_bf[...])
        up = _dot(x, wu_bf[...])
        act = (gate * _sigmoid(gate) * up).astype(BF16)
        _pack_slabs(y_ref, _dot(act, wd_bf[...]))

    @pl.when(jnp.logical_not(live))
    def _():
        y_ref[...] = jnp.zeros_like(y_ref)


def _expert_blocks(blk_e, used, slots, w_gate, w_up, w_down, layer, bm):
    nb = blk_e.shape[0]
    _, _, d, f = w_gate.shape
    chunks = d // PACKED_LANES

    def block(i, e, n):
        return (jnp.minimum(i, n[0] - 1), 0)

    def weight(i, e, n):
        return (layer, e[jnp.minimum(i, n[0] - 1)], 0, 0)

    grid_spec = pltpu.PrefetchScalarGridSpec(
        num_scalar_prefetch=2,
        grid=(nb,),
        in_specs=[
            pl.BlockSpec((bm * chunks, LANES), block),
            pl.BlockSpec((1, 1, d, f), weight),
            pl.BlockSpec((1, 1, d, f), weight),
            pl.BlockSpec((1, 1, f, d), weight),
        ],
        out_specs=pl.BlockSpec((bm * chunks, LANES), lambda i, e, n: (i, 0)),
        scratch_shapes=[pltpu.VMEM((d, f), BF16), pltpu.VMEM((d, f), BF16), pltpu.VMEM((f, d), BF16)],
    )
    return pl.pallas_call(
        _expert_kernel,
        grid_spec=grid_spec,
        out_shape=jax.ShapeDtypeStruct((nb * bm * chunks, LANES), jnp.int32),
        compiler_params=_params("arbitrary"),
        name="moe_experts",
    )(blk_e, used, slots, w_gate, w_up, w_down)


def _sc_gather(y_slabs, idx):
    n = idx.shape[0]
    _, chunks, _ = y_slabs.shape
    workers = SC_CORES * SC_SUBCORES
    per_worker = n // workers
    steps = per_worker // SC_WINDOW
    assert n % (workers * SC_WINDOW * 2) == 0
    mesh = plsc.VectorSubcoreMesh(core_axis_name="core", subcore_axis_name="subcore")

    @functools.partial(
        pl.kernel, mesh=mesh,
        out_type=jax.ShapeDtypeStruct((n, chunks, LANES), y_slabs.dtype),
        scratch_types=[
            pltpu.VMEM((2, SC_WINDOW), jnp.int32),
            pltpu.VMEM((2, SC_WINDOW, chunks, LANES), y_slabs.dtype),
            pltpu.SemaphoreType.DMA((2,)),
        ])
    def gather(y_hbm, idx_hbm, out_hbm, idx_v, rows_v, sem):
        worker = lax.axis_index("subcore") * SC_CORES + lax.axis_index("core")
        base = worker * per_worker

        @pl.loop(0, steps, step=2)
        def _(step):
            for buf in range(2):
                off = base + (step + buf) * SC_WINDOW
                pltpu.sync_copy(idx_hbm.at[pl.ds(off, SC_WINDOW)], idx_v.at[buf])
                pltpu.async_copy(y_hbm.at[idx_v.at[buf]], rows_v.at[buf], sem.at[buf])
            for buf in range(2):
                off = base + (step + buf) * SC_WINDOW
                pltpu.make_async_copy(y_hbm.at[idx_v.at[buf]], rows_v.at[buf], sem.at[buf]).wait()
                pltpu.sync_copy(rows_v.at[buf], out_hbm.at[pl.ds(off, SC_WINDOW)])

    return gather(y_slabs, idx)


def _combine_kernel(g0_ref, g1_ref, route_ref, x_ref, gate_ref, fg_ref, o_ref, *, final):
    tm = x_ref.shape[0]
    route = route_ref[...]
    y = route[:, 2:3] * _unpack_slabs(g0_ref, tm) + route[:, 3:4] * _unpack_slabs(g1_ref, tm)
    out = x_ref[...] + gate_ref[0] * y
    if final:
        out = out * lax.rsqrt(jnp.mean(out * out, axis=-1, keepdims=True) + RMS_EPS) * fg_ref[...]
    o_ref[...] = out


def _combine(dest, route, y_slabs, x, gate, final_g, final, tm, pieces=8):
    b, s, d = x.shape
    t = b * s
    chunks = d // PACKED_LANES
    per_seq = s // tm
    piece = t // pieces
    tiles = piece // tm
    out = x.reshape(t, d)
    fg = final_g.reshape(1, d)
    for c in range(pieces):
        idx = dest[:, c * piece:(c + 1) * piece].reshape(MOE_TOPK * piece)
        g = _sc_gather(y_slabs, idx).reshape(MOE_TOPK * piece * chunks, LANES)
        first = c * tiles
        out = pl.pallas_call(
            functools.partial(_combine_kernel, final=final),
            grid=(tiles,),
            in_specs=[
                pl.BlockSpec((tm * chunks, LANES), lambda i: (i, 0)),
                pl.BlockSpec((tm * chunks, LANES), lambda i: (tiles + i, 0)),
                pl.BlockSpec((tm, LANES), lambda i, first=first: (first + i, 0)),
                pl.BlockSpec((tm, d), lambda i, first=first: (first + i, 0)),
                pl.BlockSpec((1, 1, d), lambda i, first=first: ((first + i) // per_seq, 0, 0)),
                pl.BlockSpec((1, d), lambda i: (0, 0)),
            ],
            out_specs=pl.BlockSpec((tm, d), lambda i, first=first: (first + i, 0)),
            out_shape=jax.ShapeDtypeStruct((t, d), F32),
            input_output_aliases={3: 0},
            compiler_params=_params("parallel"),
            name="moe_combine",
        )(g, g, route, out, gate, fg)
    return out.reshape(b, s, d)


def _slot_tables(route, counts, bm):
    t = route.shape[0]
    nb = MOE_TOPK * t // bm + MOE_EXPERTS
    counts = counts[0, :MOE_EXPERTS].astype(jnp.int32)
    padded = (counts + bm - 1) // bm * bm
    pad_end = jnp.cumsum(padded)
    pad_start = pad_end - padded
    expert = route[:, 0:MOE_TOPK].astype(jnp.int32)
    rank = route[:, 4:4 + MOE_TOPK].astype(jnp.int32)
    dest = pad_start[expert] + rank
    blk_start = jnp.arange(nb, dtype=jnp.int32) * bm
    blk_e = jnp.minimum(jnp.sum((pad_end[None, :] <= blk_start[:, None]).astype(jnp.int32), axis=1), MOE_EXPERTS - 1)
    used = (pad_end[-1:] // bm).astype(jnp.int32)
    tail_blk = jnp.maximum(pad_end // bm - 1, 0).astype(jnp.int32)
    tail_fill = jnp.where(counts % bm != 0, counts % bm, bm).astype(jnp.int32)
    return dest.T, blk_e, used, tail_blk, tail_fill, nb


def _moe_layer(x, g, sc, sh, gate, w_route, b_route, w_gate, w_up, w_down, layer, final_g, final, bm=512, tm=1024):
    b, s, d = x.shape
    t = b * s
    chunks = d // PACKED_LANES
    tm = min(tm, s)
    h_rows, route, counts = _router(x, g, sc, sh, w_route, b_route)
    dest, blk_e, used, tail_blk, tail_fill, nb = _slot_tables(route, counts, bm)
    slots = _sc_dispatch(h_rows.reshape(t, chunks, LANES), dest, nb * bm)
    slots = _zero_tails(tail_blk, tail_fill * chunks, slots.reshape(nb * bm * chunks, LANES), bm * chunks)
    y_slabs = _expert_blocks(blk_e, used, slots, w_gate, w_up, w_down, layer, bm).reshape(nb * bm, chunks, LANES)
    return _combine(dest, route, y_slabs, x, gate, final_g, final, tm)


def kernel(x, c, ada_w, ada_b, norm_g, final_g, sb_w_in, sb_w_out, gla_w_in, gla_w_gate_up, gla_b_gate, gla_norm_g, gla_w_out, conv_w_in, conv_w, conv_b, conv_w_out, dil_w_in, dil_w_out, moe_w_grp, moe_b_grp, moe_w_exp, moe_b_exp, moe_w_gate, moe_w_up, moe_w_down):
    b, s, d = x.shape
    depth = ada_w.shape[0]
    mod = _ada_mod(c, ada_w, ada_b).reshape(depth, b, N_MOD, 1, d)
    for i in range(depth):
        sh1, sc1, g1, sh2, sc2, g2 = (mod[i, :, m] for m in range(N_MOD))
        kind, j = i % 4, i // 4
        if kind == 0:
            qkv = _norm_proj(x, norm_g[i, 0], sc1, sh1, sb_w_in[j].astype(BF16))
            x = _proj_res(_sb_attention(qkv), sb_w_out[j].astype(BF16), x, g1)
        elif kind == 1:
            dk = gla_w_gate_up.shape[-1] // GLA_HEADS
            dv = gla_norm_g.shape[-1]
            n_in = gla_w_in.shape[-1]
            w_in = jnp.pad(gla_w_in[j], ((0, 0), (0, -n_in % MXU_COLUMNS))).astype(BF16)
            w_up = jnp.pad(gla_w_gate_up[j], ((0, LANES - GLA_GATE_RANK), (0, 0))).astype(BF16)
            proj = _norm_proj(x, norm_g[i, 0], sc1, sh1, w_in, tn=MXU_COLUMNS)
            o = _gla_core(proj, w_up, gla_b_gate[j].reshape(1, -1), gla_norm_g[j].reshape(1, -1), dk, dv)
            x = _proj_res(o, gla_w_out[j].astype(BF16), x, g1)
        elif kind == 2:
            proj = _norm_proj(x, norm_g[i, 0], sc1, sh1, conv_w_in[j].astype(BF16))
            x = _conv_res(proj, conv_w[j], conv_b[j], conv_w_out[j].astype(BF16), x, g1)
        else:
            group_w = 3 * DIL_HEADS * DIL_HEAD_DIM
            outs, lses = [], []
            for g, (window, dilation) in enumerate(DIL_PATTERNS):
                w_g = dil_w_in[j][:, g * group_w:(g + 1) * group_w].astype(BF16)
                view = _norm_proj(x, norm_g[i, 0], sc1, sh1, w_g, dilation=dilation)
                o_g, l_g = _dil_group(view, dilation, window, s)
                outs.append(o_g)
                lses.append(l_g)
            x = _dil_res(outs, lses, dil_w_out[j].astype(BF16), x, g1)
        w_route = jnp.pad(jnp.concatenate([moe_w_grp[i], moe_w_exp[i]], axis=1),
                          ((0, 0), (0, LANES - MOE_GROUPS - MOE_EXPERTS)))
        b_route = jnp.pad(jnp.concatenate([moe_b_grp[i], moe_b_exp[i]]), (0, LANES - MOE_GROUPS - MOE_EXPERTS))
        x = _moe_layer(x, norm_g[i, 1], sc2, sh2, g2, w_route, b_route.reshape(1, LANES),
                       moe_w_gate, moe_w_up, moe_w_down, i, final_g, final=(i == depth - 1))
    return x
```

```python
import functools

import jax
import jax.numpy as jnp
from jax import lax
from jax.experimental import pallas as pl
from jax.experimental.pallas import tpu as pltpu
from jax.experimental.pallas import tpu_sc as plsc

F32 = jnp.float32
BF16 = jnp.bfloat16

LANES = 128
MXU_COLUMNS = 256
RMS_EPS = 1e-6
N_MOD = 6

SB_HEADS = 16
GLA_HEADS = 4
GLA_GATE_RANK = 16
GLA_GATE_TAU = 16.0
GLA_CHUNK = 64
CONV_WIDTH = 3
DIL_PATTERNS = ((128, 1), (512, 4), (2048, 16))
DIL_HEADS = 8
DIL_HEAD_DIM = 64
DIL_QBLOCK = 128
DIL_TILE_TOKENS = 2048
MOE_GROUPS = 4
MOE_PER_GROUP = 8
MOE_EXPERTS = MOE_GROUPS * MOE_PER_GROUP
MOE_TOPK = 2

SC_CORES = 2
SC_SUBCORES = 16
SC_WINDOW = 64
PACKED_LANES = 2 * LANES

SB_LOG_ZERO = -104.0
SB_BAND = 2

VMEM_LIMIT = 48 * 1024 * 1024


def _params(*sem):
    return pltpu.CompilerParams(dimension_semantics=sem, vmem_limit_bytes=VMEM_LIMIT)


def _sigmoid(x):
    return 1.0 / (1.0 + jnp.exp(-x))


def _softplus(x):
    return jnp.maximum(x, 0.0) + jnp.log(1.0 + jnp.exp(-jnp.abs(x)))


def _split_bf16(x):
    hi = x.astype(BF16)
    lo = (x - hi.astype(F32)).astype(BF16)
    return hi, lo


def _dot(a, b):
    return jnp.dot(a, b, preferred_element_type=F32)


def _dot_nt(a, b):
    return lax.dot_general(a, b, (((1,), (1,)), ((), ())), preferred_element_type=F32)


def _dot_tn(a, b):
    return lax.dot_general(a, b, (((0,), (0,)), ((), ())), preferred_element_type=F32)


def _unpack_slabs(ref, rows):
    k = ref.shape[0] // rows
    words = jnp.concatenate([ref[pl.ds(s, rows, stride=k), :] for s in range(k)], axis=1)
    halves = [pltpu.unpack_elementwise(words, index=i, packed_dtype=BF16, unpacked_dtype=F32) for i in range(2)]
    return jnp.concatenate(halves, axis=1)


def _pack_slabs(ref, values):
    rows, d = values.shape
    half = d // 2
    words = pltpu.pack_elementwise([values[:, :half], values[:, half:]], packed_dtype=BF16)
    for s in range(half // LANES):
        ref[pl.ds(s, rows, stride=half // LANES), :] = words[:, s * LANES:(s + 1) * LANES]


def _mod_norm(x, g, sc, sh):
    r = lax.rsqrt(jnp.mean(x * x, axis=-1, keepdims=True) + RMS_EPS)
    return (x * r) * g * (1.0 + sc) + sh


def _ada_kernel(c_ref, w_ref, b_ref, o_ref):
    c = c_ref[...]
    cond = c * _sigmoid(c)
    o_ref[0] = _dot(cond.astype(BF16), w_ref[0].astype(BF16)) + b_ref[0]


def _ada_mod(c, ada_w, ada_b):
    depth, d, n = ada_w.shape
    b = c.shape[0]
    tn = 1536
    return pl.pallas_call(
        _ada_kernel,
        grid=(depth, n // tn),
        in_specs=[
            pl.BlockSpec((b, d), lambda i, j: (0, 0)),
            pl.BlockSpec((1, d, tn), lambda i, j: (i, 0, j)),
            pl.BlockSpec((1, 1, tn), lambda i, j: (i, 0, j)),
        ],
        out_specs=pl.BlockSpec((1, b, tn), lambda i, j: (i, 0, j)),
        out_shape=jax.ShapeDtypeStruct((depth, b, n), F32),
        compiler_params=_params("parallel", "parallel"),
        name="ada_mod",
    )(c, ada_w, ada_b.reshape(depth, 1, n))


def _norm_proj_kernel(x_ref, g_ref, sc_ref, sh_ref, w_ref, o_ref, h_ref, w_bf, *res_ref, dilation, tn):
    @pl.when(jnp.logical_and(pl.program_id(0) == 0, pl.program_id(1) == 0))
    def _():
        for j in range(w_ref.shape[1] // tn):
            w_bf[:, j * tn:(j + 1) * tn] = w_ref[:, j * tn:(j + 1) * tn].astype(BF16)

    h_ref[...] = _mod_norm(x_ref[0], g_ref[...], sc_ref[0], sh_ref[0]).astype(BF16)
    tm = h_ref.shape[0]
    for j in range(w_ref.shape[1] // tn):
        res = _dot(h_ref[...], w_bf[:, j * tn:(j + 1) * tn])
        if dilation == 1:
            o_ref[0, :, j * tn:(j + 1) * tn] = res.astype(o_ref.dtype)
            continue
        for c in range(tn // LANES):
            res_ref[0][j * (tn // LANES) + c] = res[:, c * LANES:(c + 1) * LANES]
        for rho in range(dilation):
            for c in range(tn // LANES):
                rows = res_ref[0][j * (tn // LANES) + c, pl.ds(rho, tm // dilation, stride=dilation), :]
                col = (j * dilation + rho) * tn + c * LANES
                o_ref[0, :, col:col + LANES] = rows.astype(o_ref.dtype)


def _norm_proj(x, g, sc, sh, w, tm=512, tn=512, dilation=1):
    b, s, d = x.shape
    n = w.shape[1]
    tm = min(tm, s)
    assert s % tm == 0 and n % tn == 0 and tm % (16 * dilation) == 0
    scratch = [pltpu.VMEM((tm, d), BF16), pltpu.VMEM((d, n), BF16)]
    if dilation > 1:
        scratch.append(pltpu.VMEM((n // LANES, tm, LANES), F32))
    return pl.pallas_call(
        functools.partial(_norm_proj_kernel, dilation=dilation, tn=tn),
        grid=(b, s // tm),
        in_specs=[
            pl.BlockSpec((1, tm, d), lambda bi, i: (bi, i, 0)),
            pl.BlockSpec((1, d), lambda bi, i: (0, 0)),
            pl.BlockSpec((1, 1, d), lambda bi, i: (bi, 0, 0)),
            pl.BlockSpec((1, 1, d), lambda bi, i: (bi, 0, 0)),
            pl.BlockSpec((d, n), lambda bi, i: (0, 0), pipeline_mode=pl.Buffered(1)),
        ],
        out_specs=pl.BlockSpec((1, tm // dilation, dilation * n), lambda bi, i: (bi, i, 0)),
        out_shape=jax.ShapeDtypeStruct((b, s // dilation, dilation * n), BF16),
        scratch_shapes=scratch,
        compiler_params=_params("arbitrary", "arbitrary"),
        name="norm_proj",
    )(x, g.reshape(1, d), sc, sh, w)


def _proj_res_kernel(o_ref, w_ref, x_ref, g_ref, y_ref):
    y_ref[0] = x_ref[0] + g_ref[0] * _dot(o_ref[0], w_ref[...])


def _proj_res(o, w, x, gate, tm=512):
    b, s, d = x.shape
    k = o.shape[-1]
    tm = min(tm, s)
    return pl.pallas_call(
        _proj_res_kernel,
        grid=(b, s // tm),
        in_specs=[
            pl.BlockSpec((1, tm, k), lambda bi, i: (bi, i, 0)),
            pl.BlockSpec((k, d), lambda bi, i: (0, 0)),
            pl.BlockSpec((1, tm, d), lambda bi, i: (bi, i, 0)),
            pl.BlockSpec((1, 1, d), lambda bi, i: (bi, 0, 0)),
        ],
        out_specs=pl.BlockSpec((1, tm, d), lambda bi, i: (bi, i, 0)),
        out_shape=jax.ShapeDtypeStruct((b, s, d), F32),
        compiler_params=_params("parallel", "parallel"),
        name="proj_res",
    )(o, w, x, gate)


def _sb_kernel(q_ref, k_ref, v_ref, o_ref, acc_ref, c_ref, *, blk, hd):
    scale = hd ** -0.5
    fold_scale = (hd & (hd - 1)) == 0 and (hd.bit_length() - 1) % 2 == 0
    first = lax.broadcasted_iota(jnp.int32, (blk, LANES), 1) < hd
    row = lax.broadcasted_iota(jnp.int32, (blk, blk), 0)
    col = lax.broadcasted_iota(jnp.int32, (blk, blk), 1)
    minus_later = jnp.where(row > col, -1.0, 0.0).astype(BF16)
    causal = col < row
    subs = q_ref.shape[1] // blk
    first_block = pl.program_id(2) * subs

    def visit_many(work):
        kv = []
        for _, block, _, _ in work:
            start = pl.multiple_of(block * blk, blk)
            kv.append((k_ref[0, pl.ds(start, blk), :], v_ref[0, pl.ds(start, blk), :]))
        zs = [[_dot_nt(q_heads[h], kb) for h in range(2)] for (q_heads, _, _, _), (kb, _) in zip(work, kv)]
        log_betas, sps = [], []
        for (_, _, mask, _), z2 in zip(work, zs):
            lb2, sp2 = [], []
            for z in z2:
                if not fold_scale:
                    z = z * scale
                sp = _softplus(z)
                lb2.append(z - sp)
                sp2.append(sp if mask is None else jnp.where(mask, sp, 0.0))
            log_betas.append(lb2)
            sps.append(sp2)
        suffixes = [[_dot(sp.astype(BF16), minus_later) for sp in sp2] for sp2 in sps]
        return kv, log_betas, sps, suffixes

    def finish(work, staged, c_in):
        kv, log_betas, sps, suffixes = staged
        cs = {chain: list(c) for chain, c in c_in.items()}
        weights = []
        for n, (_, _, mask, chain) in enumerate(work):
            a2 = []
            for h in range(2):
                a = jnp.exp(log_betas[n][h] + suffixes[n][h] + cs[chain][h])
                a2.append(a if mask is None else jnp.where(mask, a, 0.0))
                cs[chain][h] = cs[chain][h] - jnp.sum(sps[n][h], axis=1, keepdims=True)
            weights.append(a2)
        accs = {}
        for n, (_, _, _, chain) in enumerate(work):
            outs = [_dot(weights[n][h].astype(BF16), kv[n][1]) for h in range(2)]
            more = jnp.where(first, outs[0], outs[1])
            accs[chain] = more if chain not in accs else accs[chain] + more
        return accs, cs

    def remainder(sub, q_heads, depth):
        def cond(carry):
            j, cmax = carry
            return jnp.logical_and(j >= 0, cmax > SB_LOG_ZERO)

        def body(carry):
            j, _ = carry
            work = [(q_heads, j, None, sub)]
            accs, cs = finish(work, visit_many(work), {sub: [c_ref[sub, 0], c_ref[sub, 1]]})
            acc_ref[sub] += accs[sub]
            for h in range(2):
                c_ref[sub, h] = cs[sub][h]
            return j - 1, jnp.max(c_ref[sub])

        lax.while_loop(cond, body, (first_block + sub - depth - 1, jnp.max(c_ref[sub])))

    def run(depths):
        heads = []
        for sub in range(subs):
            q = q_ref[0, sub * blk:(sub + 1) * blk, :]
            if fold_scale:
                q = (q.astype(F32) * scale).astype(BF16)
            zero = jnp.zeros_like(q)
            heads.append((jnp.where(first, q, zero), jnp.where(first, zero, q)))
        work = [(heads[sub], first_block + sub - back, causal if back == 0 else None, sub)
                for sub in range(subs) for back in range(depths[sub] + 1)]
        zero_c = [jnp.zeros((blk, 1), F32)] * 2
        accs, cs = finish(work, visit_many(work), {sub: zero_c for sub in range(subs)})
        for sub in range(subs):
            acc_ref[sub] = accs[sub]
            for h in range(2):
                c_ref[sub, h] = cs[sub][h]
        @pl.when(jnp.max(c_ref[...]) > SB_LOG_ZERO)
        def _():
            for sub in range(subs):
                remainder(sub, heads[sub], depths[sub])

        for sub in range(subs):
            o_ref[0, sub * blk:(sub + 1) * blk, :] = acc_ref[sub].astype(o_ref.dtype)

    assert subs >= SB_BAND
    pl.when(first_block > 0)(functools.partial(run, [SB_BAND] * subs))
    pl.when(first_block == 0)(functools.partial(run, [min(SB_BAND, sub) for sub in range(subs)]))


def _sb_attention(qkv, blk=128, rows=1024):
    b, s, n = qkv.shape
    width = n // 3
    hd = width // SB_HEADS
    assert 2 * hd == LANES
    pairs = width // LANES
    rows = min(rows, s)
    return pl.pallas_call(
        functools.partial(_sb_kernel, blk=blk, hd=hd),
        grid=(b, pairs, s // rows),
        in_specs=[
            pl.BlockSpec((1, rows, LANES), lambda bi, p, i: (bi, i, p)),
            pl.BlockSpec((1, s, LANES), lambda bi, p, i: (bi, 0, pairs + p)),
            pl.BlockSpec((1, s, LANES), lambda bi, p, i: (bi, 0, 2 * pairs + p)),
        ],
        out_specs=pl.BlockSpec((1, rows, LANES), lambda bi, p, i: (bi, i, p)),
        out_shape=jax.ShapeDtypeStruct((b, s, width), BF16),
        scratch_shapes=[pltpu.VMEM((rows // blk, blk, LANES), F32), pltpu.VMEM((rows // blk, 2, blk, 1), F32)],
        compiler_params=_params("parallel", "parallel", "arbitrary"),
        name="sb_attention",
    )(qkv, qkv, qkv)


def _gla_kernel(p_ref, wg_ref, bg_ref, ng_ref, o_ref, st_ref, *, rows, dk, dv):
    @pl.when(pl.program_id(1) == 0)
    def _():
        st_ref[...] = jnp.zeros_like(st_ref)

    heads = GLA_HEADS
    ch = GLA_CHUNK
    k0 = heads * dk
    v0 = 2 * heads * dk
    r0 = v0 + heads * dv
    g0 = r0 + heads * dv
    row = lax.broadcasted_iota(jnp.int32, (ch, ch), 0)
    col = lax.broadcasted_iota(jnp.int32, (ch, ch), 1)
    causal = col <= row
    lower = jnp.where(causal, 1.0, 0.0).astype(BF16)
    chunks = [slice(s * ch, (s + 1) * ch) for s in range(rows // ch)]
    g_pres = [_dot(p_ref[0, t, g0:g0 + LANES], wg_ref[...]) + bg_ref[...] for t in chunks]
    cums = []
    for g_pre in g_pres:
        log_a = (jnp.minimum(g_pre, 0.0) - jnp.log(1.0 + jnp.exp(-jnp.abs(g_pre)))) * (1.0 / GLA_GATE_TAU)
        hi, lo = _split_bf16(log_a)
        cums.append(_dot(lower, hi) + _dot(lower, lo))
    parts = []
    for t, cum in zip(chunks, cums):
        for h in range(heads):
            bh = cum[:, h * dk:(h + 1) * dk]
            b_last = bh[ch - 1:ch, :]
            q = p_ref[0, t, h * dk:(h + 1) * dk].astype(F32) * dk ** -0.5
            k = p_ref[0, t, k0 + h * dk:k0 + (h + 1) * dk].astype(F32)
            q_in = (q * jnp.exp(bh)).astype(BF16)
            k_in = (k * jnp.exp(-bh)).astype(BF16)
            k_state = (k * jnp.exp(b_last - bh)).astype(BF16)
            parts.append((t, h, q_in, k_in, k_state, jnp.exp(b_last)))
    scores = [jnp.where(causal, _dot_nt(q_in, k_in), 0.0).astype(BF16) for _, _, q_in, k_in, _, _ in parts]
    values = [p_ref[0, t, v0 + h * dv:v0 + (h + 1) * dv] for t, h, _, _, _, _ in parts]
    intra = [_dot(sc, v) for sc, v in zip(scores, values)]
    updates = [_dot_tn(v, k_state) for v, (_, _, _, _, k_state, _) in zip(values, parts)]
    states = [st_ref[h] for h in range(heads)]
    for n, (t, h, q_in, _, _, decay) in enumerate(parts):
        o = intra[n] + _dot_nt(q_in, states[h].astype(BF16))
        states[h] = decay * states[h] + updates[n]
        o = o * lax.rsqrt(jnp.mean(o * o, axis=-1, keepdims=True) + RMS_EPS) * ng_ref[...]
        r = p_ref[0, t, r0 + h * dv:r0 + (h + 1) * dv].astype(F32)
        o_ref[0, t, h * dv:(h + 1) * dv] = (o * (r * _sigmoid(r))).astype(o_ref.dtype)
    for h in range(heads):
        st_ref[h] = states[h]


def _gla_core(proj, w_gate_up, b_gate, norm_g, dk, dv, rows=512):
    b, s, n = proj.shape
    rows = min(rows, s)
    hk = GLA_HEADS * dk
    return pl.pallas_call(
        functools.partial(_gla_kernel, rows=rows, dk=dk, dv=dv),
        grid=(b, s // rows),
        in_specs=[
            pl.BlockSpec((1, rows, n), lambda bi, i: (bi, i, 0)),
            pl.BlockSpec((LANES, hk), lambda bi, i: (0, 0)),
            pl.BlockSpec((1, hk), lambda bi, i: (0, 0)),
            pl.BlockSpec((1, dv), lambda bi, i: (0, 0)),
        ],
        out_specs=pl.BlockSpec((1, rows, GLA_HEADS * dv), lambda bi, i: (bi, i, 0)),
        out_shape=jax.ShapeDtypeStruct((b, s, GLA_HEADS * dv), BF16),
        scratch_shapes=[pltpu.VMEM((GLA_HEADS, dv, dk), F32)],
        compiler_params=_params("parallel", "arbitrary"),
        name="gla_core",
    )(proj, w_gate_up, b_gate, norm_g)


def _conv_res_kernel(gb_ref, gc_ref, u_ref, hgc_ref, hu_ref, cw_ref, cb_ref, w_ref, x_ref, g_ref, y_ref):
    u2 = gc_ref[0].astype(F32) * u_ref[0].astype(F32)
    halo = hgc_ref[0].astype(F32) * hu_ref[0].astype(F32)
    halo = jnp.where(pl.program_id(1) > 0, halo, 0.0)
    rows = lax.broadcasted_iota(jnp.int32, u2.shape, 0)
    prev1 = jnp.where(rows == 0, halo[7:8], pltpu.roll(u2, 1, 0))
    prev2 = jnp.where(rows == 0, halo[6:7], jnp.where(rows == 1, halo[7:8], pltpu.roll(u2, 2, 0)))
    cw = cw_ref[...]
    y = cw[0:1] * prev2 + cw[1:2] * prev1 + cw[2:3] * u2 + cb_ref[...]
    o = (gb_ref[0].astype(F32) * y).astype(BF16)
    y_ref[0] = x_ref[0] + g_ref[0] * _dot(o, w_ref[...])


def _conv_res(proj, conv_w, conv_b, w_out, x, gate, tm=512):
    b, s, d = x.shape
    tm = min(tm, s)
    halo_blocks = tm // 8

    def halo_map(col):
        return lambda bi, i: (bi, jnp.maximum(i * halo_blocks - 1, 0), col)

    return pl.pallas_call(
        _conv_res_kernel,
        grid=(b, s // tm),
        in_specs=[
            pl.BlockSpec((1, tm, d), lambda bi, i: (bi, i, 0)),
            pl.BlockSpec((1, tm, d), lambda bi, i: (bi, i, 1)),
            pl.BlockSpec((1, tm, d), lambda bi, i: (bi, i, 2)),
            pl.BlockSpec((1, 8, d), halo_map(1)),
            pl.BlockSpec((1, 8, d), halo_map(2)),
            pl.BlockSpec((CONV_WIDTH, d), lambda bi, i: (0, 0)),
            pl.BlockSpec((1, d), lambda bi, i: (0, 0)),
            pl.BlockSpec((d, d), lambda bi, i: (0, 0)),
            pl.BlockSpec((1, tm, d), lambda bi, i: (bi, i, 0)),
            pl.BlockSpec((1, 1, d), lambda bi, i: (bi, 0, 0)),
        ],
        out_specs=pl.BlockSpec((1, tm, d), lambda bi, i: (bi, i, 0)),
        out_shape=jax.ShapeDtypeStruct((b, s, d), F32),
        compiler_params=_params("parallel", "parallel"),
        name="conv_res",
    )(proj, proj, proj, proj, proj, conv_w, conv_b.reshape(1, d), w_out, x, gate)


def _dil_kernel(q_ref, kc_ref, vc_ref, kp_ref, vp_ref, o_ref, l_ref, *, rows, hd, dilation, group):
    qb = DIL_QBLOCK
    n = pl.program_id(1)
    width = q_ref.shape[-1] // group
    scale = hd ** -0.5
    fold_scale = (hd & (hd - 1)) == 0 and (hd.bit_length() - 1) % 2 == 0
    first = lax.broadcasted_iota(jnp.int32, (qb, LANES), 1) < hd
    row = lax.broadcasted_iota(jnp.int32, (qb, qb), 0)
    col = lax.broadcasted_iota(jnp.int32, (qb, qb), 1)
    in_prev = col >= row
    in_cur = col <= row
    neg = -jnp.inf

    tiles = []
    for u, s in [(u, s) for u in range(group) for s in range(rows // qb)]:
        t = slice(s * qb, (s + 1) * qb)
        tp = slice((s - 1) * qb, s * qb)
        prev_mask = jnp.logical_and(in_prev, n > 0) if s == 0 else in_prev
        for p in range(width // LANES):
            c = slice(u * width + p * LANES, u * width + (p + 1) * LANES)
            q = q_ref[0, t, c]
            if fold_scale:
                q = (q.astype(F32) * scale).astype(BF16)
            zero = jnp.zeros_like(q)
            q_heads = (jnp.where(first, q, zero), jnp.where(first, zero, q))
            if s == 0:
                k_prev, v_prev = kp_ref[0, :, c], vp_ref[0, :, c]
            else:
                k_prev, v_prev = kc_ref[0, tp, c], vc_ref[0, tp, c]
            tiles.append(((u, s), p, q_heads, k_prev, kc_ref[0, t, c], v_prev, vc_ref[0, t, c], prev_mask))

    scores = []
    for _, _, q_heads, k_prev, k_cur, _, _, prev_mask in tiles:
        pair = []
        for h in range(2):
            z_prev, z_cur = _dot_nt(q_heads[h], k_prev), _dot_nt(q_heads[h], k_cur)
            if not fold_scale:
                z_prev, z_cur = z_prev * scale, z_cur * scale
            pair.append((jnp.where(prev_mask, z_prev, neg), jnp.where(in_cur, z_cur, neg)))
        scores.append(pair)

    probs = []
    for pair in scores:
        stats = []
        for z_prev, z_cur in pair:
            m = jnp.max(jnp.maximum(z_prev, z_cur), axis=1, keepdims=True)
            p_prev = jnp.exp(z_prev - m)
            p_cur = jnp.exp(z_cur - m)
            den = jnp.sum(p_prev + p_cur, axis=1, keepdims=True)
            stats.append((p_prev.astype(BF16), p_cur.astype(BF16), den, m))
        probs.append(stats)

    lane = lax.broadcasted_iota(jnp.int32, (qb, LANES), 1)
    lse_rows = {}
    for ((u, s), p, _, _, _, v_prev, v_cur, _), stats in zip(tiles, probs):
        if dilation == 1:
            t_out = slice(s * qb, (s + 1) * qb)
        else:
            rho = pl.program_id(2) * group + u
            t_out = pl.ds(s * qb * dilation + rho, qb, stride=dilation)
        outs = []
        lse = lse_rows.get((u, s), jnp.zeros((qb, LANES), F32))
        for h, (p_prev, p_cur, den, m) in enumerate(stats):
            outs.append((_dot(p_prev, v_prev) + _dot(p_cur, v_cur)) / den)
            lse = jnp.where(lane == 2 * p + h, m + jnp.log(den), lse)
        lse_rows[(u, s)] = lse
        o_ref[0, p, t_out, :] = jnp.where(first, outs[0], outs[1])
        if p == width // LANES - 1:
            l_ref[0, t_out, :] = lse


def _dil_group(view, dilation, window, seq):
    b, length, _ = view.shape
    width = DIL_HEADS * DIL_HEAD_DIM
    assert window == dilation * DIL_QBLOCK and seq % (dilation * DIL_QBLOCK) == 0
    rows = min(max(DIL_TILE_TOKENS // dilation, DIL_QBLOCK), 512, length)
    group = min(dilation, max(512 // rows, 1))
    prev_blocks = rows // DIL_QBLOCK
    per_part = dilation // group

    def cur(part):
        return lambda bi, i, j: (bi, i, part * per_part + j)

    def prev(part):
        return lambda bi, i, j: (bi, jnp.maximum(i * prev_blocks - 1, 0), part * per_part + j)

    pairs = width // LANES
    out_shape = jax.ShapeDtypeStruct((b, pairs, seq, LANES), F32)
    out_spec = pl.BlockSpec((1, pairs, rows * dilation, LANES), lambda bi, i, j: (bi, 0, i, 0))
    lse_shape = jax.ShapeDtypeStruct((b, seq, LANES), F32)
    lse_spec = pl.BlockSpec((1, rows * dilation, LANES), lambda bi, i, j: (bi, i, 0))
    return pl.pallas_call(
        functools.partial(_dil_kernel, rows=rows, hd=DIL_HEAD_DIM, dilation=dilation, group=group),
        grid=(b, length // rows, per_part),
        in_specs=[
            pl.BlockSpec((1, rows, group * width), cur(0)),
            pl.BlockSpec((1, rows, group * width), cur(1)),
            pl.BlockSpec((1, rows, group * width), cur(2)),
            pl.BlockSpec((1, DIL_QBLOCK, group * width), prev(1)),
            pl.BlockSpec((1, DIL_QBLOCK, group * width), prev(2)),
        ],
        out_specs=[out_spec, lse_spec],
        out_shape=[out_shape, lse_shape],
        compiler_params=_params("parallel", "parallel", "arbitrary"),
        name=f"dil_attn_r{dilation}",
    )(view, view, view, view, view)


def _dil_res_kernel(o0, o1, o2, l0, l1, l2, w_ref, x_ref, g_ref, y_ref):
    ls = (l0[0], l1[0], l2[0])
    m = jnp.maximum(jnp.maximum(ls[0], ls[1]), ls[2])
    es = [jnp.exp(l - m) for l in ls]
    total = es[0] + es[1] + es[2]
    ws = [e / total for e in es]
    hd = LANES // 2
    first = lax.broadcasted_iota(jnp.int32, (x_ref.shape[1], LANES), 1) < hd
    parts = []
    for p in range(o0.shape[1]):
        w = [jnp.where(first, wg[:, 2 * p:2 * p + 1], wg[:, 2 * p + 1:2 * p + 2]) for wg in ws]
        parts.append(w[0] * o0[0, p] + w[1] * o1[0, p] + w[2] * o2[0, p])
    o = jnp.concatenate(parts, axis=1)
    y_ref[0] = x_ref[0] + g_ref[0] * _dot(o.astype(BF16), w_ref[...])


def _dil_res(outs, lses, w_out, x, gate, tm=512):
    b, s, d = x.shape
    k = w_out.shape[0]
    tm = min(tm, s)
    part = pl.BlockSpec((1, k // LANES, tm, LANES), lambda bi, i: (bi, 0, i, 0))
    lse = pl.BlockSpec((1, tm, LANES), lambda bi, i: (bi, i, 0))
    return pl.pallas_call(
        _dil_res_kernel,
        grid=(b, s // tm),
        in_specs=[part] * 3 + [lse] * 3 + [
            pl.BlockSpec((k, d), lambda bi, i: (0, 0)),
            pl.BlockSpec((1, tm, d), lambda bi, i: (bi, i, 0)),
            pl.BlockSpec((1, 1, d), lambda bi, i: (bi, 0, 0)),
        ],
        out_specs=pl.BlockSpec((1, tm, d), lambda bi, i: (bi, i, 0)),
        out_shape=jax.ShapeDtypeStruct((b, s, d), F32),
        compiler_params=_params("parallel", "parallel"),
        name="dil_res",
    )(*outs, *lses, w_out, x, gate)


def _router_kernel(x_ref, g_ref, sc_ref, sh_ref, wr_ref, br_ref, h_ref, route_ref, cnt_ref, run_ref, earlier_ref):
    tm = x_ref.shape[1]

    @pl.when(jnp.logical_and(pl.program_id(0) == 0, pl.program_id(1) == 0))
    def _():
        run_ref[...] = jnp.zeros_like(run_ref)
        row = lax.broadcasted_iota(jnp.int32, (tm, tm), 0)
        col = lax.broadcasted_iota(jnp.int32, (tm, tm), 1)
        earlier_ref[...] = jnp.where(col < row, 1.0, 0.0).astype(BF16)

    h = _mod_norm(x_ref[0], g_ref[...], sc_ref[0], sh_ref[0])
    _pack_slabs(h_ref, h)
    logits = _dot(h.astype(BF16), wr_ref[...].astype(BF16)) + br_ref[...]
    lane = lax.broadcasted_iota(jnp.int32, logits.shape, 1).astype(F32)
    neg = -jnp.inf
    far = float(LANES)

    def first_argmax(vals):
        top = jnp.max(vals, axis=1, keepdims=True)
        return top, jnp.min(jnp.where(vals == top, lane, far), axis=1, keepdims=True)

    grp_logits = jnp.where(lane < MOE_GROUPS, logits, neg)
    grp_max, grp = first_argmax(grp_logits)
    p_grp = 1.0 / jnp.sum(jnp.exp(grp_logits - grp_max), axis=1, keepdims=True)
    base = MOE_GROUPS + grp * MOE_PER_GROUP
    in_grp = jnp.logical_and(lane >= base, lane < base + MOE_PER_GROUP)
    exp_logits = jnp.where(in_grp, logits, neg)
    m1, i1 = first_argmax(exp_logits)
    m2, i2 = first_argmax(jnp.where(lane == i1, neg, exp_logits))
    e2 = jnp.exp(m2 - m1)
    g1 = p_grp / (1.0 + e2)
    g2 = g1 * e2
    e_a = i1 - MOE_GROUPS
    e_b = i2 - MOE_GROUPS

    pick_a = lane == e_a
    pick_b = lane == e_b
    both = jnp.where(jnp.logical_or(pick_a, pick_b), 1.0, 0.0)
    before = _dot(earlier_ref[...], both.astype(BF16)) + run_ref[...]
    rank_a = jnp.sum(jnp.where(pick_a, before, 0.0), axis=1, keepdims=True)
    rank_b = jnp.sum(jnp.where(pick_b, before, 0.0), axis=1, keepdims=True)
    run_ref[...] += jnp.sum(both, axis=0, keepdims=True)
    cnt_ref[...] = run_ref[...]

    route = jnp.where(lane == 0, e_a, jnp.where(lane == 1, e_b, 0.0))
    route = jnp.where(lane == 2, g1, jnp.where(lane == 3, g2, route))
    route_ref[...] = jnp.where(lane == 4, rank_a, jnp.where(lane == 5, rank_b, route))


def _router(x, g, sc, sh, w_route, b_route, tm=512):
    b, s, d = x.shape
    tm = min(tm, s)
    steps = s // tm
    return pl.pallas_call(
        _router_kernel,
        grid=(b, steps),
        in_specs=[
            pl.BlockSpec((1, tm, d), lambda bi, i: (bi, i, 0)),
            pl.BlockSpec((1, d), lambda bi, i: (0, 0)),
            pl.BlockSpec((1, 1, d), lambda bi, i: (bi, 0, 0)),
            pl.BlockSpec((1, 1, d), lambda bi, i: (bi, 0, 0)),
            pl.BlockSpec((d, LANES), lambda bi, i: (0, 0)),
            pl.BlockSpec((1, LANES), lambda bi, i: (0, 0)),
        ],
        out_specs=[
            pl.BlockSpec((tm * (d // PACKED_LANES), LANES), lambda bi, i: (bi * steps + i, 0)),
            pl.BlockSpec((tm, LANES), lambda bi, i: (bi * steps + i, 0)),
            pl.BlockSpec((1, LANES), lambda bi, i: (0, 0)),
        ],
        out_shape=[
            jax.ShapeDtypeStruct((b * s * (d // PACKED_LANES), LANES), jnp.int32),
            jax.ShapeDtypeStruct((b * s, LANES), F32),
            jax.ShapeDtypeStruct((1, LANES), F32),
        ],
        scratch_shapes=[pltpu.VMEM((1, LANES), F32), pltpu.VMEM((tm, tm), BF16)],
        compiler_params=_params("arbitrary", "arbitrary"),
        name="moe_router",
    )(x, g.reshape(1, d), sc, sh, w_route, b_route)


def _sc_dispatch(h_slabs, dest, n_slots):
    t, chunks, _ = h_slabs.shape
    workers = SC_CORES * SC_SUBCORES
    per_worker = t // workers
    steps = per_worker // SC_WINDOW
    assert t % (workers * SC_WINDOW * 2) == 0
    mesh = plsc.VectorSubcoreMesh(core_axis_name="core", subcore_axis_name="subcore")

    @functools.partial(
        pl.kernel, mesh=mesh,
        out_type=jax.ShapeDtypeStruct((n_slots, chunks, LANES), h_slabs.dtype),
        scratch_types=[
            pltpu.VMEM((2, MOE_TOPK, SC_WINDOW), jnp.int32),
            pltpu.VMEM((2, SC_WINDOW, chunks, LANES), h_slabs.dtype),
            pltpu.SemaphoreType.DMA((2,)),
            pltpu.SemaphoreType.DMA((2,)),
        ])
    def dispatch(h_hbm, dest_hbm, slots_hbm, idx_v, rows_v, load_sem, store_sem):
        worker = lax.axis_index("subcore") * SC_CORES + lax.axis_index("core")
        base = worker * per_worker

        @pl.loop(0, steps, step=2)
        def _(step):
            for buf in range(2):
                off = base + (step + buf) * SC_WINDOW
                pltpu.async_copy(h_hbm.at[pl.ds(off, SC_WINDOW)], rows_v.at[buf], load_sem.at[buf])
                for k in range(MOE_TOPK):
                    pltpu.sync_copy(dest_hbm.at[k, pl.ds(off, SC_WINDOW)], idx_v.at[buf, k])
            for buf in range(2):
                off = base + (step + buf) * SC_WINDOW
                pltpu.make_async_copy(h_hbm.at[pl.ds(off, SC_WINDOW)], rows_v.at[buf], load_sem.at[buf]).wait()
                copies = [pltpu.async_copy(rows_v.at[buf], slots_hbm.at[idx_v.at[buf, k]], store_sem.at[buf])
                          for k in range(MOE_TOPK)]
                for cp in copies:
                    cp.wait()

    return dispatch(h_slabs, dest)


def _zero_tail_kernel(tail_ref, valid_ref, s_in, s_out):
    del tail_ref
    rows = lax.broadcasted_iota(jnp.int32, s_in.shape, 0)
    s_out[...] = jnp.where(rows < valid_ref[pl.program_id(0)], s_in[...], 0)


def _zero_tails(tail_blk, valid_rows, slots, block_rows):
    grid_spec = pltpu.PrefetchScalarGridSpec(
        num_scalar_prefetch=2,
        grid=(tail_blk.shape[0],),
        in_specs=[pl.BlockSpec((block_rows, LANES), lambda e, tail, valid: (tail[e], 0))],
        out_specs=pl.BlockSpec((block_rows, LANES), lambda e, tail, valid: (tail[e], 0)),
    )
    return pl.pallas_call(
        _zero_tail_kernel,
        grid_spec=grid_spec,
        out_shape=jax.ShapeDtypeStruct(slots.shape, slots.dtype),
        input_output_aliases={2: 0},
        compiler_params=_params("arbitrary"),
        name="moe_zero_tails",
    )(tail_blk, valid_rows, slots)


def _expert_kernel(blk_e_ref, used_ref, x_ref, wg_ref, wu_ref, wd_ref, y_ref, wg_bf, wu_bf, wd_bf):
    i = pl.program_id(0)
    live = i < used_ref[0]
    changed = jnp.logical_or(i == 0, blk_e_ref[i] != blk_e_ref[jnp.maximum(i - 1, 0)])

    @pl.when(jnp.logical_and(live, changed))
    def _():
        wg_bf[...] = wg_ref[0, 0].astype(BF16)
        wu_bf[...] = wu_ref[0, 0].astype(BF16)
        wd_bf[...] = wd_ref[0, 0].astype(BF16)

    @pl.when(live)
    def _():
        bm = x_ref.shape[0] // (wg_bf.shape[0] // PACKED_LANES)
        x = _unpack_slabs(x_ref, bm).astype(BF16)
        gate = _dot(x, wg_bf[...])
        up = _dot(x, wu_bf[...])
        act = (gate * _sigmoid(gate) * up).astype(BF16)
        _pack_slabs(y_ref, _dot(act, wd_bf[...]))

    @pl.when(jnp.logical_not(live))
    def _():
        y_ref[...] = jnp.zeros_like(y_ref)


def _expert_blocks(blk_e, used, slots, w_gate, w_up, w_down, layer, bm):
    nb = blk_e.shape[0]
    _, _, d, f = w_gate.shape
    chunks = d // PACKED_LANES

    def block(i, e, n):
        return (jnp.minimum(i, n[0] - 1), 0)

    def weight(i, e, n):
        return (layer, e[jnp.minimum(i, n[0] - 1)], 0, 0)

    grid_spec = pltpu.PrefetchScalarGridSpec(
        num_scalar_prefetch=2,
        grid=(nb,),
        in_specs=[
            pl.BlockSpec((bm * chunks, LANES), block),
            pl.BlockSpec((1, 1, d, f), weight),
            pl.BlockSpec((1, 1, d, f), weight),
            pl.BlockSpec((1, 1, f, d), weight),
        ],
        out_specs=pl.BlockSpec((bm * chunks, LANES), lambda i, e, n: (i, 0)),
        scratch_shapes=[pltpu.VMEM((d, f), BF16), pltpu.VMEM((d, f), BF16), pltpu.VMEM((f, d), BF16)],
    )
    return pl.pallas_call(
        _expert_kernel,
        grid_spec=grid_spec,
        out_shape=jax.ShapeDtypeStruct((nb * bm * chunks, LANES), jnp.int32),
        compiler_params=_params("arbitrary"),
        name="moe_experts",
    )(blk_e, used, slots, w_gate, w_up, w_down)


def _sc_gather(y_slabs, idx):
    n = idx.shape[0]
    _, chunks, _ = y_slabs.shape
    workers = SC_CORES * SC_SUBCORES
    per_worker = n // workers
    steps = per_worker // SC_WINDOW
    assert n % (workers * SC_WINDOW * 2) == 0
    mesh = plsc.VectorSubcoreMesh(core_axis_name="core", subcore_axis_name="subcore")

    @functools.partial(
        pl.kernel, mesh=mesh,
        out_type=jax.ShapeDtypeStruct((n, chunks, LANES), y_slabs.dtype),
        scratch_types=[
            pltpu.VMEM((2, SC_WINDOW), jnp.int32),
            pltpu.VMEM((2, SC_WINDOW, chunks, LANES), y_slabs.dtype),
            pltpu.SemaphoreType.DMA((2,)),
        ])
    def gather(y_hbm, idx_hbm, out_hbm, idx_v, rows_v, sem):
        worker = lax.axis_index("subcore") * SC_CORES + lax.axis_index("core")
        base = worker * per_worker

        @pl.loop(0, steps, step=2)
        def _(step):
            for buf in range(2):
                off = base + (step + buf) * SC_WINDOW
                pltpu.sync_copy(idx_hbm.at[pl.ds(off, SC_WINDOW)], idx_v.at[buf])
                pltpu.async_copy(y_hbm.at[idx_v.at[buf]], rows_v.at[buf], sem.at[buf])
            for buf in range(2):
                off = base + (step + buf) * SC_WINDOW
                pltpu.make_async_copy(y_hbm.at[idx_v.at[buf]], rows_v.at[buf], sem.at[buf]).wait()
                pltpu.sync_copy(rows_v.at[buf], out_hbm.at[pl.ds(off, SC_WINDOW)])

    return gather(y_slabs, idx)


def _combine_kernel(g0_ref, g1_ref, route_ref, x_ref, gate_ref, fg_ref, o_ref, *, final):
    tm = x_ref.shape[0]
    route = route_ref[...]
    y = route[:, 2:3] * _unpack_slabs(g0_ref, tm) + route[:, 3:4] * _unpack_slabs(g1_ref, tm)
    out = x_ref[...] + gate_ref[0] * y
    if final:
        out = out * lax.rsqrt(jnp.mean(out * out, axis=-1, keepdims=True) + RMS_EPS) * fg_ref[...]
    o_ref[...] = out


def _combine(dest, route, y_slabs, x, gate, final_g, final, tm, pieces=8):
    b, s, d = x.shape
    t = b * s
    chunks = d // PACKED_LANES
    per_seq = s // tm
    piece = t // pieces
    tiles = piece // tm
    out = x.reshape(t, d)
    fg = final_g.reshape(1, d)
    for c in range(pieces):
        idx = dest[:, c * piece:(c + 1) * piece].reshape(MOE_TOPK * piece)
        g = _sc_gather(y_slabs, idx).reshape(MOE_TOPK * piece * chunks, LANES)
        first = c * tiles
        out = pl.pallas_call(
            functools.partial(_combine_kernel, final=final),
            grid=(tiles,),
            in_specs=[
                pl.BlockSpec((tm * chunks, LANES), lambda i: (i, 0)),
                pl.BlockSpec((tm * chunks, LANES), lambda i: (tiles + i, 0)),
                pl.BlockSpec((tm, LANES), lambda i, first=first: (first + i, 0)),
                pl.BlockSpec((tm, d), lambda i, first=first: (first + i, 0)),
                pl.BlockSpec((1, 1, d), lambda i, first=first: ((first + i) // per_seq, 0, 0)),
                pl.BlockSpec((1, d), lambda i: (0, 0)),
            ],
            out_specs=pl.BlockSpec((tm, d), lambda i, first=first: (first + i, 0)),
            out_shape=jax.ShapeDtypeStruct((t, d), F32),
            input_output_aliases={3: 0},
            compiler_params=_params("parallel"),
            name="moe_combine",
        )(g, g, route, out, gate, fg)
    return out.reshape(b, s, d)


def _slot_tables(route, counts, bm):
    t = route.shape[0]
    nb = MOE_TOPK * t // bm + MOE_EXPERTS
    counts = counts[0, :MOE_EXPERTS].astype(jnp.int32)
    padded = (counts + bm - 1) // bm * bm
    pad_end = jnp.cumsum(padded)
    pad_start = pad_end - padded
    expert = route[:, 0:MOE_TOPK].astype(jnp.int32)
    rank = route[:, 4:4 + MOE_TOPK].astype(jnp.int32)
    dest = pad_start[expert] + rank
    blk_start = jnp.arange(nb, dtype=jnp.int32) * bm
    blk_e = jnp.minimum(jnp.sum((pad_end[None, :] <= blk_start[:, None]).astype(jnp.int32), axis=1), MOE_EXPERTS - 1)
    used = (pad_end[-1:] // bm).astype(jnp.int32)
    tail_blk = jnp.maximum(pad_end // bm - 1, 0).astype(jnp.int32)
    tail_fill = jnp.where(counts % bm != 0, counts % bm, bm).astype(jnp.int32)
    return dest.T, blk_e, used, tail_blk, tail_fill, nb


def _moe_layer(x, g, sc, sh, gate, w_route, b_route, w_gate, w_up, w_down, layer, final_g, final, bm=512, tm=1024):
    b, s, d = x.shape
    t = b * s
    chunks = d // PACKED_LANES
    tm = min(tm, s)
    h_rows, route, counts = _router(x, g, sc, sh, w_route, b_route)
    dest, blk_e, used, tail_blk, tail_fill, nb = _slot_tables(route, counts, bm)
    slots = _sc_dispatch(h_rows.reshape(t, chunks, LANES), dest, nb * bm)
    slots = _zero_tails(tail_blk, tail_fill * chunks, slots.reshape(nb * bm * chunks, LANES), bm * chunks)
    y_slabs = _expert_blocks(blk_e, used, slots, w_gate, w_up, w_down, layer, bm).reshape(nb * bm, chunks, LANES)
    return _combine(dest, route, y_slabs, x, gate, final_g, final, tm)


def kernel(x, c, ada_w, ada_b, norm_g, final_g, sb_w_in, sb_w_out, gla_w_in, gla_w_gate_up, gla_b_gate, gla_norm_g, gla_w_out, conv_w_in, conv_w, conv_b, conv_w_out, dil_w_in, dil_w_out, moe_w_grp, moe_b_grp, moe_w_exp, moe_b_exp, moe_w_gate, moe_w_up, moe_w_down):
    b, s, d = x.shape
    depth = ada_w.shape[0]
    mod = _ada_mod(c, ada_w, ada_b).reshape(depth, b, N_MOD, 1, d)
    for i in range(depth):
        sh1, sc1, g1, sh2, sc2, g2 = (mod[i, :, m] for m in range(N_MOD))
        kind, j = i % 4, i // 4
        if kind == 0:
            qkv = _norm_proj(x, norm_g[i, 0], sc1, sh1, sb_w_in[j])
            x = _proj_res(_sb_attention(qkv), sb_w_out[j].astype(BF16), x, g1)
        elif kind == 1:
            dk = gla_w_gate_up.shape[-1] // GLA_HEADS
            dv = gla_norm_g.shape[-1]
            n_in = gla_w_in.shape[-1]
            w_in = jnp.pad(gla_w_in[j], ((0, 0), (0, -n_in % MXU_COLUMNS)))
            w_up = jnp.pad(gla_w_gate_up[j], ((0, LANES - GLA_GATE_RANK), (0, 0))).astype(BF16)
            proj = _norm_proj(x, norm_g[i, 0], sc1, sh1, w_in, tn=MXU_COLUMNS)
            o = _gla_core(proj, w_up, gla_b_gate[j].reshape(1, -1), gla_norm_g[j].reshape(1, -1), dk, dv)
            x = _proj_res(o, gla_w_out[j].astype(BF16), x, g1)
        elif kind == 2:
            proj = _norm_proj(x, norm_g[i, 0], sc1, sh1, conv_w_in[j])
            x = _conv_res(proj, conv_w[j], conv_b[j], conv_w_out[j].astype(BF16), x, g1)
        else:
            group_w = 3 * DIL_HEADS * DIL_HEAD_DIM
            outs, lses = [], []
            for g, (window, dilation) in enumerate(DIL_PATTERNS):
                w_g = dil_w_in[j][:, g * group_w:(g + 1) * group_w]
                view = _norm_proj(x, norm_g[i, 0], sc1, sh1, w_g, dilation=dilation)
                o_g, l_g = _dil_group(view, dilation, window, s)
                outs.append(o_g)
                lses.append(l_g)
            x = _dil_res(outs, lses, dil_w_out[j].astype(BF16), x, g1)
        w_route = jnp.pad(jnp.concatenate([moe_w_grp[i], moe_w_exp[i]], axis=1),
                          ((0, 0), (0, LANES - MOE_GROUPS - MOE_EXPERTS)))
        b_route = jnp.pad(jnp.concatenate([moe_b_grp[i], moe_b_exp[i]]), (0, LANES - MOE_GROUPS - MOE_EXPERTS))
        x = _moe_layer(x, norm_g[i, 1], sc2, sh2, g2, w_route, b_route.reshape(1, LANES),
                       moe_w_gate, moe_w_up, moe_w_down, i, final_g, final=(i == depth - 1))
    return x
```

```python
import functools

import jax
import jax.numpy as jnp
from jax import lax
from jax.experimental import pallas as pl
from jax.experimental.pallas import tpu as pltpu
from jax.experimental.pallas import tpu_sc as plsc

F32 = jnp.float32
BF16 = jnp.bfloat16

LANES = 128
MXU_COLUMNS = 256
RMS_EPS = 1e-6
N_MOD = 6

SB_HEADS = 16
GLA_HEADS = 4
GLA_GATE_RANK = 16
GLA_GATE_TAU = 16.0
GLA_CHUNK = 64
CONV_WIDTH = 3
DIL_PATTERNS = ((128, 1), (512, 4), (2048, 16))
DIL_HEADS = 8
DIL_HEAD_DIM = 64
DIL_QBLOCK = 128
DIL_TILE_TOKENS = 2048
MOE_GROUPS = 4
MOE_PER_GROUP = 8
MOE_EXPERTS = MOE_GROUPS * MOE_PER_GROUP
MOE_TOPK = 2

SC_CORES = 2
SC_SUBCORES = 16
SC_WINDOW = 64
PACKED_LANES = 2 * LANES

SB_LOG_ZERO = -104.0
SB_BAND = 2

VMEM_LIMIT = 48 * 1024 * 1024


def _params(*sem):
    return pltpu.CompilerParams(dimension_semantics=sem, vmem_limit_bytes=VMEM_LIMIT)


def _sigmoid(x):
    return 1.0 / (1.0 + jnp.exp(-x))


def _softplus(x):
    return jnp.maximum(x, 0.0) + jnp.log(1.0 + jnp.exp(-jnp.abs(x)))


def _split_bf16(x):
    hi = x.astype(BF16)
    lo = (x - hi.astype(F32)).astype(BF16)
    return hi, lo


def _dot(a, b):
    return jnp.dot(a, b, preferred_element_type=F32)


def _dot_nt(a, b):
    return lax.dot_general(a, b, (((1,), (1,)), ((), ())), preferred_element_type=F32)


def _dot_tn(a, b):
    return lax.dot_general(a, b, (((0,), (0,)), ((), ())), preferred_element_type=F32)


def _unpack_slabs(ref, rows):
    k = ref.shape[0] // rows
    words = jnp.concatenate([ref[pl.ds(s, rows, stride=k), :] for s in range(k)], axis=1)
    halves = [pltpu.unpack_elementwise(words, index=i, packed_dtype=BF16, unpacked_dtype=F32) for i in range(2)]
    return jnp.concatenate(halves, axis=1)


def _pack_slabs(ref, values):
    rows, d = values.shape
    half = d // 2
    words = pltpu.pack_elementwise([values[:, :half], values[:, half:]], packed_dtype=BF16)
    for s in range(half // LANES):
        ref[pl.ds(s, rows, stride=half // LANES), :] = words[:, s * LANES:(s + 1) * LANES]


def _mod_norm(x, g, sc, sh):
    r = lax.rsqrt(jnp.mean(x * x, axis=-1, keepdims=True) + RMS_EPS)
    return (x * r) * g * (1.0 + sc) + sh


def _ada_kernel(c_ref, w_ref, b_ref, o_ref):
    c = c_ref[...]
    cond = c * _sigmoid(c)
    o_ref[0] = _dot(cond.astype(BF16), w_ref[0].astype(BF16)) + b_ref[0]


def _ada_mod(c, ada_w, ada_b):
    depth, d, n = ada_w.shape
    b = c.shape[0]
    tn = 1536
    return pl.pallas_call(
        _ada_kernel,
        grid=(depth, n // tn),
        in_specs=[
            pl.BlockSpec((b, d), lambda i, j: (0, 0)),
            pl.BlockSpec((1, d, tn), lambda i, j: (i, 0, j)),
            pl.BlockSpec((1, 1, tn), lambda i, j: (i, 0, j)),
        ],
        out_specs=pl.BlockSpec((1, b, tn), lambda i, j: (i, 0, j)),
        out_shape=jax.ShapeDtypeStruct((depth, b, n), F32),
        compiler_params=_params("parallel", "parallel"),
        name="ada_mod",
    )(c, ada_w, ada_b.reshape(depth, 1, n))


def _norm_proj_kernel(x_ref, g_ref, sc_ref, sh_ref, w_ref, o_ref, h_ref, *res_ref, dilation, tn):
    h_ref[...] = _mod_norm(x_ref[0], g_ref[...], sc_ref[0], sh_ref[0]).astype(BF16)
    tm = h_ref.shape[0]
    for j in range(w_ref.shape[1] // tn):
        res = _dot(h_ref[...], w_ref[:, j * tn:(j + 1) * tn])
        if dilation == 1:
            o_ref[0, :, j * tn:(j + 1) * tn] = res.astype(o_ref.dtype)
            continue
        for c in range(tn // LANES):
            res_ref[0][j * (tn // LANES) + c] = res[:, c * LANES:(c + 1) * LANES]
        for rho in range(dilation):
            for c in range(tn // LANES):
                rows = res_ref[0][j * (tn // LANES) + c, pl.ds(rho, tm // dilation, stride=dilation), :]
                col = (j * dilation + rho) * tn + c * LANES
                o_ref[0, :, col:col + LANES] = rows.astype(o_ref.dtype)


def _norm_proj(x, g, sc, sh, w, tm=512, tn=512, dilation=1):
    b, s, d = x.shape
    n = w.shape[1]
    tm = min(tm, s)
    assert s % tm == 0 and n % tn == 0 and tm % (16 * dilation) == 0
    scratch = [pltpu.VMEM((tm, d), BF16)]
    if dilation > 1:
        scratch.append(pltpu.VMEM((n // LANES, tm, LANES), F32))
    return pl.pallas_call(
        functools.partial(_norm_proj_kernel, dilation=dilation, tn=tn),
        grid=(b, s // tm),
        in_specs=[
            pl.BlockSpec((1, tm, d), lambda bi, i: (bi, i, 0)),
            pl.BlockSpec((1, d), lambda bi, i: (0, 0)),
            pl.BlockSpec((1, 1, d), lambda bi, i: (bi, 0, 0)),
            pl.BlockSpec((1, 1, d), lambda bi, i: (bi, 0, 0)),
            pl.BlockSpec((d, n), lambda bi, i: (0, 0)),
        ],
        out_specs=pl.BlockSpec((1, tm // dilation, dilation * n), lambda bi, i: (bi, i, 0)),
        out_shape=jax.ShapeDtypeStruct((b, s // dilation, dilation * n), BF16),
        scratch_shapes=scratch,
        compiler_params=_params("parallel", "parallel"),
        name="norm_proj",
    )(x, g.reshape(1, d), sc, sh, w)


def _proj_res_kernel(o_ref, w_ref, x_ref, g_ref, y_ref):
    y_ref[0] = x_ref[0] + g_ref[0] * _dot(o_ref[0], w_ref[...])


def _proj_res(o, w, x, gate, tm=1024):
    b, s, d = x.shape
    k = o.shape[-1]
    tm = min(tm, s)
    return pl.pallas_call(
        _proj_res_kernel,
        grid=(b, s // tm),
        in_specs=[
            pl.BlockSpec((1, tm, k), lambda bi, i: (bi, i, 0)),
            pl.BlockSpec((k, d), lambda bi, i: (0, 0)),
            pl.BlockSpec((1, tm, d), lambda bi, i: (bi, i, 0)),
            pl.BlockSpec((1, 1, d), lambda bi, i: (bi, 0, 0)),
        ],
        out_specs=pl.BlockSpec((1, tm, d), lambda bi, i: (bi, i, 0)),
        out_shape=jax.ShapeDtypeStruct((b, s, d), F32),
        compiler_params=_params("parallel", "parallel"),
        name="proj_res",
    )(o, w, x, gate)


def _sb_kernel(q_ref, k_ref, v_ref, o_ref, acc_ref, c_ref, *, blk, hd):
    scale = hd ** -0.5
    fold_scale = (hd & (hd - 1)) == 0 and (hd.bit_length() - 1) % 2 == 0
    first = lax.broadcasted_iota(jnp.int32, (blk, LANES), 1) < hd
    row = lax.broadcasted_iota(jnp.int32, (blk, blk), 0)
    col = lax.broadcasted_iota(jnp.int32, (blk, blk), 1)
    minus_later = jnp.where(row > col, -1.0, 0.0).astype(BF16)
    causal = col < row
    subs = q_ref.shape[1] // blk
    first_block = pl.program_id(2) * subs

    def visit_many(work):
        kv = []
        for _, block, _, _ in work:
            start = pl.multiple_of(block * blk, blk)
            kv.append((k_ref[0, pl.ds(start, blk), :], v_ref[0, pl.ds(start, blk), :]))
        zs = [[_dot_nt(q_heads[h], kb) for h in range(2)] for (q_heads, _, _, _), (kb, _) in zip(work, kv)]
        log_betas, sps = [], []
        for (_, _, mask, _), z2 in zip(work, zs):
            lb2, sp2 = [], []
            for z in z2:
                if not fold_scale:
                    z = z * scale
                sp = _softplus(z)
                lb2.append(z - sp)
                sp2.append(sp if mask is None else jnp.where(mask, sp, 0.0))
            log_betas.append(lb2)
            sps.append(sp2)
        suffixes = [[_dot(sp.astype(BF16), minus_later) for sp in sp2] for sp2 in sps]
        return kv, log_betas, sps, suffixes

    def finish(work, staged, c_in):
        kv, log_betas, sps, suffixes = staged
        cs = {chain: list(c) for chain, c in c_in.items()}
        weights = []
        for n, (_, _, mask, chain) in enumerate(work):
            a2 = []
            for h in range(2):
                a = jnp.exp(log_betas[n][h] + suffixes[n][h] + cs[chain][h])
                a2.append(a if mask is None else jnp.where(mask, a, 0.0))
                cs[chain][h] = cs[chain][h] - jnp.sum(sps[n][h], axis=1, keepdims=True)
            weights.append(a2)
        accs = {}
        for n, (_, _, _, chain) in enumerate(work):
            outs = [_dot(weights[n][h].astype(BF16), kv[n][1]) for h in range(2)]
            more = jnp.where(first, outs[0], outs[1])
            accs[chain] = more if chain not in accs else accs[chain] + more
        return accs, cs

    def remainder(sub, q_heads, depth):
        def cond(carry):
            j, cmax = carry
            return jnp.logical_and(j >= 0, cmax > SB_LOG_ZERO)

        def body(carry):
            j, _ = carry
            work = [(q_heads, j, None, sub)]
            accs, cs = finish(work, visit_many(work), {sub: [c_ref[sub, 0], c_ref[sub, 1]]})
            acc_ref[sub] += accs[sub]
            for h in range(2):
                c_ref[sub, h] = cs[sub][h]
            return j - 1, jnp.max(c_ref[sub])

        lax.while_loop(cond, body, (first_block + sub - depth - 1, jnp.max(c_ref[sub])))

    def run(depths):
        heads = []
        for sub in range(subs):
            q = q_ref[0, sub * blk:(sub + 1) * blk, :]
            if fold_scale:
                q = (q.astype(F32) * scale).astype(BF16)
            zero = jnp.zeros_like(q)
            heads.append((jnp.where(first, q, zero), jnp.where(first, zero, q)))
        work = [(heads[sub], first_block + sub - back, causal if back == 0 else None, sub)
                for sub in range(subs) for back in range(depths[sub] + 1)]
        zero_c = [jnp.zeros((blk, 1), F32)] * 2
        accs, cs = finish(work, visit_many(work), {sub: zero_c for sub in range(subs)})
        for sub in range(subs):
            acc_ref[sub] = accs[sub]
            for h in range(2):
                c_ref[sub, h] = cs[sub][h]
        @pl.when(jnp.max(c_ref[...]) > SB_LOG_ZERO)
        def _():
            for sub in range(subs):
                remainder(sub, heads[sub], depths[sub])

        for sub in range(subs):
            o_ref[0, sub * blk:(sub + 1) * blk, :] = acc_ref[sub].astype(o_ref.dtype)

    assert subs >= SB_BAND
    pl.when(first_block > 0)(functools.partial(run, [SB_BAND] * subs))
    pl.when(first_block == 0)(functools.partial(run, [min(SB_BAND, sub) for sub in range(subs)]))


def _sb_attention(qkv, blk=128, rows=2048):
    b, s, n = qkv.shape
    width = n // 3
    hd = width // SB_HEADS
    assert 2 * hd == LANES
    pairs = width // LANES
    rows = min(rows, s)
    return pl.pallas_call(
        functools.partial(_sb_kernel, blk=blk, hd=hd),
        grid=(b, pairs, s // rows),
        in_specs=[
            pl.BlockSpec((1, rows, LANES), lambda bi, p, i: (bi, i, p)),
            pl.BlockSpec((1, s, LANES), lambda bi, p, i: (bi, 0, pairs + p)),
            pl.BlockSpec((1, s, LANES), lambda bi, p, i: (bi, 0, 2 * pairs + p)),
        ],
        out_specs=pl.BlockSpec((1, rows, LANES), lambda bi, p, i: (bi, i, p)),
        out_shape=jax.ShapeDtypeStruct((b, s, width), BF16),
        scratch_shapes=[pltpu.VMEM((rows // blk, blk, LANES), F32), pltpu.VMEM((rows // blk, 2, blk, 1), F32)],
        compiler_params=_params("parallel", "parallel", "arbitrary"),
        name="sb_attention",
    )(qkv, qkv, qkv)


def _gla_kernel(p_ref, wg_ref, bg_ref, ng_ref, o_ref, st_ref, *, rows, dk, dv):
    @pl.when(pl.program_id(1) == 0)
    def _():
        st_ref[...] = jnp.zeros_like(st_ref)

    heads = GLA_HEADS
    ch = GLA_CHUNK
    k0 = heads * dk
    v0 = 2 * heads * dk
    r0 = v0 + heads * dv
    g0 = r0 + heads * dv
    row = lax.broadcasted_iota(jnp.int32, (ch, ch), 0)
    col = lax.broadcasted_iota(jnp.int32, (ch, ch), 1)
    causal = col <= row
    lower = jnp.where(causal, 1.0, 0.0).astype(BF16)
    chunks = [slice(s * ch, (s + 1) * ch) for s in range(rows // ch)]
    g_pres = [_dot(p_ref[0, t, g0:g0 + LANES], wg_ref[...]) + bg_ref[...] for t in chunks]
    cums = []
    for g_pre in g_pres:
        log_a = (jnp.minimum(g_pre, 0.0) - jnp.log(1.0 + jnp.exp(-jnp.abs(g_pre)))) * (1.0 / GLA_GATE_TAU)
        hi, lo = _split_bf16(log_a)
        cums.append(_dot(lower, hi) + _dot(lower, lo))
    parts = []
    for t, cum in zip(chunks, cums):
        for h in range(heads):
            bh = cum[:, h * dk:(h + 1) * dk]
            b_last = bh[ch - 1:ch, :]
            q = p_ref[0, t, h * dk:(h + 1) * dk].astype(F32) * dk ** -0.5
            k = p_ref[0, t, k0 + h * dk:k0 + (h + 1) * dk].astype(F32)
            q_in = (q * jnp.exp(bh)).astype(BF16)
            k_in = (k * jnp.exp(-bh)).astype(BF16)
            k_state = (k * jnp.exp(b_last - bh)).astype(BF16)
            parts.append((t, h, q_in, k_in, k_state, jnp.exp(b_last)))
    scores = [jnp.where(causal, _dot_nt(q_in, k_in), 0.0).astype(BF16) for _, _, q_in, k_in, _, _ in parts]
    values = [p_ref[0, t, v0 + h * dv:v0 + (h + 1) * dv] for t, h, _, _, _, _ in parts]
    intra = [_dot(sc, v) for sc, v in zip(scores, values)]
    updates = [_dot_tn(v, k_state) for v, (_, _, _, _, k_state, _) in zip(values, parts)]
    states = [st_ref[h] for h in range(heads)]
    for n, (t, h, q_in, _, _, decay) in enumerate(parts):
        o = intra[n] + _dot_nt(q_in, states[h].astype(BF16))
        states[h] = decay * states[h] + updates[n]
        o = o * lax.rsqrt(jnp.mean(o * o, axis=-1, keepdims=True) + RMS_EPS) * ng_ref[...]
        r = p_ref[0, t, r0 + h * dv:r0 + (h + 1) * dv].astype(F32)
        o_ref[0, t, h * dv:(h + 1) * dv] = (o * (r * _sigmoid(r))).astype(o_ref.dtype)
    for h in range(heads):
        st_ref[h] = states[h]


def _gla_core(proj, w_gate_up, b_gate, norm_g, dk, dv, rows=256):
    b, s, n = proj.shape
    rows = min(rows, s)
    hk = GLA_HEADS * dk
    return pl.pallas_call(
        functools.partial(_gla_kernel, rows=rows, dk=dk, dv=dv),
        grid=(b, s // rows),
        in_specs=[
            pl.BlockSpec((1, rows, n), lambda bi, i: (bi, i, 0)),
            pl.BlockSpec((LANES, hk), lambda bi, i: (0, 0)),
            pl.BlockSpec((1, hk), lambda bi, i: (0, 0)),
            pl.BlockSpec((1, dv), lambda bi, i: (0, 0)),
        ],
        out_specs=pl.BlockSpec((1, rows, GLA_HEADS * dv), lambda bi, i: (bi, i, 0)),
        out_shape=jax.ShapeDtypeStruct((b, s, GLA_HEADS * dv), BF16),
        scratch_shapes=[pltpu.VMEM((GLA_HEADS, dv, dk), F32)],
        compiler_params=_params("parallel", "arbitrary"),
        name="gla_core",
    )(proj, w_gate_up, b_gate, norm_g)


def _conv_res_kernel(gb_ref, gc_ref, u_ref, hgc_ref, hu_ref, cw_ref, cb_ref, w_ref, x_ref, g_ref, y_ref):
    u2 = gc_ref[0].astype(F32) * u_ref[0].astype(F32)
    halo = hgc_ref[0].astype(F32) * hu_ref[0].astype(F32)
    halo = jnp.where(pl.program_id(1) > 0, halo, 0.0)
    rows = lax.broadcasted_iota(jnp.int32, u2.shape, 0)
    prev1 = jnp.where(rows == 0, halo[7:8], pltpu.roll(u2, 1, 0))
    prev2 = jnp.where(rows == 0, halo[6:7], jnp.where(rows == 1, halo[7:8], pltpu.roll(u2, 2, 0)))
    cw = cw_ref[...]
    y = cw[0:1] * prev2 + cw[1:2] * prev1 + cw[2:3] * u2 + cb_ref[...]
    o = (gb_ref[0].astype(F32) * y).astype(BF16)
    y_ref[0] = x_ref[0] + g_ref[0] * _dot(o, w_ref[...])


def _conv_res(proj, conv_w, conv_b, w_out, x, gate, tm=512):
    b, s, d = x.shape
    tm = min(tm, s)
    halo_blocks = tm // 8

    def halo_map(col):
        return lambda bi, i: (bi, jnp.maximum(i * halo_blocks - 1, 0), col)

    return pl.pallas_call(
        _conv_res_kernel,
        grid=(b, s // tm),
        in_specs=[
            pl.BlockSpec((1, tm, d), lambda bi, i: (bi, i, 0)),
            pl.BlockSpec((1, tm, d), lambda bi, i: (bi, i, 1)),
            pl.BlockSpec((1, tm, d), lambda bi, i: (bi, i, 2)),
            pl.BlockSpec((1, 8, d), halo_map(1)),
            pl.BlockSpec((1, 8, d), halo_map(2)),
            pl.BlockSpec((CONV_WIDTH, d), lambda bi, i: (0, 0)),
            pl.BlockSpec((1, d), lambda bi, i: (0, 0)),
            pl.BlockSpec((d, d), lambda bi, i: (0, 0)),
            pl.BlockSpec((1, tm, d), lambda bi, i: (bi, i, 0)),
            pl.BlockSpec((1, 1, d), lambda bi, i: (bi, 0, 0)),
        ],
        out_specs=pl.BlockSpec((1, tm, d), lambda bi, i: (bi, i, 0)),
        out_shape=jax.ShapeDtypeStruct((b, s, d), F32),
        compiler_params=_params("parallel", "parallel"),
        name="conv_res",
    )(proj, proj, proj, proj, proj, conv_w, conv_b.reshape(1, d), w_out, x, gate)


def _dil_kernel(q_ref, kc_ref, vc_ref, kp_ref, vp_ref, o_ref, l_ref, *, rows, hd, dilation, group):
    qb = DIL_QBLOCK
    n = pl.program_id(1)
    width = q_ref.shape[-1] // group
    scale = hd ** -0.5
    fold_scale = (hd & (hd - 1)) == 0 and (hd.bit_length() - 1) % 2 == 0
    first = lax.broadcasted_iota(jnp.int32, (qb, LANES), 1) < hd
    row = lax.broadcasted_iota(jnp.int32, (qb, qb), 0)
    col = lax.broadcasted_iota(jnp.int32, (qb, qb), 1)
    in_prev = col >= row
    in_cur = col <= row
    neg = -jnp.inf

    tiles = []
    for u, s in [(u, s) for u in range(group) for s in range(rows // qb)]:
        t = slice(s * qb, (s + 1) * qb)
        tp = slice((s - 1) * qb, s * qb)
        prev_mask = jnp.logical_and(in_prev, n > 0) if s == 0 else in_prev
        for p in range(width // LANES):
            c = slice(u * width + p * LANES, u * width + (p + 1) * LANES)
            q = q_ref[0, t, c]
            if fold_scale:
                q = (q.astype(F32) * scale).astype(BF16)
            zero = jnp.zeros_like(q)
            q_heads = (jnp.where(first, q, zero), jnp.where(first, zero, q))
            if s == 0:
                k_prev, v_prev = kp_ref[0, :, c], vp_ref[0, :, c]
            else:
                k_prev, v_prev = kc_ref[0, tp, c], vc_ref[0, tp, c]
            tiles.append(((u, s), p, q_heads, k_prev, kc_ref[0, t, c], v_prev, vc_ref[0, t, c], prev_mask))

    scores = []
    for _, _, q_heads, k_prev, k_cur, _, _, prev_mask in tiles:
        pair = []
        for h in range(2):
            z_prev, z_cur = _dot_nt(q_heads[h], k_prev), _dot_nt(q_heads[h], k_cur)
            if not fold_scale:
                z_prev, z_cur = z_prev * scale, z_cur * scale
            pair.append((jnp.where(prev_mask, z_prev, neg), jnp.where(in_cur, z_cur, neg)))
        scores.append(pair)

    probs = []
    for pair in scores:
        stats = []
        for z_prev, z_cur in pair:
            m = jnp.max(jnp.maximum(z_prev, z_cur), axis=1, keepdims=True)
            p_prev = jnp.exp(z_prev - m)
            p_cur = jnp.exp(z_cur - m)
            den = jnp.sum(p_prev + p_cur, axis=1, keepdims=True)
            stats.append((p_prev.astype(BF16), p_cur.astype(BF16), den, m))
        probs.append(stats)

    lane = lax.broadcasted_iota(jnp.int32, (qb, LANES), 1)
    lse_rows = {}
    for ((u, s), p, _, _, _, v_prev, v_cur, _), stats in zip(tiles, probs):
        if dilation == 1:
            t_out = slice(s * qb, (s + 1) * qb)
        else:
            rho = pl.program_id(2) * group + u
            t_out = pl.ds(s * qb * dilation + rho, qb, stride=dilation)
        outs = []
        lse = lse_rows.get((u, s), jnp.zeros((qb, LANES), F32))
        for h, (p_prev, p_cur, den, m) in enumerate(stats):
            outs.append((_dot(p_prev, v_prev) + _dot(p_cur, v_cur)) / den)
            lse = jnp.where(lane == 2 * p + h, m + jnp.log(den), lse)
        lse_rows[(u, s)] = lse
        o_ref[0, p, t_out, :] = jnp.where(first, outs[0], outs[1])
        if p == width // LANES - 1:
            l_ref[0, t_out, :] = lse


def _dil_group(view, dilation, window, seq):
    b, length, _ = view.shape
    width = DIL_HEADS * DIL_HEAD_DIM
    assert window == dilation * DIL_QBLOCK and seq % (dilation * DIL_QBLOCK) == 0
    rows = min(max(DIL_TILE_TOKENS // dilation, DIL_QBLOCK), 512, length)
    group = min(dilation, max(512 // rows, 1))
    prev_blocks = rows // DIL_QBLOCK
    per_part = dilation // group

    def cur(part):
        return lambda bi, i, j: (bi, i, part * per_part + j)

    def prev(part):
        return lambda bi, i, j: (bi, jnp.maximum(i * prev_blocks - 1, 0), part * per_part + j)

    pairs = width // LANES
    out_shape = jax.ShapeDtypeStruct((b, pairs, seq, LANES), F32)
    out_spec = pl.BlockSpec((1, pairs, rows * dilation, LANES), lambda bi, i, j: (bi, 0, i, 0))
    lse_shape = jax.ShapeDtypeStruct((b, seq, LANES), F32)
    lse_spec = pl.BlockSpec((1, rows * dilation, LANES), lambda bi, i, j: (bi, i, 0))
    return pl.pallas_call(
        functools.partial(_dil_kernel, rows=rows, hd=DIL_HEAD_DIM, dilation=dilation, group=group),
        grid=(b, length // rows, per_part),
        in_specs=[
            pl.BlockSpec((1, rows, group * width), cur(0)),
            pl.BlockSpec((1, rows, group * width), cur(1)),
            pl.BlockSpec((1, rows, group * width), cur(2)),
            pl.BlockSpec((1, DIL_QBLOCK, group * width), prev(1)),
            pl.BlockSpec((1, DIL_QBLOCK, group * width), prev(2)),
        ],
        out_specs=[out_spec, lse_spec],
        out_shape=[out_shape, lse_shape],
        compiler_params=_params("parallel", "parallel", "arbitrary"),
        name=f"dil_attn_r{dilation}",
    )(view, view, view, view, view)


def _dil_res_kernel(o0, o1, o2, l0, l1, l2, w_ref, x_ref, g_ref, y_ref):
    ls = (l0[0], l1[0], l2[0])
    m = jnp.maximum(jnp.maximum(ls[0], ls[1]), ls[2])
    es = [jnp.exp(l - m) for l in ls]
    total = es[0] + es[1] + es[2]
    ws = [e / total for e in es]
    hd = LANES // 2
    first = lax.broadcasted_iota(jnp.int32, (x_ref.shape[1], LANES), 1) < hd
    parts = []
    for p in range(o0.shape[1]):
        w = [jnp.where(first, wg[:, 2 * p:2 * p + 1], wg[:, 2 * p + 1:2 * p + 2]) for wg in ws]
        parts.append(w[0] * o0[0, p] + w[1] * o1[0, p] + w[2] * o2[0, p])
    o = jnp.concatenate(parts, axis=1)
    y_ref[0] = x_ref[0] + g_ref[0] * _dot(o.astype(BF16), w_ref[...])


def _dil_res(outs, lses, w_out, x, gate, tm=1024):
    b, s, d = x.shape
    k = w_out.shape[0]
    tm = min(tm, s)
    part = pl.BlockSpec((1, k // LANES, tm, LANES), lambda bi, i: (bi, 0, i, 0))
    lse = pl.BlockSpec((1, tm, LANES), lambda bi, i: (bi, i, 0))
    return pl.pallas_call(
        _dil_res_kernel,
        grid=(b, s // tm),
        in_specs=[part] * 3 + [lse] * 3 + [
            pl.BlockSpec((k, d), lambda bi, i: (0, 0)),
            pl.BlockSpec((1, tm, d), lambda bi, i: (bi, i, 0)),
            pl.BlockSpec((1, 1, d), lambda bi, i: (bi, 0, 0)),
        ],
        out_specs=pl.BlockSpec((1, tm, d), lambda bi, i: (bi, i, 0)),
        out_shape=jax.ShapeDtypeStruct((b, s, d), F32),
        compiler_params=_params("parallel", "parallel"),
        name="dil_res",
    )(*outs, *lses, w_out, x, gate)


def _router_kernel(x_ref, g_ref, sc_ref, sh_ref, wr_ref, br_ref, h_ref, route_ref, cnt_ref, run_ref, earlier_ref):
    tm = x_ref.shape[1]

    @pl.when(jnp.logical_and(pl.program_id(0) == 0, pl.program_id(1) == 0))
    def _():
        run_ref[...] = jnp.zeros_like(run_ref)
        row = lax.broadcasted_iota(jnp.int32, (tm, tm), 0)
        col = lax.broadcasted_iota(jnp.int32, (tm, tm), 1)
        earlier_ref[...] = jnp.where(col < row, 1.0, 0.0).astype(BF16)

    h = _mod_norm(x_ref[0], g_ref[...], sc_ref[0], sh_ref[0])
    _pack_slabs(h_ref, h)
    logits = _dot(h.astype(BF16), wr_ref[...].astype(BF16)) + br_ref[...]
    lane = lax.broadcasted_iota(jnp.int32, logits.shape, 1).astype(F32)
    neg = -jnp.inf
    far = float(LANES)

    def first_argmax(vals):
        top = jnp.max(vals, axis=1, keepdims=True)
        return top, jnp.min(jnp.where(vals == top, lane, far), axis=1, keepdims=True)

    grp_logits = jnp.where(lane < MOE_GROUPS, logits, neg)
    grp_max, grp = first_argmax(grp_logits)
    p_grp = 1.0 / jnp.sum(jnp.exp(grp_logits - grp_max), axis=1, keepdims=True)
    base = MOE_GROUPS + grp * MOE_PER_GROUP
    in_grp = jnp.logical_and(lane >= base, lane < base + MOE_PER_GROUP)
    exp_logits = jnp.where(in_grp, logits, neg)
    m1, i1 = first_argmax(exp_logits)
    m2, i2 = first_argmax(jnp.where(lane == i1, neg, exp_logits))
    e2 = jnp.exp(m2 - m1)
    g1 = p_grp / (1.0 + e2)
    g2 = g1 * e2
    e_a = i1 - MOE_GROUPS
    e_b = i2 - MOE_GROUPS

    pick_a = lane == e_a
    pick_b = lane == e_b
    both = jnp.where(jnp.logical_or(pick_a, pick_b), 1.0, 0.0)
    before = _dot(earlier_ref[...], both.astype(BF16)) + run_ref[...]
    rank_a = jnp.sum(jnp.where(pick_a, before, 0.0), axis=1, keepdims=True)
    rank_b = jnp.sum(jnp.where(pick_b, before, 0.0), axis=1, keepdims=True)
    run_ref[...] += jnp.sum(both, axis=0, keepdims=True)
    cnt_ref[...] = run_ref[...]

    route = jnp.where(lane == 0, e_a, jnp.where(lane == 1, e_b, 0.0))
    route = jnp.where(lane == 2, g1, jnp.where(lane == 3, g2, route))
    route_ref[...] = jnp.where(lane == 4, rank_a, jnp.where(lane == 5, rank_b, route))


def _router(x, g, sc, sh, w_route, b_route, tm=512):
    b, s, d = x.shape
    tm = min(tm, s)
    steps = s // tm
    return pl.pallas_call(
        _router_kernel,
        grid=(b, steps),
        in_specs=[
            pl.BlockSpec((1, tm, d), lambda bi, i: (bi, i, 0)),
            pl.BlockSpec((1, d), lambda bi, i: (0, 0)),
            pl.BlockSpec((1, 1, d), lambda bi, i: (bi, 0, 0)),
            pl.BlockSpec((1, 1, d), lambda bi, i: (bi, 0, 0)),
            pl.BlockSpec((d, LANES), lambda bi, i: (0, 0)),
            pl.BlockSpec((1, LANES), lambda bi, i: (0, 0)),
        ],
        out_specs=[
            pl.BlockSpec((tm * (d // PACKED_LANES), LANES), lambda bi, i: (bi * steps + i, 0)),
            pl.BlockSpec((tm, LANES), lambda bi, i: (bi * steps + i, 0)),
            pl.BlockSpec((1, LANES), lambda bi, i: (0, 0)),
        ],
        out_shape=[
            jax.ShapeDtypeStruct((b * s * (d // PACKED_LANES), LANES), jnp.int32),
            jax.ShapeDtypeStruct((b * s, LANES), F32),
            jax.ShapeDtypeStruct((1, LANES), F32),
        ],
        scratch_shapes=[pltpu.VMEM((1, LANES), F32), pltpu.VMEM((tm, tm), BF16)],
        compiler_params=_params("arbitrary", "arbitrary"),
        name="moe_router",
    )(x, g.reshape(1, d), sc, sh, w_route, b_route)


def _sc_dispatch(h_slabs, dest, n_slots):
    t, chunks, _ = h_slabs.shape
    workers = SC_CORES * SC_SUBCORES
    per_worker = t // workers
    steps = per_worker // SC_WINDOW
    assert t % (workers * SC_WINDOW * 2) == 0
    mesh = plsc.VectorSubcoreMesh(core_axis_name="core", subcore_axis_name="subcore")

    @functools.partial(
        pl.kernel, mesh=mesh,
        out_type=jax.ShapeDtypeStruct((n_slots, chunks, LANES), h_slabs.dtype),
        scratch_types=[
            pltpu.VMEM((2, MOE_TOPK, SC_WINDOW), jnp.int32),
            pltpu.VMEM((2, SC_WINDOW, chunks, LANES), h_slabs.dtype),
            pltpu.SemaphoreType.DMA((2,)),
            pltpu.SemaphoreType.DMA((2,)),
        ])
    def dispatch(h_hbm, dest_hbm, slots_hbm, idx_v, rows_v, load_sem, store_sem):
        worker = lax.axis_index("subcore") * SC_CORES + lax.axis_index("core")
        base = worker * per_worker

        @pl.loop(0, steps, step=2)
        def _(step):
            for buf in range(2):
                off = base + (step + buf) * SC_WINDOW
                pltpu.async_copy(h_hbm.at[pl.ds(off, SC_WINDOW)], rows_v.at[buf], load_sem.at[buf])
                for k in range(MOE_TOPK):
                    pltpu.sync_copy(dest_hbm.at[k, pl.ds(off, SC_WINDOW)], idx_v.at[buf, k])
            for buf in range(2):
                off = base + (step + buf) * SC_WINDOW
                pltpu.make_async_copy(h_hbm.at[pl.ds(off, SC_WINDOW)], rows_v.at[buf], load_sem.at[buf]).wait()
                copies = [pltpu.async_copy(rows_v.at[buf], slots_hbm.at[idx_v.at[buf, k]], store_sem.at[buf])
                          for k in range(MOE_TOPK)]
                for cp in copies:
                    cp.wait()

    return dispatch(h_slabs, dest)


def _zero_tail_kernel(tail_ref, valid_ref, s_in, s_out):
    del tail_ref
    rows = lax.broadcasted_iota(jnp.int32, s_in.shape, 0)
    s_out[...] = jnp.where(rows < valid_ref[pl.program_id(0)], s_in[...], 0)


def _zero_tails(tail_blk, valid_rows, slots, block_rows):
    grid_spec = pltpu.PrefetchScalarGridSpec(
        num_scalar_prefetch=2,
        grid=(tail_blk.shape[0],),
        in_specs=[pl.BlockSpec((block_rows, LANES), lambda e, tail, valid: (tail[e], 0))],
        out_specs=pl.BlockSpec((block_rows, LANES), lambda e, tail, valid: (tail[e], 0)),
    )
    return pl.pallas_call(
        _zero_tail_kernel,
        grid_spec=grid_spec,
        out_shape=jax.ShapeDtypeStruct(slots.shape, slots.dtype),
        input_output_aliases={2: 0},
        compiler_params=_params("arbitrary"),
        name="moe_zero_tails",
    )(tail_blk, valid_rows, slots)


def _expert_kernel(blk_e_ref, used_ref, x_ref, wg_ref, wu_ref, wd_ref, y_ref, wg_bf, wu_bf, wd_bf):
    i = pl.program_id(0)
    live = i < used_ref[0]
    changed = jnp.logical_or(i == 0, blk_e_ref[i] != blk_e_ref[jnp.maximum(i - 1, 0)])

    @pl.when(jnp.logical_and(live, changed))
    def _():
        wg_bf[...] = wg_ref[0, 0].astype(BF16)
        wu_bf[...] = wu_ref[0, 0].astype(BF16)
        wd_bf[...] = wd_ref[0, 0].astype(BF16)

    @pl.when(live)
    def _():
        bm = x_ref.shape[0] // (wg_bf.shape[0] // PACKED_LANES)
        x = _unpack_slabs(x_ref, bm).astype(BF16)
        gate = _dot(x, wg_bf[...])
        up = _dot(x, wu_bf[...])
        act = (gate * _sigmoid(gate) * up).astype(BF16)
        _pack_slabs(y_ref, _dot(act, wd_bf[...]))

    @pl.when(jnp.logical_not(live))
    def _():
        y_ref[...] = jnp.zeros_like(y_ref)


def _expert_blocks(blk_e, used, slots, w_gate, w_up, w_down, layer, bm):
    nb = blk_e.shape[0]
    _, _, d, f = w_gate.shape
    chunks = d // PACKED_LANES

    def block(i, e, n):
        return (jnp.minimum(i, n[0] - 1), 0)

    def weight(i, e, n):
        return (layer, e[jnp.minimum(i, n[0] - 1)], 0, 0)

    grid_spec = pltpu.PrefetchScalarGridSpec(
        num_scalar_prefetch=2,
        grid=(nb,),
        in_specs=[
            pl.BlockSpec((bm * chunks, LANES), block),
            pl.BlockSpec((1, 1, d, f), weight),
            pl.BlockSpec((1, 1, d, f), weight),
            pl.BlockSpec((1, 1, f, d), weight),
        ],
        out_specs=pl.BlockSpec((bm * chunks, LANES), lambda i, e, n: (i, 0)),
        scratch_shapes=[pltpu.VMEM((d, f), BF16), pltpu.VMEM((d, f), BF16), pltpu.VMEM((f, d), BF16)],
    )
    return pl.pallas_call(
        _expert_kernel,
        grid_spec=grid_spec,
        out_shape=jax.ShapeDtypeStruct((nb * bm * chunks, LANES), jnp.int32),
        compiler_params=_params("arbitrary"),
        name="moe_experts",
    )(blk_e, used, slots, w_gate, w_up, w_down)


def _sc_gather(y_slabs, idx):
    n = idx.shape[0]
    _, chunks, _ = y_slabs.shape
    workers = SC_CORES * SC_SUBCORES
    per_worker = n // workers
    steps = per_worker // SC_WINDOW
    assert n % (workers * SC_WINDOW * 2) == 0
    mesh = plsc.VectorSubcoreMesh(core_axis_name="core", subcore_axis_name="subcore")

    @functools.partial(
        pl.kernel, mesh=mesh,
        out_type=jax.ShapeDtypeStruct((n, chunks, LANES), y_slabs.dtype),
        scratch_types=[
            pltpu.VMEM((2, SC_WINDOW), jnp.int32),
            pltpu.VMEM((2, SC_WINDOW, chunks, LANES), y_slabs.dtype),
            pltpu.SemaphoreType.DMA((2,)),
        ])
    def gather(y_hbm, idx_hbm, out_hbm, idx_v, rows_v, sem):
        worker = lax.axis_index("subcore") * SC_CORES + lax.axis_index("core")
        base = worker * per_worker

        @pl.loop(0, steps, step=2)
        def _(step):
            for buf in range(2):
                off = base + (step + buf) * SC_WINDOW
                pltpu.sync_copy(idx_hbm.at[pl.ds(off, SC_WINDOW)], idx_v.at[buf])
                pltpu.async_copy(y_hbm.at[idx_v.at[buf]], rows_v.at[buf], sem.at[buf])
            for buf in range(2):
                off = base + (step + buf) * SC_WINDOW
                pltpu.make_async_copy(y_hbm.at[idx_v.at[buf]], rows_v.at[buf], sem.at[buf]).wait()
                pltpu.sync_copy(rows_v.at[buf], out_hbm.at[pl.ds(off, SC_WINDOW)])

    return gather(y_slabs, idx)


def _combine_kernel(g0_ref, g1_ref, route_ref, x_ref, gate_ref, fg_ref, o_ref, *, final):
    tm = x_ref.shape[0]
    route = route_ref[...]
    y = route[:, 2:3] * _unpack_slabs(g0_ref, tm) + route[:, 3:4] * _unpack_slabs(g1_ref, tm)
    out = x_ref[...] + gate_ref[0] * y
    if final:
        out = out * lax.rsqrt(jnp.mean(out * out, axis=-1, keepdims=True) + RMS_EPS) * fg_ref[...]
    o_ref[...] = out


def _combine(dest, route, y_slabs, x, gate, final_g, final, tm, pieces=8):
    b, s, d = x.shape
    t = b * s
    chunks = d // PACKED_LANES
    per_seq = s // tm
    piece = t // pieces
    tiles = piece // tm
    out = x.reshape(t, d)
    fg = final_g.reshape(1, d)
    for c in range(pieces):
        idx = dest[:, c * piece:(c + 1) * piece].reshape(MOE_TOPK * piece)
        g = _sc_gather(y_slabs, idx).reshape(MOE_TOPK * piece * chunks, LANES)
        first = c * tiles
        out = pl.pallas_call(
            functools.partial(_combine_kernel, final=final),
            grid=(tiles,),
            in_specs=[
                pl.BlockSpec((tm * chunks, LANES), lambda i: (i, 0)),
                pl.BlockSpec((tm * chunks, LANES), lambda i: (tiles + i, 0)),
                pl.BlockSpec((tm, LANES), lambda i, first=first: (first + i, 0)),
                pl.BlockSpec((tm, d), lambda i, first=first: (first + i, 0)),
                pl.BlockSpec((1, 1, d), lambda i, first=first: ((first + i) // per_seq, 0, 0)),
                pl.BlockSpec((1, d), lambda i: (0, 0)),
            ],
            out_specs=pl.BlockSpec((tm, d), lambda i, first=first: (first + i, 0)),
            out_shape=jax.ShapeDtypeStruct((t, d), F32),
            input_output_aliases={3: 0},
            compiler_params=_params("parallel"),
            name="moe_combine",
        )(g, g, route, out, gate, fg)
    return out.reshape(b, s, d)


def _slot_tables(route, counts, bm):
    t = route.shape[0]
    nb = MOE_TOPK * t // bm + MOE_EXPERTS
    counts = counts[0, :MOE_EXPERTS].astype(jnp.int32)
    padded = (counts + bm - 1) // bm * bm
    pad_end = jnp.cumsum(padded)
    pad_start = pad_end - padded
    expert = route[:, 0:MOE_TOPK].astype(jnp.int32)
    rank = route[:, 4:4 + MOE_TOPK].astype(jnp.int32)
    dest = pad_start[expert] + rank
    blk_start = jnp.arange(nb, dtype=jnp.int32) * bm
    blk_e = jnp.minimum(jnp.sum((pad_end[None, :] <= blk_start[:, None]).astype(jnp.int32), axis=1), MOE_EXPERTS - 1)
    used = (pad_end[-1:] // bm).astype(jnp.int32)
    tail_blk = jnp.maximum(pad_end // bm - 1, 0).astype(jnp.int32)
    tail_fill = jnp.where(counts % bm != 0, counts % bm, bm).astype(jnp.int32)
    return dest.T, blk_e, used, tail_blk, tail_fill, nb


def _moe_layer(x, g, sc, sh, gate, w_route, b_route, w_gate, w_up, w_down, layer, final_g, final, bm=512, tm=1024):
    b, s, d = x.shape
    t = b * s
    chunks = d // PACKED_LANES
    tm = min(tm, s)
    h_rows, route, counts = _router(x, g, sc, sh, w_route, b_route)
    dest, blk_e, used, tail_blk, tail_fill, nb = _slot_tables(route, counts, bm)
    slots = _sc_dispatch(h_rows.reshape(t, chunks, LANES), dest, nb * bm)
    slots = _zero_tails(tail_blk, tail_fill * chunks, slots.reshape(nb * bm * chunks, LANES), bm * chunks)
    y_slabs = _expert_blocks(blk_e, used, slots, w_gate, w_up, w_down, layer, bm).reshape(nb * bm, chunks, LANES)
    return _combine(dest, route, y_slabs, x, gate, final_g, final, tm)


def kernel(x, c, ada_w, ada_b, norm_g, final_g, sb_w_in, sb_w_out, gla_w_in, gla_w_gate_up, gla_b_gate, gla_norm_g, gla_w_out, conv_w_in, conv_w, conv_b, conv_w_out, dil_w_in, dil_w_out, moe_w_grp, moe_b_grp, moe_w_exp, moe_b_exp, moe_w_gate, moe_w_up, moe_w_down):
    b, s, d = x.shape
    depth = ada_w.shape[0]
    mod = _ada_mod(c, ada_w, ada_b).reshape(depth, b, N_MOD, 1, d)
    for i in range(depth):
        sh1, sc1, g1, sh2, sc2, g2 = (mod[i, :, m] for m in range(N_MOD))
        kind, j = i % 4, i // 4
        if kind == 0:
            qkv = _norm_proj(x, norm_g[i, 0], sc1, sh1, sb_w_in[j].astype(BF16))
            x = _proj_res(_sb_attention(qkv), sb_w_out[j].astype(BF16), x, g1)
        elif kind == 1:
            dk = gla_w_gate_up.shape[-1] // GLA_HEADS
            dv = gla_norm_g.shape[-1]
            n_in = gla_w_in.shape[-1]
            w_in = jnp.pad(gla_w_in[j], ((0, 0), (0, -n_in % MXU_COLUMNS))).astype(BF16)
            w_up = jnp.pad(gla_w_gate_up[j], ((0, LANES - GLA_GATE_RANK), (0, 0))).astype(BF16)
            proj = _norm_proj(x, norm_g[i, 0], sc1, sh1, w_in, tn=MXU_COLUMNS)
            o = _gla_core(proj, w_up, gla_b_gate[j].reshape(1, -1), gla_norm_g[j].reshape(1, -1), dk, dv)
            x = _proj_res(o, gla_w_out[j].astype(BF16), x, g1)
        elif kind == 2:
            proj = _norm_proj(x, norm_g[i, 0], sc1, sh1, conv_w_in[j].astype(BF16))
            x = _conv_res(proj, conv_w[j], conv_b[j], conv_w_out[j].astype(BF16), x, g1)
        else:
            group_w = 3 * DIL_HEADS * DIL_HEAD_DIM
            outs, lses = [], []
            for g, (window, dilation) in enumerate(DIL_PATTERNS):
                w_g = dil_w_in[j][:, g * group_w:(g + 1) * group_w].astype(BF16)
                view = _norm_proj(x, norm_g[i, 0], sc1, sh1, w_g, dilation=dilation)
                o_g, l_g = _dil_group(view, dilation, window, s)
                outs.append(o_g)
                lses.append(l_g)
            x = _dil_res(outs, lses, dil_w_out[j].astype(BF16), x, g1)
        w_route = jnp.pad(jnp.concatenate([moe_w_grp[i], moe_w_exp[i]], axis=1),
                          ((0, 0), (0, LANES - MOE_GROUPS - MOE_EXPERTS)))
        b_route = jnp.pad(jnp.concatenate([moe_b_grp[i], moe_b_exp[i]]), (0, LANES - MOE_GROUPS - MOE_EXPERTS))
        x = _moe_layer(x, norm_g[i, 1], sc2, sh2, g2, w_route, b_route.reshape(1, LANES),
                       moe_w_gate, moe_w_up, moe_w_down, i, final_g, final=(i == depth - 1))
    return x
```

```python
import functools

import jax
import jax.numpy as jnp
from jax import lax
from jax.experimental import pallas as pl
from jax.experimental.pallas import tpu as pltpu
from jax.experimental.pallas import tpu_sc as plsc

F32 = jnp.float32
BF16 = jnp.bfloat16

LANES = 128
MXU_COLUMNS = 256
RMS_EPS = 1e-6
N_MOD = 6

SB_HEADS = 16
GLA_HEADS = 4
GLA_GATE_RANK = 16
GLA_GATE_TAU = 16.0
GLA_CHUNK = 64
CONV_WIDTH = 3
DIL_PATTERNS = ((128, 1), (512, 4), (2048, 16))
DIL_HEADS = 8
DIL_HEAD_DIM = 64
DIL_QBLOCK = 128
DIL_TILE_TOKENS = 2048
MOE_GROUPS = 4
MOE_PER_GROUP = 8
MOE_EXPERTS = MOE_GROUPS * MOE_PER_GROUP
MOE_TOPK = 2

SC_CORES = 2
SC_SUBCORES = 16
SC_WINDOW = 64
PACKED_LANES = 2 * LANES

SB_LOG_ZERO = -104.0
SB_BAND = 2

VMEM_LIMIT = 48 * 1024 * 1024


def _params(*sem):
    return pltpu.CompilerParams(dimension_semantics=sem, vmem_limit_bytes=VMEM_LIMIT)


def _sigmoid(x):
    return 1.0 / (1.0 + jnp.exp(-x))


def _softplus(x):
    return jnp.maximum(x, 0.0) + jnp.log(1.0 + jnp.exp(-jnp.abs(x)))


def _split_bf16(x):
    hi = x.astype(BF16)
    lo = (x - hi.astype(F32)).astype(BF16)
    return hi, lo


def _dot(a, b):
    return jnp.dot(a, b, preferred_element_type=F32)


def _dot_nt(a, b):
    return lax.dot_general(a, b, (((1,), (1,)), ((), ())), preferred_element_type=F32)


def _dot_tn(a, b):
    return lax.dot_general(a, b, (((0,), (0,)), ((), ())), preferred_element_type=F32)


def _unpack_slabs(ref, rows):
    k = ref.shape[0] // rows
    words = jnp.concatenate([ref[pl.ds(s, rows, stride=k), :] for s in range(k)], axis=1)
    halves = [pltpu.unpack_elementwise(words, index=i, packed_dtype=BF16, unpacked_dtype=F32) for i in range(2)]
    return jnp.concatenate(halves, axis=1)


def _pack_slabs(ref, values):
    rows, d = values.shape
    half = d // 2
    words = pltpu.pack_elementwise([values[:, :half], values[:, half:]], packed_dtype=BF16)
    for s in range(half // LANES):
        ref[pl.ds(s, rows, stride=half // LANES), :] = words[:, s * LANES:(s + 1) * LANES]


def _mod_norm(x, g, sc, sh):
    r = lax.rsqrt(jnp.mean(x * x, axis=-1, keepdims=True) + RMS_EPS)
    return (x * r) * g * (1.0 + sc) + sh


def _ada_kernel(c_ref, w_ref, b_ref, o_ref):
    c = c_ref[...]
    cond = c * _sigmoid(c)
    o_ref[0] = _dot(cond.astype(BF16), w_ref[0].astype(BF16)) + b_ref[0]


def _ada_mod(c, ada_w, ada_b):
    depth, d, n = ada_w.shape
    b = c.shape[0]
    tn = 1536
    return pl.pallas_call(
        _ada_kernel,
        grid=(depth, n // tn),
        in_specs=[
            pl.BlockSpec((b, d), lambda i, j: (0, 0)),
            pl.BlockSpec((1, d, tn), lambda i, j: (i, 0, j)),
            pl.BlockSpec((1, 1, tn), lambda i, j: (i, 0, j)),
        ],
        out_specs=pl.BlockSpec((1, b, tn), lambda i, j: (i, 0, j)),
        out_shape=jax.ShapeDtypeStruct((depth, b, n), F32),
        compiler_params=_params("parallel", "parallel"),
        name="ada_mod",
    )(c, ada_w, ada_b.reshape(depth, 1, n))


def _norm_proj_kernel(x_ref, g_ref, sc_ref, sh_ref, w_ref, o_ref, h_ref, *res_ref, dilation, tn):
    h_ref[...] = _mod_norm(x_ref[0], g_ref[...], sc_ref[0], sh_ref[0]).astype(BF16)
    tm = h_ref.shape[0]
    for j in range(w_ref.shape[1] // tn):
        res = _dot(h_ref[...], w_ref[:, j * tn:(j + 1) * tn])
        if dilation == 1:
            o_ref[0, :, j * tn:(j + 1) * tn] = res.astype(o_ref.dtype)
            continue
        for c in range(tn // LANES):
            res_ref[0][j * (tn // LANES) + c] = res[:, c * LANES:(c + 1) * LANES]
        for rho in range(dilation):
            for c in range(tn // LANES):
                rows = res_ref[0][j * (tn // LANES) + c, pl.ds(rho, tm // dilation, stride=dilation), :]
                col = (j * dilation + rho) * tn + c * LANES
                o_ref[0, :, col:col + LANES] = rows.astype(o_ref.dtype)


def _norm_proj(x, g, sc, sh, w, tm=512, tn=512, dilation=1):
    b, s, d = x.shape
    n = w.shape[1]
    tm = min(tm, s)
    assert s % tm == 0 and n % tn == 0 and tm % (16 * dilation) == 0
    scratch = [pltpu.VMEM((tm, d), BF16)]
    if dilation > 1:
        scratch.append(pltpu.VMEM((n // LANES, tm, LANES), F32))
    return pl.pallas_call(
        functools.partial(_norm_proj_kernel, dilation=dilation, tn=tn),
        grid=(b, s // tm),
        in_specs=[
            pl.BlockSpec((1, tm, d), lambda bi, i: (bi, i, 0)),
            pl.BlockSpec((1, d), lambda bi, i: (0, 0)),
            pl.BlockSpec((1, 1, d), lambda bi, i: (bi, 0, 0)),
            pl.BlockSpec((1, 1, d), lambda bi, i: (bi, 0, 0)),
            pl.BlockSpec((d, n), lambda bi, i: (0, 0)),
        ],
        out_specs=pl.BlockSpec((1, tm // dilation, dilation * n), lambda bi, i: (bi, i, 0)),
        out_shape=jax.ShapeDtypeStruct((b, s // dilation, dilation * n), BF16),
        scratch_shapes=scratch,
        compiler_params=_params("parallel", "parallel"),
        name="norm_proj",
    )(x, g.reshape(1, d), sc, sh, w)


def _proj_res_kernel(o_ref, w_ref, x_ref, g_ref, y_ref):
    y_ref[0] = x_ref[0] + g_ref[0] * _dot(o_ref[0], w_ref[...])


def _proj_res(o, w, x, gate, tm=1024):
    b, s, d = x.shape
    k = o.shape[-1]
    tm = min(tm, s)
    return pl.pallas_call(
        _proj_res_kernel,
        grid=(b, s // tm),
        in_specs=[
            pl.BlockSpec((1, tm, k), lambda bi, i: (bi, i, 0)),
            pl.BlockSpec((k, d), lambda bi, i: (0, 0)),
            pl.BlockSpec((1, tm, d), lambda bi, i: (bi, i, 0)),
            pl.BlockSpec((1, 1, d), lambda bi, i: (bi, 0, 0)),
        ],
        out_specs=pl.BlockSpec((1, tm, d), lambda bi, i: (bi, i, 0)),
        out_shape=jax.ShapeDtypeStruct((b, s, d), F32),
        compiler_params=_params("parallel", "parallel"),
        name="proj_res",
    )(o, w, x, gate)


def _sb_kernel(q_ref, k_ref, v_ref, o_ref, acc_ref, c_ref, *, blk, hd):
    scale = hd ** -0.5
    fold_scale = (hd & (hd - 1)) == 0 and (hd.bit_length() - 1) % 2 == 0
    first = lax.broadcasted_iota(jnp.int32, (blk, LANES), 1) < hd
    row = lax.broadcasted_iota(jnp.int32, (blk, blk), 0)
    col = lax.broadcasted_iota(jnp.int32, (blk, blk), 1)
    minus_later = jnp.where(row > col, -1.0, 0.0).astype(BF16)
    causal = col < row
    subs = q_ref.shape[1] // blk
    first_block = pl.program_id(2) * subs

    def visit_many(work):
        kv = []
        for _, block, _, _ in work:
            start = pl.multiple_of(block * blk, blk)
            kv.append((k_ref[0, pl.ds(start, blk), :], v_ref[0, pl.ds(start, blk), :]))
        zs = [[_dot_nt(q_heads[h], kb) for h in range(2)] for (q_heads, _, _, _), (kb, _) in zip(work, kv)]
        log_betas, sps = [], []
        for (_, _, mask, _), z2 in zip(work, zs):
            lb2, sp2 = [], []
            for z in z2:
                if not fold_scale:
                    z = z * scale
                sp = _softplus(z)
                lb2.append(z - sp)
                sp2.append(sp if mask is None else jnp.where(mask, sp, 0.0))
            log_betas.append(lb2)
            sps.append(sp2)
        suffixes = [[_dot(sp.astype(BF16), minus_later) for sp in sp2] for sp2 in sps]
        return kv, log_betas, sps, suffixes

    def finish(work, staged, c_in):
        kv, log_betas, sps, suffixes = staged
        cs = {chain: list(c) for chain, c in c_in.items()}
        weights = []
        for n, (_, _, mask, chain) in enumerate(work):
            a2 = []
            for h in range(2):
                a = jnp.exp(log_betas[n][h] + suffixes[n][h] + cs[chain][h])
                a2.append(a if mask is None else jnp.where(mask, a, 0.0))
                cs[chain][h] = cs[chain][h] - jnp.sum(sps[n][h], axis=1, keepdims=True)
            weights.append(a2)
        accs = {}
        for n, (_, _, _, chain) in enumerate(work):
            outs = [_dot(weights[n][h].astype(BF16), kv[n][1]) for h in range(2)]
            more = jnp.where(first, outs[0], outs[1])
            accs[chain] = more if chain not in accs else accs[chain] + more
        return accs, cs

    def remainder(sub, q_heads, depth):
        def cond(carry):
            j, cmax = carry
            return jnp.logical_and(j >= 0, cmax > SB_LOG_ZERO)

        def body(carry):
            j, _ = carry
            work = [(q_heads, j, None, sub)]
            accs, cs = finish(work, visit_many(work), {sub: [c_ref[sub, 0], c_ref[sub, 1]]})
            acc_ref[sub] += accs[sub]
            for h in range(2):
                c_ref[sub, h] = cs[sub][h]
            return j - 1, jnp.max(c_ref[sub])

        lax.while_loop(cond, body, (first_block + sub - depth - 1, jnp.max(c_ref[sub])))

    def run(depths):
        heads = []
        for sub in range(subs):
            q = q_ref[0, sub * blk:(sub + 1) * blk, :]
            if fold_scale:
                q = (q.astype(F32) * scale).astype(BF16)
            zero = jnp.zeros_like(q)
            heads.append((jnp.where(first, q, zero), jnp.where(first, zero, q)))
        work = [(heads[sub], first_block + sub - back, causal if back == 0 else None, sub)
                for sub in range(subs) for back in range(depths[sub] + 1)]
        zero_c = [jnp.zeros((blk, 1), F32)] * 2
        accs, cs = finish(work, visit_many(work), {sub: zero_c for sub in range(subs)})
        for sub in range(subs):
            acc_ref[sub] = accs[sub]
            for h in range(2):
                c_ref[sub, h] = cs[sub][h]
        @pl.when(jnp.max(c_ref[...]) > SB_LOG_ZERO)
        def _():
            for sub in range(subs):
                remainder(sub, heads[sub], depths[sub])

        for sub in range(subs):
            o_ref[0, sub * blk:(sub + 1) * blk, :] = acc_ref[sub].astype(o_ref.dtype)

    assert subs >= SB_BAND
    pl.when(first_block > 0)(functools.partial(run, [SB_BAND] * subs))
    pl.when(first_block == 0)(functools.partial(run, [min(SB_BAND, sub) for sub in range(subs)]))


def _sb_attention(qkv, blk=128, rows=1024):
    b, s, n = qkv.shape
    width = n // 3
    hd = width // SB_HEADS
    assert 2 * hd == LANES
    pairs = width // LANES
    rows = min(rows, s)
    return pl.pallas_call(
        functools.partial(_sb_kernel, blk=blk, hd=hd),
        grid=(b, pairs, s // rows),
        in_specs=[
            pl.BlockSpec((1, rows, LANES), lambda bi, p, i: (bi, i, p)),
            pl.BlockSpec((1, s, LANES), lambda bi, p, i: (bi, 0, pairs + p)),
            pl.BlockSpec((1, s, LANES), lambda bi, p, i: (bi, 0, 2 * pairs + p)),
        ],
        out_specs=pl.BlockSpec((1, rows, LANES), lambda bi, p, i: (bi, i, p)),
        out_shape=jax.ShapeDtypeStruct((b, s, width), BF16),
        scratch_shapes=[pltpu.VMEM((rows // blk, blk, LANES), F32), pltpu.VMEM((rows // blk, 2, blk, 1), F32)],
        compiler_params=_params("parallel", "parallel", "arbitrary"),
        name="sb_attention",
    )(qkv, qkv, qkv)


def _gla_kernel(p_ref, wg_ref, bg_ref, ng_ref, o_ref, st_ref, *, rows, dk, dv):
    @pl.when(pl.program_id(1) == 0)
    def _():
        st_ref[...] = jnp.zeros_like(st_ref)

    heads = GLA_HEADS
    ch = GLA_CHUNK
    k0 = heads * dk
    v0 = 2 * heads * dk
    r0 = v0 + heads * dv
    g0 = r0 + heads * dv
    row = lax.broadcasted_iota(jnp.int32, (ch, ch), 0)
    col = lax.broadcasted_iota(jnp.int32, (ch, ch), 1)
    causal = col <= row
    lower = jnp.where(causal, 1.0, 0.0).astype(BF16)
    chunks = [slice(s * ch, (s + 1) * ch) for s in range(rows // ch)]
    g_pres = [_dot(p_ref[0, t, g0:g0 + LANES], wg_ref[...]) + bg_ref[...] for t in chunks]
    cums = []
    for g_pre in g_pres:
        log_a = (jnp.minimum(g_pre, 0.0) - jnp.log(1.0 + jnp.exp(-jnp.abs(g_pre)))) * (1.0 / GLA_GATE_TAU)
        hi, lo = _split_bf16(log_a)
        cums.append(_dot(lower, hi) + _dot(lower, lo))
    parts = []
    for t, cum in zip(chunks, cums):
        for h in range(heads):
            bh = cum[:, h * dk:(h + 1) * dk]
            b_last = bh[ch - 1:ch, :]
            q = p_ref[0, t, h * dk:(h + 1) * dk].astype(F32) * dk ** -0.5
            k = p_ref[0, t, k0 + h * dk:k0 + (h + 1) * dk].astype(F32)
            q_in = (q * jnp.exp(bh)).astype(BF16)
            k_in = (k * jnp.exp(-bh)).astype(BF16)
            k_state = (k * jnp.exp(b_last - bh)).astype(BF16)
            parts.append((t, h, q_in, k_in, k_state, jnp.exp(b_last)))
    scores = [jnp.where(causal, _dot_nt(q_in, k_in), 0.0).astype(BF16) for _, _, q_in, k_in, _, _ in parts]
    values = [p_ref[0, t, v0 + h * dv:v0 + (h + 1) * dv] for t, h, _, _, _, _ in parts]
    intra = [_dot(sc, v) for sc, v in zip(scores, values)]
    updates = [_dot_tn(v, k_state) for v, (_, _, _, _, k_state, _) in zip(values, parts)]
    states = [st_ref[h] for h in range(heads)]
    for n, (t, h, q_in, _, _, decay) in enumerate(parts):
        o = intra[n] + _dot_nt(q_in, states[h].astype(BF16))
        states[h] = decay * states[h] + updates[n]
        o = o * lax.rsqrt(jnp.mean(o * o, axis=-1, keepdims=True) + RMS_EPS) * ng_ref[...]
        r = p_ref[0, t, r0 + h * dv:r0 + (h + 1) * dv].astype(F32)
        o_ref[0, t, h * dv:(h + 1) * dv] = (o * (r * _sigmoid(r))).astype(o_ref.dtype)
    for h in range(heads):
        st_ref[h] = states[h]


def _gla_core(proj, w_gate_up, b_gate, norm_g, dk, dv, rows=512):
    b, s, n = proj.shape
    rows = min(rows, s)
    hk = GLA_HEADS * dk
    return pl.pallas_call(
        functools.partial(_gla_kernel, rows=rows, dk=dk, dv=dv),
        grid=(b, s // rows),
        in_specs=[
            pl.BlockSpec((1, rows, n), lambda bi, i: (bi, i, 0)),
            pl.BlockSpec((LANES, hk), lambda bi, i: (0, 0)),
            pl.BlockSpec((1, hk), lambda bi, i: (0, 0)),
            pl.BlockSpec((1, dv), lambda bi, i: (0, 0)),
        ],
        out_specs=pl.BlockSpec((1, rows, GLA_HEADS * dv), lambda bi, i: (bi, i, 0)),
        out_shape=jax.ShapeDtypeStruct((b, s, GLA_HEADS * dv), BF16),
        scratch_shapes=[pltpu.VMEM((GLA_HEADS, dv, dk), F32)],
        compiler_params=_params("parallel", "arbitrary"),
        name="gla_core",
    )(proj, w_gate_up, b_gate, norm_g)


def _conv_res_kernel(gb_ref, gc_ref, u_ref, hgc_ref, hu_ref, cw_ref, cb_ref, w_ref, x_ref, g_ref, y_ref):
    u2 = gc_ref[0].astype(F32) * u_ref[0].astype(F32)
    halo = hgc_ref[0].astype(F32) * hu_ref[0].astype(F32)
    halo = jnp.where(pl.program_id(1) > 0, halo, 0.0)
    rows = lax.broadcasted_iota(jnp.int32, u2.shape, 0)
    prev1 = jnp.where(rows == 0, halo[7:8], pltpu.roll(u2, 1, 0))
    prev2 = jnp.where(rows == 0, halo[6:7], jnp.where(rows == 1, halo[7:8], pltpu.roll(u2, 2, 0)))
    cw = cw_ref[...]
    y = cw[0:1] * prev2 + cw[1:2] * prev1 + cw[2:3] * u2 + cb_ref[...]
    o = (gb_ref[0].astype(F32) * y).astype(BF16)
    y_ref[0] = x_ref[0] + g_ref[0] * _dot(o, w_ref[...])


def _conv_res(proj, conv_w, conv_b, w_out, x, gate, tm=1024):
    b, s, d = x.shape
    tm = min(tm, s)
    halo_blocks = tm // 8

    def halo_map(col):
        return lambda bi, i: (bi, jnp.maximum(i * halo_blocks - 1, 0), col)

    return pl.pallas_call(
        _conv_res_kernel,
        grid=(b, s // tm),
        in_specs=[
            pl.BlockSpec((1, tm, d), lambda bi, i: (bi, i, 0)),
            pl.BlockSpec((1, tm, d), lambda bi, i: (bi, i, 1)),
            pl.BlockSpec((1, tm, d), lambda bi, i: (bi, i, 2)),
            pl.BlockSpec((1, 8, d), halo_map(1)),
            pl.BlockSpec((1, 8, d), halo_map(2)),
            pl.BlockSpec((CONV_WIDTH, d), lambda bi, i: (0, 0)),
            pl.BlockSpec((1, d), lambda bi, i: (0, 0)),
            pl.BlockSpec((d, d), lambda bi, i: (0, 0)),
            pl.BlockSpec((1, tm, d), lambda bi, i: (bi, i, 0)),
            pl.BlockSpec((1, 1, d), lambda bi, i: (bi, 0, 0)),
        ],
        out_specs=pl.BlockSpec((1, tm, d), lambda bi, i: (bi, i, 0)),
        out_shape=jax.ShapeDtypeStruct((b, s, d), F32),
        compiler_params=_params("parallel", "parallel"),
        name="conv_res",
    )(proj, proj, proj, proj, proj, conv_w, conv_b.reshape(1, d), w_out, x, gate)


def _dil_kernel(q_ref, kc_ref, vc_ref, kp_ref, vp_ref, o_ref, l_ref, *, rows, hd, dilation, group):
    qb = DIL_QBLOCK
    n = pl.program_id(1)
    width = q_ref.shape[-1] // group
    scale = hd ** -0.5
    fold_scale = (hd & (hd - 1)) == 0 and (hd.bit_length() - 1) % 2 == 0
    first = lax.broadcasted_iota(jnp.int32, (qb, LANES), 1) < hd
    row = lax.broadcasted_iota(jnp.int32, (qb, qb), 0)
    col = lax.broadcasted_iota(jnp.int32, (qb, qb), 1)
    in_prev = col >= row
    in_cur = col <= row
    neg = -jnp.inf

    tiles = []
    for u, s in [(u, s) for u in range(group) for s in range(rows // qb)]:
        t = slice(s * qb, (s + 1) * qb)
        tp = slice((s - 1) * qb, s * qb)
        prev_mask = jnp.logical_and(in_prev, n > 0) if s == 0 else in_prev
        for p in range(width // LANES):
            c = slice(u * width + p * LANES, u * width + (p + 1) * LANES)
            q = q_ref[0, t, c]
            if fold_scale:
                q = (q.astype(F32) * scale).astype(BF16)
            zero = jnp.zeros_like(q)
            q_heads = (jnp.where(first, q, zero), jnp.where(first, zero, q))
            if s == 0:
                k_prev, v_prev = kp_ref[0, :, c], vp_ref[0, :, c]
            else:
                k_prev, v_prev = kc_ref[0, tp, c], vc_ref[0, tp, c]
            tiles.append(((u, s), p, q_heads, k_prev, kc_ref[0, t, c], v_prev, vc_ref[0, t, c], prev_mask))

    scores = []
    for _, _, q_heads, k_prev, k_cur, _, _, prev_mask in tiles:
        pair = []
        for h in range(2):
            z_prev, z_cur = _dot_nt(q_heads[h], k_prev), _dot_nt(q_heads[h], k_cur)
            if not fold_scale:
                z_prev, z_cur = z_prev * scale, z_cur * scale
            pair.append((jnp.where(prev_mask, z_prev, neg), jnp.where(in_cur, z_cur, neg)))
        scores.append(pair)

    probs = []
    for pair in scores:
        stats = []
        for z_prev, z_cur in pair:
            m = jnp.max(jnp.maximum(z_prev, z_cur), axis=1, keepdims=True)
            p_prev = jnp.exp(z_prev - m)
            p_cur = jnp.exp(z_cur - m)
            den = jnp.sum(p_prev + p_cur, axis=1, keepdims=True)
            stats.append((p_prev.astype(BF16), p_cur.astype(BF16), den, m))
        probs.append(stats)

    lane = lax.broadcasted_iota(jnp.int32, (qb, LANES), 1)
    lse_rows = {}
    for ((u, s), p, _, _, _, v_prev, v_cur, _), stats in zip(tiles, probs):
        if dilation == 1:
            t_out = slice(s * qb, (s + 1) * qb)
        else:
            rho = pl.program_id(2) * group + u
            t_out = pl.ds(s * qb * dilation + rho, qb, stride=dilation)
        outs = []
        lse = lse_rows.get((u, s), jnp.zeros((qb, LANES), F32))
        for h, (p_prev, p_cur, den, m) in enumerate(stats):
            outs.append((_dot(p_prev, v_prev) + _dot(p_cur, v_cur)) / den)
            lse = jnp.where(lane == 2 * p + h, m + jnp.log(den), lse)
        lse_rows[(u, s)] = lse
        o_ref[0, p, t_out, :] = jnp.where(first, outs[0], outs[1])
        if p == width // LANES - 1:
            l_ref[0, t_out, :] = lse


def _dil_group(view, dilation, window, seq):
    b, length, _ = view.shape
    width = DIL_HEADS * DIL_HEAD_DIM
    assert window == dilation * DIL_QBLOCK and seq % (dilation * DIL_QBLOCK) == 0
    rows = min(max(DIL_TILE_TOKENS // dilation, DIL_QBLOCK), 512, length)
    group = min(dilation, max(512 // rows, 1))
    prev_blocks = rows // DIL_QBLOCK
    per_part = dilation // group

    def cur(part):
        return lambda bi, i, j: (bi, i, part * per_part + j)

    def prev(part):
        return lambda bi, i, j: (bi, jnp.maximum(i * prev_blocks - 1, 0), part * per_part + j)

    pairs = width // LANES
    out_shape = jax.ShapeDtypeStruct((b, pairs, seq, LANES), F32)
    out_spec = pl.BlockSpec((1, pairs, rows * dilation, LANES), lambda bi, i, j: (bi, 0, i, 0))
    lse_shape = jax.ShapeDtypeStruct((b, seq, LANES), F32)
    lse_spec = pl.BlockSpec((1, rows * dilation, LANES), lambda bi, i, j: (bi, i, 0))
    return pl.pallas_call(
        functools.partial(_dil_kernel, rows=rows, hd=DIL_HEAD_DIM, dilation=dilation, group=group),
        grid=(b, length // rows, per_part),
        in_specs=[
            pl.BlockSpec((1, rows, group * width), cur(0)),
            pl.BlockSpec((1, rows, group * width), cur(1)),
            pl.BlockSpec((1, rows, group * width), cur(2)),
            pl.BlockSpec((1, DIL_QBLOCK, group * width), prev(1)),
            pl.BlockSpec((1, DIL_QBLOCK, group * width), prev(2)),
        ],
        out_specs=[out_spec, lse_spec],
        out_shape=[out_shape, lse_shape],
        compiler_params=_params("parallel", "parallel", "arbitrary"),
        name=f"dil_attn_r{dilation}",
    )(view, view, view, view, view)


def _dil_res_kernel(o0, o1, o2, l0, l1, l2, w_ref, x_ref, g_ref, y_ref):
    ls = (l0[0], l1[0], l2[0])
    m = jnp.maximum(jnp.maximum(ls[0], ls[1]), ls[2])
    es = [jnp.exp(l - m) for l in ls]
    total = es[0] + es[1] + es[2]
    ws = [e / total for e in es]
    hd = LANES // 2
    first = lax.broadcasted_iota(jnp.int32, (x_ref.shape[1], LANES), 1) < hd
    parts = []
    for p in range(o0.shape[1]):
        w = [jnp.where(first, wg[:, 2 * p:2 * p + 1], wg[:, 2 * p + 1:2 * p + 2]) for wg in ws]
        parts.append(w[0] * o0[0, p] + w[1] * o1[0, p] + w[2] * o2[0, p])
    o = jnp.concatenate(parts, axis=1)
    y_ref[0] = x_ref[0] + g_ref[0] * _dot(o.astype(BF16), w_ref[...])


def _dil_res(outs, lses, w_out, x, gate, tm=1024):
    b, s, d = x.shape
    k = w_out.shape[0]
    tm = min(tm, s)
    part = pl.BlockSpec((1, k // LANES, tm, LANES), lambda bi, i: (bi, 0, i, 0))
    lse = pl.BlockSpec((1, tm, LANES), lambda bi, i: (bi, i, 0))
    return pl.pallas_call(
        _dil_res_kernel,
        grid=(b, s // tm),
        in_specs=[part] * 3 + [lse] * 3 + [
            pl.BlockSpec((k, d), lambda bi, i: (0, 0)),
            pl.BlockSpec((1, tm, d), lambda bi, i: (bi, i, 0)),
            pl.BlockSpec((1, 1, d), lambda bi, i: (bi, 0, 0)),
        ],
        out_specs=pl.BlockSpec((1, tm, d), lambda bi, i: (bi, i, 0)),
        out_shape=jax.ShapeDtypeStruct((b, s, d), F32),
        compiler_params=_params("parallel", "parallel"),
        name="dil_res",
    )(*outs, *lses, w_out, x, gate)


def _router_kernel(x_ref, g_ref, sc_ref, sh_ref, wr_ref, br_ref, h_ref, route_ref, cnt_ref, run_ref, earlier_ref):
    tm = x_ref.shape[1]

    @pl.when(jnp.logical_and(pl.program_id(0) == 0, pl.program_id(1) == 0))
    def _():
        run_ref[...] = jnp.zeros_like(run_ref)
        row = lax.broadcasted_iota(jnp.int32, (tm, tm), 0)
        col = lax.broadcasted_iota(jnp.int32, (tm, tm), 1)
        earlier_ref[...] = jnp.where(col < row, 1.0, 0.0).astype(BF16)

    h = _mod_norm(x_ref[0], g_ref[...], sc_ref[0], sh_ref[0])
    _pack_slabs(h_ref, h)
    logits = _dot(h.astype(BF16), wr_ref[...].astype(BF16)) + br_ref[...]
    lane = lax.broadcasted_iota(jnp.int32, logits.shape, 1).astype(F32)
    neg = -jnp.inf
    far = float(LANES)

    def first_argmax(vals):
        top = jnp.max(vals, axis=1, keepdims=True)
        return top, jnp.min(jnp.where(vals == top, lane, far), axis=1, keepdims=True)

    grp_logits = jnp.where(lane < MOE_GROUPS, logits, neg)
    grp_max, grp = first_argmax(grp_logits)
    p_grp = 1.0 / jnp.sum(jnp.exp(grp_logits - grp_max), axis=1, keepdims=True)
    base = MOE_GROUPS + grp * MOE_PER_GROUP
    in_grp = jnp.logical_and(lane >= base, lane < base + MOE_PER_GROUP)
    exp_logits = jnp.where(in_grp, logits, neg)
    m1, i1 = first_argmax(exp_logits)
    m2, i2 = first_argmax(jnp.where(lane == i1, neg, exp_logits))
    e2 = jnp.exp(m2 - m1)
    g1 = p_grp / (1.0 + e2)
    g2 = g1 * e2
    e_a = i1 - MOE_GROUPS
    e_b = i2 - MOE_GROUPS

    pick_a = lane == e_a
    pick_b = lane == e_b
    both = jnp.where(jnp.logical_or(pick_a, pick_b), 1.0, 0.0)
    before = _dot(earlier_ref[...], both.astype(BF16)) + run_ref[...]
    rank_a = jnp.sum(jnp.where(pick_a, before, 0.0), axis=1, keepdims=True)
    rank_b = jnp.sum(jnp.where(pick_b, before, 0.0), axis=1, keepdims=True)
    run_ref[...] += jnp.sum(both, axis=0, keepdims=True)
    cnt_ref[...] = run_ref[...]

    route = jnp.where(lane == 0, e_a, jnp.where(lane == 1, e_b, 0.0))
    route = jnp.where(lane == 2, g1, jnp.where(lane == 3, g2, route))
    route_ref[...] = jnp.where(lane == 4, rank_a, jnp.where(lane == 5, rank_b, route))


def _router(x, g, sc, sh, w_route, b_route, tm=512):
    b, s, d = x.shape
    tm = min(tm, s)
    steps = s // tm
    return pl.pallas_call(
        _router_kernel,
        grid=(b, steps),
        in_specs=[
            pl.BlockSpec((1, tm, d), lambda bi, i: (bi, i, 0)),
            pl.BlockSpec((1, d), lambda bi, i: (0, 0)),
            pl.BlockSpec((1, 1, d), lambda bi, i: (bi, 0, 0)),
            pl.BlockSpec((1, 1, d), lambda bi, i: (bi, 0, 0)),
            pl.BlockSpec((d, LANES), lambda bi, i: (0, 0)),
            pl.BlockSpec((1, LANES), lambda bi, i: (0, 0)),
        ],
        out_specs=[
            pl.BlockSpec((tm * (d // PACKED_LANES), LANES), lambda bi, i: (bi * steps + i, 0)),
            pl.BlockSpec((tm, LANES), lambda bi, i: (bi * steps + i, 0)),
            pl.BlockSpec((1, LANES), lambda bi, i: (0, 0)),
        ],
        out_shape=[
            jax.ShapeDtypeStruct((b * s * (d // PACKED_LANES), LANES), jnp.int32),
            jax.ShapeDtypeStruct((b * s, LANES), F32),
            jax.ShapeDtypeStruct((1, LANES), F32),
        ],
        scratch_shapes=[pltpu.VMEM((1, LANES), F32), pltpu.VMEM((tm, tm), BF16)],
        compiler_params=_params("arbitrary", "arbitrary"),
        name="moe_router",
    )(x, g.reshape(1, d), sc, sh, w_route, b_route)


def _sc_dispatch(h_slabs, dest, n_slots):
    t, chunks, _ = h_slabs.shape
    workers = SC_CORES * SC_SUBCORES
    per_worker = t // workers
    steps = per_worker // SC_WINDOW
    assert t % (workers * SC_WINDOW * 2) == 0
    mesh = plsc.VectorSubcoreMesh(core_axis_name="core", subcore_axis_name="subcore")

    @functools.partial(
        pl.kernel, mesh=mesh,
        out_type=jax.ShapeDtypeStruct((n_slots, chunks, LANES), h_slabs.dtype),
        scratch_types=[
            pltpu.VMEM((2, MOE_TOPK, SC_WINDOW), jnp.int32),
            pltpu.VMEM((2, SC_WINDOW, chunks, LANES), h_slabs.dtype),
            pltpu.SemaphoreType.DMA((2,)),
            pltpu.SemaphoreType.DMA((2,)),
        ])
    def dispatch(h_hbm, dest_hbm, slots_hbm, idx_v, rows_v, load_sem, store_sem):
        worker = lax.axis_index("subcore") * SC_CORES + lax.axis_index("core")
        base = worker * per_worker

        @pl.loop(0, steps, step=2)
        def _(step):
            for buf in range(2):
                off = base + (step + buf) * SC_WINDOW
                pltpu.async_copy(h_hbm.at[pl.ds(off, SC_WINDOW)], rows_v.at[buf], load_sem.at[buf])
                for k in range(MOE_TOPK):
                    pltpu.sync_copy(dest_hbm.at[k, pl.ds(off, SC_WINDOW)], idx_v.at[buf, k])
            for buf in range(2):
                off = base + (step + buf) * SC_WINDOW
                pltpu.make_async_copy(h_hbm.at[pl.ds(off, SC_WINDOW)], rows_v.at[buf], load_sem.at[buf]).wait()
                copies = [pltpu.async_copy(rows_v.at[buf], slots_hbm.at[idx_v.at[buf, k]], store_sem.at[buf])
                          for k in range(MOE_TOPK)]
                for cp in copies:
                    cp.wait()

    return dispatch(h_slabs, dest)


def _zero_tail_kernel(tail_ref, valid_ref, s_in, s_out):
    del tail_ref
    rows = lax.broadcasted_iota(jnp.int32, s_in.shape, 0)
    s_out[...] = jnp.where(rows < valid_ref[pl.program_id(0)], s_in[...], 0)


def _zero_tails(tail_blk, valid_rows, slots, block_rows):
    grid_spec = pltpu.PrefetchScalarGridSpec(
        num_scalar_prefetch=2,
        grid=(tail_blk.shape[0],),
        in_specs=[pl.BlockSpec((block_rows, LANES), lambda e, tail, valid: (tail[e], 0))],
        out_specs=pl.BlockSpec((block_rows, LANES), lambda e, tail, valid: (tail[e], 0)),
    )
    return pl.pallas_call(
        _zero_tail_kernel,
        grid_spec=grid_spec,
        out_shape=jax.ShapeDtypeStruct(slots.shape, slots.dtype),
        input_output_aliases={2: 0},
        compiler_params=_params("arbitrary"),
        name="moe_zero_tails",
    )(tail_blk, valid_rows, slots)


def _expert_kernel(blk_e_ref, used_ref, x_ref, wg_ref, wu_ref, wd_ref, y_ref, wg_bf, wu_bf, wd_bf):
    i = pl.program_id(0)
    live = i < used_ref[0]
    changed = jnp.logical_or(i == 0, blk_e_ref[i] != blk_e_ref[jnp.maximum(i - 1, 0)])

    @pl.when(jnp.logical_and(live, changed))
    def _():
        wg_bf[...] = wg_ref[0, 0].astype(BF16)
        wu_bf[...] = wu_ref[0, 0].astype(BF16)
        wd_bf[...] = wd_ref[0, 0].astype(BF16)

    @pl.when(live)
    def _():
        bm = x_ref.shape[0] // (wg_bf.shape[0] // PACKED_LANES)
        x = _unpack_slabs(x_ref, bm).astype(BF16)
        gate = _dot(x, wg_bf[...])
        up = _dot(x, wu_bf[...])
        act = (gate * _sigmoid(gate) * up).astype(BF16)
        _pack_slabs(y_ref, _dot(act, wd_bf[...]))

    @pl.when(jnp.logical_not(live))
    def _():
        y_ref[...] = jnp.zeros_like(y_ref)


def _expert_blocks(blk_e, used, slots, w_gate, w_up, w_down, layer, bm):
    nb = blk_e.shape[0]
    _, _, d, f = w_gate.shape
    chunks = d // PACKED_LANES

    def block(i, e, n):
        return (jnp.minimum(i, n[0] - 1), 0)

    def weight(i, e, n):
        return (layer, e[jnp.minimum(i, n[0] - 1)], 0, 0)

    grid_spec = pltpu.PrefetchScalarGridSpec(
        num_scalar_prefetch=2,
        grid=(nb,),
        in_specs=[
            pl.BlockSpec((bm * chunks, LANES), block),
            pl.BlockSpec((1, 1, d, f), weight),
            pl.BlockSpec((1, 1, d, f), weight),
            pl.BlockSpec((1, 1, f, d), weight),
        ],
        out_specs=pl.BlockSpec((bm * chunks, LANES), lambda i, e, n: (i, 0)),
        scratch_shapes=[pltpu.VMEM((d, f), BF16), pltpu.VMEM((d, f), BF16), pltpu.VMEM((f, d), BF16)],
    )
    return pl.pallas_call(
        _expert_kernel,
        grid_spec=grid_spec,
        out_shape=jax.ShapeDtypeStruct((nb * bm * chunks, LANES), jnp.int32),
        compiler_params=_params("arbitrary"),
        name="moe_experts",
    )(blk_e, used, slots, w_gate, w_up, w_down)


def _sc_gather(y_slabs, idx):
    n = idx.shape[0]
    _, chunks, _ = y_slabs.shape
    workers = SC_CORES * SC_SUBCORES
    per_worker = n // workers
    steps = per_worker // SC_WINDOW
    assert n % (workers * SC_WINDOW * 2) == 0
    mesh = plsc.VectorSubcoreMesh(core_axis_name="core", subcore_axis_name="subcore")

    @functools.partial(
        pl.kernel, mesh=mesh,
        out_type=jax.ShapeDtypeStruct((n, chunks, LANES), y_slabs.dtype),
        scratch_types=[
            pltpu.VMEM((2, SC_WINDOW), jnp.int32),
            pltpu.VMEM((2, SC_WINDOW, chunks, LANES), y_slabs.dtype),
            pltpu.SemaphoreType.DMA((2,)),
        ])
    def gather(y_hbm, idx_hbm, out_hbm, idx_v, rows_v, sem):
        worker = lax.axis_index("subcore") * SC_CORES + lax.axis_index("core")
        base = worker * per_worker

        @pl.loop(0, steps, step=2)
        def _(step):
            for buf in range(2):
                off = base + (step + buf) * SC_WINDOW
                pltpu.sync_copy(idx_hbm.at[pl.ds(off, SC_WINDOW)], idx_v.at[buf])
                pltpu.async_copy(y_hbm.at[idx_v.at[buf]], rows_v.at[buf], sem.at[buf])
            for buf in range(2):
                off = base + (step + buf) * SC_WINDOW
                pltpu.make_async_copy(y_hbm.at[idx_v.at[buf]], rows_v.at[buf], sem.at[buf]).wait()
                pltpu.sync_copy(rows_v.at[buf], out_hbm.at[pl.ds(off, SC_WINDOW)])

    return gather(y_slabs, idx)


def _combine_kernel(g0_ref, g1_ref, route_ref, x_ref, gate_ref, fg_ref, o_ref, *, final):
    tm = x_ref.shape[0]
    route = route_ref[...]
    y = route[:, 2:3] * _unpack_slabs(g0_ref, tm) + route[:, 3:4] * _unpack_slabs(g1_ref, tm)
    out = x_ref[...] + gate_ref[0] * y
    if final:
        out = out * lax.rsqrt(jnp.mean(out * out, axis=-1, keepdims=True) + RMS_EPS) * fg_ref[...]
    o_ref[...] = out


def _combine(dest, route, y_slabs, x, gate, final_g, final, tm, pieces=8):
    b, s, d = x.shape
    t = b * s
    chunks = d // PACKED_LANES
    per_seq = s // tm
    piece = t // pieces
    tiles = piece // tm
    out = x.reshape(t, d)
    fg = final_g.reshape(1, d)
    for c in range(pieces):
        idx = dest[:, c * piece:(c + 1) * piece].reshape(MOE_TOPK * piece)
        g = _sc_gather(y_slabs, idx).reshape(MOE_TOPK * piece * chunks, LANES)
        first = c * tiles
        out = pl.pallas_call(
            functools.partial(_combine_kernel, final=final),
            grid=(tiles,),
            in_specs=[
                pl.BlockSpec((tm * chunks, LANES), lambda i: (i, 0)),
                pl.BlockSpec((tm * chunks, LANES), lambda i: (tiles + i, 0)),
                pl.BlockSpec((tm, LANES), lambda i, first=first: (first + i, 0)),
                pl.BlockSpec((tm, d), lambda i, first=first: (first + i, 0)),
                pl.BlockSpec((1, 1, d), lambda i, first=first: ((first + i) // per_seq, 0, 0)),
                pl.BlockSpec((1, d), lambda i: (0, 0)),
            ],
            out_specs=pl.BlockSpec((tm, d), lambda i, first=first: (first + i, 0)),
            out_shape=jax.ShapeDtypeStruct((t, d), F32),
            input_output_aliases={3: 0},
            compiler_params=_params("parallel"),
            name="moe_combine",
        )(g, g, route, out, gate, fg)
    return out.reshape(b, s, d)


def _slot_tables(route, counts, bm):
    t = route.shape[0]
    nb = MOE_TOPK * t // bm + MOE_EXPERTS
    counts = counts[0, :MOE_EXPERTS].astype(jnp.int32)
    padded = (counts + bm - 1) // bm * bm
    pad_end = jnp.cumsum(padded)
    pad_start = pad_end - padded
    expert = route[:, 0:MOE_TOPK].astype(jnp.int32)
    rank = route[:, 4:4 + MOE_TOPK].astype(jnp.int32)
    dest = pad_start[expert] + rank
    blk_start = jnp.arange(nb, dtype=jnp.int32) * bm
    blk_e = jnp.minimum(jnp.sum((pad_end[None, :] <= blk_start[:, None]).astype(jnp.int32), axis=1), MOE_EXPERTS - 1)
    used = (pad_end[-1:] // bm).astype(jnp.int32)
    tail_blk = jnp.maximum(pad_end // bm - 1, 0).astype(jnp.int32)
    tail_fill = jnp.where(counts % bm != 0, counts % bm, bm).astype(jnp.int32)
    return dest.T, blk_e, used, tail_blk, tail_fill, nb


def _moe_layer(x, g, sc, sh, gate, w_route, b_route, w_gate, w_up, w_down, layer, final_g, final, bm=512, tm=1024):
    b, s, d = x.shape
    t = b * s
    chunks = d // PACKED_LANES
    tm = min(tm, s)
    h_rows, route, counts = _router(x, g, sc, sh, w_route, b_route)
    dest, blk_e, used, tail_blk, tail_fill, nb = _slot_tables(route, counts, bm)
    slots = _sc_dispatch(h_rows.reshape(t, chunks, LANES), dest, nb * bm)
    slots = _zero_tails(tail_blk, tail_fill * chunks, slots.reshape(nb * bm * chunks, LANES), bm * chunks)
    y_slabs = _expert_blocks(blk_e, used, slots, w_gate, w_up, w_down, layer, bm).reshape(nb * bm, chunks, LANES)
    return _combine(dest, route, y_slabs, x, gate, final_g, final, tm)


def kernel(x, c, ada_w, ada_b, norm_g, final_g, sb_w_in, sb_w_out, gla_w_in, gla_w_gate_up, gla_b_gate, gla_norm_g, gla_w_out, conv_w_in, conv_w, conv_b, conv_w_out, dil_w_in, dil_w_out, moe_w_grp, moe_b_grp, moe_w_exp, moe_b_exp, moe_w_gate, moe_w_up, moe_w_down):
    b, s, d = x.shape
    depth = ada_w.shape[0]
    mod = _ada_mod(c, ada_w, ada_b).reshape(depth, b, N_MOD, 1, d)
    for i in range(depth):
        sh1, sc1, g1, sh2, sc2, g2 = (mod[i, :, m] for m in range(N_MOD))
        kind, j = i % 4, i // 4
        if kind == 0:
            qkv = _norm_proj(x, norm_g[i, 0], sc1, sh1, sb_w_in[j].astype(BF16))
            x = _proj_res(_sb_attention(qkv), sb_w_out[j].astype(BF16), x, g1)
        elif kind == 1:
            dk = gla_w_gate_up.shape[-1] // GLA_HEADS
            dv = gla_norm_g.shape[-1]
            n_in = gla_w_in.shape[-1]
            w_in = jnp.pad(gla_w_in[j], ((0, 0), (0, -n_in % MXU_COLUMNS))).astype(BF16)
            w_up = jnp.pad(gla_w_gate_up[j], ((0, LANES - GLA_GATE_RANK), (0, 0))).astype(BF16)
            proj = _norm_proj(x, norm_g[i, 0], sc1, sh1, w_in, tn=MXU_COLUMNS)
            o = _gla_core(proj, w_up, gla_b_gate[j].reshape(1, -1), gla_norm_g[j].reshape(1, -1), dk, dv)
            x = _proj_res(o, gla_w_out[j].astype(BF16), x, g1)
        elif kind == 2:
            proj = _norm_proj(x, norm_g[i, 0], sc1, sh1, conv_w_in[j].astype(BF16))
            x = _conv_res(proj, conv_w[j], conv_b[j], conv_w_out[j].astype(BF16), x, g1)
        else:
            group_w = 3 * DIL_HEADS * DIL_HEAD_DIM
            outs, lses = [], []
            for g, (window, dilation) in enumerate(DIL_PATTERNS):
                w_g = dil_w_in[j][:, g * group_w:(g + 1) * group_w].astype(BF16)
                view = _norm_proj(x, norm_g[i, 0], sc1, sh1, w_g, dilation=dilation)
                o_g, l_g = _dil_group(view, dilation, window, s)
                outs.append(o_g)
                lses.append(l_g)
            x = _dil_res(outs, lses, dil_w_out[j].astype(BF16), x, g1)
        w_route = jnp.pad(jnp.concatenate([moe_w_grp[i], moe_w_exp[i]], axis=1),
                          ((0, 0), (0, LANES - MOE_GROUPS - MOE_EXPERTS)))
        b_route = jnp.pad(jnp.concatenate([moe_b_grp[i], moe_b_exp[i]]), (0, LANES - MOE_GROUPS - MOE_EXPERTS))
        x = _moe_layer(x, norm_g[i, 1], sc2, sh2, g2, w_route, b_route.reshape(1, LANES),
                       moe_w_gate, moe_w_up, moe_w_down, i, final_g, final=(i == depth - 1))
    return x
```

```python
import functools

import jax
import jax.numpy as jnp
from jax import lax
from jax.experimental import pallas as pl
from jax.experimental.pallas import tpu as pltpu
from jax.experimental.pallas import tpu_sc as plsc

F32 = jnp.float32
BF16 = jnp.bfloat16

LANES = 128
MXU_COLUMNS = 256
RMS_EPS = 1e-6
N_MOD = 6

SB_HEADS = 16
GLA_HEADS = 4
GLA_GATE_RANK = 16
GLA_GATE_TAU = 16.0
GLA_CHUNK = 64
CONV_WIDTH = 3
DIL_PATTERNS = ((128, 1), (512, 4), (2048, 16))
DIL_HEADS = 8
DIL_HEAD_DIM = 64
DIL_QBLOCK = 128
DIL_TILE_TOKENS = 2048
MOE_GROUPS = 4
MOE_PER_GROUP = 8
MOE_EXPERTS = MOE_GROUPS * MOE_PER_GROUP
MOE_TOPK = 2

SC_CORES = 2
SC_SUBCORES = 16
SC_WINDOW = 64
PACKED_LANES = 2 * LANES

SB_LOG_ZERO = -104.0
SB_BAND = 2

VMEM_LIMIT = 48 * 1024 * 1024


def _params(*sem):
    return pltpu.CompilerParams(dimension_semantics=sem, vmem_limit_bytes=VMEM_LIMIT)


def _sigmoid(x):
    return 1.0 / (1.0 + jnp.exp(-x))


def _softplus(x):
    return jnp.maximum(x, 0.0) + jnp.log(1.0 + jnp.exp(-jnp.abs(x)))


def _split_bf16(x):
    hi = x.astype(BF16)
    lo = (x - hi.astype(F32)).astype(BF16)
    return hi, lo


def _dot(a, b):
    return jnp.dot(a, b, preferred_element_type=F32)


def _dot_nt(a, b):
    return lax.dot_general(a, b, (((1,), (1,)), ((), ())), preferred_element_type=F32)


def _dot_tn(a, b):
    return lax.dot_general(a, b, (((0,), (0,)), ((), ())), preferred_element_type=F32)


def _unpack_slabs(ref, rows):
    k = ref.shape[0] // rows
    words = jnp.concatenate([ref[pl.ds(s, rows, stride=k), :] for s in range(k)], axis=1)
    halves = [pltpu.unpack_elementwise(words, index=i, packed_dtype=BF16, unpacked_dtype=F32) for i in range(2)]
    return jnp.concatenate(halves, axis=1)


def _pack_slabs(ref, values):
    rows, d = values.shape
    half = d // 2
    words = pltpu.pack_elementwise([values[:, :half], values[:, half:]], packed_dtype=BF16)
    for s in range(half // LANES):
        ref[pl.ds(s, rows, stride=half // LANES), :] = words[:, s * LANES:(s + 1) * LANES]


def _mod_norm(x, g, sc, sh):
    r = lax.rsqrt(jnp.mean(x * x, axis=-1, keepdims=True) + RMS_EPS)
    return (x * r) * g * (1.0 + sc) + sh


def _ada_kernel(c_ref, w_ref, b_ref, o_ref):
    c = c_ref[...]
    cond = c * _sigmoid(c)
    o_ref[0] = _dot(cond.astype(BF16), w_ref[0].astype(BF16)) + b_ref[0]


def _ada_mod(c, ada_w, ada_b):
    depth, d, n = ada_w.shape
    b = c.shape[0]
    tn = 1536
    return pl.pallas_call(
        _ada_kernel,
        grid=(depth, n // tn),
        in_specs=[
            pl.BlockSpec((b, d), lambda i, j: (0, 0)),
            pl.BlockSpec((1, d, tn), lambda i, j: (i, 0, j)),
            pl.BlockSpec((1, 1, tn), lambda i, j: (i, 0, j)),
        ],
        out_specs=pl.BlockSpec((1, b, tn), lambda i, j: (i, 0, j)),
        out_shape=jax.ShapeDtypeStruct((depth, b, n), F32),
        compiler_params=_params("parallel", "parallel"),
        name="ada_mod",
    )(c, ada_w, ada_b.reshape(depth, 1, n))


def _norm_proj_kernel(x_ref, g_ref, sc_ref, sh_ref, w_ref, o_ref, h_ref, *res_ref, dilation, tn):
    h_ref[...] = _mod_norm(x_ref[0], g_ref[...], sc_ref[0], sh_ref[0]).astype(BF16)
    tm = h_ref.shape[0]
    for j in range(w_ref.shape[1] // tn):
        res = _dot(h_ref[...], w_ref[:, j * tn:(j + 1) * tn])
        if dilation == 1:
            o_ref[0, :, j * tn:(j + 1) * tn] = res.astype(o_ref.dtype)
            continue
        for c in range(tn // LANES):
            res_ref[0][j * (tn // LANES) + c] = res[:, c * LANES:(c + 1) * LANES]
        for rho in range(dilation):
            for c in range(tn // LANES):
                rows = res_ref[0][j * (tn // LANES) + c, pl.ds(rho, tm // dilation, stride=dilation), :]
                col = (j * dilation + rho) * tn + c * LANES
                o_ref[0, :, col:col + LANES] = rows.astype(o_ref.dtype)


def _norm_proj(x, g, sc, sh, w, tm=1024, tn=512, dilation=1):
    b, s, d = x.shape
    n = w.shape[1]
    tm = min(tm, s)
    assert s % tm == 0 and n % tn == 0 and tm % (16 * dilation) == 0
    scratch = [pltpu.VMEM((tm, d), BF16)]
    if dilation > 1:
        scratch.append(pltpu.VMEM((n // LANES, tm, LANES), F32))
    return pl.pallas_call(
        functools.partial(_norm_proj_kernel, dilation=dilation, tn=tn),
        grid=(b, s // tm),
        in_specs=[
            pl.BlockSpec((1, tm, d), lambda bi, i: (bi, i, 0)),
            pl.BlockSpec((1, d), lambda bi, i: (0, 0)),
            pl.BlockSpec((1, 1, d), lambda bi, i: (bi, 0, 0)),
            pl.BlockSpec((1, 1, d), lambda bi, i: (bi, 0, 0)),
            pl.BlockSpec((d, n), lambda bi, i: (0, 0)),
        ],
        out_specs=pl.BlockSpec((1, tm // dilation, dilation * n), lambda bi, i: (bi, i, 0)),
        out_shape=jax.ShapeDtypeStruct((b, s // dilation, dilation * n), BF16),
        scratch_shapes=scratch,
        compiler_params=_params("parallel", "parallel"),
        name="norm_proj",
    )(x, g.reshape(1, d), sc, sh, w)


def _proj_res_kernel(o_ref, w_ref, x_ref, g_ref, y_ref):
    y_ref[0] = x_ref[0] + g_ref[0] * _dot(o_ref[0], w_ref[...])


def _proj_res(o, w, x, gate, tm=1024):
    b, s, d = x.shape
    k = o.shape[-1]
    tm = min(tm, s)
    return pl.pallas_call(
        _proj_res_kernel,
        grid=(b, s // tm),
        in_specs=[
            pl.BlockSpec((1, tm, k), lambda bi, i: (bi, i, 0)),
            pl.BlockSpec((k, d), lambda bi, i: (0, 0)),
            pl.BlockSpec((1, tm, d), lambda bi, i: (bi, i, 0)),
            pl.BlockSpec((1, 1, d), lambda bi, i: (bi, 0, 0)),
        ],
        out_specs=pl.BlockSpec((1, tm, d), lambda bi, i: (bi, i, 0)),
        out_shape=jax.ShapeDtypeStruct((b, s, d), F32),
        compiler_params=_params("parallel", "parallel"),
        name="proj_res",
    )(o, w, x, gate)


def _sb_kernel(q_ref, k_ref, v_ref, o_ref, acc_ref, c_ref, *, blk, hd):
    scale = hd ** -0.5
    fold_scale = (hd & (hd - 1)) == 0 and (hd.bit_length() - 1) % 2 == 0
    first = lax.broadcasted_iota(jnp.int32, (blk, LANES), 1) < hd
    row = lax.broadcasted_iota(jnp.int32, (blk, blk), 0)
    col = lax.broadcasted_iota(jnp.int32, (blk, blk), 1)
    minus_later = jnp.where(row > col, -1.0, 0.0).astype(BF16)
    causal = col < row
    subs = q_ref.shape[1] // blk
    first_block = pl.program_id(2) * subs

    def visit_many(work):
        kv = []
        for _, block, _, _ in work:
            start = pl.multiple_of(block * blk, blk)
            kv.append((k_ref[0, pl.ds(start, blk), :], v_ref[0, pl.ds(start, blk), :]))
        zs = [[_dot_nt(q_heads[h], kb) for h in range(2)] for (q_heads, _, _, _), (kb, _) in zip(work, kv)]
        log_betas, sps = [], []
        for (_, _, mask, _), z2 in zip(work, zs):
            lb2, sp2 = [], []
            for z in z2:
                if not fold_scale:
                    z = z * scale
                sp = _softplus(z)
                lb2.append(z - sp)
                sp2.append(sp if mask is None else jnp.where(mask, sp, 0.0))
            log_betas.append(lb2)
            sps.append(sp2)
        suffixes = [[_dot(sp.astype(BF16), minus_later) for sp in sp2] for sp2 in sps]
        return kv, log_betas, sps, suffixes

    def finish(work, staged, c_in):
        kv, log_betas, sps, suffixes = staged
        cs = {chain: list(c) for chain, c in c_in.items()}
        weights = []
        for n, (_, _, mask, chain) in enumerate(work):
            a2 = []
            for h in range(2):
                a = jnp.exp(log_betas[n][h] + suffixes[n][h] + cs[chain][h])
                a2.append(a if mask is None else jnp.where(mask, a, 0.0))
                cs[chain][h] = cs[chain][h] - jnp.sum(sps[n][h], axis=1, keepdims=True)
            weights.append(a2)
        accs = {}
        for n, (_, _, _, chain) in enumerate(work):
            outs = [_dot(weights[n][h].astype(BF16), kv[n][1]) for h in range(2)]
            more = jnp.where(first, outs[0], outs[1])
            accs[chain] = more if chain not in accs else accs[chain] + more
        return accs, cs

    def remainder(sub, q_heads, depth):
        def cond(carry):
            j, cmax = carry
            return jnp.logical_and(j >= 0, cmax > SB_LOG_ZERO)

        def body(carry):
            j, _ = carry
            work = [(q_heads, j, None, sub)]
            accs, cs = finish(work, visit_many(work), {sub: [c_ref[sub, 0], c_ref[sub, 1]]})
            acc_ref[sub] += accs[sub]
            for h in range(2):
                c_ref[sub, h] = cs[sub][h]
            return j - 1, jnp.max(c_ref[sub])

        lax.while_loop(cond, body, (first_block + sub - depth - 1, jnp.max(c_ref[sub])))

    def run(depths):
        heads = []
        for sub in range(subs):
            q = q_ref[0, sub * blk:(sub + 1) * blk, :]
            if fold_scale:
                q = (q.astype(F32) * scale).astype(BF16)
            zero = jnp.zeros_like(q)
            heads.append((jnp.where(first, q, zero), jnp.where(first, zero, q)))
        work = [(heads[sub], first_block + sub - back, causal if back == 0 else None, sub)
                for sub in range(subs) for back in range(depths[sub] + 1)]
        zero_c = [jnp.zeros((blk, 1), F32)] * 2
        accs, cs = finish(work, visit_many(work), {sub: zero_c for sub in range(subs)})
        for sub in range(subs):
            acc_ref[sub] = accs[sub]
            for h in range(2):
                c_ref[sub, h] = cs[sub][h]
        @pl.when(jnp.max(c_ref[...]) > SB_LOG_ZERO)
        def _():
            for sub in range(subs):
                remainder(sub, heads[sub], depths[sub])

        for sub in range(subs):
            o_ref[0, sub * blk:(sub + 1) * blk, :] = acc_ref[sub].astype(o_ref.dtype)

    assert subs >= SB_BAND
    pl.when(first_block > 0)(functools.partial(run, [SB_BAND] * subs))
    pl.when(first_block == 0)(functools.partial(run, [min(SB_BAND, sub) for sub in range(subs)]))


def _sb_attention(qkv, blk=128, rows=1024):
    b, s, n = qkv.shape
    width = n // 3
    hd = width // SB_HEADS
    assert 2 * hd == LANES
    pairs = width // LANES
    rows = min(rows, s)
    return pl.pallas_call(
        functools.partial(_sb_kernel, blk=blk, hd=hd),
        grid=(b, pairs, s // rows),
        in_specs=[
            pl.BlockSpec((1, rows, LANES), lambda bi, p, i: (bi, i, p)),
            pl.BlockSpec((1, s, LANES), lambda bi, p, i: (bi, 0, pairs + p)),
            pl.BlockSpec((1, s, LANES), lambda bi, p, i: (bi, 0, 2 * pairs + p)),
        ],
        out_specs=pl.BlockSpec((1, rows, LANES), lambda bi, p, i: (bi, i, p)),
        out_shape=jax.ShapeDtypeStruct((b, s, width), BF16),
        scratch_shapes=[pltpu.VMEM((rows // blk, blk, LANES), F32), pltpu.VMEM((rows // blk, 2, blk, 1), F32)],
        compiler_params=_params("parallel", "parallel", "arbitrary"),
        name="sb_attention",
    )(qkv, qkv, qkv)


def _gla_kernel(p_ref, wg_ref, bg_ref, ng_ref, o_ref, st_ref, *, rows, dk, dv):
    @pl.when(pl.program_id(1) == 0)
    def _():
        st_ref[...] = jnp.zeros_like(st_ref)

    heads = GLA_HEADS
    ch = GLA_CHUNK
    k0 = heads * dk
    v0 = 2 * heads * dk
    r0 = v0 + heads * dv
    g0 = r0 + heads * dv
    row = lax.broadcasted_iota(jnp.int32, (ch, ch), 0)
    col = lax.broadcasted_iota(jnp.int32, (ch, ch), 1)
    causal = col <= row
    lower = jnp.where(causal, 1.0, 0.0).astype(BF16)
    chunks = [slice(s * ch, (s + 1) * ch) for s in range(rows // ch)]
    g_pres = [_dot(p_ref[0, t, g0:g0 + LANES], wg_ref[...]) + bg_ref[...] for t in chunks]
    cums = []
    for g_pre in g_pres:
        log_a = (jnp.minimum(g_pre, 0.0) - jnp.log(1.0 + jnp.exp(-jnp.abs(g_pre)))) * (1.0 / GLA_GATE_TAU)
        hi, lo = _split_bf16(log_a)
        cums.append(_dot(lower, hi) + _dot(lower, lo))
    parts = []
    for t, cum in zip(chunks, cums):
        for h in range(heads):
            bh = cum[:, h * dk:(h + 1) * dk]
            b_last = bh[ch - 1:ch, :]
            q = p_ref[0, t, h * dk:(h + 1) * dk].astype(F32) * dk ** -0.5
            k = p_ref[0, t, k0 + h * dk:k0 + (h + 1) * dk].astype(F32)
            q_in = (q * jnp.exp(bh)).astype(BF16)
            k_in = (k * jnp.exp(-bh)).astype(BF16)
            k_state = (k * jnp.exp(b_last - bh)).astype(BF16)
            parts.append((t, h, q_in, k_in, k_state, jnp.exp(b_last)))
    scores = [jnp.where(causal, _dot_nt(q_in, k_in), 0.0).astype(BF16) for _, _, q_in, k_in, _, _ in parts]
    values = [p_ref[0, t, v0 + h * dv:v0 + (h + 1) * dv] for t, h, _, _, _, _ in parts]
    intra = [_dot(sc, v) for sc, v in zip(scores, values)]
    updates = [_dot_tn(v, k_state) for v, (_, _, _, _, k_state, _) in zip(values, parts)]
    states = [st_ref[h] for h in range(heads)]
    for n, (t, h, q_in, _, _, decay) in enumerate(parts):
        o = intra[n] + _dot_nt(q_in, states[h].astype(BF16))
        states[h] = decay * states[h] + updates[n]
        o = o * lax.rsqrt(jnp.mean(o * o, axis=-1, keepdims=True) + RMS_EPS) * ng_ref[...]
        r = p_ref[0, t, r0 + h * dv:r0 + (h + 1) * dv].astype(F32)
        o_ref[0, t, h * dv:(h + 1) * dv] = (o * (r * _sigmoid(r))).astype(o_ref.dtype)
    for h in range(heads):
        st_ref[h] = states[h]


def _gla_core(proj, w_gate_up, b_gate, norm_g, dk, dv, rows=512):
    b, s, n = proj.shape
    rows = min(rows, s)
    hk = GLA_HEADS * dk
    return pl.pallas_call(
        functools.partial(_gla_kernel, rows=rows, dk=dk, dv=dv),
        grid=(b, s // rows),
        in_specs=[
            pl.BlockSpec((1, rows, n), lambda bi, i: (bi, i, 0)),
            pl.BlockSpec((LANES, hk), lambda bi, i: (0, 0)),
            pl.BlockSpec((1, hk), lambda bi, i: (0, 0)),
            pl.BlockSpec((1, dv), lambda bi, i: (0, 0)),
        ],
        out_specs=pl.BlockSpec((1, rows, GLA_HEADS * dv), lambda bi, i: (bi, i, 0)),
        out_shape=jax.ShapeDtypeStruct((b, s, GLA_HEADS * dv), BF16),
        scratch_shapes=[pltpu.VMEM((GLA_HEADS, dv, dk), F32)],
        compiler_params=_params("parallel", "arbitrary"),
        name="gla_core",
    )(proj, w_gate_up, b_gate, norm_g)


def _conv_res_kernel(gb_ref, gc_ref, u_ref, hgc_ref, hu_ref, cw_ref, cb_ref, w_ref, x_ref, g_ref, y_ref):
    u2 = gc_ref[0].astype(F32) * u_ref[0].astype(F32)
    halo = hgc_ref[0].astype(F32) * hu_ref[0].astype(F32)
    halo = jnp.where(pl.program_id(1) > 0, halo, 0.0)
    rows = lax.broadcasted_iota(jnp.int32, u2.shape, 0)
    prev1 = jnp.where(rows == 0, halo[7:8], pltpu.roll(u2, 1, 0))
    prev2 = jnp.where(rows == 0, halo[6:7], jnp.where(rows == 1, halo[7:8], pltpu.roll(u2, 2, 0)))
    cw = cw_ref[...]
    y = cw[0:1] * prev2 + cw[1:2] * prev1 + cw[2:3] * u2 + cb_ref[...]
    o = (gb_ref[0].astype(F32) * y).astype(BF16)
    y_ref[0] = x_ref[0] + g_ref[0] * _dot(o, w_ref[...])


def _conv_res(proj, conv_w, conv_b, w_out, x, gate, tm=1024):
    b, s, d = x.shape
    tm = min(tm, s)
    halo_blocks = tm // 8

    def halo_map(col):
        return lambda bi, i: (bi, jnp.maximum(i * halo_blocks - 1, 0), col)

    return pl.pallas_call(
        _conv_res_kernel,
        grid=(b, s // tm),
        in_specs=[
            pl.BlockSpec((1, tm, d), lambda bi, i: (bi, i, 0)),
            pl.BlockSpec((1, tm, d), lambda bi, i: (bi, i, 1)),
            pl.BlockSpec((1, tm, d), lambda bi, i: (bi, i, 2)),
            pl.BlockSpec((1, 8, d), halo_map(1)),
            pl.BlockSpec((1, 8, d), halo_map(2)),
            pl.BlockSpec((CONV_WIDTH, d), lambda bi, i: (0, 0)),
            pl.BlockSpec((1, d), lambda bi, i: (0, 0)),
            pl.BlockSpec((d, d), lambda bi, i: (0, 0)),
            pl.BlockSpec((1, tm, d), lambda bi, i: (bi, i, 0)),
            pl.BlockSpec((1, 1, d), lambda bi, i: (bi, 0, 0)),
        ],
        out_specs=pl.BlockSpec((1, tm, d), lambda bi, i: (bi, i, 0)),
        out_shape=jax.ShapeDtypeStruct((b, s, d), F32),
        compiler_params=_params("parallel", "parallel"),
        name="conv_res",
    )(proj, proj, proj, proj, proj, conv_w, conv_b.reshape(1, d), w_out, x, gate)


def _dil_kernel(q_ref, kc_ref, vc_ref, kp_ref, vp_ref, o_ref, l_ref, *, rows, hd, dilation, group):
    qb = DIL_QBLOCK
    n = pl.program_id(1)
    width = q_ref.shape[-1] // group
    scale = hd ** -0.5
    fold_scale = (hd & (hd - 1)) == 0 and (hd.bit_length() - 1) % 2 == 0
    first = lax.broadcasted_iota(jnp.int32, (qb, LANES), 1) < hd
    row = lax.broadcasted_iota(jnp.int32, (qb, qb), 0)
    col = lax.broadcasted_iota(jnp.int32, (qb, qb), 1)
    in_prev = col >= row
    in_cur = col <= row
    neg = -jnp.inf

    tiles = []
    for u, s in [(u, s) for u in range(group) for s in range(rows // qb)]:
        t = slice(s * qb, (s + 1) * qb)
        tp = slice((s - 1) * qb, s * qb)
        prev_mask = jnp.logical_and(in_prev, n > 0) if s == 0 else in_prev
        for p in range(width // LANES):
            c = slice(u * width + p * LANES, u * width + (p + 1) * LANES)
            q = q_ref[0, t, c]
            if fold_scale:
                q = (q.astype(F32) * scale).astype(BF16)
            zero = jnp.zeros_like(q)
            q_heads = (jnp.where(first, q, zero), jnp.where(first, zero, q))
            if s == 0:
                k_prev, v_prev = kp_ref[0, :, c], vp_ref[0, :, c]
            else:
                k_prev, v_prev = kc_ref[0, tp, c], vc_ref[0, tp, c]
            tiles.append(((u, s), p, q_heads, k_prev, kc_ref[0, t, c], v_prev, vc_ref[0, t, c], prev_mask))

    scores = []
    for _, _, q_heads, k_prev, k_cur, _, _, prev_mask in tiles:
        pair = []
        for h in range(2):
            z_prev, z_cur = _dot_nt(q_heads[h], k_prev), _dot_nt(q_heads[h], k_cur)
            if not fold_scale:
                z_prev, z_cur = z_prev * scale, z_cur * scale
            pair.append((jnp.where(prev_mask, z_prev, neg), jnp.where(in_cur, z_cur, neg)))
        scores.append(pair)

    probs = []
    for pair in scores:
        stats = []
        for z_prev, z_cur in pair:
            m = jnp.max(jnp.maximum(z_prev, z_cur), axis=1, keepdims=True)
            p_prev = jnp.exp(z_prev - m)
            p_cur = jnp.exp(z_cur - m)
            den = jnp.sum(p_prev + p_cur, axis=1, keepdims=True)
            stats.append((p_prev.astype(BF16), p_cur.astype(BF16), den, m))
        probs.append(stats)

    lane = lax.broadcasted_iota(jnp.int32, (qb, LANES), 1)
    lse_rows = {}
    for ((u, s), p, _, _, _, v_prev, v_cur, _), stats in zip(tiles, probs):
        if dilation == 1:
            t_out = slice(s * qb, (s + 1) * qb)
        else:
            rho = pl.program_id(2) * group + u
            t_out = pl.ds(s * qb * dilation + rho, qb, stride=dilation)
        outs = []
        lse = lse_rows.get((u, s), jnp.zeros((qb, LANES), F32))
        for h, (p_prev, p_cur, den, m) in enumerate(stats):
            outs.append((_dot(p_prev, v_prev) + _dot(p_cur, v_cur)) / den)
            lse = jnp.where(lane == 2 * p + h, m + jnp.log(den), lse)
        lse_rows[(u, s)] = lse
        o_ref[0, p, t_out, :] = jnp.where(first, outs[0], outs[1])
        if p == width // LANES - 1:
            l_ref[0, t_out, :] = lse


def _dil_group(view, dilation, window, seq):
    b, length, _ = view.shape
    width = DIL_HEADS * DIL_HEAD_DIM
    assert window == dilation * DIL_QBLOCK and seq % (dilation * DIL_QBLOCK) == 0
    rows = min(max(DIL_TILE_TOKENS // dilation, DIL_QBLOCK), 512, length)
    group = min(dilation, max(512 // rows, 1))
    prev_blocks = rows // DIL_QBLOCK
    per_part = dilation // group

    def cur(part):
        return lambda bi, i, j: (bi, i, part * per_part + j)

    def prev(part):
        return lambda bi, i, j: (bi, jnp.maximum(i * prev_blocks - 1, 0), part * per_part + j)

    pairs = width // LANES
    out_shape = jax.ShapeDtypeStruct((b, pairs, seq, LANES), F32)
    out_spec = pl.BlockSpec((1, pairs, rows * dilation, LANES), lambda bi, i, j: (bi, 0, i, 0))
    lse_shape = jax.ShapeDtypeStruct((b, seq, LANES), F32)
    lse_spec = pl.BlockSpec((1, rows * dilation, LANES), lambda bi, i, j: (bi, i, 0))
    return pl.pallas_call(
        functools.partial(_dil_kernel, rows=rows, hd=DIL_HEAD_DIM, dilation=dilation, group=group),
        grid=(b, length // rows, per_part),
        in_specs=[
            pl.BlockSpec((1, rows, group * width), cur(0)),
            pl.BlockSpec((1, rows, group * width), cur(1)),
            pl.BlockSpec((1, rows, group * width), cur(2)),
            pl.BlockSpec((1, DIL_QBLOCK, group * width), prev(1)),
            pl.BlockSpec((1, DIL_QBLOCK, group * width), prev(2)),
        ],
        out_specs=[out_spec, lse_spec],
        out_shape=[out_shape, lse_shape],
        compiler_params=_params("parallel", "parallel", "arbitrary"),
        name=f"dil_attn_r{dilation}",
    )(view, view, view, view, view)


def _dil_res_kernel(o0, o1, o2, l0, l1, l2, w_ref, x_ref, g_ref, y_ref):
    ls = (l0[0], l1[0], l2[0])
    m = jnp.maximum(jnp.maximum(ls[0], ls[1]), ls[2])
    es = [jnp.exp(l - m) for l in ls]
    total = es[0] + es[1] + es[2]
    ws = [e / total for e in es]
    hd = LANES // 2
    first = lax.broadcasted_iota(jnp.int32, (x_ref.shape[1], LANES), 1) < hd
    parts = []
    for p in range(o0.shape[1]):
        w = [jnp.where(first, wg[:, 2 * p:2 * p + 1], wg[:, 2 * p + 1:2 * p + 2]) for wg in ws]
        parts.append(w[0] * o0[0, p] + w[1] * o1[0, p] + w[2] * o2[0, p])
    o = jnp.concatenate(parts, axis=1)
    y_ref[0] = x_ref[0] + g_ref[0] * _dot(o.astype(BF16), w_ref[...])


def _dil_res(outs, lses, w_out, x, gate, tm=1024):
    b, s, d = x.shape
    k = w_out.shape[0]
    tm = min(tm, s)
    part = pl.BlockSpec((1, k // LANES, tm, LANES), lambda bi, i: (bi, 0, i, 0))
    lse = pl.BlockSpec((1, tm, LANES), lambda bi, i: (bi, i, 0))
    return pl.pallas_call(
        _dil_res_kernel,
        grid=(b, s // tm),
        in_specs=[part] * 3 + [lse] * 3 + [
            pl.BlockSpec((k, d), lambda bi, i: (0, 0)),
            pl.BlockSpec((1, tm, d), lambda bi, i: (bi, i, 0)),
            pl.BlockSpec((1, 1, d), lambda bi, i: (bi, 0, 0)),
        ],
        out_specs=pl.BlockSpec((1, tm, d), lambda bi, i: (bi, i, 0)),
        out_shape=jax.ShapeDtypeStruct((b, s, d), F32),
        compiler_params=_params("parallel", "parallel"),
        name="dil_res",
    )(*outs, *lses, w_out, x, gate)


def _router_kernel(x_ref, g_ref, sc_ref, sh_ref, wr_ref, br_ref, h_ref, route_ref, cnt_ref, run_ref, earlier_ref):
    tm = x_ref.shape[1]

    @pl.when(jnp.logical_and(pl.program_id(0) == 0, pl.program_id(1) == 0))
    def _():
        run_ref[...] = jnp.zeros_like(run_ref)
        row = lax.broadcasted_iota(jnp.int32, (tm, tm), 0)
        col = lax.broadcasted_iota(jnp.int32, (tm, tm), 1)
        earlier_ref[...] = jnp.where(col < row, 1.0, 0.0).astype(BF16)

    h = _mod_norm(x_ref[0], g_ref[...], sc_ref[0], sh_ref[0])
    _pack_slabs(h_ref, h)
    logits = _dot(h.astype(BF16), wr_ref[...].astype(BF16)) + br_ref[...]
    lane = lax.broadcasted_iota(jnp.int32, logits.shape, 1).astype(F32)
    neg = -jnp.inf
    far = float(LANES)

    def first_argmax(vals):
        top = jnp.max(vals, axis=1, keepdims=True)
        return top, jnp.min(jnp.where(vals == top, lane, far), axis=1, keepdims=True)

    grp_logits = jnp.where(lane < MOE_GROUPS, logits, neg)
    grp_max, grp = first_argmax(grp_logits)
    p_grp = 1.0 / jnp.sum(jnp.exp(grp_logits - grp_max), axis=1, keepdims=True)
    base = MOE_GROUPS + grp * MOE_PER_GROUP
    in_grp = jnp.logical_and(lane >= base, lane < base + MOE_PER_GROUP)
    exp_logits = jnp.where(in_grp, logits, neg)
    m1, i1 = first_argmax(exp_logits)
    m2, i2 = first_argmax(jnp.where(lane == i1, neg, exp_logits))
    e2 = jnp.exp(m2 - m1)
    g1 = p_grp / (1.0 + e2)
    g2 = g1 * e2
    e_a = i1 - MOE_GROUPS
    e_b = i2 - MOE_GROUPS

    pick_a = lane == e_a
    pick_b = lane == e_b
    both = jnp.where(jnp.logical_or(pick_a, pick_b), 1.0, 0.0)
    before = _dot(earlier_ref[...], both.astype(BF16)) + run_ref[...]
    rank_a = jnp.sum(jnp.where(pick_a, before, 0.0), axis=1, keepdims=True)
    rank_b = jnp.sum(jnp.where(pick_b, before, 0.0), axis=1, keepdims=True)
    run_ref[...] += jnp.sum(both, axis=0, keepdims=True)
    cnt_ref[...] = run_ref[...]

    route = jnp.where(lane == 0, e_a, jnp.where(lane == 1, e_b, 0.0))
    route = jnp.where(lane == 2, g1, jnp.where(lane == 3, g2, route))
    route_ref[...] = jnp.where(lane == 4, rank_a, jnp.where(lane == 5, rank_b, route))


def _router(x, g, sc, sh, w_route, b_route, tm=1024):
    b, s, d = x.shape
    tm = min(tm, s)
    steps = s // tm
    return pl.pallas_call(
        _router_kernel,
        grid=(b, steps),
        in_specs=[
            pl.BlockSpec((1, tm, d), lambda bi, i: (bi, i, 0)),
            pl.BlockSpec((1, d), lambda bi, i: (0, 0)),
            pl.BlockSpec((1, 1, d), lambda bi, i: (bi, 0, 0)),
            pl.BlockSpec((1, 1, d), lambda bi, i: (bi, 0, 0)),
            pl.BlockSpec((d, LANES), lambda bi, i: (0, 0)),
            pl.BlockSpec((1, LANES), lambda bi, i: (0, 0)),
        ],
        out_specs=[
            pl.BlockSpec((tm * (d // PACKED_LANES), LANES), lambda bi, i: (bi * steps + i, 0)),
            pl.BlockSpec((tm, LANES), lambda bi, i: (bi * steps + i, 0)),
            pl.BlockSpec((1, LANES), lambda bi, i: (0, 0)),
        ],
        out_shape=[
            jax.ShapeDtypeStruct((b * s * (d // PACKED_LANES), LANES), jnp.int32),
            jax.ShapeDtypeStruct((b * s, LANES), F32),
            jax.ShapeDtypeStruct((1, LANES), F32),
        ],
        scratch_shapes=[pltpu.VMEM((1, LANES), F32), pltpu.VMEM((tm, tm), BF16)],
        compiler_params=_params("arbitrary", "arbitrary"),
        name="moe_router",
    )(x, g.reshape(1, d), sc, sh, w_route, b_route)


def _sc_dispatch(h_slabs, dest, n_slots):
    t, chunks, _ = h_slabs.shape
    workers = SC_CORES * SC_SUBCORES
    per_worker = t // workers
    steps = per_worker // SC_WINDOW
    assert t % (workers * SC_WINDOW * 2) == 0
    mesh = plsc.VectorSubcoreMesh(core_axis_name="core", subcore_axis_name="subcore")

    @functools.partial(
        pl.kernel, mesh=mesh,
        out_type=jax.ShapeDtypeStruct((n_slots, chunks, LANES), h_slabs.dtype),
        scratch_types=[
            pltpu.VMEM((2, MOE_TOPK, SC_WINDOW), jnp.int32),
            pltpu.VMEM((2, SC_WINDOW, chunks, LANES), h_slabs.dtype),
            pltpu.SemaphoreType.DMA((2,)),
            pltpu.SemaphoreType.DMA((2,)),
        ])
    def dispatch(h_hbm, dest_hbm, slots_hbm, idx_v, rows_v, load_sem, store_sem):
        worker = lax.axis_index("subcore") * SC_CORES + lax.axis_index("core")
        base = worker * per_worker

        @pl.loop(0, steps, step=2)
        def _(step):
            for buf in range(2):
                off = base + (step + buf) * SC_WINDOW
                pltpu.async_copy(h_hbm.at[pl.ds(off, SC_WINDOW)], rows_v.at[buf], load_sem.at[buf])
                for k in range(MOE_TOPK):
                    pltpu.sync_copy(dest_hbm.at[k, pl.ds(off, SC_WINDOW)], idx_v.at[buf, k])
            for buf in range(2):
                off = base + (step + buf) * SC_WINDOW
                pltpu.make_async_copy(h_hbm.at[pl.ds(off, SC_WINDOW)], rows_v.at[buf], load_sem.at[buf]).wait()
                copies = [pltpu.async_copy(rows_v.at[buf], slots_hbm.at[idx_v.at[buf, k]], store_sem.at[buf])
                          for k in range(MOE_TOPK)]
                for cp in copies:
                    cp.wait()

    return dispatch(h_slabs, dest)


def _zero_tail_kernel(tail_ref, valid_ref, s_in, s_out):
    del tail_ref
    rows = lax.broadcasted_iota(jnp.int32, s_in.shape, 0)
    s_out[...] = jnp.where(rows < valid_ref[pl.program_id(0)], s_in[...], 0)


def _zero_tails(tail_blk, valid_rows, slots, block_rows):
    grid_spec = pltpu.PrefetchScalarGridSpec(
        num_scalar_prefetch=2,
        grid=(tail_blk.shape[0],),
        in_specs=[pl.BlockSpec((block_rows, LANES), lambda e, tail, valid: (tail[e], 0))],
        out_specs=pl.BlockSpec((block_rows, LANES), lambda e, tail, valid: (tail[e], 0)),
    )
    return pl.pallas_call(
        _zero_tail_kernel,
        grid_spec=grid_spec,
        out_shape=jax.ShapeDtypeStruct(slots.shape, slots.dtype),
        input_output_aliases={2: 0},
        compiler_params=_params("arbitrary"),
        name="moe_zero_tails",
    )(tail_blk, valid_rows, slots)


def _expert_kernel(blk_e_ref, used_ref, x_ref, wg_ref, wu_ref, wd_ref, y_ref, wg_bf, wu_bf, wd_bf):
    i = pl.program_id(0)
    live = i < used_ref[0]
    changed = jnp.logical_or(i == 0, blk_e_ref[i] != blk_e_ref[jnp.maximum(i - 1, 0)])

    @pl.when(jnp.logical_and(live, changed))
    def _():
        wg_bf[...] = wg_ref[0, 0].astype(BF16)
        wu_bf[...] = wu_ref[0, 0].astype(BF16)
        wd_bf[...] = wd_ref[0, 0].astype(BF16)

    @pl.when(live)
    def _():
        bm = x_ref.shape[0] // (wg_bf.shape[0] // PACKED_LANES)
        x = _unpack_slabs(x_ref, bm).astype(BF16)
        gate = _dot(x, wg_bf[...])
        up = _dot(x, wu_bf[...])
        act = (gate * _sigmoid(gate) * up).astype(BF16)
        _pack_slabs(y_ref, _dot(act, wd_bf[...]))

    @pl.when(jnp.logical_not(live))
    def _():
        y_ref[...] = jnp.zeros_like(y_ref)


def _expert_blocks(blk_e, used, slots, w_gate, w_up, w_down, layer, bm):
    nb = blk_e.shape[0]
    _, _, d, f = w_gate.shape
    chunks = d // PACKED_LANES

    def block(i, e, n):
        return (jnp.minimum(i, n[0] - 1), 0)

    def weight(i, e, n):
        return (layer, e[jnp.minimum(i, n[0] - 1)], 0, 0)

    grid_spec = pltpu.PrefetchScalarGridSpec(
        num_scalar_prefetch=2,
        grid=(nb,),
        in_specs=[
            pl.BlockSpec((bm * chunks, LANES), block),
            pl.BlockSpec((1, 1, d, f), weight),
            pl.BlockSpec((1, 1, d, f), weight),
            pl.BlockSpec((1, 1, f, d), weight),
        ],
        out_specs=pl.BlockSpec((bm * chunks, LANES), lambda i, e, n: (i, 0)),
        scratch_shapes=[pltpu.VMEM((d, f), BF16), pltpu.VMEM((d, f), BF16), pltpu.VMEM((f, d), BF16)],
    )
    return pl.pallas_call(
        _expert_kernel,
        grid_spec=grid_spec,
        out_shape=jax.ShapeDtypeStruct((nb * bm * chunks, LANES), jnp.int32),
        compiler_params=_params("arbitrary"),
        name="moe_experts",
    )(blk_e, used, slots, w_gate, w_up, w_down)


def _sc_gather(y_slabs, idx):
    n = idx.shape[0]
    _, chunks, _ = y_slabs.shape
    workers = SC_CORES * SC_SUBCORES
    per_worker = n // workers
    steps = per_worker // SC_WINDOW
    assert n % (workers * SC_WINDOW * 2) == 0
    mesh = plsc.VectorSubcoreMesh(core_axis_name="core", subcore_axis_name="subcore")

    @functools.partial(
        pl.kernel, mesh=mesh,
        out_type=jax.ShapeDtypeStruct((n, chunks, LANES), y_slabs.dtype),
        scratch_types=[
            pltpu.VMEM((2, SC_WINDOW), jnp.int32),
            pltpu.VMEM((2, SC_WINDOW, chunks, LANES), y_slabs.dtype),
            pltpu.SemaphoreType.DMA((2,)),
        ])
    def gather(y_hbm, idx_hbm, out_hbm, idx_v, rows_v, sem):
        worker = lax.axis_index("subcore") * SC_CORES + lax.axis_index("core")
        base = worker * per_worker

        @pl.loop(0, steps, step=2)
        def _(step):
            for buf in range(2):
                off = base + (step + buf) * SC_WINDOW
                pltpu.sync_copy(idx_hbm.at[pl.ds(off, SC_WINDOW)], idx_v.at[buf])
                pltpu.async_copy(y_hbm.at[idx_v.at[buf]], rows_v.at[buf], sem.at[buf])
            for buf in range(2):
                off = base + (step + buf) * SC_WINDOW
                pltpu.make_async_copy(y_hbm.at[idx_v.at[buf]], rows_v.at[buf], sem.at[buf]).wait()
                pltpu.sync_copy(rows_v.at[buf], out_hbm.at[pl.ds(off, SC_WINDOW)])

    return gather(y_slabs, idx)


def _combine_kernel(g0_ref, g1_ref, route_ref, x_ref, gate_ref, fg_ref, o_ref, *, final):
    tm = x_ref.shape[0]
    route = route_ref[...]
    y = route[:, 2:3] * _unpack_slabs(g0_ref, tm) + route[:, 3:4] * _unpack_slabs(g1_ref, tm)
    out = x_ref[...] + gate_ref[0] * y
    if final:
        out = out * lax.rsqrt(jnp.mean(out * out, axis=-1, keepdims=True) + RMS_EPS) * fg_ref[...]
    o_ref[...] = out


def _combine(dest, route, y_slabs, x, gate, final_g, final, tm, pieces=8):
    b, s, d = x.shape
    t = b * s
    chunks = d // PACKED_LANES
    per_seq = s // tm
    piece = t // pieces
    tiles = piece // tm
    out = x.reshape(t, d)
    fg = final_g.reshape(1, d)
    for c in range(pieces):
        idx = dest[:, c * piece:(c + 1) * piece].reshape(MOE_TOPK * piece)
        g = _sc_gather(y_slabs, idx).reshape(MOE_TOPK * piece * chunks, LANES)
        first = c * tiles
        out = pl.pallas_call(
            functools.partial(_combine_kernel, final=final),
            grid=(tiles,),
            in_specs=[
                pl.BlockSpec((tm * chunks, LANES), lambda i: (i, 0)),
                pl.BlockSpec((tm * chunks, LANES), lambda i: (tiles + i, 0)),
                pl.BlockSpec((tm, LANES), lambda i, first=first: (first + i, 0)),
                pl.BlockSpec((tm, d), lambda i, first=first: (first + i, 0)),
                pl.BlockSpec((1, 1, d), lambda i, first=first: ((first + i) // per_seq, 0, 0)),
                pl.BlockSpec((1, d), lambda i: (0, 0)),
            ],
            out_specs=pl.BlockSpec((tm, d), lambda i, first=first: (first + i, 0)),
            out_shape=jax.ShapeDtypeStruct((t, d), F32),
            input_output_aliases={3: 0},
            compiler_params=_params("parallel"),
            name="moe_combine",
        )(g, g, route, out, gate, fg)
    return out.reshape(b, s, d)


def _slot_tables(route, counts, bm):
    t = route.shape[0]
    nb = MOE_TOPK * t // bm + MOE_EXPERTS
    counts = counts[0, :MOE_EXPERTS].astype(jnp.int32)
    padded = (counts + bm - 1) // bm * bm
    pad_end = jnp.cumsum(padded)
    pad_start = pad_end - padded
    expert = route[:, 0:MOE_TOPK].astype(jnp.int32)
    rank = route[:, 4:4 + MOE_TOPK].astype(jnp.int32)
    dest = pad_start[expert] + rank
    blk_start = jnp.arange(nb, dtype=jnp.int32) * bm
    blk_e = jnp.minimum(jnp.sum((pad_end[None, :] <= blk_start[:, None]).astype(jnp.int32), axis=1), MOE_EXPERTS - 1)
    used = (pad_end[-1:] // bm).astype(jnp.int32)
    tail_blk = jnp.maximum(pad_end // bm - 1, 0).astype(jnp.int32)
    tail_fill = jnp.where(counts % bm != 0, counts % bm, bm).astype(jnp.int32)
    return dest.T, blk_e, used, tail_blk, tail_fill, nb


def _moe_layer(x, g, sc, sh, gate, w_route, b_route, w_gate, w_up, w_down, layer, final_g, final, bm=512, tm=1024):
    b, s, d = x.shape
    t = b * s
    chunks = d // PACKED_LANES
    tm = min(tm, s)
    h_rows, route, counts = _router(x, g, sc, sh, w_route, b_route)
    dest, blk_e, used, tail_blk, tail_fill, nb = _slot_tables(route, counts, bm)
    slots = _sc_dispatch(h_rows.reshape(t, chunks, LANES), dest, nb * bm)
    slots = _zero_tails(tail_blk, tail_fill * chunks, slots.reshape(nb * bm * chunks, LANES), bm * chunks)
    y_slabs = _expert_blocks(blk_e, used, slots, w_gate, w_up, w_down, layer, bm).reshape(nb * bm, chunks, LANES)
    return _combine(dest, route, y_slabs, x, gate, final_g, final, tm)


def kernel(x, c, ada_w, ada_b, norm_g, final_g, sb_w_in, sb_w_out, gla_w_in, gla_w_gate_up, gla_b_gate, gla_norm_g, gla_w_out, conv_w_in, conv_w, conv_b, conv_w_out, dil_w_in, dil_w_out, moe_w_grp, moe_b_grp, moe_w_exp, moe_b_exp, moe_w_gate, moe_w_up, moe_w_down):
    b, s, d = x.shape
    depth = ada_w.shape[0]
    mod = _ada_mod(c, ada_w, ada_b).reshape(depth, b, N_MOD, 1, d)
    for i in range(depth):
        sh1, sc1, g1, sh2, sc2, g2 = (mod[i, :, m] for m in range(N_MOD))
        kind, j = i % 4, i // 4
        if kind == 0:
            qkv = _norm_proj(x, norm_g[i, 0], sc1, sh1, sb_w_in[j].astype(BF16))
            x = _proj_res(_sb_attention(qkv), sb_w_out[j].astype(BF16), x, g1)
        elif kind == 1:
            dk = gla_w_gate_up.shape[-1] // GLA_HEADS
            dv = gla_norm_g.shape[-1]
            n_in = gla_w_in.shape[-1]
            w_in = jnp.pad(gla_w_in[j], ((0, 0), (0, -n_in % MXU_COLUMNS))).astype(BF16)
            w_up = jnp.pad(gla_w_gate_up[j], ((0, LANES - GLA_GATE_RANK), (0, 0))).astype(BF16)
            proj = _norm_proj(x, norm_g[i, 0], sc1, sh1, w_in, tn=MXU_COLUMNS)
            o = _gla_core(proj, w_up, gla_b_gate[j].reshape(1, -1), gla_norm_g[j].reshape(1, -1), dk, dv)
            x = _proj_res(o, gla_w_out[j].astype(BF16), x, g1)
        elif kind == 2:
            proj = _norm_proj(x, norm_g[i, 0], sc1, sh1, conv_w_in[j].astype(BF16))
            x = _conv_res(proj, conv_w[j], conv_b[j], conv_w_out[j].astype(BF16), x, g1)
        else:
            group_w = 3 * DIL_HEADS * DIL_HEAD_DIM
            outs, lses = [], []
            for g, (window, dilation) in enumerate(DIL_PATTERNS):
                w_g = dil_w_in[j][:, g * group_w:(g + 1) * group_w].astype(BF16)
                view = _norm_proj(x, norm_g[i, 0], sc1, sh1, w_g, dilation=dilation)
                o_g, l_g = _dil_group(view, dilation, window, s)
                outs.append(o_g)
                lses.append(l_g)
            x = _dil_res(outs, lses, dil_w_out[j].astype(BF16), x, g1)
        w_route = jnp.pad(jnp.concatenate([moe_w_grp[i], moe_w_exp[i]], axis=1),
                          ((0, 0), (0, LANES - MOE_GROUPS - MOE_EXPERTS)))
        b_route = jnp.pad(jnp.concatenate([moe_b_grp[i], moe_b_exp[i]]), (0, LANES - MOE_GROUPS - MOE_EXPERTS))
        x = _moe_layer(x, norm_g[i, 1], sc2, sh2, g2, w_route, b_route.reshape(1, LANES),
                       moe_w_gate, moe_w_up, moe_w_down, i, final_g, final=(i == depth - 1))
    return x
```

```python
import functools

import jax
import jax.numpy as jnp
from jax import lax
from jax.experimental import pallas as pl
from jax.experimental.pallas import tpu as pltpu
from jax.experimental.pallas import tpu_sc as plsc

F32 = jnp.float32
BF16 = jnp.bfloat16

LANES = 128
MXU_COLUMNS = 256
RMS_EPS = 1e-6
N_MOD = 6

SB_HEADS = 16
GLA_HEADS = 4
GLA_GATE_RANK = 16
GLA_GATE_TAU = 16.0
GLA_CHUNK = 64
CONV_WIDTH = 3
DIL_PATTERNS = ((128, 1), (512, 4), (2048, 16))
DIL_HEADS = 8
DIL_HEAD_DIM = 64
DIL_QBLOCK = 128
DIL_TILE_TOKENS = 2048
MOE_GROUPS = 4
MOE_PER_GROUP = 8
MOE_EXPERTS = MOE_GROUPS * MOE_PER_GROUP
MOE_TOPK = 2

SC_CORES = 2
SC_SUBCORES = 16
SC_WINDOW = 64
PACKED_LANES = 2 * LANES

SB_LOG_ZERO = -104.0
SB_BAND = 2

VMEM_LIMIT = 48 * 1024 * 1024


def _params(*sem):
    return pltpu.CompilerParams(dimension_semantics=sem, vmem_limit_bytes=VMEM_LIMIT)


def _sigmoid(x):
    return 1.0 / (1.0 + jnp.exp(-x))


def _softplus(x):
    return jnp.maximum(x, 0.0) + jnp.log(1.0 + jnp.exp(-jnp.abs(x)))


def _split_bf16(x):
    hi = x.astype(BF16)
    lo = (x - hi.astype(F32)).astype(BF16)
    return hi, lo


def _dot(a, b):
    return jnp.dot(a, b, preferred_element_type=F32)


def _dot_nt(a, b):
    return lax.dot_general(a, b, (((1,), (1,)), ((), ())), preferred_element_type=F32)


def _dot_tn(a, b):
    return lax.dot_general(a, b, (((0,), (0,)), ((), ())), preferred_element_type=F32)


def _unpack_slabs(ref, rows):
    k = ref.shape[0] // rows
    words = jnp.concatenate([ref[pl.ds(s, rows, stride=k), :] for s in range(k)], axis=1)
    halves = [pltpu.unpack_elementwise(words, index=i, packed_dtype=BF16, unpacked_dtype=F32) for i in range(2)]
    return jnp.concatenate(halves, axis=1)


def _pack_slabs(ref, values):
    rows, d = values.shape
    half = d // 2
    words = pltpu.pack_elementwise([values[:, :half], values[:, half:]], packed_dtype=BF16)
    for s in range(half // LANES):
        ref[pl.ds(s, rows, stride=half // LANES), :] = words[:, s * LANES:(s + 1) * LANES]


def _mod_norm(x, g, sc, sh):
    r = lax.rsqrt(jnp.mean(x * x, axis=-1, keepdims=True) + RMS_EPS)
    return (x * r) * g * (1.0 + sc) + sh


def _ada_kernel(c_ref, w_ref, b_ref, o_ref):
    c = c_ref[...]
    cond = c * _sigmoid(c)
    o_ref[0] = _dot(cond.astype(BF16), w_ref[0].astype(BF16)) + b_ref[0]


def _ada_mod(c, ada_w, ada_b):
    depth, d, n = ada_w.shape
    b = c.shape[0]
    tn = 1536
    return pl.pallas_call(
        _ada_kernel,
        grid=(depth, n // tn),
        in_specs=[
            pl.BlockSpec((b, d), lambda i, j: (0, 0)),
            pl.BlockSpec((1, d, tn), lambda i, j: (i, 0, j)),
            pl.BlockSpec((1, 1, tn), lambda i, j: (i, 0, j)),
        ],
        out_specs=pl.BlockSpec((1, b, tn), lambda i, j: (i, 0, j)),
        out_shape=jax.ShapeDtypeStruct((depth, b, n), F32),
        compiler_params=_params("parallel", "parallel"),
        name="ada_mod",
    )(c, ada_w, ada_b.reshape(depth, 1, n))


def _norm_proj_kernel(x_ref, g_ref, sc_ref, sh_ref, w_ref, o_ref, h_ref, *res_ref, dilation, tn):
    h_ref[...] = _mod_norm(x_ref[0], g_ref[...], sc_ref[0], sh_ref[0]).astype(BF16)
    tm = h_ref.shape[0]
    for j in range(w_ref.shape[1] // tn):
        res = _dot(h_ref[...], w_ref[:, j * tn:(j + 1) * tn])
        if dilation == 1:
            o_ref[0, :, j * tn:(j + 1) * tn] = res.astype(o_ref.dtype)
            continue
        for c in range(tn // LANES):
            res_ref[0][j * (tn // LANES) + c] = res[:, c * LANES:(c + 1) * LANES]
        for rho in range(dilation):
            for c in range(tn // LANES):
                rows = res_ref[0][j * (tn // LANES) + c, pl.ds(rho, tm // dilation, stride=dilation), :]
                col = (j * dilation + rho) * tn + c * LANES
                o_ref[0, :, col:col + LANES] = rows.astype(o_ref.dtype)


def _norm_proj(x, g, sc, sh, w, tm=1024, tn=512, dilation=1):
    b, s, d = x.shape
    n = w.shape[1]
    tm = min(tm, s)
    assert s % tm == 0 and n % tn == 0 and tm % (16 * dilation) == 0
    scratch = [pltpu.VMEM((tm, d), BF16)]
    if dilation > 1:
        scratch.append(pltpu.VMEM((n // LANES, tm, LANES), F32))
    return pl.pallas_call(
        functools.partial(_norm_proj_kernel, dilation=dilation, tn=tn),
        grid=(b, s // tm),
        in_specs=[
            pl.BlockSpec((1, tm, d), lambda bi, i: (bi, i, 0)),
            pl.BlockSpec((1, d), lambda bi, i: (0, 0)),
            pl.BlockSpec((1, 1, d), lambda bi, i: (bi, 0, 0)),
            pl.BlockSpec((1, 1, d), lambda bi, i: (bi, 0, 0)),
            pl.BlockSpec((d, n), lambda bi, i: (0, 0)),
        ],
        out_specs=pl.BlockSpec((1, tm // dilation, dilation * n), lambda bi, i: (bi, i, 0)),
        out_shape=jax.ShapeDtypeStruct((b, s // dilation, dilation * n), BF16),
        scratch_shapes=scratch,
        compiler_params=_params("parallel", "parallel"),
        name="norm_proj",
    )(x, g.reshape(1, d), sc, sh, w)


def _proj_res_kernel(o_ref, w_ref, x_ref, g_ref, y_ref):
    y_ref[0] = x_ref[0] + g_ref[0] * _dot(o_ref[0], w_ref[...])


def _proj_res(o, w, x, gate, tm=1024):
    b, s, d = x.shape
    k = o.shape[-1]
    tm = min(tm, s)
    return pl.pallas_call(
        _proj_res_kernel,
        grid=(b, s // tm),
        in_specs=[
            pl.BlockSpec((1, tm, k), lambda bi, i: (bi, i, 0)),
            pl.BlockSpec((k, d), lambda bi, i: (0, 0)),
            pl.BlockSpec((1, tm, d), lambda bi, i: (bi, i, 0)),
            pl.BlockSpec((1, 1, d), lambda bi, i: (bi, 0, 0)),
        ],
        out_specs=pl.BlockSpec((1, tm, d), lambda bi, i: (bi, i, 0)),
        out_shape=jax.ShapeDtypeStruct((b, s, d), F32),
        compiler_params=_params("parallel", "parallel"),
        name="proj_res",
    )(o, w, x, gate)


def _sb_kernel(q_ref, k_ref, v_ref, o_ref, acc_ref, c_ref, *, blk, hd):
    scale = hd ** -0.5
    fold_scale = (hd & (hd - 1)) == 0 and (hd.bit_length() - 1) % 2 == 0
    first = lax.broadcasted_iota(jnp.int32, (blk, LANES), 1) < hd
    row = lax.broadcasted_iota(jnp.int32, (blk, blk), 0)
    col = lax.broadcasted_iota(jnp.int32, (blk, blk), 1)
    minus_later = jnp.where(row > col, -1.0, 0.0).astype(BF16)
    causal = col < row
    subs = q_ref.shape[1] // blk
    first_block = pl.program_id(2) * subs

    def visit_many(work):
        kv = []
        for _, block, _, _ in work:
            start = pl.multiple_of(block * blk, blk)
            kv.append((k_ref[0, pl.ds(start, blk), :], v_ref[0, pl.ds(start, blk), :]))
        zs = [[_dot_nt(q_heads[h], kb) for h in range(2)] for (q_heads, _, _, _), (kb, _) in zip(work, kv)]
        log_betas, sps = [], []
        for (_, _, mask, _), z2 in zip(work, zs):
            lb2, sp2 = [], []
            for z in z2:
                if not fold_scale:
                    z = z * scale
                sp = _softplus(z)
                lb2.append(z - sp)
                sp2.append(sp if mask is None else jnp.where(mask, sp, 0.0))
            log_betas.append(lb2)
            sps.append(sp2)
        suffixes = [[_dot(sp.astype(BF16), minus_later) for sp in sp2] for sp2 in sps]
        return kv, log_betas, sps, suffixes

    def finish(work, staged, c_in):
        kv, log_betas, sps, suffixes = staged
        cs = {chain: list(c) for chain, c in c_in.items()}
        weights = []
        for n, (_, _, mask, chain) in enumerate(work):
            a2 = []
            for h in range(2):
                a = jnp.exp(log_betas[n][h] + suffixes[n][h] + cs[chain][h])
                a2.append(a if mask is None else jnp.where(mask, a, 0.0))
                cs[chain][h] = cs[chain][h] - jnp.sum(sps[n][h], axis=1, keepdims=True)
            weights.append(a2)
        accs = {}
        for n, (_, _, _, chain) in enumerate(work):
            outs = [_dot(weights[n][h].astype(BF16), kv[n][1]) for h in range(2)]
            more = jnp.where(first, outs[0], outs[1])
            accs[chain] = more if chain not in accs else accs[chain] + more
        return accs, cs

    def remainder(sub, q_heads, depth):
        def cond(carry):
            j, cmax = carry
            return jnp.logical_and(j >= 0, cmax > SB_LOG_ZERO)

        def body(carry):
            j, _ = carry
            work = [(q_heads, j, None, sub)]
            accs, cs = finish(work, visit_many(work), {sub: [c_ref[sub, 0], c_ref[sub, 1]]})
            acc_ref[sub] += accs[sub]
            for h in range(2):
                c_ref[sub, h] = cs[sub][h]
            return j - 1, jnp.max(c_ref[sub])

        lax.while_loop(cond, body, (first_block + sub - depth - 1, jnp.max(c_ref[sub])))

    def run(depths):
        heads = []
        for sub in range(subs):
            q = q_ref[0, sub * blk:(sub + 1) * blk, :]
            if fold_scale:
                q = (q.astype(F32) * scale).astype(BF16)
            zero = jnp.zeros_like(q)
            heads.append((jnp.where(first, q, zero), jnp.where(first, zero, q)))
        work = [(heads[sub], first_block + sub - back, causal if back == 0 else None, sub)
                for sub in range(subs) for back in range(depths[sub] + 1)]
        zero_c = [jnp.zeros((blk, 1), F32)] * 2
        accs, cs = finish(work, visit_many(work), {sub: zero_c for sub in range(subs)})
        for sub in range(subs):
            acc_ref[sub] = accs[sub]
            for h in range(2):
                c_ref[sub, h] = cs[sub][h]
        @pl.when(jnp.max(c_ref[...]) > SB_LOG_ZERO)
        def _():
            for sub in range(subs):
                remainder(sub, heads[sub], depths[sub])

        for sub in range(subs):
            o_ref[0, sub * blk:(sub + 1) * blk, :] = acc_ref[sub].astype(o_ref.dtype)

    assert subs >= SB_BAND
    pl.when(first_block > 0)(functools.partial(run, [SB_BAND] * subs))
    pl.when(first_block == 0)(functools.partial(run, [min(SB_BAND, sub) for sub in range(subs)]))


def _sb_attention(qkv, blk=128, rows=1024):
    b, s, n = qkv.shape
    width = n // 3
    hd = width // SB_HEADS
    assert 2 * hd == LANES
    pairs = width // LANES
    rows = min(rows, s)
    return pl.pallas_call(
        functools.partial(_sb_kernel, blk=blk, hd=hd),
        grid=(b, pairs, s // rows),
        in_specs=[
            pl.BlockSpec((1, rows, LANES), lambda bi, p, i: (bi, i, p)),
            pl.BlockSpec((1, s, LANES), lambda bi, p, i: (bi, 0, pairs + p)),
            pl.BlockSpec((1, s, LANES), lambda bi, p, i: (bi, 0, 2 * pairs + p)),
        ],
        out_specs=pl.BlockSpec((1, rows, LANES), lambda bi, p, i: (bi, i, p)),
        out_shape=jax.ShapeDtypeStruct((b, s, width), BF16),
        scratch_shapes=[pltpu.VMEM((rows // blk, blk, LANES), F32), pltpu.VMEM((rows // blk, 2, blk, 1), F32)],
        compiler_params=_params("parallel", "parallel", "arbitrary"),
        name="sb_attention",
    )(qkv, qkv, qkv)


def _gla_kernel(p_ref, wg_ref, bg_ref, ng_ref, o_ref, st_ref, *, rows, dk, dv):
    @pl.when(pl.program_id(1) == 0)
    def _():
        st_ref[...] = jnp.zeros_like(st_ref)

    heads = GLA_HEADS
    ch = GLA_CHUNK
    k0 = heads * dk
    v0 = 2 * heads * dk
    r0 = v0 + heads * dv
    g0 = r0 + heads * dv
    row = lax.broadcasted_iota(jnp.int32, (ch, ch), 0)
    col = lax.broadcasted_iota(jnp.int32, (ch, ch), 1)
    causal = col <= row
    lower = jnp.where(causal, 1.0, 0.0).astype(BF16)
    chunks = [slice(s * ch, (s + 1) * ch) for s in range(rows // ch)]
    g_pres = [_dot(p_ref[0, t, g0:g0 + LANES], wg_ref[...]) + bg_ref[...] for t in chunks]
    cums = []
    for g_pre in g_pres:
        log_a = (jnp.minimum(g_pre, 0.0) - jnp.log(1.0 + jnp.exp(-jnp.abs(g_pre)))) * (1.0 / GLA_GATE_TAU)
        hi, lo = _split_bf16(log_a)
        cums.append(_dot(lower, hi) + _dot(lower, lo))
    parts = []
    for t, cum in zip(chunks, cums):
        for h in range(heads):
            bh = cum[:, h * dk:(h + 1) * dk]
            b_last = bh[ch - 1:ch, :]
            q = p_ref[0, t, h * dk:(h + 1) * dk].astype(F32) * dk ** -0.5
            k = p_ref[0, t, k0 + h * dk:k0 + (h + 1) * dk].astype(F32)
            q_in = (q * jnp.exp(bh)).astype(BF16)
            k_in = (k * jnp.exp(-bh)).astype(BF16)
            k_state = (k * jnp.exp(b_last - bh)).astype(BF16)
            parts.append((t, h, q_in, k_in, k_state, jnp.exp(b_last)))
    scores = [jnp.where(causal, _dot_nt(q_in, k_in), 0.0).astype(BF16) for _, _, q_in, k_in, _, _ in parts]
    values = [p_ref[0, t, v0 + h * dv:v0 + (h + 1) * dv] for t, h, _, _, _, _ in parts]
    intra = [_dot(sc, v) for sc, v in zip(scores, values)]
    updates = [_dot_tn(v, k_state) for v, (_, _, _, _, k_state, _) in zip(values, parts)]
    states = [st_ref[h] for h in range(heads)]
    for n, (t, h, q_in, _, _, decay) in enumerate(parts):
        o = intra[n] + _dot_nt(q_in, states[h].astype(BF16))
        states[h] = decay * states[h] + updates[n]
        o = o * lax.rsqrt(jnp.mean(o * o, axis=-1, keepdims=True) + RMS_EPS) * ng_ref[...]
        r = p_ref[0, t, r0 + h * dv:r0 + (h + 1) * dv].astype(F32)
        o_ref[0, t, h * dv:(h + 1) * dv] = (o * (r * _sigmoid(r))).astype(o_ref.dtype)
    for h in range(heads):
        st_ref[h] = states[h]


def _gla_core(proj, w_gate_up, b_gate, norm_g, dk, dv, rows=512):
    b, s, n = proj.shape
    rows = min(rows, s)
    hk = GLA_HEADS * dk
    return pl.pallas_call(
        functools.partial(_gla_kernel, rows=rows, dk=dk, dv=dv),
        grid=(b, s // rows),
        in_specs=[
            pl.BlockSpec((1, rows, n), lambda bi, i: (bi, i, 0)),
            pl.BlockSpec((LANES, hk), lambda bi, i: (0, 0)),
            pl.BlockSpec((1, hk), lambda bi, i: (0, 0)),
            pl.BlockSpec((1, dv), lambda bi, i: (0, 0)),
        ],
        out_specs=pl.BlockSpec((1, rows, GLA_HEADS * dv), lambda bi, i: (bi, i, 0)),
        out_shape=jax.ShapeDtypeStruct((b, s, GLA_HEADS * dv), BF16),
        scratch_shapes=[pltpu.VMEM((GLA_HEADS, dv, dk), F32)],
        compiler_params=_params("parallel", "arbitrary"),
        name="gla_core",
    )(proj, w_gate_up, b_gate, norm_g)


def _conv_res_kernel(gb_ref, gc_ref, u_ref, hgc_ref, hu_ref, cw_ref, cb_ref, w_ref, x_ref, g_ref, y_ref):
    u2 = gc_ref[0].astype(F32) * u_ref[0].astype(F32)
    halo = hgc_ref[0].astype(F32) * hu_ref[0].astype(F32)
    halo = jnp.where(pl.program_id(1) > 0, halo, 0.0)
    rows = lax.broadcasted_iota(jnp.int32, u2.shape, 0)
    prev1 = jnp.where(rows == 0, halo[7:8], pltpu.roll(u2, 1, 0))
    prev2 = jnp.where(rows == 0, halo[6:7], jnp.where(rows == 1, halo[7:8], pltpu.roll(u2, 2, 0)))
    cw = cw_ref[...]
    y = cw[0:1] * prev2 + cw[1:2] * prev1 + cw[2:3] * u2 + cb_ref[...]
    o = (gb_ref[0].astype(F32) * y).astype(BF16)
    y_ref[0] = x_ref[0] + g_ref[0] * _dot(o, w_ref[...])


def _conv_res(proj, conv_w, conv_b, w_out, x, gate, tm=1024):
    b, s, d = x.shape
    tm = min(tm, s)
    halo_blocks = tm // 8

    def halo_map(col):
        return lambda bi, i: (bi, jnp.maximum(i * halo_blocks - 1, 0), col)

    return pl.pallas_call(
        _conv_res_kernel,
        grid=(b, s // tm),
        in_specs=[
            pl.BlockSpec((1, tm, d), lambda bi, i: (bi, i, 0)),
            pl.BlockSpec((1, tm, d), lambda bi, i: (bi, i, 1)),
            pl.BlockSpec((1, tm, d), lambda bi, i: (bi, i, 2)),
            pl.BlockSpec((1, 8, d), halo_map(1)),
            pl.BlockSpec((1, 8, d), halo_map(2)),
            pl.BlockSpec((CONV_WIDTH, d), lambda bi, i: (0, 0)),
            pl.BlockSpec((1, d), lambda bi, i: (0, 0)),
            pl.BlockSpec((d, d), lambda bi, i: (0, 0)),
            pl.BlockSpec((1, tm, d), lambda bi, i: (bi, i, 0)),
            pl.BlockSpec((1, 1, d), lambda bi, i: (bi, 0, 0)),
        ],
        out_specs=pl.BlockSpec((1, tm, d), lambda bi, i: (bi, i, 0)),
        out_shape=jax.ShapeDtypeStruct((b, s, d), F32),
        compiler_params=_params("parallel", "parallel"),
        name="conv_res",
    )(proj, proj, proj, proj, proj, conv_w, conv_b.reshape(1, d), w_out, x, gate)


def _dil_kernel(q_ref, kc_ref, vc_ref, kp_ref, vp_ref, o_ref, l_ref, *, rows, hd, dilation, group):
    qb = DIL_QBLOCK
    n = pl.program_id(1)
    width = q_ref.shape[-1] // group
    scale = hd ** -0.5
    fold_scale = (hd & (hd - 1)) == 0 and (hd.bit_length() - 1) % 2 == 0
    first = lax.broadcasted_iota(jnp.int32, (qb, LANES), 1) < hd
    row = lax.broadcasted_iota(jnp.int32, (qb, qb), 0)
    col = lax.broadcasted_iota(jnp.int32, (qb, qb), 1)
    in_prev = col >= row
    in_cur = col <= row
    neg = -jnp.inf

    tiles = []
    for u, s in [(u, s) for u in range(group) for s in range(rows // qb)]:
        t = slice(s * qb, (s + 1) * qb)
        tp = slice((s - 1) * qb, s * qb)
        prev_mask = jnp.logical_and(in_prev, n > 0) if s == 0 else in_prev
        for p in range(width // LANES):
            c = slice(u * width + p * LANES, u * width + (p + 1) * LANES)
            q = q_ref[0, t, c]
            if fold_scale:
                q = (q.astype(F32) * scale).astype(BF16)
            zero = jnp.zeros_like(q)
            q_heads = (jnp.where(first, q, zero), jnp.where(first, zero, q))
            if s == 0:
                k_prev, v_prev = kp_ref[0, :, c], vp_ref[0, :, c]
            else:
                k_prev, v_prev = kc_ref[0, tp, c], vc_ref[0, tp, c]
            tiles.append(((u, s), p, q_heads, k_prev, kc_ref[0, t, c], v_prev, vc_ref[0, t, c], prev_mask))

    scores = []
    for _, _, q_heads, k_prev, k_cur, _, _, prev_mask in tiles:
        pair = []
        for h in range(2):
            z_prev, z_cur = _dot_nt(q_heads[h], k_prev), _dot_nt(q_heads[h], k_cur)
            if not fold_scale:
                z_prev, z_cur = z_prev * scale, z_cur * scale
            pair.append((jnp.where(prev_mask, z_prev, neg), jnp.where(in_cur, z_cur, neg)))
        scores.append(pair)

    probs = []
    for pair in scores:
        stats = []
        for z_prev, z_cur in pair:
            m = jnp.max(jnp.maximum(z_prev, z_cur), axis=1, keepdims=True)
            p_prev = jnp.exp(z_prev - m)
            p_cur = jnp.exp(z_cur - m)
            den = jnp.sum(p_prev + p_cur, axis=1, keepdims=True)
            stats.append((p_prev.astype(BF16), p_cur.astype(BF16), den, m))
        probs.append(stats)

    lane = lax.broadcasted_iota(jnp.int32, (qb, LANES), 1)
    lse_rows = {}
    for ((u, s), p, _, _, _, v_prev, v_cur, _), stats in zip(tiles, probs):
        if dilation == 1:
            t_out = slice(s * qb, (s + 1) * qb)
        else:
            rho = pl.program_id(2) * group + u
            t_out = pl.ds(s * qb * dilation + rho, qb, stride=dilation)
        outs = []
        lse = lse_rows.get((u, s), jnp.zeros((qb, LANES), F32))
        for h, (p_prev, p_cur, den, m) in enumerate(stats):
            outs.append((_dot(p_prev, v_prev) + _dot(p_cur, v_cur)) / den)
            lse = jnp.where(lane == 2 * p + h, m + jnp.log(den), lse)
        lse_rows[(u, s)] = lse
        o_ref[0, p, t_out, :] = jnp.where(first, outs[0], outs[1])
        if p == width // LANES - 1:
            l_ref[0, t_out, :] = lse


def _dil_group(view, dilation, window, seq):
    b, length, _ = view.shape
    width = DIL_HEADS * DIL_HEAD_DIM
    assert window == dilation * DIL_QBLOCK and seq % (dilation * DIL_QBLOCK) == 0
    rows = min(max(DIL_TILE_TOKENS // dilation, DIL_QBLOCK), 512, length)
    group = min(dilation, max(512 // rows, 1))
    prev_blocks = rows // DIL_QBLOCK
    per_part = dilation // group

    def cur(part):
        return lambda bi, i, j: (bi, i, part * per_part + j)

    def prev(part):
        return lambda bi, i, j: (bi, jnp.maximum(i * prev_blocks - 1, 0), part * per_part + j)

    pairs = width // LANES
    out_shape = jax.ShapeDtypeStruct((b, pairs, seq, LANES), F32)
    out_spec = pl.BlockSpec((1, pairs, rows * dilation, LANES), lambda bi, i, j: (bi, 0, i, 0))
    lse_shape = jax.ShapeDtypeStruct((b, seq, LANES), F32)
    lse_spec = pl.BlockSpec((1, rows * dilation, LANES), lambda bi, i, j: (bi, i, 0))
    return pl.pallas_call(
        functools.partial(_dil_kernel, rows=rows, hd=DIL_HEAD_DIM, dilation=dilation, group=group),
        grid=(b, length // rows, per_part),
        in_specs=[
            pl.BlockSpec((1, rows, group * width), cur(0)),
            pl.BlockSpec((1, rows, group * width), cur(1)),
            pl.BlockSpec((1, rows, group * width), cur(2)),
            pl.BlockSpec((1, DIL_QBLOCK, group * width), prev(1)),
            pl.BlockSpec((1, DIL_QBLOCK, group * width), prev(2)),
        ],
        out_specs=[out_spec, lse_spec],
        out_shape=[out_shape, lse_shape],
        compiler_params=_params("parallel", "parallel", "arbitrary"),
        name=f"dil_attn_r{dilation}",
    )(view, view, view, view, view)


def _dil_res_kernel(o0, o1, o2, l0, l1, l2, w_ref, x_ref, g_ref, y_ref):
    ls = (l0[0], l1[0], l2[0])
    m = jnp.maximum(jnp.maximum(ls[0], ls[1]), ls[2])
    es = [jnp.exp(l - m) for l in ls]
    total = es[0] + es[1] + es[2]
    ws = [e / total for e in es]
    hd = LANES // 2
    first = lax.broadcasted_iota(jnp.int32, (x_ref.shape[1], LANES), 1) < hd
    parts = []
    for p in range(o0.shape[1]):
        w = [jnp.where(first, wg[:, 2 * p:2 * p + 1], wg[:, 2 * p + 1:2 * p + 2]) for wg in ws]
        parts.append(w[0] * o0[0, p] + w[1] * o1[0, p] + w[2] * o2[0, p])
    o = jnp.concatenate(parts, axis=1)
    y_ref[0] = x_ref[0] + g_ref[0] * _dot(o.astype(BF16), w_ref[...])


def _dil_res(outs, lses, w_out, x, gate, tm=1024):
    b, s, d = x.shape
    k = w_out.shape[0]
    tm = min(tm, s)
    part = pl.BlockSpec((1, k // LANES, tm, LANES), lambda bi, i: (bi, 0, i, 0))
    lse = pl.BlockSpec((1, tm, LANES), lambda bi, i: (bi, i, 0))
    return pl.pallas_call(
        _dil_res_kernel,
        grid=(b, s // tm),
        in_specs=[part] * 3 + [lse] * 3 + [
            pl.BlockSpec((k, d), lambda bi, i: (0, 0)),
            pl.BlockSpec((1, tm, d), lambda bi, i: (bi, i, 0)),
            pl.BlockSpec((1, 1, d), lambda bi, i: (bi, 0, 0)),
        ],
        out_specs=pl.BlockSpec((1, tm, d), lambda bi, i: (bi, i, 0)),
        out_shape=jax.ShapeDtypeStruct((b, s, d), F32),
        compiler_params=_params("parallel", "parallel"),
        name="dil_res",
    )(*outs, *lses, w_out, x, gate)


def _router_kernel(x_ref, g_ref, sc_ref, sh_ref, wr_ref, br_ref, h_ref, route_ref, cnt_ref, run_ref, earlier_ref):
    tm = x_ref.shape[1]

    @pl.when(jnp.logical_and(pl.program_id(0) == 0, pl.program_id(1) == 0))
    def _():
        run_ref[...] = jnp.zeros_like(run_ref)
        row = lax.broadcasted_iota(jnp.int32, (tm, tm), 0)
        col = lax.broadcasted_iota(jnp.int32, (tm, tm), 1)
        earlier_ref[...] = jnp.where(col < row, 1.0, 0.0).astype(BF16)

    h = _mod_norm(x_ref[0], g_ref[...], sc_ref[0], sh_ref[0])
    _pack_slabs(h_ref, h)
    logits = _dot(h.astype(BF16), wr_ref[...].astype(BF16)) + br_ref[...]
    lane = lax.broadcasted_iota(jnp.int32, logits.shape, 1).astype(F32)
    neg = -jnp.inf
    far = float(LANES)

    def first_argmax(vals):
        top = jnp.max(vals, axis=1, keepdims=True)
        return top, jnp.min(jnp.where(vals == top, lane, far), axis=1, keepdims=True)

    grp_logits = jnp.where(lane < MOE_GROUPS, logits, neg)
    grp_max, grp = first_argmax(grp_logits)
    p_grp = 1.0 / jnp.sum(jnp.exp(grp_logits - grp_max), axis=1, keepdims=True)
    base = MOE_GROUPS + grp * MOE_PER_GROUP
    in_grp = jnp.logical_and(lane >= base, lane < base + MOE_PER_GROUP)
    exp_logits = jnp.where(in_grp, logits, neg)
    m1, i1 = first_argmax(exp_logits)
    m2, i2 = first_argmax(jnp.where(lane == i1, neg, exp_logits))
    e2 = jnp.exp(m2 - m1)
    g1 = p_grp / (1.0 + e2)
    g2 = g1 * e2
    e_a = i1 - MOE_GROUPS
    e_b = i2 - MOE_GROUPS

    pick_a = lane == e_a
    pick_b = lane == e_b
    both = jnp.where(jnp.logical_or(pick_a, pick_b), 1.0, 0.0)
    before = _dot(earlier_ref[...], both.astype(BF16)) + run_ref[...]
    rank_a = jnp.sum(jnp.where(pick_a, before, 0.0), axis=1, keepdims=True)
    rank_b = jnp.sum(jnp.where(pick_b, before, 0.0), axis=1, keepdims=True)
    run_ref[...] += jnp.sum(both, axis=0, keepdims=True)
    cnt_ref[...] = run_ref[...]

    route = jnp.where(lane == 0, e_a, jnp.where(lane == 1, e_b, 0.0))
    route = jnp.where(lane == 2, g1, jnp.where(lane == 3, g2, route))
    route_ref[...] = jnp.where(lane == 4, rank_a, jnp.where(lane == 5, rank_b, route))


def _router(x, g, sc, sh, w_route, b_route, tm=1024):
    b, s, d = x.shape
    tm = min(tm, s)
    steps = s // tm
    return pl.pallas_call(
        _router_kernel,
        grid=(b, steps),
        in_specs=[
            pl.BlockSpec((1, tm, d), lambda bi, i: (bi, i, 0)),
            pl.BlockSpec((1, d), lambda bi, i: (0, 0)),
            pl.BlockSpec((1, 1, d), lambda bi, i: (bi, 0, 0)),
            pl.BlockSpec((1, 1, d), lambda bi, i: (bi, 0, 0)),
            pl.BlockSpec((d, LANES), lambda bi, i: (0, 0)),
            pl.BlockSpec((1, LANES), lambda bi, i: (0, 0)),
        ],
        out_specs=[
            pl.BlockSpec((tm * (d // PACKED_LANES), LANES), lambda bi, i: (bi * steps + i, 0)),
            pl.BlockSpec((tm, LANES), lambda bi, i: (bi * steps + i, 0)),
            pl.BlockSpec((1, LANES), lambda bi, i: (0, 0)),
        ],
        out_shape=[
            jax.ShapeDtypeStruct((b * s * (d // PACKED_LANES), LANES), jnp.int32),
            jax.ShapeDtypeStruct((b * s, LANES), F32),
            jax.ShapeDtypeStruct((1, LANES), F32),
        ],
        scratch_shapes=[pltpu.VMEM((1, LANES), F32), pltpu.VMEM((tm, tm), BF16)],
        compiler_params=_params("arbitrary", "arbitrary"),
        name="moe_router",
    )(x, g.reshape(1, d), sc, sh, w_route, b_route)


def _sc_dispatch(h_slabs, dest, n_slots):
    t, chunks, _ = h_slabs.shape
    workers = SC_CORES * SC_SUBCORES
    per_worker = t // workers
    steps = per_worker // SC_WINDOW
    assert t % (workers * SC_WINDOW * 2) == 0
    mesh = plsc.VectorSubcoreMesh(core_axis_name="core", subcore_axis_name="subcore")

    @functools.partial(
        pl.kernel, mesh=mesh,
        out_type=jax.ShapeDtypeStruct((n_slots, chunks, LANES), h_slabs.dtype),
        scratch_types=[
            pltpu.VMEM((2, MOE_TOPK, SC_WINDOW), jnp.int32),
            pltpu.VMEM((2, SC_WINDOW, chunks, LANES), h_slabs.dtype),
            pltpu.SemaphoreType.DMA((2,)),
            pltpu.SemaphoreType.DMA((2,)),
        ])
    def dispatch(h_hbm, dest_hbm, slots_hbm, idx_v, rows_v, load_sem, store_sem):
        worker = lax.axis_index("subcore") * SC_CORES + lax.axis_index("core")
        base = worker * per_worker

        @pl.loop(0, steps, step=2)
        def _(step):
            for buf in range(2):
                off = base + (step + buf) * SC_WINDOW
                pltpu.async_copy(h_hbm.at[pl.ds(off, SC_WINDOW)], rows_v.at[buf], load_sem.at[buf])
                for k in range(MOE_TOPK):
                    pltpu.sync_copy(dest_hbm.at[k, pl.ds(off, SC_WINDOW)], idx_v.at[buf, k])
            for buf in range(2):
                off = base + (step + buf) * SC_WINDOW
                pltpu.make_async_copy(h_hbm.at[pl.ds(off, SC_WINDOW)], rows_v.at[buf], load_sem.at[buf]).wait()
                copies = [pltpu.async_copy(rows_v.at[buf], slots_hbm.at[idx_v.at[buf, k]], store_sem.at[buf])
                          for k in range(MOE_TOPK)]
                for cp in copies:
                    cp.wait()

    return dispatch(h_slabs, dest)


def _zero_tail_kernel(tail_ref, valid_ref, s_in, s_out):
    del tail_ref
    rows = lax.broadcasted_iota(jnp.int32, s_in.shape, 0)
    s_out[...] = jnp.where(rows < valid_ref[pl.program_id(0)], s_in[...], 0)


def _zero_tails(tail_blk, valid_rows, slots, block_rows):
    grid_spec = pltpu.PrefetchScalarGridSpec(
        num_scalar_prefetch=2,
        grid=(tail_blk.shape[0],),
        in_specs=[pl.BlockSpec((block_rows, LANES), lambda e, tail, valid: (tail[e], 0))],
        out_specs=pl.BlockSpec((block_rows, LANES), lambda e, tail, valid: (tail[e], 0)),
    )
    return pl.pallas_call(
        _zero_tail_kernel,
        grid_spec=grid_spec,
        out_shape=jax.ShapeDtypeStruct(slots.shape, slots.dtype),
        input_output_aliases={2: 0},
        compiler_params=_params("arbitrary"),
        name="moe_zero_tails",
    )(tail_blk, valid_rows, slots)


def _expert_kernel(blk_e_ref, used_ref, x_ref, wg_ref, wu_ref, wd_ref, y_ref, wg_bf, wu_bf, wd_bf):
    i = pl.program_id(0)
    live = i < used_ref[0]
    changed = jnp.logical_or(i == 0, blk_e_ref[i] != blk_e_ref[jnp.maximum(i - 1, 0)])

    @pl.when(jnp.logical_and(live, changed))
    def _():
        wg_bf[...] = wg_ref[0, 0].astype(BF16)
        wu_bf[...] = wu_ref[0, 0].astype(BF16)
        wd_bf[...] = wd_ref[0, 0].astype(BF16)

    @pl.when(live)
    def _():
        bm = x_ref.shape[0] // (wg_bf.shape[0] // PACKED_LANES)
        x = _unpack_slabs(x_ref, bm).astype(BF16)
        gate = _dot(x, wg_bf[...])
        up = _dot(x, wu_bf[...])
        act = (gate * _sigmoid(gate) * up).astype(BF16)
        _pack_slabs(y_ref, _dot(act, wd_bf[...]))

    @pl.when(jnp.logical_not(live))
    def _():
        y_ref[...] = jnp.zeros_like(y_ref)


def _expert_blocks(blk_e, used, slots, w_gate, w_up, w_down, layer, bm):
    nb = blk_e.shape[0]
    _, _, d, f = w_gate.shape
    chunks = d // PACKED_LANES

    def block(i, e, n):
        return (jnp.minimum(i, n[0] - 1), 0)

    def weight(i, e, n):
        return (layer, e[jnp.minimum(i, n[0] - 1)], 0, 0)

    grid_spec = pltpu.PrefetchScalarGridSpec(
        num_scalar_prefetch=2,
        grid=(nb,),
        in_specs=[
            pl.BlockSpec((bm * chunks, LANES), block),
            pl.BlockSpec((1, 1, d, f), weight),
            pl.BlockSpec((1, 1, d, f), weight),
            pl.BlockSpec((1, 1, f, d), weight),
        ],
        out_specs=pl.BlockSpec((bm * chunks, LANES), lambda i, e, n: (i, 0)),
        scratch_shapes=[pltpu.VMEM((d, f), BF16), pltpu.VMEM((d, f), BF16), pltpu.VMEM((f, d), BF16)],
    )
    return pl.pallas_call(
        _expert_kernel,
        grid_spec=grid_spec,
        out_shape=jax.ShapeDtypeStruct((nb * bm * chunks, LANES), jnp.int32),
        compiler_params=_params("arbitrary"),
        name="moe_experts",
    )(blk_e, used, slots, w_gate, w_up, w_down)


def _sc_gather(y_slabs, idx):
    n = idx.shape[0]
    _, chunks, _ = y_slabs.shape
    workers = SC_CORES * SC_SUBCORES
    per_worker = n // workers
    steps = per_worker // SC_WINDOW
    assert n % (workers * SC_WINDOW * 2) == 0
    mesh = plsc.VectorSubcoreMesh(core_axis_name="core", subcore_axis_name="subcore")

    @functools.partial(
        pl.kernel, mesh=mesh,
        out_type=jax.ShapeDtypeStruct((n, chunks, LANES), y_slabs.dtype),
        scratch_types=[
            pltpu.VMEM((2, SC_WINDOW), jnp.int32),
            pltpu.VMEM((2, SC_WINDOW, chunks, LANES), y_slabs.dtype),
            pltpu.SemaphoreType.DMA((2,)),
        ])
    def gather(y_hbm, idx_hbm, out_hbm, idx_v, rows_v, sem):
        worker = lax.axis_index("subcore") * SC_CORES + lax.axis_index("core")
        base = worker * per_worker

        @pl.loop(0, steps, step=2)
        def _(step):
            for buf in range(2):
                off = base + (step + buf) * SC_WINDOW
                pltpu.sync_copy(idx_hbm.at[pl.ds(off, SC_WINDOW)], idx_v.at[buf])
                pltpu.async_copy(y_hbm.at[idx_v.at[buf]], rows_v.at[buf], sem.at[buf])
            for buf in range(2):
                off = base + (step + buf) * SC_WINDOW
                pltpu.make_async_copy(y_hbm.at[idx_v.at[buf]], rows_v.at[buf], sem.at[buf]).wait()
                pltpu.sync_copy(rows_v.at[buf], out_hbm.at[pl.ds(off, SC_WINDOW)])

    return gather(y_slabs, idx)


def _combine_kernel(g0_ref, g1_ref, route_ref, x_ref, gate_ref, fg_ref, o_ref, *, final):
    tm = x_ref.shape[0]
    route = route_ref[...]
    y = route[:, 2:3] * _unpack_slabs(g0_ref, tm) + route[:, 3:4] * _unpack_slabs(g1_ref, tm)
    out = x_ref[...] + gate_ref[0] * y
    if final:
        out = out * lax.rsqrt(jnp.mean(out * out, axis=-1, keepdims=True) + RMS_EPS) * fg_ref[...]
    o_ref[...] = out


def _combine(dest, route, y_slabs, x, gate, final_g, final, tm, pieces=16):
    b, s, d = x.shape
    t = b * s
    chunks = d // PACKED_LANES
    per_seq = s // tm
    piece = t // pieces
    tiles = piece // tm
    out = x.reshape(t, d)
    fg = final_g.reshape(1, d)
    for c in range(pieces):
        idx = dest[:, c * piece:(c + 1) * piece].reshape(MOE_TOPK * piece)
        g = _sc_gather(y_slabs, idx).reshape(MOE_TOPK * piece * chunks, LANES)
        first = c * tiles
        out = pl.pallas_call(
            functools.partial(_combine_kernel, final=final),
            grid=(tiles,),
            in_specs=[
                pl.BlockSpec((tm * chunks, LANES), lambda i: (i, 0)),
                pl.BlockSpec((tm * chunks, LANES), lambda i: (tiles + i, 0)),
                pl.BlockSpec((tm, LANES), lambda i, first=first: (first + i, 0)),
                pl.BlockSpec((tm, d), lambda i, first=first: (first + i, 0)),
                pl.BlockSpec((1, 1, d), lambda i, first=first: ((first + i) // per_seq, 0, 0)),
                pl.BlockSpec((1, d), lambda i: (0, 0)),
            ],
            out_specs=pl.BlockSpec((tm, d), lambda i, first=first: (first + i, 0)),
            out_shape=jax.ShapeDtypeStruct((t, d), F32),
            input_output_aliases={3: 0},
            compiler_params=_params("parallel"),
            name="moe_combine",
        )(g, g, route, out, gate, fg)
    return out.reshape(b, s, d)


def _slot_tables(route, counts, bm):
    t = route.shape[0]
    nb = MOE_TOPK * t // bm + MOE_EXPERTS
    counts = counts[0, :MOE_EXPERTS].astype(jnp.int32)
    padded = (counts + bm - 1) // bm * bm
    pad_end = jnp.cumsum(padded)
    pad_start = pad_end - padded
    expert = route[:, 0:MOE_TOPK].astype(jnp.int32)
    rank = route[:, 4:4 + MOE_TOPK].astype(jnp.int32)
    dest = pad_start[expert] + rank
    blk_start = jnp.arange(nb, dtype=jnp.int32) * bm
    blk_e = jnp.minimum(jnp.sum((pad_end[None, :] <= blk_start[:, None]).astype(jnp.int32), axis=1), MOE_EXPERTS - 1)
    used = (pad_end[-1:] // bm).astype(jnp.int32)
    tail_blk = jnp.maximum(pad_end // bm - 1, 0).astype(jnp.int32)
    tail_fill = jnp.where(counts % bm != 0, counts % bm, bm).astype(jnp.int32)
    return dest.T, blk_e, used, tail_blk, tail_fill, nb


def _moe_layer(x, g, sc, sh, gate, w_route, b_route, w_gate, w_up, w_down, layer, final_g, final, bm=512, tm=1024):
    b, s, d = x.shape
    t = b * s
    chunks = d // PACKED_LANES
    tm = min(tm, s)
    h_rows, route, counts = _router(x, g, sc, sh, w_route, b_route)
    dest, blk_e, used, tail_blk, tail_fill, nb = _slot_tables(route, counts, bm)
    slots = _sc_dispatch(h_rows.reshape(t, chunks, LANES), dest, nb * bm)
    slots = _zero_tails(tail_blk, tail_fill * chunks, slots.reshape(nb * bm * chunks, LANES), bm * chunks)
    y_slabs = _expert_blocks(blk_e, used, slots, w_gate, w_up, w_down, layer, bm).reshape(nb * bm, chunks, LANES)
    return _combine(dest, route, y_slabs, x, gate, final_g, final, tm)


def kernel(x, c, ada_w, ada_b, norm_g, final_g, sb_w_in, sb_w_out, gla_w_in, gla_w_gate_up, gla_b_gate, gla_norm_g, gla_w_out, conv_w_in, conv_w, conv_b, conv_w_out, dil_w_in, dil_w_out, moe_w_grp, moe_b_grp, moe_w_exp, moe_b_exp, moe_w_gate, moe_w_up, moe_w_down):
    b, s, d = x.shape
    depth = ada_w.shape[0]
    mod = _ada_mod(c, ada_w, ada_b).reshape(depth, b, N_MOD, 1, d)
    for i in range(depth):
        sh1, sc1, g1, sh2, sc2, g2 = (mod[i, :, m] for m in range(N_MOD))
        kind, j = i % 4, i // 4
        if kind == 0:
            qkv = _norm_proj(x, norm_g[i, 0], sc1, sh1, sb_w_in[j].astype(BF16))
            x = _proj_res(_sb_attention(qkv), sb_w_out[j].astype(BF16), x, g1)
        elif kind == 1:
            dk = gla_w_gate_up.shape[-1] // GLA_HEADS
            dv = gla_norm_g.shape[-1]
            n_in = gla_w_in.shape[-1]
            w_in = jnp.pad(gla_w_in[j], ((0, 0), (0, -n_in % MXU_COLUMNS))).astype(BF16)
            w_up = jnp.pad(gla_w_gate_up[j], ((0, LANES - GLA_GATE_RANK), (0, 0))).astype(BF16)
            proj = _norm_proj(x, norm_g[i, 0], sc1, sh1, w_in, tn=MXU_COLUMNS)
            o = _gla_core(proj, w_up, gla_b_gate[j].reshape(1, -1), gla_norm_g[j].reshape(1, -1), dk, dv)
            x = _proj_res(o, gla_w_out[j].astype(BF16), x, g1)
        elif kind == 2:
            proj = _norm_proj(x, norm_g[i, 0], sc1, sh1, conv_w_in[j].astype(BF16))
            x = _conv_res(proj, conv_w[j], conv_b[j], conv_w_out[j].astype(BF16), x, g1)
        else:
            group_w = 3 * DIL_HEADS * DIL_HEAD_DIM
            outs, lses = [], []
            for g, (window, dilation) in enumerate(DIL_PATTERNS):
                w_g = dil_w_in[j][:, g * group_w:(g + 1) * group_w].astype(BF16)
                view = _norm_proj(x, norm_g[i, 0], sc1, sh1, w_g, dilation=dilation)
                o_g, l_g = _dil_group(view, dilation, window, s)
                outs.append(o_g)
                lses.append(l_g)
            x = _dil_res(outs, lses, dil_w_out[j].astype(BF16), x, g1)
        w_route = jnp.pad(jnp.concatenate([moe_w_grp[i], moe_w_exp[i]], axis=1),
                          ((0, 0), (0, LANES - MOE_GROUPS - MOE_EXPERTS)))
        b_route = jnp.pad(jnp.concatenate([moe_b_grp[i], moe_b_exp[i]]), (0, LANES - MOE_GROUPS - MOE_EXPERTS))
        x = _moe_layer(x, norm_g[i, 1], sc2, sh2, g2, w_route, b_route.reshape(1, LANES),
                       moe_w_gate, moe_w_up, moe_w_down, i, final_g, final=(i == depth - 1))
    return x
```

```python
import functools

import jax
import jax.numpy as jnp
from jax import lax
from jax.experimental import pallas as pl
from jax.experimental.pallas import tpu as pltpu
from jax.experimental.pallas import tpu_sc as plsc

F32 = jnp.float32
BF16 = jnp.bfloat16

LANES = 128
MXU_COLUMNS = 256
RMS_EPS = 1e-6
N_MOD = 6

SB_HEADS = 16
GLA_HEADS = 4
GLA_GATE_RANK = 16
GLA_GATE_TAU = 16.0
GLA_CHUNK = 64
CONV_WIDTH = 3
DIL_PATTERNS = ((128, 1), (512, 4), (2048, 16))
DIL_HEADS = 8
DIL_HEAD_DIM = 64
DIL_QBLOCK = 128
DIL_TILE_TOKENS = 2048
MOE_GROUPS = 4
MOE_PER_GROUP = 8
MOE_EXPERTS = MOE_GROUPS * MOE_PER_GROUP
MOE_TOPK = 2

SC_CORES = 2
SC_SUBCORES = 16
SC_WINDOW = 64
PACKED_LANES = 2 * LANES

SB_LOG_ZERO = -104.0
SB_BAND = 2

VMEM_LIMIT = 48 * 1024 * 1024
EXPERT_VMEM_LIMIT = 56 * 1024 * 1024


def _params(*sem):
    return pltpu.CompilerParams(dimension_semantics=sem, vmem_limit_bytes=VMEM_LIMIT)


def _sigmoid(x):
    return 1.0 / (1.0 + jnp.exp(-x))


def _softplus(x):
    return jnp.maximum(x, 0.0) + jnp.log(1.0 + jnp.exp(-jnp.abs(x)))


def _split_bf16(x):
    hi = x.astype(BF16)
    lo = (x - hi.astype(F32)).astype(BF16)
    return hi, lo


def _dot(a, b):
    return jnp.dot(a, b, preferred_element_type=F32)


def _dot_nt(a, b):
    return lax.dot_general(a, b, (((1,), (1,)), ((), ())), preferred_element_type=F32)


def _dot_tn(a, b):
    return lax.dot_general(a, b, (((0,), (0,)), ((), ())), preferred_element_type=F32)


def _unpack_slabs(ref, rows):
    k = ref.shape[0] // rows
    words = jnp.concatenate([ref[pl.ds(s, rows, stride=k), :] for s in range(k)], axis=1)
    halves = [pltpu.unpack_elementwise(words, index=i, packed_dtype=BF16, unpacked_dtype=F32) for i in range(2)]
    return jnp.concatenate(halves, axis=1)


def _pack_slabs(ref, values):
    rows, d = values.shape
    half = d // 2
    words = pltpu.pack_elementwise([values[:, :half], values[:, half:]], packed_dtype=BF16)
    for s in range(half // LANES):
        ref[pl.ds(s, rows, stride=half // LANES), :] = words[:, s * LANES:(s + 1) * LANES]


def _mod_norm(x, g, sc, sh):
    r = lax.rsqrt(jnp.mean(x * x, axis=-1, keepdims=True) + RMS_EPS)
    return (x * r) * g * (1.0 + sc) + sh


def _ada_kernel(c_ref, w_ref, b_ref, o_ref):
    c = c_ref[...]
    cond = c * _sigmoid(c)
    o_ref[0] = _dot(cond.astype(BF16), w_ref[0].astype(BF16)) + b_ref[0]


def _ada_mod(c, ada_w, ada_b):
    depth, d, n = ada_w.shape
    b = c.shape[0]
    tn = 1536
    return pl.pallas_call(
        _ada_kernel,
        grid=(depth, n // tn),
        in_specs=[
            pl.BlockSpec((b, d), lambda i, j: (0, 0)),
            pl.BlockSpec((1, d, tn), lambda i, j: (i, 0, j)),
            pl.BlockSpec((1, 1, tn), lambda i, j: (i, 0, j)),
        ],
        out_specs=pl.BlockSpec((1, b, tn), lambda i, j: (i, 0, j)),
        out_shape=jax.ShapeDtypeStruct((depth, b, n), F32),
        compiler_params=_params("parallel", "parallel"),
        name="ada_mod",
    )(c, ada_w, ada_b.reshape(depth, 1, n))


def _norm_proj_kernel(x_ref, g_ref, sc_ref, sh_ref, w_ref, o_ref, h_ref, *res_ref, dilation, tn):
    h_ref[...] = _mod_norm(x_ref[0], g_ref[...], sc_ref[0], sh_ref[0]).astype(BF16)
    tm = h_ref.shape[0]
    for j in range(w_ref.shape[1] // tn):
        res = _dot(h_ref[...], w_ref[:, j * tn:(j + 1) * tn])
        if dilation == 1:
            o_ref[0, :, j * tn:(j + 1) * tn] = res.astype(o_ref.dtype)
            continue
        for c in range(tn // LANES):
            res_ref[0][j * (tn // LANES) + c] = res[:, c * LANES:(c + 1) * LANES]
        for rho in range(dilation):
            for c in range(tn // LANES):
                rows = res_ref[0][j * (tn // LANES) + c, pl.ds(rho, tm // dilation, stride=dilation), :]
                col = (j * dilation + rho) * tn + c * LANES
                o_ref[0, :, col:col + LANES] = rows.astype(o_ref.dtype)


def _norm_proj(x, g, sc, sh, w, tm=1024, tn=512, dilation=1):
    b, s, d = x.shape
    n = w.shape[1]
    tm = min(tm, s)
    assert s % tm == 0 and n % tn == 0 and tm % (16 * dilation) == 0
    scratch = [pltpu.VMEM((tm, d), BF16)]
    if dilation > 1:
        scratch.append(pltpu.VMEM((n // LANES, tm, LANES), F32))
    return pl.pallas_call(
        functools.partial(_norm_proj_kernel, dilation=dilation, tn=tn),
        grid=(b, s // tm),
        in_specs=[
            pl.BlockSpec((1, tm, d), lambda bi, i: (bi, i, 0)),
            pl.BlockSpec((1, d), lambda bi, i: (0, 0)),
            pl.BlockSpec((1, 1, d), lambda bi, i: (bi, 0, 0)),
            pl.BlockSpec((1, 1, d), lambda bi, i: (bi, 0, 0)),
            pl.BlockSpec((d, n), lambda bi, i: (0, 0)),
        ],
        out_specs=pl.BlockSpec((1, tm // dilation, dilation * n), lambda bi, i: (bi, i, 0)),
        out_shape=jax.ShapeDtypeStruct((b, s // dilation, dilation * n), BF16),
        scratch_shapes=scratch,
        compiler_params=_params("parallel", "parallel"),
        name="norm_proj",
    )(x, g.reshape(1, d), sc, sh, w)


def _proj_res_kernel(o_ref, w_ref, x_ref, g_ref, y_ref):
    y_ref[0] = x_ref[0] + g_ref[0] * _dot(o_ref[0], w_ref[...])


def _proj_res(o, w, x, gate, tm=1024):
    b, s, d = x.shape
    k = o.shape[-1]
    tm = min(tm, s)
    return pl.pallas_call(
        _proj_res_kernel,
        grid=(b, s // tm),
        in_specs=[
            pl.BlockSpec((1, tm, k), lambda bi, i: (bi, i, 0)),
            pl.BlockSpec((k, d), lambda bi, i: (0, 0)),
            pl.BlockSpec((1, tm, d), lambda bi, i: (bi, i, 0)),
            pl.BlockSpec((1, 1, d), lambda bi, i: (bi, 0, 0)),
        ],
        out_specs=pl.BlockSpec((1, tm, d), lambda bi, i: (bi, i, 0)),
        out_shape=jax.ShapeDtypeStruct((b, s, d), F32),
        compiler_params=_params("parallel", "parallel"),
        name="proj_res",
    )(o, w, x, gate)


def _sb_kernel(q_ref, k_ref, v_ref, o_ref, acc_ref, c_ref, *, blk, hd):
    scale = hd ** -0.5
    fold_scale = (hd & (hd - 1)) == 0 and (hd.bit_length() - 1) % 2 == 0
    first = lax.broadcasted_iota(jnp.int32, (blk, LANES), 1) < hd
    row = lax.broadcasted_iota(jnp.int32, (blk, blk), 0)
    col = lax.broadcasted_iota(jnp.int32, (blk, blk), 1)
    minus_later = jnp.where(row > col, -1.0, 0.0).astype(BF16)
    causal = col < row
    subs = q_ref.shape[1] // blk
    first_block = pl.program_id(2) * subs

    def visit_many(work):
        kv = []
        for _, block, _, _ in work:
            start = pl.multiple_of(block * blk, blk)
            kv.append((k_ref[0, pl.ds(start, blk), :], v_ref[0, pl.ds(start, blk), :]))
        zs = [[_dot_nt(q_heads[h], kb) for h in range(2)] for (q_heads, _, _, _), (kb, _) in zip(work, kv)]
        log_betas, sps = [], []
        for (_, _, mask, _), z2 in zip(work, zs):
            lb2, sp2 = [], []
            for z in z2:
                if not fold_scale:
                    z = z * scale
                sp = _softplus(z)
                lb2.append(z - sp)
                sp2.append(sp if mask is None else jnp.where(mask, sp, 0.0))
            log_betas.append(lb2)
            sps.append(sp2)
        suffixes = [[_dot(sp.astype(BF16), minus_later) for sp in sp2] for sp2 in sps]
        return kv, log_betas, sps, suffixes

    def finish(work, staged, c_in):
        kv, log_betas, sps, suffixes = staged
        cs = {chain: list(c) for chain, c in c_in.items()}
        weights = []
        for n, (_, _, mask, chain) in enumerate(work):
            a2 = []
            for h in range(2):
                a = jnp.exp(log_betas[n][h] + suffixes[n][h] + cs[chain][h])
                a2.append(a if mask is None else jnp.where(mask, a, 0.0))
                cs[chain][h] = cs[chain][h] - jnp.sum(sps[n][h], axis=1, keepdims=True)
            weights.append(a2)
        accs = {}
        for n, (_, _, _, chain) in enumerate(work):
            outs = [_dot(weights[n][h].astype(BF16), kv[n][1]) for h in range(2)]
            more = jnp.where(first, outs[0], outs[1])
            accs[chain] = more if chain not in accs else accs[chain] + more
        return accs, cs

    def remainder(sub, q_heads, depth):
        def cond(carry):
            j, cmax = carry
            return jnp.logical_and(j >= 0, cmax > SB_LOG_ZERO)

        def body(carry):
            j, _ = carry
            work = [(q_heads, j, None, sub)]
            accs, cs = finish(work, visit_many(work), {sub: [c_ref[sub, 0], c_ref[sub, 1]]})
            acc_ref[sub] += accs[sub]
            for h in range(2):
                c_ref[sub, h] = cs[sub][h]
            return j - 1, jnp.max(c_ref[sub])

        lax.while_loop(cond, body, (first_block + sub - depth - 1, jnp.max(c_ref[sub])))

    def run(depths):
        heads = []
        for sub in range(subs):
            q = q_ref[0, sub * blk:(sub + 1) * blk, :]
            if fold_scale:
                q = (q.astype(F32) * scale).astype(BF16)
            zero = jnp.zeros_like(q)
            heads.append((jnp.where(first, q, zero), jnp.where(first, zero, q)))
        work = [(heads[sub], first_block + sub - back, causal if back == 0 else None, sub)
                for sub in range(subs) for back in range(depths[sub] + 1)]
        zero_c = [jnp.zeros((blk, 1), F32)] * 2
        accs, cs = finish(work, visit_many(work), {sub: zero_c for sub in range(subs)})
        for sub in range(subs):
            acc_ref[sub] = accs[sub]
            for h in range(2):
                c_ref[sub, h] = cs[sub][h]
        @pl.when(jnp.max(c_ref[...]) > SB_LOG_ZERO)
        def _():
            for sub in range(subs):
                remainder(sub, heads[sub], depths[sub])

        for sub in range(subs):
            o_ref[0, sub * blk:(sub + 1) * blk, :] = acc_ref[sub].astype(o_ref.dtype)

    assert subs >= SB_BAND
    pl.when(first_block > 0)(functools.partial(run, [SB_BAND] * subs))
    pl.when(first_block == 0)(functools.partial(run, [min(SB_BAND, sub) for sub in range(subs)]))


def _sb_attention(qkv, blk=128, rows=1024):
    b, s, n = qkv.shape
    width = n // 3
    hd = width // SB_HEADS
    assert 2 * hd == LANES
    pairs = width // LANES
    rows = min(rows, s)
    return pl.pallas_call(
        functools.partial(_sb_kernel, blk=blk, hd=hd),
        grid=(b, pairs, s // rows),
        in_specs=[
            pl.BlockSpec((1, rows, LANES), lambda bi, p, i: (bi, i, p)),
            pl.BlockSpec((1, s, LANES), lambda bi, p, i: (bi, 0, pairs + p)),
            pl.BlockSpec((1, s, LANES), lambda bi, p, i: (bi, 0, 2 * pairs + p)),
        ],
        out_specs=pl.BlockSpec((1, rows, LANES), lambda bi, p, i: (bi, i, p)),
        out_shape=jax.ShapeDtypeStruct((b, s, width), BF16),
        scratch_shapes=[pltpu.VMEM((rows // blk, blk, LANES), F32), pltpu.VMEM((rows // blk, 2, blk, 1), F32)],
        compiler_params=_params("parallel", "parallel", "arbitrary"),
        name="sb_attention",
    )(qkv, qkv, qkv)


def _gla_kernel(p_ref, wg_ref, bg_ref, ng_ref, o_ref, st_ref, *, rows, dk, dv):
    @pl.when(pl.program_id(1) == 0)
    def _():
        st_ref[...] = jnp.zeros_like(st_ref)

    heads = GLA_HEADS
    ch = GLA_CHUNK
    k0 = heads * dk
    v0 = 2 * heads * dk
    r0 = v0 + heads * dv
    g0 = r0 + heads * dv
    row = lax.broadcasted_iota(jnp.int32, (ch, ch), 0)
    col = lax.broadcasted_iota(jnp.int32, (ch, ch), 1)
    causal = col <= row
    lower = jnp.where(causal, 1.0, 0.0).astype(BF16)
    chunks = [slice(s * ch, (s + 1) * ch) for s in range(rows // ch)]
    g_pres = [_dot(p_ref[0, t, g0:g0 + LANES], wg_ref[...]) + bg_ref[...] for t in chunks]
    cums = []
    for g_pre in g_pres:
        log_a = (jnp.minimum(g_pre, 0.0) - jnp.log(1.0 + jnp.exp(-jnp.abs(g_pre)))) * (1.0 / GLA_GATE_TAU)
        hi, lo = _split_bf16(log_a)
        cums.append(_dot(lower, hi) + _dot(lower, lo))
    parts = []
    for t, cum in zip(chunks, cums):
        for h in range(heads):
            bh = cum[:, h * dk:(h + 1) * dk]
            b_last = bh[ch - 1:ch, :]
            q = p_ref[0, t, h * dk:(h + 1) * dk].astype(F32) * dk ** -0.5
            k = p_ref[0, t, k0 + h * dk:k0 + (h + 1) * dk].astype(F32)
            q_in = (q * jnp.exp(bh)).astype(BF16)
            k_in = (k * jnp.exp(-bh)).astype(BF16)
            k_state = (k * jnp.exp(b_last - bh)).astype(BF16)
            parts.append((t, h, q_in, k_in, k_state, jnp.exp(b_last)))
    scores = [jnp.where(causal, _dot_nt(q_in, k_in), 0.0).astype(BF16) for _, _, q_in, k_in, _, _ in parts]
    values = [p_ref[0, t, v0 + h * dv:v0 + (h + 1) * dv] for t, h, _, _, _, _ in parts]
    intra = [_dot(sc, v) for sc, v in zip(scores, values)]
    updates = [_dot_tn(v, k_state) for v, (_, _, _, _, k_state, _) in zip(values, parts)]
    states = [st_ref[h] for h in range(heads)]
    for n, (t, h, q_in, _, _, decay) in enumerate(parts):
        o = intra[n] + _dot_nt(q_in, states[h].astype(BF16))
        states[h] = decay * states[h] + updates[n]
        o = o * lax.rsqrt(jnp.mean(o * o, axis=-1, keepdims=True) + RMS_EPS) * ng_ref[...]
        r = p_ref[0, t, r0 + h * dv:r0 + (h + 1) * dv].astype(F32)
        o_ref[0, t, h * dv:(h + 1) * dv] = (o * (r * _sigmoid(r))).astype(o_ref.dtype)
    for h in range(heads):
        st_ref[h] = states[h]


def _gla_core(proj, w_gate_up, b_gate, norm_g, dk, dv, rows=512):
    b, s, n = proj.shape
    rows = min(rows, s)
    hk = GLA_HEADS * dk
    return pl.pallas_call(
        functools.partial(_gla_kernel, rows=rows, dk=dk, dv=dv),
        grid=(b, s // rows),
        in_specs=[
            pl.BlockSpec((1, rows, n), lambda bi, i: (bi, i, 0)),
            pl.BlockSpec((LANES, hk), lambda bi, i: (0, 0)),
            pl.BlockSpec((1, hk), lambda bi, i: (0, 0)),
            pl.BlockSpec((1, dv), lambda bi, i: (0, 0)),
        ],
        out_specs=pl.BlockSpec((1, rows, GLA_HEADS * dv), lambda bi, i: (bi, i, 0)),
        out_shape=jax.ShapeDtypeStruct((b, s, GLA_HEADS * dv), BF16),
        scratch_shapes=[pltpu.VMEM((GLA_HEADS, dv, dk), F32)],
        compiler_params=_params("parallel", "arbitrary"),
        name="gla_core",
    )(proj, w_gate_up, b_gate, norm_g)


def _conv_res_kernel(gb_ref, gc_ref, u_ref, hgc_ref, hu_ref, cw_ref, cb_ref, w_ref, x_ref, g_ref, y_ref):
    u2 = gc_ref[0].astype(F32) * u_ref[0].astype(F32)
    halo = hgc_ref[0].astype(F32) * hu_ref[0].astype(F32)
    halo = jnp.where(pl.program_id(1) > 0, halo, 0.0)
    rows = lax.broadcasted_iota(jnp.int32, u2.shape, 0)
    prev1 = jnp.where(rows == 0, halo[7:8], pltpu.roll(u2, 1, 0))
    prev2 = jnp.where(rows == 0, halo[6:7], jnp.where(rows == 1, halo[7:8], pltpu.roll(u2, 2, 0)))
    cw = cw_ref[...]
    y = cw[0:1] * prev2 + cw[1:2] * prev1 + cw[2:3] * u2 + cb_ref[...]
    o = (gb_ref[0].astype(F32) * y).astype(BF16)
    y_ref[0] = x_ref[0] + g_ref[0] * _dot(o, w_ref[...])


def _conv_res(proj, conv_w, conv_b, w_out, x, gate, tm=1024):
    b, s, d = x.shape
    tm = min(tm, s)
    halo_blocks = tm // 8

    def halo_map(col):
        return lambda bi, i: (bi, jnp.maximum(i * halo_blocks - 1, 0), col)

    return pl.pallas_call(
        _conv_res_kernel,
        grid=(b, s // tm),
        in_specs=[
            pl.BlockSpec((1, tm, d), lambda bi, i: (bi, i, 0)),
            pl.BlockSpec((1, tm, d), lambda bi, i: (bi, i, 1)),
            pl.BlockSpec((1, tm, d), lambda bi, i: (bi, i, 2)),
            pl.BlockSpec((1, 8, d), halo_map(1)),
            pl.BlockSpec((1, 8, d), halo_map(2)),
            pl.BlockSpec((CONV_WIDTH, d), lambda bi, i: (0, 0)),
            pl.BlockSpec((1, d), lambda bi, i: (0, 0)),
            pl.BlockSpec((d, d), lambda bi, i: (0, 0)),
            pl.BlockSpec((1, tm, d), lambda bi, i: (bi, i, 0)),
            pl.BlockSpec((1, 1, d), lambda bi, i: (bi, 0, 0)),
        ],
        out_specs=pl.BlockSpec((1, tm, d), lambda bi, i: (bi, i, 0)),
        out_shape=jax.ShapeDtypeStruct((b, s, d), F32),
        compiler_params=_params("parallel", "parallel"),
        name="conv_res",
    )(proj, proj, proj, proj, proj, conv_w, conv_b.reshape(1, d), w_out, x, gate)


def _dil_kernel(q_ref, kc_ref, vc_ref, kp_ref, vp_ref, o_ref, l_ref, *, rows, hd, dilation, group):
    qb = DIL_QBLOCK
    n = pl.program_id(1)
    width = q_ref.shape[-1] // group
    scale = hd ** -0.5
    fold_scale = (hd & (hd - 1)) == 0 and (hd.bit_length() - 1) % 2 == 0
    first = lax.broadcasted_iota(jnp.int32, (qb, LANES), 1) < hd
    row = lax.broadcasted_iota(jnp.int32, (qb, qb), 0)
    col = lax.broadcasted_iota(jnp.int32, (qb, qb), 1)
    in_prev = col >= row
    in_cur = col <= row
    neg = -jnp.inf

    tiles = []
    for u, s in [(u, s) for u in range(group) for s in range(rows // qb)]:
        t = slice(s * qb, (s + 1) * qb)
        tp = slice((s - 1) * qb, s * qb)
        prev_mask = jnp.logical_and(in_prev, n > 0) if s == 0 else in_prev
        for p in range(width // LANES):
            c = slice(u * width + p * LANES, u * width + (p + 1) * LANES)
            q = q_ref[0, t, c]
            if fold_scale:
                q = (q.astype(F32) * scale).astype(BF16)
            zero = jnp.zeros_like(q)
            q_heads = (jnp.where(first, q, zero), jnp.where(first, zero, q))
            if s == 0:
                k_prev, v_prev = kp_ref[0, :, c], vp_ref[0, :, c]
            else:
                k_prev, v_prev = kc_ref[0, tp, c], vc_ref[0, tp, c]
            tiles.append(((u, s), p, q_heads, k_prev, kc_ref[0, t, c], v_prev, vc_ref[0, t, c], prev_mask))

    scores = []
    for _, _, q_heads, k_prev, k_cur, _, _, prev_mask in tiles:
        pair = []
        for h in range(2):
            z_prev, z_cur = _dot_nt(q_heads[h], k_prev), _dot_nt(q_heads[h], k_cur)
            if not fold_scale:
                z_prev, z_cur = z_prev * scale, z_cur * scale
            pair.append((jnp.where(prev_mask, z_prev, neg), jnp.where(in_cur, z_cur, neg)))
        scores.append(pair)

    probs = []
    for pair in scores:
        stats = []
        for z_prev, z_cur in pair:
            m = jnp.max(jnp.maximum(z_prev, z_cur), axis=1, keepdims=True)
            p_prev = jnp.exp(z_prev - m)
            p_cur = jnp.exp(z_cur - m)
            den = jnp.sum(p_prev + p_cur, axis=1, keepdims=True)
            stats.append((p_prev.astype(BF16), p_cur.astype(BF16), den, m))
        probs.append(stats)

    lane = lax.broadcasted_iota(jnp.int32, (qb, LANES), 1)
    lse_rows = {}
    for ((u, s), p, _, _, _, v_prev, v_cur, _), stats in zip(tiles, probs):
        if dilation == 1:
            t_out = slice(s * qb, (s + 1) * qb)
        else:
            rho = pl.program_id(2) * group + u
            t_out = pl.ds(s * qb * dilation + rho, qb, stride=dilation)
        outs = []
        lse = lse_rows.get((u, s), jnp.zeros((qb, LANES), F32))
        for h, (p_prev, p_cur, den, m) in enumerate(stats):
            outs.append((_dot(p_prev, v_prev) + _dot(p_cur, v_cur)) / den)
            lse = jnp.where(lane == 2 * p + h, m + jnp.log(den), lse)
        lse_rows[(u, s)] = lse
        o_ref[0, p, t_out, :] = jnp.where(first, outs[0], outs[1])
        if p == width // LANES - 1:
            l_ref[0, t_out, :] = lse


def _dil_group(view, dilation, window, seq):
    b, length, _ = view.shape
    width = DIL_HEADS * DIL_HEAD_DIM
    assert window == dilation * DIL_QBLOCK and seq % (dilation * DIL_QBLOCK) == 0
    rows = min(max(DIL_TILE_TOKENS // dilation, DIL_QBLOCK), 512, length)
    group = min(dilation, max(512 // rows, 1))
    prev_blocks = rows // DIL_QBLOCK
    per_part = dilation // group

    def cur(part):
        return lambda bi, i, j: (bi, i, part * per_part + j)

    def prev(part):
        return lambda bi, i, j: (bi, jnp.maximum(i * prev_blocks - 1, 0), part * per_part + j)

    pairs = width // LANES
    out_shape = jax.ShapeDtypeStruct((b, pairs, seq, LANES), F32)
    out_spec = pl.BlockSpec((1, pairs, rows * dilation, LANES), lambda bi, i, j: (bi, 0, i, 0))
    lse_shape = jax.ShapeDtypeStruct((b, seq, LANES), F32)
    lse_spec = pl.BlockSpec((1, rows * dilation, LANES), lambda bi, i, j: (bi, i, 0))
    return pl.pallas_call(
        functools.partial(_dil_kernel, rows=rows, hd=DIL_HEAD_DIM, dilation=dilation, group=group),
        grid=(b, length // rows, per_part),
        in_specs=[
            pl.BlockSpec((1, rows, group * width), cur(0)),
            pl.BlockSpec((1, rows, group * width), cur(1)),
            pl.BlockSpec((1, rows, group * width), cur(2)),
            pl.BlockSpec((1, DIL_QBLOCK, group * width), prev(1)),
            pl.BlockSpec((1, DIL_QBLOCK, group * width), prev(2)),
        ],
        out_specs=[out_spec, lse_spec],
        out_shape=[out_shape, lse_shape],
        compiler_params=_params("parallel", "parallel", "arbitrary"),
        name=f"dil_attn_r{dilation}",
    )(view, view, view, view, view)


def _dil_res_kernel(o0, o1, o2, l0, l1, l2, w_ref, x_ref, g_ref, y_ref):
    ls = (l0[0], l1[0], l2[0])
    m = jnp.maximum(jnp.maximum(ls[0], ls[1]), ls[2])
    es = [jnp.exp(l - m) for l in ls]
    total = es[0] + es[1] + es[2]
    ws = [e / total for e in es]
    hd = LANES // 2
    first = lax.broadcasted_iota(jnp.int32, (x_ref.shape[1], LANES), 1) < hd
    parts = []
    for p in range(o0.shape[1]):
        w = [jnp.where(first, wg[:, 2 * p:2 * p + 1], wg[:, 2 * p + 1:2 * p + 2]) for wg in ws]
        parts.append(w[0] * o0[0, p] + w[1] * o1[0, p] + w[2] * o2[0, p])
    o = jnp.concatenate(parts, axis=1)
    y_ref[0] = x_ref[0] + g_ref[0] * _dot(o.astype(BF16), w_ref[...])


def _dil_res(outs, lses, w_out, x, gate, tm=1024):
    b, s, d = x.shape
    k = w_out.shape[0]
    tm = min(tm, s)
    part = pl.BlockSpec((1, k // LANES, tm, LANES), lambda bi, i: (bi, 0, i, 0))
    lse = pl.BlockSpec((1, tm, LANES), lambda bi, i: (bi, i, 0))
    return pl.pallas_call(
        _dil_res_kernel,
        grid=(b, s // tm),
        in_specs=[part] * 3 + [lse] * 3 + [
            pl.BlockSpec((k, d), lambda bi, i: (0, 0)),
            pl.BlockSpec((1, tm, d), lambda bi, i: (bi, i, 0)),
            pl.BlockSpec((1, 1, d), lambda bi, i: (bi, 0, 0)),
        ],
        out_specs=pl.BlockSpec((1, tm, d), lambda bi, i: (bi, i, 0)),
        out_shape=jax.ShapeDtypeStruct((b, s, d), F32),
        compiler_params=_params("parallel", "parallel"),
        name="dil_res",
    )(*outs, *lses, w_out, x, gate)


def _router_kernel(x_ref, g_ref, sc_ref, sh_ref, wr_ref, br_ref, h_ref, route_ref, cnt_ref, run_ref, earlier_ref):
    tm = x_ref.shape[1]

    @pl.when(jnp.logical_and(pl.program_id(0) == 0, pl.program_id(1) == 0))
    def _():
        run_ref[...] = jnp.zeros_like(run_ref)
        row = lax.broadcasted_iota(jnp.int32, (tm, tm), 0)
        col = lax.broadcasted_iota(jnp.int32, (tm, tm), 1)
        earlier_ref[...] = jnp.where(col < row, 1.0, 0.0).astype(BF16)

    h = _mod_norm(x_ref[0], g_ref[...], sc_ref[0], sh_ref[0])
    _pack_slabs(h_ref, h)
    logits = _dot(h.astype(BF16), wr_ref[...].astype(BF16)) + br_ref[...]
    lane = lax.broadcasted_iota(jnp.int32, logits.shape, 1).astype(F32)
    neg = -jnp.inf
    far = float(LANES)

    def first_argmax(vals):
        top = jnp.max(vals, axis=1, keepdims=True)
        return top, jnp.min(jnp.where(vals == top, lane, far), axis=1, keepdims=True)

    grp_logits = jnp.where(lane < MOE_GROUPS, logits, neg)
    grp_max, grp = first_argmax(grp_logits)
    p_grp = 1.0 / jnp.sum(jnp.exp(grp_logits - grp_max), axis=1, keepdims=True)
    base = MOE_GROUPS + grp * MOE_PER_GROUP
    in_grp = jnp.logical_and(lane >= base, lane < base + MOE_PER_GROUP)
    exp_logits = jnp.where(in_grp, logits, neg)
    m1, i1 = first_argmax(exp_logits)
    m2, i2 = first_argmax(jnp.where(lane == i1, neg, exp_logits))
    e2 = jnp.exp(m2 - m1)
    g1 = p_grp / (1.0 + e2)
    g2 = g1 * e2
    e_a = i1 - MOE_GROUPS
    e_b = i2 - MOE_GROUPS

    pick_a = lane == e_a
    pick_b = lane == e_b
    both = jnp.where(jnp.logical_or(pick_a, pick_b), 1.0, 0.0)
    before = _dot(earlier_ref[...], both.astype(BF16)) + run_ref[...]
    rank_a = jnp.sum(jnp.where(pick_a, before, 0.0), axis=1, keepdims=True)
    rank_b = jnp.sum(jnp.where(pick_b, before, 0.0), axis=1, keepdims=True)
    run_ref[...] += jnp.sum(both, axis=0, keepdims=True)
    cnt_ref[...] = run_ref[...]

    route = jnp.where(lane == 0, e_a, jnp.where(lane == 1, e_b, 0.0))
    route = jnp.where(lane == 2, g1, jnp.where(lane == 3, g2, route))
    route_ref[...] = jnp.where(lane == 4, rank_a, jnp.where(lane == 5, rank_b, route))


def _router(x, g, sc, sh, w_route, b_route, tm=1024):
    b, s, d = x.shape
    tm = min(tm, s)
    steps = s // tm
    return pl.pallas_call(
        _router_kernel,
        grid=(b, steps),
        in_specs=[
            pl.BlockSpec((1, tm, d), lambda bi, i: (bi, i, 0)),
            pl.BlockSpec((1, d), lambda bi, i: (0, 0)),
            pl.BlockSpec((1, 1, d), lambda bi, i: (bi, 0, 0)),
            pl.BlockSpec((1, 1, d), lambda bi, i: (bi, 0, 0)),
            pl.BlockSpec((d, LANES), lambda bi, i: (0, 0)),
            pl.BlockSpec((1, LANES), lambda bi, i: (0, 0)),
        ],
        out_specs=[
            pl.BlockSpec((tm * (d // PACKED_LANES), LANES), lambda bi, i: (bi * steps + i, 0)),
            pl.BlockSpec((tm, LANES), lambda bi, i: (bi * steps + i, 0)),
            pl.BlockSpec((1, LANES), lambda bi, i: (0, 0)),
        ],
        out_shape=[
            jax.ShapeDtypeStruct((b * s * (d // PACKED_LANES), LANES), jnp.int32),
            jax.ShapeDtypeStruct((b * s, LANES), F32),
            jax.ShapeDtypeStruct((1, LANES), F32),
        ],
        scratch_shapes=[pltpu.VMEM((1, LANES), F32), pltpu.VMEM((tm, tm), BF16)],
        compiler_params=_params("arbitrary", "arbitrary"),
        name="moe_router",
    )(x, g.reshape(1, d), sc, sh, w_route, b_route)


def _sc_dispatch(h_slabs, dest, n_slots):
    t, chunks, _ = h_slabs.shape
    workers = SC_CORES * SC_SUBCORES
    per_worker = t // workers
    steps = per_worker // SC_WINDOW
    assert t % (workers * SC_WINDOW * 2) == 0
    mesh = plsc.VectorSubcoreMesh(core_axis_name="core", subcore_axis_name="subcore")

    @functools.partial(
        pl.kernel, mesh=mesh,
        out_type=jax.ShapeDtypeStruct((n_slots, chunks, LANES), h_slabs.dtype),
        scratch_types=[
            pltpu.VMEM((2, MOE_TOPK, SC_WINDOW), jnp.int32),
            pltpu.VMEM((2, SC_WINDOW, chunks, LANES), h_slabs.dtype),
            pltpu.SemaphoreType.DMA((2,)),
            pltpu.SemaphoreType.DMA((2,)),
        ])
    def dispatch(h_hbm, dest_hbm, slots_hbm, idx_v, rows_v, load_sem, store_sem):
        worker = lax.axis_index("subcore") * SC_CORES + lax.axis_index("core")
        base = worker * per_worker

        def scatters(buf):
            return [pltpu.make_async_copy(rows_v.at[buf], slots_hbm.at[idx_v.at[buf, k]], store_sem.at[buf])
                    for k in range(MOE_TOPK)]

        @pl.loop(0, steps, step=2)
        def _(step):
            for buf in range(2):
                @pl.when(step > 0)
                def _():
                    for cp in scatters(buf):
                        cp.wait()

                off = base + (step + buf) * SC_WINDOW
                pltpu.async_copy(h_hbm.at[pl.ds(off, SC_WINDOW)], rows_v.at[buf], load_sem.at[buf])
                for k in range(MOE_TOPK):
                    pltpu.sync_copy(dest_hbm.at[k, pl.ds(off, SC_WINDOW)], idx_v.at[buf, k])
            for buf in range(2):
                off = base + (step + buf) * SC_WINDOW
                pltpu.make_async_copy(h_hbm.at[pl.ds(off, SC_WINDOW)], rows_v.at[buf], load_sem.at[buf]).wait()
                for cp in scatters(buf):
                    cp.start()

        for buf in range(2):
            for cp in scatters(buf):
                cp.wait()

    return dispatch(h_slabs, dest)


def _zero_tail_kernel(tail_ref, valid_ref, s_in, s_out):
    del tail_ref
    rows = lax.broadcasted_iota(jnp.int32, s_in.shape, 0)
    s_out[...] = jnp.where(rows < valid_ref[pl.program_id(0)], s_in[...], 0)


def _zero_tails(tail_blk, valid_rows, slots, block_rows):
    grid_spec = pltpu.PrefetchScalarGridSpec(
        num_scalar_prefetch=2,
        grid=(tail_blk.shape[0],),
        in_specs=[pl.BlockSpec((block_rows, LANES), lambda e, tail, valid: (tail[e], 0))],
        out_specs=pl.BlockSpec((block_rows, LANES), lambda e, tail, valid: (tail[e], 0)),
    )
    return pl.pallas_call(
        _zero_tail_kernel,
        grid_spec=grid_spec,
        out_shape=jax.ShapeDtypeStruct(slots.shape, slots.dtype),
        input_output_aliases={2: 0},
        compiler_params=_params("arbitrary"),
        name="moe_zero_tails",
    )(tail_blk, valid_rows, slots)


def _expert_kernel(blk_e_ref, used_ref, x_ref, *refs, per_step):
    weights = [refs[3 * p:3 * p + 3] for p in range(per_step)]
    y_ref, wg_bf, wu_bf, wd_bf = refs[3 * per_step:]
    k = wg_bf.shape[0] // PACKED_LANES
    bm = x_ref.shape[0] // k // per_step
    for p in range(per_step):
        blk = pl.program_id(0) * per_step + p
        live = blk < used_ref[0]
        changed = jnp.logical_or(blk == 0, blk_e_ref[blk] != blk_e_ref[jnp.maximum(blk - 1, 0)])
        wg_ref, wu_ref, wd_ref = weights[p]
        rows = pl.ds(p * bm * k, bm * k)

        @pl.when(jnp.logical_and(live, changed))
        def _():
            wg_bf[...] = wg_ref[0, 0].astype(BF16)
            wu_bf[...] = wu_ref[0, 0].astype(BF16)
            wd_bf[...] = wd_ref[0, 0].astype(BF16)

        @pl.when(live)
        def _():
            x = _unpack_slabs(x_ref.at[rows], bm).astype(BF16)
            gate = _dot(x, wg_bf[...])
            up = _dot(x, wu_bf[...])
            act = (gate * _sigmoid(gate) * up).astype(BF16)
            _pack_slabs(y_ref.at[rows], _dot(act, wd_bf[...]))

        @pl.when(jnp.logical_not(live))
        def _():
            y_ref[rows, :] = jnp.zeros((bm * k, LANES), y_ref.dtype)


def _expert_blocks(blk_e, used, slots, w_gate, w_up, w_down, layer, bm, per_step=2):
    nb = blk_e.shape[0]
    _, _, d, f = w_gate.shape
    chunks = d // PACKED_LANES
    assert nb % per_step == 0

    def block(i, e, n):
        return (jnp.minimum(i, (n[0] - 1) // per_step), 0)

    def weight(p):
        return lambda i, e, n: (layer, e[jnp.minimum(i * per_step + p, n[0] - 1)], 0, 0)

    weight_specs = []
    for p in range(per_step):
        weight_specs += [pl.BlockSpec((1, 1, d, f), weight(p)), pl.BlockSpec((1, 1, d, f), weight(p)),
                         pl.BlockSpec((1, 1, f, d), weight(p))]
    grid_spec = pltpu.PrefetchScalarGridSpec(
        num_scalar_prefetch=2,
        grid=(nb // per_step,),
        in_specs=[pl.BlockSpec((per_step * bm * chunks, LANES), block)] + weight_specs,
        out_specs=pl.BlockSpec((per_step * bm * chunks, LANES), lambda i, e, n: (i, 0)),
        scratch_shapes=[pltpu.VMEM((d, f), BF16), pltpu.VMEM((d, f), BF16), pltpu.VMEM((f, d), BF16)],
    )
    return pl.pallas_call(
        functools.partial(_expert_kernel, per_step=per_step),
        grid_spec=grid_spec,
        out_shape=jax.ShapeDtypeStruct((nb * bm * chunks, LANES), jnp.int32),
        compiler_params=pltpu.CompilerParams(dimension_semantics=("arbitrary",), vmem_limit_bytes=EXPERT_VMEM_LIMIT),
        name="moe_experts",
    )(blk_e, used, slots, *([w_gate, w_up, w_down] * per_step))


def _sc_gather(y_slabs, idx):
    n = idx.shape[0]
    _, chunks, _ = y_slabs.shape
    workers = SC_CORES * SC_SUBCORES
    per_worker = n // workers
    steps = per_worker // SC_WINDOW
    assert n % (workers * SC_WINDOW * 2) == 0
    mesh = plsc.VectorSubcoreMesh(core_axis_name="core", subcore_axis_name="subcore")

    @functools.partial(
        pl.kernel, mesh=mesh,
        out_type=jax.ShapeDtypeStruct((n, chunks, LANES), y_slabs.dtype),
        scratch_types=[
            pltpu.VMEM((2, SC_WINDOW), jnp.int32),
            pltpu.VMEM((2, SC_WINDOW, chunks, LANES), y_slabs.dtype),
            pltpu.SemaphoreType.DMA((2,)),
            pltpu.SemaphoreType.DMA((2,)),
        ])
    def gather(y_hbm, idx_hbm, out_hbm, idx_v, rows_v, sem, store_sem):
        worker = lax.axis_index("subcore") * SC_CORES + lax.axis_index("core")
        base = worker * per_worker

        def store(buf, off):
            return pltpu.make_async_copy(rows_v.at[buf], out_hbm.at[pl.ds(off, SC_WINDOW)], store_sem.at[buf])

        @pl.loop(0, steps, step=2)
        def _(step):
            for buf in range(2):
                off = base + (step + buf) * SC_WINDOW

                @pl.when(step > 0)
                def _():
                    store(buf, off).wait()

                pltpu.sync_copy(idx_hbm.at[pl.ds(off, SC_WINDOW)], idx_v.at[buf])
                pltpu.async_copy(y_hbm.at[idx_v.at[buf]], rows_v.at[buf], sem.at[buf])
            for buf in range(2):
                off = base + (step + buf) * SC_WINDOW
                pltpu.make_async_copy(y_hbm.at[idx_v.at[buf]], rows_v.at[buf], sem.at[buf]).wait()
                store(buf, off).start()

        for buf in range(2):
            store(buf, base).wait()

    return gather(y_slabs, idx)


def _combine_kernel(g0_ref, g1_ref, route_ref, x_ref, gate_ref, fg_ref, o_ref, *, final):
    tm = x_ref.shape[0]
    route = route_ref[...]
    y = route[:, 2:3] * _unpack_slabs(g0_ref, tm) + route[:, 3:4] * _unpack_slabs(g1_ref, tm)
    out = x_ref[...] + gate_ref[0] * y
    if final:
        out = out * lax.rsqrt(jnp.mean(out * out, axis=-1, keepdims=True) + RMS_EPS) * fg_ref[...]
    o_ref[...] = out


def _combine(dest, route, y_slabs, x, gate, final_g, final, tm, pieces=8):
    b, s, d = x.shape
    t = b * s
    chunks = d // PACKED_LANES
    per_seq = s // tm
    piece = t // pieces
    tiles = piece // tm
    out = x.reshape(t, d)
    fg = final_g.reshape(1, d)
    for c in range(pieces):
        idx = dest[:, c * piece:(c + 1) * piece].reshape(MOE_TOPK * piece)
        g = _sc_gather(y_slabs, idx).reshape(MOE_TOPK * piece * chunks, LANES)
        first = c * tiles
        out = pl.pallas_call(
            functools.partial(_combine_kernel, final=final),
            grid=(tiles,),
            in_specs=[
                pl.BlockSpec((tm * chunks, LANES), lambda i: (i, 0)),
                pl.BlockSpec((tm * chunks, LANES), lambda i: (tiles + i, 0)),
                pl.BlockSpec((tm, LANES), lambda i, first=first: (first + i, 0)),
                pl.BlockSpec((tm, d), lambda i, first=first: (first + i, 0)),
                pl.BlockSpec((1, 1, d), lambda i, first=first: ((first + i) // per_seq, 0, 0)),
                pl.BlockSpec((1, d), lambda i: (0, 0)),
            ],
            out_specs=pl.BlockSpec((tm, d), lambda i, first=first: (first + i, 0)),
            out_shape=jax.ShapeDtypeStruct((t, d), F32),
            input_output_aliases={3: 0},
            compiler_params=_params("parallel"),
            name="moe_combine",
        )(g, g, route, out, gate, fg)
    return out.reshape(b, s, d)


def _slot_tables(route, counts, bm):
    t = route.shape[0]
    nb = MOE_TOPK * t // bm + MOE_EXPERTS
    counts = counts[0, :MOE_EXPERTS].astype(jnp.int32)
    padded = (counts + bm - 1) // bm * bm
    pad_end = jnp.cumsum(padded)
    pad_start = pad_end - padded
    expert = route[:, 0:MOE_TOPK].astype(jnp.int32)
    rank = route[:, 4:4 + MOE_TOPK].astype(jnp.int32)
    dest = pad_start[expert] + rank
    blk_start = jnp.arange(nb, dtype=jnp.int32) * bm
    blk_e = jnp.minimum(jnp.sum((pad_end[None, :] <= blk_start[:, None]).astype(jnp.int32), axis=1), MOE_EXPERTS - 1)
    used = (pad_end[-1:] // bm).astype(jnp.int32)
    tail_blk = jnp.maximum(pad_end // bm - 1, 0).astype(jnp.int32)
    tail_fill = jnp.where(counts % bm != 0, counts % bm, bm).astype(jnp.int32)
    return dest.T, blk_e, used, tail_blk, tail_fill, nb


def _moe_layer(x, g, sc, sh, gate, w_route, b_route, w_gate, w_up, w_down, layer, final_g, final, bm=512, tm=1024):
    b, s, d = x.shape
    t = b * s
    chunks = d // PACKED_LANES
    tm = min(tm, s)
    h_rows, route, counts = _router(x, g, sc, sh, w_route, b_route)
    dest, blk_e, used, tail_blk, tail_fill, nb = _slot_tables(route, counts, bm)
    slots = _sc_dispatch(h_rows.reshape(t, chunks, LANES), dest, nb * bm)
    slots = _zero_tails(tail_blk, tail_fill * chunks, slots.reshape(nb * bm * chunks, LANES), bm * chunks)
    y_slabs = _expert_blocks(blk_e, used, slots, w_gate, w_up, w_down, layer, bm).reshape(nb * bm, chunks, LANES)
    return _combine(dest, route, y_slabs, x, gate, final_g, final, tm)


def kernel(x, c, ada_w, ada_b, norm_g, final_g, sb_w_in, sb_w_out, gla_w_in, gla_w_gate_up, gla_b_gate, gla_norm_g, gla_w_out, conv_w_in, conv_w, conv_b, conv_w_out, dil_w_in, dil_w_out, moe_w_grp, moe_b_grp, moe_w_exp, moe_b_exp, moe_w_gate, moe_w_up, moe_w_down):
    b, s, d = x.shape
    depth = ada_w.shape[0]
    mod = _ada_mod(c, ada_w, ada_b).reshape(depth, b, N_MOD, 1, d)
    for i in range(depth):
        sh1, sc1, g1, sh2, sc2, g2 = (mod[i, :, m] for m in range(N_MOD))
        kind, j = i % 4, i // 4
        if kind == 0:
            qkv = _norm_proj(x, norm_g[i, 0], sc1, sh1, sb_w_in[j].astype(BF16))
            x = _proj_res(_sb_attention(qkv), sb_w_out[j].astype(BF16), x, g1)
        elif kind == 1:
            dk = gla_w_gate_up.shape[-1] // GLA_HEADS
            dv = gla_norm_g.shape[-1]
            n_in = gla_w_in.shape[-1]
            w_in = jnp.pad(gla_w_in[j], ((0, 0), (0, -n_in % MXU_COLUMNS))).astype(BF16)
            w_up = jnp.pad(gla_w_gate_up[j], ((0, LANES - GLA_GATE_RANK), (0, 0))).astype(BF16)
            proj = _norm_proj(x, norm_g[i, 0], sc1, sh1, w_in, tn=MXU_COLUMNS)
            o = _gla_core(proj, w_up, gla_b_gate[j].reshape(1, -1), gla_norm_g[j].reshape(1, -1), dk, dv)
            x = _proj_res(o, gla_w_out[j].astype(BF16), x, g1)
        elif kind == 2:
            proj = _norm_proj(x, norm_g[i, 0], sc1, sh1, conv_w_in[j].astype(BF16))
            x = _conv_res(proj, conv_w[j], conv_b[j], conv_w_out[j].astype(BF16), x, g1)
        else:
            group_w = 3 * DIL_HEADS * DIL_HEAD_DIM
            outs, lses = [], []
            for g, (window, dilation) in enumerate(DIL_PATTERNS):
                w_g = dil_w_in[j][:, g * group_w:(g + 1) * group_w].astype(BF16)
                view = _norm_proj(x, norm_g[i, 0], sc1, sh1, w_g, dilation=dilation)
                o_g, l_g = _dil_group(view, dilation, window, s)
                outs.append(o_g)
                lses.append(l_g)
            x = _dil_res(outs, lses, dil_w_out[j].astype(BF16), x, g1)
        w_route = jnp.pad(jnp.concatenate([moe_w_grp[i], moe_w_exp[i]], axis=1),
                          ((0, 0), (0, LANES - MOE_GROUPS - MOE_EXPERTS)))
        b_route = jnp.pad(jnp.concatenate([moe_b_grp[i], moe_b_exp[i]]), (0, LANES - MOE_GROUPS - MOE_EXPERTS))
        x = _moe_layer(x, norm_g[i, 1], sc2, sh2, g2, w_route, b_route.reshape(1, LANES),
                       moe_w_gate, moe_w_up, moe_w_down, i, final_g, final=(i == depth - 1))
    return x
```

```python
import functools

import jax
import jax.numpy as jnp
from jax import lax
from jax.experimental import pallas as pl
from jax.experimental.pallas import tpu as pltpu
from jax.experimental.pallas import tpu_sc as plsc

F32 = jnp.float32
BF16 = jnp.bfloat16

LANES = 128
MXU_COLUMNS = 256
RMS_EPS = 1e-6
N_MOD = 6

SB_HEADS = 16
GLA_HEADS = 4
GLA_GATE_RANK = 16
GLA_GATE_TAU = 16.0
GLA_CHUNK = 64
CONV_WIDTH = 3
DIL_PATTERNS = ((128, 1), (512, 4), (2048, 16))
DIL_HEADS = 8
DIL_HEAD_DIM = 64
DIL_QBLOCK = 128
DIL_TILE_TOKENS = 2048
MOE_GROUPS = 4
MOE_PER_GROUP = 8
MOE_EXPERTS = MOE_GROUPS * MOE_PER_GROUP
MOE_TOPK = 2

SC_CORES = 2
SC_SUBCORES = 16
SC_WINDOW = 64
PACKED_LANES = 2 * LANES

SB_LOG_ZERO = -104.0
SB_BAND = 2

VMEM_LIMIT = 48 * 1024 * 1024


def _params(*sem):
    return pltpu.CompilerParams(dimension_semantics=sem, vmem_limit_bytes=VMEM_LIMIT)


def _sigmoid(x):
    return 1.0 / (1.0 + jnp.exp(-x))


def _softplus(x):
    return jnp.maximum(x, 0.0) + jnp.log(1.0 + jnp.exp(-jnp.abs(x)))


def _split_bf16(x):
    hi = x.astype(BF16)
    lo = (x - hi.astype(F32)).astype(BF16)
    return hi, lo


def _dot(a, b):
    return jnp.dot(a, b, preferred_element_type=F32)


def _dot_nt(a, b):
    return lax.dot_general(a, b, (((1,), (1,)), ((), ())), preferred_element_type=F32)


def _dot_tn(a, b):
    return lax.dot_general(a, b, (((0,), (0,)), ((), ())), preferred_element_type=F32)


def _unpack_slabs(ref, rows):
    k = ref.shape[0] // rows
    words = jnp.concatenate([ref[pl.ds(s, rows, stride=k), :] for s in range(k)], axis=1)
    halves = [pltpu.unpack_elementwise(words, index=i, packed_dtype=BF16, unpacked_dtype=F32) for i in range(2)]
    return jnp.concatenate(halves, axis=1)


def _pack_slabs(ref, values):
    rows, d = values.shape
    half = d // 2
    words = pltpu.pack_elementwise([values[:, :half], values[:, half:]], packed_dtype=BF16)
    for s in range(half // LANES):
        ref[pl.ds(s, rows, stride=half // LANES), :] = words[:, s * LANES:(s + 1) * LANES]


def _mod_norm(x, g, sc, sh):
    r = lax.rsqrt(jnp.mean(x * x, axis=-1, keepdims=True) + RMS_EPS)
    return (x * r) * g * (1.0 + sc) + sh


def _ada_kernel(c_ref, w_ref, b_ref, o_ref):
    c = c_ref[...]
    cond = c * _sigmoid(c)
    o_ref[0] = _dot(cond.astype(BF16), w_ref[0].astype(BF16)) + b_ref[0]


def _ada_mod(c, ada_w, ada_b):
    depth, d, n = ada_w.shape
    b = c.shape[0]
    tn = 1536
    return pl.pallas_call(
        _ada_kernel,
        grid=(depth, n // tn),
        in_specs=[
            pl.BlockSpec((b, d), lambda i, j: (0, 0)),
            pl.BlockSpec((1, d, tn), lambda i, j: (i, 0, j)),
            pl.BlockSpec((1, 1, tn), lambda i, j: (i, 0, j)),
        ],
        out_specs=pl.BlockSpec((1, b, tn), lambda i, j: (i, 0, j)),
        out_shape=jax.ShapeDtypeStruct((depth, b, n), F32),
        compiler_params=_params("parallel", "parallel"),
        name="ada_mod",
    )(c, ada_w, ada_b.reshape(depth, 1, n))


def _norm_proj_kernel(x_ref, g_ref, sc_ref, sh_ref, w_ref, o_ref, h_ref, *res_ref, dilation, tn):
    h_ref[...] = _mod_norm(x_ref[0], g_ref[...], sc_ref[0], sh_ref[0]).astype(BF16)
    tm = h_ref.shape[0]
    for j in range(w_ref.shape[1] // tn):
        res = _dot(h_ref[...], w_ref[:, j * tn:(j + 1) * tn])
        if dilation == 1:
            o_ref[0, :, j * tn:(j + 1) * tn] = res.astype(o_ref.dtype)
            continue
        for c in range(tn // LANES):
            res_ref[0][j * (tn // LANES) + c] = res[:, c * LANES:(c + 1) * LANES]
        for rho in range(dilation):
            for c in range(tn // LANES):
                rows = res_ref[0][j * (tn // LANES) + c, pl.ds(rho, tm // dilation, stride=dilation), :]
                col = (j * dilation + rho) * tn + c * LANES
                o_ref[0, :, col:col + LANES] = rows.astype(o_ref.dtype)


def _norm_proj(x, g, sc, sh, w, tm=1024, tn=512, dilation=1):
    b, s, d = x.shape
    n = w.shape[1]
    tm = min(tm, s)
    assert s % tm == 0 and n % tn == 0 and tm % (16 * dilation) == 0
    scratch = [pltpu.VMEM((tm, d), BF16)]
    if dilation > 1:
        scratch.append(pltpu.VMEM((n // LANES, tm, LANES), F32))
    return pl.pallas_call(
        functools.partial(_norm_proj_kernel, dilation=dilation, tn=tn),
        grid=(b, s // tm),
        in_specs=[
            pl.BlockSpec((1, tm, d), lambda bi, i: (bi, i, 0)),
            pl.BlockSpec((1, d), lambda bi, i: (0, 0)),
            pl.BlockSpec((1, 1, d), lambda bi, i: (bi, 0, 0)),
            pl.BlockSpec((1, 1, d), lambda bi, i: (bi, 0, 0)),
            pl.BlockSpec((d, n), lambda bi, i: (0, 0)),
        ],
        out_specs=pl.BlockSpec((1, tm // dilation, dilation * n), lambda bi, i: (bi, i, 0)),
        out_shape=jax.ShapeDtypeStruct((b, s // dilation, dilation * n), BF16),
        scratch_shapes=scratch,
        compiler_params=_params("parallel", "parallel"),
        name="norm_proj",
    )(x, g.reshape(1, d), sc, sh, w)


def _proj_res_kernel(o_ref, w_ref, x_ref, g_ref, y_ref):
    y_ref[0] = x_ref[0] + g_ref[0] * _dot(o_ref[0], w_ref[...])


def _proj_res(o, w, x, gate, tm=1024):
    b, s, d = x.shape
    k = o.shape[-1]
    tm = min(tm, s)
    return pl.pallas_call(
        _proj_res_kernel,
        grid=(b, s // tm),
        in_specs=[
            pl.BlockSpec((1, tm, k), lambda bi, i: (bi, i, 0)),
            pl.BlockSpec((k, d), lambda bi, i: (0, 0)),
            pl.BlockSpec((1, tm, d), lambda bi, i: (bi, i, 0)),
            pl.BlockSpec((1, 1, d), lambda bi, i: (bi, 0, 0)),
        ],
        out_specs=pl.BlockSpec((1, tm, d), lambda bi, i: (bi, i, 0)),
        out_shape=jax.ShapeDtypeStruct((b, s, d), F32),
        compiler_params=_params("parallel", "parallel"),
        name="proj_res",
    )(o, w, x, gate)


def _sb_kernel(q_ref, k_ref, v_ref, o_ref, acc_ref, c_ref, *, blk, hd):
    scale = hd ** -0.5
    fold_scale = (hd & (hd - 1)) == 0 and (hd.bit_length() - 1) % 2 == 0
    first = lax.broadcasted_iota(jnp.int32, (blk, LANES), 1) < hd
    row = lax.broadcasted_iota(jnp.int32, (blk, blk), 0)
    col = lax.broadcasted_iota(jnp.int32, (blk, blk), 1)
    minus_later = jnp.where(row > col, -1.0, 0.0).astype(BF16)
    causal = col < row
    subs = q_ref.shape[1] // blk
    first_block = pl.program_id(2) * subs

    def visit_many(work):
        kv = []
        for _, block, _, _ in work:
            start = pl.multiple_of(block * blk, blk)
            kv.append((k_ref[0, pl.ds(start, blk), :], v_ref[0, pl.ds(start, blk), :]))
        zs = [[_dot_nt(q_heads[h], kb) for h in range(2)] for (q_heads, _, _, _), (kb, _) in zip(work, kv)]
        log_betas, sps = [], []
        for (_, _, mask, _), z2 in zip(work, zs):
            lb2, sp2 = [], []
            for z in z2:
                if not fold_scale:
                    z = z * scale
                sp = _softplus(z)
                lb2.append(z - sp)
                sp2.append(sp if mask is None else jnp.where(mask, sp, 0.0))
            log_betas.append(lb2)
            sps.append(sp2)
        suffixes = [[_dot(sp.astype(BF16), minus_later) for sp in sp2] for sp2 in sps]
        return kv, log_betas, sps, suffixes

    def finish(work, staged, c_in):
        kv, log_betas, sps, suffixes = staged
        cs = {chain: list(c) for chain, c in c_in.items()}
        weights = []
        for n, (_, _, mask, chain) in enumerate(work):
            a2 = []
            for h in range(2):
                a = jnp.exp(log_betas[n][h] + suffixes[n][h] + cs[chain][h])
                a2.append(a if mask is None else jnp.where(mask, a, 0.0))
                cs[chain][h] = cs[chain][h] - jnp.sum(sps[n][h], axis=1, keepdims=True)
            weights.append(a2)
        accs = {}
        for n, (_, _, _, chain) in enumerate(work):
            outs = [_dot(weights[n][h].astype(BF16), kv[n][1]) for h in range(2)]
            more = jnp.where(first, outs[0], outs[1])
            accs[chain] = more if chain not in accs else accs[chain] + more
        return accs, cs

    def remainder(sub, q_heads, depth):
        def cond(carry):
            j, cmax = carry
            return jnp.logical_and(j >= 0, cmax > SB_LOG_ZERO)

        def body(carry):
            j, _ = carry
            work = [(q_heads, j, None, sub)]
            accs, cs = finish(work, visit_many(work), {sub: [c_ref[sub, 0], c_ref[sub, 1]]})
            acc_ref[sub] += accs[sub]
            for h in range(2):
                c_ref[sub, h] = cs[sub][h]
            return j - 1, jnp.max(c_ref[sub])

        lax.while_loop(cond, body, (first_block + sub - depth - 1, jnp.max(c_ref[sub])))

    def run(depths):
        heads = []
        for sub in range(subs):
            q = q_ref[0, sub * blk:(sub + 1) * blk, :]
            if fold_scale:
                q = (q.astype(F32) * scale).astype(BF16)
            zero = jnp.zeros_like(q)
            heads.append((jnp.where(first, q, zero), jnp.where(first, zero, q)))
        work = [(heads[sub], first_block + sub - back, causal if back == 0 else None, sub)
                for sub in range(subs) for back in range(depths[sub] + 1)]
        zero_c = [jnp.zeros((blk, 1), F32)] * 2
        accs, cs = finish(work, visit_many(work), {sub: zero_c for sub in range(subs)})
        for sub in range(subs):
            acc_ref[sub] = accs[sub]
            for h in range(2):
                c_ref[sub, h] = cs[sub][h]
        @pl.when(jnp.max(c_ref[...]) > SB_LOG_ZERO)
        def _():
            for sub in range(subs):
                remainder(sub, heads[sub], depths[sub])

        for sub in range(subs):
            o_ref[0, sub * blk:(sub + 1) * blk, :] = acc_ref[sub].astype(o_ref.dtype)

    assert subs >= SB_BAND
    pl.when(first_block > 0)(functools.partial(run, [SB_BAND] * subs))
    pl.when(first_block == 0)(functools.partial(run, [min(SB_BAND, sub) for sub in range(subs)]))


def _sb_attention(qkv, blk=128, rows=1024):
    b, s, n = qkv.shape
    width = n // 3
    hd = width // SB_HEADS
    assert 2 * hd == LANES
    pairs = width // LANES
    rows = min(rows, s)
    return pl.pallas_call(
        functools.partial(_sb_kernel, blk=blk, hd=hd),
        grid=(b, pairs, s // rows),
        in_specs=[
            pl.BlockSpec((1, rows, LANES), lambda bi, p, i: (bi, i, p)),
            pl.BlockSpec((1, s, LANES), lambda bi, p, i: (bi, 0, pairs + p)),
            pl.BlockSpec((1, s, LANES), lambda bi, p, i: (bi, 0, 2 * pairs + p)),
        ],
        out_specs=pl.BlockSpec((1, rows, LANES), lambda bi, p, i: (bi, i, p)),
        out_shape=jax.ShapeDtypeStruct((b, s, width), BF16),
        scratch_shapes=[pltpu.VMEM((rows // blk, blk, LANES), F32), pltpu.VMEM((rows // blk, 2, blk, 1), F32)],
        compiler_params=_params("parallel", "parallel", "arbitrary"),
        name="sb_attention",
    )(qkv, qkv, qkv)


def _gla_kernel(p_ref, wg_ref, bg_ref, ng_ref, o_ref, st_ref, *, rows, dk, dv):
    @pl.when(pl.program_id(1) == 0)
    def _():
        st_ref[...] = jnp.zeros_like(st_ref)

    heads = GLA_HEADS
    ch = GLA_CHUNK
    k0 = heads * dk
    v0 = 2 * heads * dk
    r0 = v0 + heads * dv
    g0 = r0 + heads * dv
    row = lax.broadcasted_iota(jnp.int32, (ch, ch), 0)
    col = lax.broadcasted_iota(jnp.int32, (ch, ch), 1)
    causal = col <= row
    lower = jnp.where(causal, 1.0, 0.0).astype(BF16)
    chunks = [slice(s * ch, (s + 1) * ch) for s in range(rows // ch)]
    g_pres = [_dot(p_ref[0, t, g0:g0 + LANES], wg_ref[...]) + bg_ref[...] for t in chunks]
    cums = []
    for g_pre in g_pres:
        log_a = (jnp.minimum(g_pre, 0.0) - jnp.log(1.0 + jnp.exp(-jnp.abs(g_pre)))) * (1.0 / GLA_GATE_TAU)
        hi, lo = _split_bf16(log_a)
        cums.append(_dot(lower, hi) + _dot(lower, lo))
    parts = []
    for t, cum in zip(chunks, cums):
        for h in range(heads):
            bh = cum[:, h * dk:(h + 1) * dk]
            b_last = bh[ch - 1:ch, :]
            q = p_ref[0, t, h * dk:(h + 1) * dk].astype(F32) * dk ** -0.5
            k = p_ref[0, t, k0 + h * dk:k0 + (h + 1) * dk].astype(F32)
            q_in = (q * jnp.exp(bh)).astype(BF16)
            k_in = (k * jnp.exp(-bh)).astype(BF16)
            k_state = (k * jnp.exp(b_last - bh)).astype(BF16)
            parts.append((t, h, q_in, k_in, k_state, jnp.exp(b_last)))
    scores = [jnp.where(causal, _dot_nt(q_in, k_in), 0.0).astype(BF16) for _, _, q_in, k_in, _, _ in parts]
    values = [p_ref[0, t, v0 + h * dv:v0 + (h + 1) * dv] for t, h, _, _, _, _ in parts]
    intra = [_dot(sc, v) for sc, v in zip(scores, values)]
    updates = [_dot_tn(v, k_state) for v, (_, _, _, _, k_state, _) in zip(values, parts)]
    states = [st_ref[h] for h in range(heads)]
    for n, (t, h, q_in, _, _, decay) in enumerate(parts):
        o = intra[n] + _dot_nt(q_in, states[h].astype(BF16))
        states[h] = decay * states[h] + updates[n]
        o = o * lax.rsqrt(jnp.mean(o * o, axis=-1, keepdims=True) + RMS_EPS) * ng_ref[...]
        r = p_ref[0, t, r0 + h * dv:r0 + (h + 1) * dv].astype(F32)
        o_ref[0, t, h * dv:(h + 1) * dv] = (o * (r * _sigmoid(r))).astype(o_ref.dtype)
    for h in range(heads):
        st_ref[h] = states[h]


def _gla_core(proj, w_gate_up, b_gate, norm_g, dk, dv, rows=512):
    b, s, n = proj.shape
    rows = min(rows, s)
    hk = GLA_HEADS * dk
    return pl.pallas_call(
        functools.partial(_gla_kernel, rows=rows, dk=dk, dv=dv),
        grid=(b, s // rows),
        in_specs=[
            pl.BlockSpec((1, rows, n), lambda bi, i: (bi, i, 0)),
            pl.BlockSpec((LANES, hk), lambda bi, i: (0, 0)),
            pl.BlockSpec((1, hk), lambda bi, i: (0, 0)),
            pl.BlockSpec((1, dv), lambda bi, i: (0, 0)),
        ],
        out_specs=pl.BlockSpec((1, rows, GLA_HEADS * dv), lambda bi, i: (bi, i, 0)),
        out_shape=jax.ShapeDtypeStruct((b, s, GLA_HEADS * dv), BF16),
        scratch_shapes=[pltpu.VMEM((GLA_HEADS, dv, dk), F32)],
        compiler_params=_params("parallel", "arbitrary"),
        name="gla_core",
    )(proj, w_gate_up, b_gate, norm_g)


def _conv_res_kernel(gb_ref, gc_ref, u_ref, hgc_ref, hu_ref, cw_ref, cb_ref, w_ref, x_ref, g_ref, y_ref):
    u2 = gc_ref[0].astype(F32) * u_ref[0].astype(F32)
    halo = hgc_ref[0].astype(F32) * hu_ref[0].astype(F32)
    halo = jnp.where(pl.program_id(1) > 0, halo, 0.0)
    rows = lax.broadcasted_iota(jnp.int32, u2.shape, 0)
    prev1 = jnp.where(rows == 0, halo[7:8], pltpu.roll(u2, 1, 0))
    prev2 = jnp.where(rows == 0, halo[6:7], jnp.where(rows == 1, halo[7:8], pltpu.roll(u2, 2, 0)))
    cw = cw_ref[...]
    y = cw[0:1] * prev2 + cw[1:2] * prev1 + cw[2:3] * u2 + cb_ref[...]
    o = (gb_ref[0].astype(F32) * y).astype(BF16)
    y_ref[0] = x_ref[0] + g_ref[0] * _dot(o, w_ref[...])


def _conv_res(proj, conv_w, conv_b, w_out, x, gate, tm=1024):
    b, s, d = x.shape
    tm = min(tm, s)
    halo_blocks = tm // 8

    def halo_map(col):
        return lambda bi, i: (bi, jnp.maximum(i * halo_blocks - 1, 0), col)

    return pl.pallas_call(
        _conv_res_kernel,
        grid=(b, s // tm),
        in_specs=[
            pl.BlockSpec((1, tm, d), lambda bi, i: (bi, i, 0)),
            pl.BlockSpec((1, tm, d), lambda bi, i: (bi, i, 1)),
            pl.BlockSpec((1, tm, d), lambda bi, i: (bi, i, 2)),
            pl.BlockSpec((1, 8, d), halo_map(1)),
            pl.BlockSpec((1, 8, d), halo_map(2)),
            pl.BlockSpec((CONV_WIDTH, d), lambda bi, i: (0, 0)),
            pl.BlockSpec((1, d), lambda bi, i: (0, 0)),
            pl.BlockSpec((d, d), lambda bi, i: (0, 0)),
            pl.BlockSpec((1, tm, d), lambda bi, i: (bi, i, 0)),
            pl.BlockSpec((1, 1, d), lambda bi, i: (bi, 0, 0)),
        ],
        out_specs=pl.BlockSpec((1, tm, d), lambda bi, i: (bi, i, 0)),
        out_shape=jax.ShapeDtypeStruct((b, s, d), F32),
        compiler_params=_params("parallel", "parallel"),
        name="conv_res",
    )(proj, proj, proj, proj, proj, conv_w, conv_b.reshape(1, d), w_out, x, gate)


def _dil_kernel(q_ref, kc_ref, vc_ref, kp_ref, vp_ref, o_ref, l_ref, *, rows, hd, dilation, group):
    qb = DIL_QBLOCK
    n = pl.program_id(1)
    width = q_ref.shape[-1] // group
    scale = hd ** -0.5
    fold_scale = (hd & (hd - 1)) == 0 and (hd.bit_length() - 1) % 2 == 0
    first = lax.broadcasted_iota(jnp.int32, (qb, LANES), 1) < hd
    row = lax.broadcasted_iota(jnp.int32, (qb, qb), 0)
    col = lax.broadcasted_iota(jnp.int32, (qb, qb), 1)
    in_prev = col >= row
    in_cur = col <= row
    neg = -jnp.inf

    tiles = []
    for u, s in [(u, s) for u in range(group) for s in range(rows // qb)]:
        t = slice(s * qb, (s + 1) * qb)
        tp = slice((s - 1) * qb, s * qb)
        prev_mask = jnp.logical_and(in_prev, n > 0) if s == 0 else in_prev
        for p in range(width // LANES):
            c = slice(u * width + p * LANES, u * width + (p + 1) * LANES)
            q = q_ref[0, t, c]
            if fold_scale:
                q = (q.astype(F32) * scale).astype(BF16)
            zero = jnp.zeros_like(q)
            q_heads = (jnp.where(first, q, zero), jnp.where(first, zero, q))
            if s == 0:
                k_prev, v_prev = kp_ref[0, :, c], vp_ref[0, :, c]
            else:
                k_prev, v_prev = kc_ref[0, tp, c], vc_ref[0, tp, c]
            tiles.append(((u, s), p, q_heads, k_prev, kc_ref[0, t, c], v_prev, vc_ref[0, t, c], prev_mask))

    scores = []
    for _, _, q_heads, k_prev, k_cur, _, _, prev_mask in tiles:
        pair = []
        for h in range(2):
            z_prev, z_cur = _dot_nt(q_heads[h], k_prev), _dot_nt(q_heads[h], k_cur)
            if not fold_scale:
                z_prev, z_cur = z_prev * scale, z_cur * scale
            pair.append((jnp.where(prev_mask, z_prev, neg), jnp.where(in_cur, z_cur, neg)))
        scores.append(pair)

    probs = []
    for pair in scores:
        stats = []
        for z_prev, z_cur in pair:
            m = jnp.max(jnp.maximum(z_prev, z_cur), axis=1, keepdims=True)
            p_prev = jnp.exp(z_prev - m)
            p_cur = jnp.exp(z_cur - m)
            stats.append((p_prev.astype(BF16), p_cur.astype(BF16), m))
        probs.append(stats)

    lane = lax.broadcasted_iota(jnp.int32, (qb, LANES), 1)
    lse_rows = {}
    for ((u, s), p, _, _, _, v_prev, v_cur, _), stats in zip(tiles, probs):
        if dilation == 1:
            t_out = slice(s * qb, (s + 1) * qb)
        else:
            rho = pl.program_id(2) * group + u
            t_out = pl.ds(s * qb * dilation + rho, qb, stride=dilation)
        ones = jnp.ones_like(v_prev)
        outs = []
        lse = lse_rows.get((u, s), jnp.zeros((qb, LANES), F32))
        for h, (p_prev, p_cur, m) in enumerate(stats):
            vp = jnp.where(first, v_prev, ones) if h == 0 else jnp.where(first, ones, v_prev)
            vc = jnp.where(first, v_cur, ones) if h == 0 else jnp.where(first, ones, v_cur)
            out = _dot(p_prev, vp) + _dot(p_cur, vc)
            den = out[:, hd:hd + 1] if h == 0 else out[:, 0:1]
            outs.append(out)
            lse = jnp.where(lane == 2 * p + h, m + jnp.log(den), lse)
        lse_rows[(u, s)] = lse
        num = jnp.where(first, outs[0], outs[1])
        den = pltpu.roll(jnp.where(first, outs[1], outs[0]), hd, 1)
        o_ref[0, p, t_out, :] = num / den
        if p == width // LANES - 1:
            l_ref[0, t_out, :] = lse


def _dil_group(view, dilation, window, seq):
    b, length, _ = view.shape
    width = DIL_HEADS * DIL_HEAD_DIM
    assert window == dilation * DIL_QBLOCK and seq % (dilation * DIL_QBLOCK) == 0
    rows = min(max(DIL_TILE_TOKENS // dilation, DIL_QBLOCK), 512, length)
    group = min(dilation, max(512 // rows, 1))
    prev_blocks = rows // DIL_QBLOCK
    per_part = dilation // group

    def cur(part):
        return lambda bi, i, j: (bi, i, part * per_part + j)

    def prev(part):
        return lambda bi, i, j: (bi, jnp.maximum(i * prev_blocks - 1, 0), part * per_part + j)

    pairs = width // LANES
    out_shape = jax.ShapeDtypeStruct((b, pairs, seq, LANES), F32)
    out_spec = pl.BlockSpec((1, pairs, rows * dilation, LANES), lambda bi, i, j: (bi, 0, i, 0))
    lse_shape = jax.ShapeDtypeStruct((b, seq, LANES), F32)
    lse_spec = pl.BlockSpec((1, rows * dilation, LANES), lambda bi, i, j: (bi, i, 0))
    return pl.pallas_call(
        functools.partial(_dil_kernel, rows=rows, hd=DIL_HEAD_DIM, dilation=dilation, group=group),
        grid=(b, length // rows, per_part),
        in_specs=[
            pl.BlockSpec((1, rows, group * width), cur(0)),
            pl.BlockSpec((1, rows, group * width), cur(1)),
            pl.BlockSpec((1, rows, group * width), cur(2)),
            pl.BlockSpec((1, DIL_QBLOCK, group * width), prev(1)),
            pl.BlockSpec((1, DIL_QBLOCK, group * width), prev(2)),
        ],
        out_specs=[out_spec, lse_spec],
        out_shape=[out_shape, lse_shape],
        compiler_params=_params("parallel", "parallel", "arbitrary"),
        name=f"dil_attn_r{dilation}",
    )(view, view, view, view, view)


def _dil_res_kernel(o0, o1, o2, l0, l1, l2, w_ref, x_ref, g_ref, y_ref):
    ls = (l0[0], l1[0], l2[0])
    m = jnp.maximum(jnp.maximum(ls[0], ls[1]), ls[2])
    es = [jnp.exp(l - m) for l in ls]
    total = es[0] + es[1] + es[2]
    ws = [e / total for e in es]
    hd = LANES // 2
    first = lax.broadcasted_iota(jnp.int32, (x_ref.shape[1], LANES), 1) < hd
    parts = []
    for p in range(o0.shape[1]):
        w = [jnp.where(first, wg[:, 2 * p:2 * p + 1], wg[:, 2 * p + 1:2 * p + 2]) for wg in ws]
        parts.append(w[0] * o0[0, p] + w[1] * o1[0, p] + w[2] * o2[0, p])
    o = jnp.concatenate(parts, axis=1)
    y_ref[0] = x_ref[0] + g_ref[0] * _dot(o.astype(BF16), w_ref[...])


def _dil_res(outs, lses, w_out, x, gate, tm=1024):
    b, s, d = x.shape
    k = w_out.shape[0]
    tm = min(tm, s)
    part = pl.BlockSpec((1, k // LANES, tm, LANES), lambda bi, i: (bi, 0, i, 0))
    lse = pl.BlockSpec((1, tm, LANES), lambda bi, i: (bi, i, 0))
    return pl.pallas_call(
        _dil_res_kernel,
        grid=(b, s // tm),
        in_specs=[part] * 3 + [lse] * 3 + [
            pl.BlockSpec((k, d), lambda bi, i: (0, 0)),
            pl.BlockSpec((1, tm, d), lambda bi, i: (bi, i, 0)),
            pl.BlockSpec((1, 1, d), lambda bi, i: (bi, 0, 0)),
        ],
        out_specs=pl.BlockSpec((1, tm, d), lambda bi, i: (bi, i, 0)),
        out_shape=jax.ShapeDtypeStruct((b, s, d), F32),
        compiler_params=_params("parallel", "parallel"),
        name="dil_res",
    )(*outs, *lses, w_out, x, gate)


def _router_kernel(x_ref, g_ref, sc_ref, sh_ref, wr_ref, br_ref, h_ref, route_ref, cnt_ref, run_ref, earlier_ref):
    tm = x_ref.shape[1]

    @pl.when(jnp.logical_and(pl.program_id(0) == 0, pl.program_id(1) == 0))
    def _():
        run_ref[...] = jnp.zeros_like(run_ref)
        row = lax.broadcasted_iota(jnp.int32, (tm, tm), 0)
        col = lax.broadcasted_iota(jnp.int32, (tm, tm), 1)
        earlier_ref[...] = jnp.where(col < row, 1.0, 0.0).astype(BF16)

    h = _mod_norm(x_ref[0], g_ref[...], sc_ref[0], sh_ref[0])
    _pack_slabs(h_ref, h)
    logits = _dot(h.astype(BF16), wr_ref[...].astype(BF16)) + br_ref[...]
    lane = lax.broadcasted_iota(jnp.int32, logits.shape, 1).astype(F32)
    neg = -jnp.inf
    far = float(LANES)

    def first_argmax(vals):
        top = jnp.max(vals, axis=1, keepdims=True)
        return top, jnp.min(jnp.where(vals == top, lane, far), axis=1, keepdims=True)

    grp_logits = jnp.where(lane < MOE_GROUPS, logits, neg)
    grp_max, grp = first_argmax(grp_logits)
    p_grp = 1.0 / jnp.sum(jnp.exp(grp_logits - grp_max), axis=1, keepdims=True)
    base = MOE_GROUPS + grp * MOE_PER_GROUP
    in_grp = jnp.logical_and(lane >= base, lane < base + MOE_PER_GROUP)
    exp_logits = jnp.where(in_grp, logits, neg)
    m1, i1 = first_argmax(exp_logits)
    m2, i2 = first_argmax(jnp.where(lane == i1, neg, exp_logits))
    e2 = jnp.exp(m2 - m1)
    g1 = p_grp / (1.0 + e2)
    g2 = g1 * e2
    e_a = i1 - MOE_GROUPS
    e_b = i2 - MOE_GROUPS

    pick_a = lane == e_a
    pick_b = lane == e_b
    both = jnp.where(jnp.logical_or(pick_a, pick_b), 1.0, 0.0)
    before = _dot(earlier_ref[...], both.astype(BF16)) + run_ref[...]
    rank_a = jnp.sum(jnp.where(pick_a, before, 0.0), axis=1, keepdims=True)
    rank_b = jnp.sum(jnp.where(pick_b, before, 0.0), axis=1, keepdims=True)
    run_ref[...] += jnp.sum(both, axis=0, keepdims=True)
    cnt_ref[...] = run_ref[...]

    route = jnp.where(lane == 0, e_a, jnp.where(lane == 1, e_b, 0.0))
    route = jnp.where(lane == 2, g1, jnp.where(lane == 3, g2, route))
    route_ref[...] = jnp.where(lane == 4, rank_a, jnp.where(lane == 5, rank_b, route))


def _router(x, g, sc, sh, w_route, b_route, tm=1024):
    b, s, d = x.shape
    tm = min(tm, s)
    steps = s // tm
    return pl.pallas_call(
        _router_kernel,
        grid=(b, steps),
        in_specs=[
            pl.BlockSpec((1, tm, d), lambda bi, i: (bi, i, 0)),
            pl.BlockSpec((1, d), lambda bi, i: (0, 0)),
            pl.BlockSpec((1, 1, d), lambda bi, i: (bi, 0, 0)),
            pl.BlockSpec((1, 1, d), lambda bi, i: (bi, 0, 0)),
            pl.BlockSpec((d, LANES), lambda bi, i: (0, 0)),
            pl.BlockSpec((1, LANES), lambda bi, i: (0, 0)),
        ],
        out_specs=[
            pl.BlockSpec((tm * (d // PACKED_LANES), LANES), lambda bi, i: (bi * steps + i, 0)),
            pl.BlockSpec((tm, LANES), lambda bi, i: (bi * steps + i, 0)),
            pl.BlockSpec((1, LANES), lambda bi, i: (0, 0)),
        ],
        out_shape=[
            jax.ShapeDtypeStruct((b * s * (d // PACKED_LANES), LANES), jnp.int32),
            jax.ShapeDtypeStruct((b * s, LANES), F32),
            jax.ShapeDtypeStruct((1, LANES), F32),
        ],
        scratch_shapes=[pltpu.VMEM((1, LANES), F32), pltpu.VMEM((tm, tm), BF16)],
        compiler_params=_params("arbitrary", "arbitrary"),
        name="moe_router",
    )(x, g.reshape(1, d), sc, sh, w_route, b_route)


def _sc_dispatch(h_slabs, dest, n_slots):
    t, chunks, _ = h_slabs.shape
    workers = SC_CORES * SC_SUBCORES
    per_worker = t // workers
    steps = per_worker // SC_WINDOW
    assert t % (workers * SC_WINDOW * 2) == 0
    mesh = plsc.VectorSubcoreMesh(core_axis_name="core", subcore_axis_name="subcore")

    @functools.partial(
        pl.kernel, mesh=mesh,
        out_type=jax.ShapeDtypeStruct((n_slots, chunks, LANES), h_slabs.dtype),
        scratch_types=[
            pltpu.VMEM((2, MOE_TOPK, SC_WINDOW), jnp.int32),
            pltpu.VMEM((2, SC_WINDOW, chunks, LANES), h_slabs.dtype),
            pltpu.SemaphoreType.DMA((2,)),
            pltpu.SemaphoreType.DMA((2,)),
        ])
    def dispatch(h_hbm, dest_hbm, slots_hbm, idx_v, rows_v, load_sem, store_sem):
        worker = lax.axis_index("subcore") * SC_CORES + lax.axis_index("core")
        base = worker * per_worker

        @pl.loop(0, steps, step=2)
        def _(step):
            for buf in range(2):
                off = base + (step + buf) * SC_WINDOW
                pltpu.async_copy(h_hbm.at[pl.ds(off, SC_WINDOW)], rows_v.at[buf], load_sem.at[buf])
                for k in range(MOE_TOPK):
                    pltpu.sync_copy(dest_hbm.at[k, pl.ds(off, SC_WINDOW)], idx_v.at[buf, k])
            for buf in range(2):
                off = base + (step + buf) * SC_WINDOW
                pltpu.make_async_copy(h_hbm.at[pl.ds(off, SC_WINDOW)], rows_v.at[buf], load_sem.at[buf]).wait()
                copies = [pltpu.async_copy(rows_v.at[buf], slots_hbm.at[idx_v.at[buf, k]], store_sem.at[buf])
                          for k in range(MOE_TOPK)]
                for cp in copies:
                    cp.wait()

    return dispatch(h_slabs, dest)


def _zero_tail_kernel(tail_ref, valid_ref, s_in, s_out):
    del tail_ref
    rows = lax.broadcasted_iota(jnp.int32, s_in.shape, 0)
    s_out[...] = jnp.where(rows < valid_ref[pl.program_id(0)], s_in[...], 0)


def _zero_tails(tail_blk, valid_rows, slots, block_rows):
    grid_spec = pltpu.PrefetchScalarGridSpec(
        num_scalar_prefetch=2,
        grid=(tail_blk.shape[0],),
        in_specs=[pl.BlockSpec((block_rows, LANES), lambda e, tail, valid: (tail[e], 0))],
        out_specs=pl.BlockSpec((block_rows, LANES), lambda e, tail, valid: (tail[e], 0)),
    )
    return pl.pallas_call(
        _zero_tail_kernel,
        grid_spec=grid_spec,
        out_shape=jax.ShapeDtypeStruct(slots.shape, slots.dtype),
        input_output_aliases={2: 0},
        compiler_params=_params("arbitrary"),
        name="moe_zero_tails",
    )(tail_blk, valid_rows, slots)


def _expert_kernel(blk_e_ref, used_ref, x_ref, wg_ref, wu_ref, wd_ref, y_ref, wg_bf, wu_bf, wd_bf):
    i = pl.program_id(0)
    live = i < used_ref[0]
    changed = jnp.logical_or(i == 0, blk_e_ref[i] != blk_e_ref[jnp.maximum(i - 1, 0)])

    @pl.when(jnp.logical_and(live, changed))
    def _():
        wg_bf[...] = wg_ref[0, 0].astype(BF16)
        wu_bf[...] = wu_ref[0, 0].astype(BF16)
        wd_bf[...] = wd_ref[0, 0].astype(BF16)

    @pl.when(live)
    def _():
        bm = x_ref.shape[0] // (wg_bf.shape[0] // PACKED_LANES)
        x = _unpack_slabs(x_ref, bm).astype(BF16)
        gate = _dot(x, wg_bf[...])
        up = _dot(x, wu_bf[...])
        act = (gate * _sigmoid(gate) * up).astype(BF16)
        _pack_slabs(y_ref, _dot(act, wd_bf[...]))

    @pl.when(jnp.logical_not(live))
    def _():
        y_ref[...] = jnp.zeros_like(y_ref)


def _expert_blocks(blk_e, used, slots, w_gate, w_up, w_down, layer, bm):
    nb = blk_e.shape[0]
    _, _, d, f = w_gate.shape
    chunks = d // PACKED_LANES

    def block(i, e, n):
        return (jnp.minimum(i, n[0] - 1), 0)

    def weight(i, e, n):
        return (layer, e[jnp.minimum(i, n[0] - 1)], 0, 0)

    grid_spec = pltpu.PrefetchScalarGridSpec(
        num_scalar_prefetch=2,
        grid=(nb,),
        in_specs=[
            pl.BlockSpec((bm * chunks, LANES), block),
            pl.BlockSpec((1, 1, d, f), weight),
            pl.BlockSpec((1, 1, d, f), weight),
            pl.BlockSpec((1, 1, f, d), weight),
        ],
        out_specs=pl.BlockSpec((bm * chunks, LANES), lambda i, e, n: (i, 0)),
        scratch_shapes=[pltpu.VMEM((d, f), BF16), pltpu.VMEM((d, f), BF16), pltpu.VMEM((f, d), BF16)],
    )
    return pl.pallas_call(
        _expert_kernel,
        grid_spec=grid_spec,
        out_shape=jax.ShapeDtypeStruct((nb * bm * chunks, LANES), jnp.int32),
        compiler_params=_params("arbitrary"),
        name="moe_experts",
    )(blk_e, used, slots, w_gate, w_up, w_down)


def _sc_gather(y_slabs, idx):
    n = idx.shape[0]
    _, chunks, _ = y_slabs.shape
    workers = SC_CORES * SC_SUBCORES
    per_worker = n // workers
    steps = per_worker // SC_WINDOW
    assert n % (workers * SC_WINDOW * 2) == 0
    mesh = plsc.VectorSubcoreMesh(core_axis_name="core", subcore_axis_name="subcore")

    @functools.partial(
        pl.kernel, mesh=mesh,
        out_type=jax.ShapeDtypeStruct((n, chunks, LANES), y_slabs.dtype),
        scratch_types=[
            pltpu.VMEM((2, SC_WINDOW), jnp.int32),
            pltpu.VMEM((2, SC_WINDOW, chunks, LANES), y_slabs.dtype),
            pltpu.SemaphoreType.DMA((2,)),
        ])
    def gather(y_hbm, idx_hbm, out_hbm, idx_v, rows_v, sem):
        worker = lax.axis_index("subcore") * SC_CORES + lax.axis_index("core")
        base = worker * per_worker

        @pl.loop(0, steps, step=2)
        def _(step):
            for buf in range(2):
                off = base + (step + buf) * SC_WINDOW
                pltpu.sync_copy(idx_hbm.at[pl.ds(off, SC_WINDOW)], idx_v.at[buf])
                pltpu.async_copy(y_hbm.at[idx_v.at[buf]], rows_v.at[buf], sem.at[buf])
            for buf in range(2):
                off = base + (step + buf) * SC_WINDOW
                pltpu.make_async_copy(y_hbm.at[idx_v.at[buf]], rows_v.at[buf], sem.at[buf]).wait()
                pltpu.sync_copy(rows_v.at[buf], out_hbm.at[pl.ds(off, SC_WINDOW)])

    return gather(y_slabs, idx)


def _combine_kernel(g0_ref, g1_ref, route_ref, x_ref, gate_ref, fg_ref, o_ref, *, final):
    tm = x_ref.shape[0]
    route = route_ref[...]
    y = route[:, 2:3] * _unpack_slabs(g0_ref, tm) + route[:, 3:4] * _unpack_slabs(g1_ref, tm)
    out = x_ref[...] + gate_ref[0] * y
    if final:
        out = out * lax.rsqrt(jnp.mean(out * out, axis=-1, keepdims=True) + RMS_EPS) * fg_ref[...]
    o_ref[...] = out


def _combine(dest, route, y_slabs, x, gate, final_g, final, tm, pieces=8):
    b, s, d = x.shape
    t = b * s
    chunks = d // PACKED_LANES
    per_seq = s // tm
    piece = t // pieces
    tiles = piece // tm
    out = x.reshape(t, d)
    fg = final_g.reshape(1, d)
    for c in range(pieces):
        idx = dest[:, c * piece:(c + 1) * piece].reshape(MOE_TOPK * piece)
        g = _sc_gather(y_slabs, idx).reshape(MOE_TOPK * piece * chunks, LANES)
        first = c * tiles
        out = pl.pallas_call(
            functools.partial(_combine_kernel, final=final),
            grid=(tiles,),
            in_specs=[
                pl.BlockSpec((tm * chunks, LANES), lambda i: (i, 0)),
                pl.BlockSpec((tm * chunks, LANES), lambda i: (tiles + i, 0)),
                pl.BlockSpec((tm, LANES), lambda i, first=first: (first + i, 0)),
                pl.BlockSpec((tm, d), lambda i, first=first: (first + i, 0)),
                pl.BlockSpec((1, 1, d), lambda i, first=first: ((first + i) // per_seq, 0, 0)),
                pl.BlockSpec((1, d), lambda i: (0, 0)),
            ],
            out_specs=pl.BlockSpec((tm, d), lambda i, first=first: (first + i, 0)),
            out_shape=jax.ShapeDtypeStruct((t, d), F32),
            input_output_aliases={3: 0},
            compiler_params=_params("parallel"),
            name="moe_combine",
        )(g, g, route, out, gate, fg)
    return out.reshape(b, s, d)


def _slot_tables(route, counts, bm):
    t = route.shape[0]
    nb = MOE_TOPK * t // bm + MOE_EXPERTS
    counts = counts[0, :MOE_EXPERTS].astype(jnp.int32)
    padded = (counts + bm - 1) // bm * bm
    pad_end = jnp.cumsum(padded)
    pad_start = pad_end - padded
    expert = route[:, 0:MOE_TOPK].astype(jnp.int32)
    rank = route[:, 4:4 + MOE_TOPK].astype(jnp.int32)
    dest = pad_start[expert] + rank
    blk_start = jnp.arange(nb, dtype=jnp.int32) * bm
    blk_e = jnp.minimum(jnp.sum((pad_end[None, :] <= blk_start[:, None]).astype(jnp.int32), axis=1), MOE_EXPERTS - 1)
    used = (pad_end[-1:] // bm).astype(jnp.int32)
    tail_blk = jnp.maximum(pad_end // bm - 1, 0).astype(jnp.int32)
    tail_fill = jnp.where(counts % bm != 0, counts % bm, bm).astype(jnp.int32)
    return dest.T, blk_e, used, tail_blk, tail_fill, nb


def _moe_layer(x, g, sc, sh, gate, w_route, b_route, w_gate, w_up, w_down, layer, final_g, final, bm=512, tm=1024):
    b, s, d = x.shape
    t = b * s
    chunks = d // PACKED_LANES
    tm = min(tm, s)
    h_rows, route, counts = _router(x, g, sc, sh, w_route, b_route)
    dest, blk_e, used, tail_blk, tail_fill, nb = _slot_tables(route, counts, bm)
    slots = _sc_dispatch(h_rows.reshape(t, chunks, LANES), dest, nb * bm)
    slots = _zero_tails(tail_blk, tail_fill * chunks, slots.reshape(nb * bm * chunks, LANES), bm * chunks)
    y_slabs = _expert_blocks(blk_e, used, slots, w_gate, w_up, w_down, layer, bm).reshape(nb * bm, chunks, LANES)
    return _combine(dest, route, y_slabs, x, gate, final_g, final, tm)


def kernel(x, c, ada_w, ada_b, norm_g, final_g, sb_w_in, sb_w_out, gla_w_in, gla_w_gate_up, gla_b_gate, gla_norm_g, gla_w_out, conv_w_in, conv_w, conv_b, conv_w_out, dil_w_in, dil_w_out, moe_w_grp, moe_b_grp, moe_w_exp, moe_b_exp, moe_w_gate, moe_w_up, moe_w_down):
    b, s, d = x.shape
    depth = ada_w.shape[0]
    mod = _ada_mod(c, ada_w, ada_b).reshape(depth, b, N_MOD, 1, d)
    for i in range(depth):
        sh1, sc1, g1, sh2, sc2, g2 = (mod[i, :, m] for m in range(N_MOD))
        kind, j = i % 4, i // 4
        if kind == 0:
            qkv = _norm_proj(x, norm_g[i, 0], sc1, sh1, sb_w_in[j].astype(BF16))
            x = _proj_res(_sb_attention(qkv), sb_w_out[j].astype(BF16), x, g1)
        elif kind == 1:
            dk = gla_w_gate_up.shape[-1] // GLA_HEADS
            dv = gla_norm_g.shape[-1]
            n_in = gla_w_in.shape[-1]
            w_in = jnp.pad(gla_w_in[j], ((0, 0), (0, -n_in % MXU_COLUMNS))).astype(BF16)
            w_up = jnp.pad(gla_w_gate_up[j], ((0, LANES - GLA_GATE_RANK), (0, 0))).astype(BF16)
            proj = _norm_proj(x, norm_g[i, 0], sc1, sh1, w_in, tn=MXU_COLUMNS)
            o = _gla_core(proj, w_up, gla_b_gate[j].reshape(1, -1), gla_norm_g[j].reshape(1, -1), dk, dv)
            x = _proj_res(o, gla_w_out[j].astype(BF16), x, g1)
        elif kind == 2:
            proj = _norm_proj(x, norm_g[i, 0], sc1, sh1, conv_w_in[j].astype(BF16))
            x = _conv_res(proj, conv_w[j], conv_b[j], conv_w_out[j].astype(BF16), x, g1)
        else:
            group_w = 3 * DIL_HEADS * DIL_HEAD_DIM
            outs, lses = [], []
            for g, (window, dilation) in enumerate(DIL_PATTERNS):
                w_g = dil_w_in[j][:, g * group_w:(g + 1) * group_w].astype(BF16)
                view = _norm_proj(x, norm_g[i, 0], sc1, sh1, w_g, dilation=dilation)
                o_g, l_g = _dil_group(view, dilation, window, s)
                outs.append(o_g)
                lses.append(l_g)
            x = _dil_res(outs, lses, dil_w_out[j].astype(BF16), x, g1)
        w_route = jnp.pad(jnp.concatenate([moe_w_grp[i], moe_w_exp[i]], axis=1),
                          ((0, 0), (0, LANES - MOE_GROUPS - MOE_EXPERTS)))
        b_route = jnp.pad(jnp.concatenate([moe_b_grp[i], moe_b_exp[i]]), (0, LANES - MOE_GROUPS - MOE_EXPERTS))
        x = _moe_layer(x, norm_g[i, 1], sc2, sh2, g2, w_route, b_route.reshape(1, LANES),
                       moe_w_gate, moe_w_up, moe_w_down, i, final_g, final=(i == depth - 1))
    return x
```
